```python
import jax, jax.numpy as jnp
from jax import lax
import numpy as np

D_MODEL = 1024
BATCH = 8
SEQ = 8192
DEPTH = 4

HEAD_DIM = 64
D_MIX = D_MODEL
N_Q_HEADS = 8
N_KV_HEADS = 2
GQA_GROUP = N_Q_HEADS // N_KV_HEADS
ATTN_W = N_Q_HEADS * HEAD_DIM
KV_W = N_KV_HEADS * HEAD_DIM
N_SG_HEADS = 4
SG_W = N_SG_HEADS * HEAD_DIM
SG_CHUNK = 128
N_CONV_GROUPS = 4
CONV_W = N_CONV_GROUPS * HEAD_DIM
CONV_K = 3
D_FF = 2816
Q_BLOCK = 128
GRID_W = 64
ROPE_THETA = 10000.0
AXIS_DIM = HEAD_DIM // 2
EPS = 1e-6
PROJ_SPLITS = (ATTN_W, KV_W, KV_W, SG_W, SG_W, CONV_W, CONV_W, CONV_W)
PROJ_W = sum(PROJ_SPLITS)

kernel_name = "hybrid_parallel_heads_encoder"


def _rmsnorm(x, g):
    x32 = x.astype(jnp.float32)
    y = x32 * lax.rsqrt(jnp.mean(x32 * x32, axis=-1, keepdims=True) + EPS)
    return y.astype(x.dtype) * g


def _dwconv3(x, w):
    xp = jnp.pad(x, ((0, 0), (1, 1), (0, 0)))
    return xp[:, :-2] * w[0] + xp[:, 1:-1] * w[1] + xp[:, 2:] * w[2]


def _axial_angles(seq):
    rows = seq // GRID_W
    row = jnp.broadcast_to(jnp.arange(rows)[:, None], (rows, GRID_W)).reshape(-1)
    col = jnp.broadcast_to(jnp.arange(GRID_W)[None, :], (rows, GRID_W)).reshape(-1)
    inv = 1.0 / (ROPE_THETA ** (jnp.arange(AXIS_DIM // 2, dtype=jnp.float32) * 2.0 / AXIS_DIM))
    ang_r = row.astype(jnp.float32)[:, None] * inv[None, :]
    ang_c = col.astype(jnp.float32)[:, None] * inv[None, :]
    return ang_r, ang_c


def _rotate(x, cos, sin):
    x1, x2 = jnp.split(x, 2, axis=-1)
    return jnp.concatenate([x1 * cos - x2 * sin, x1 * sin + x2 * cos], axis=-1)


def _axial_rope(x, tabs):
    cr, sr, cc, sc = tabs
    return jnp.concatenate([_rotate(x[..., :AXIS_DIM], cr, sr),
                            _rotate(x[..., AXIS_DIM:], cc, sc)], axis=-1)


def _block_attention(q, k, v):
    b, s = q.shape[0], q.shape[1]
    nblk = s // Q_BLOCK
    scale = HEAD_DIM ** -0.5
    qb = q.reshape(b, nblk, Q_BLOCK, N_KV_HEADS, GQA_GROUP, HEAD_DIM).transpose(1, 0, 2, 3, 4, 5)

    def one_block(qblk):
        sc = jnp.einsum('bqgrd,bkgd->bgrqk', qblk, k).astype(jnp.float32) * scale
        p = jax.nn.softmax(sc, axis=-1).astype(v.dtype)
        return jnp.einsum('bgrqk,bkgd->bqgrd', p, v)

    out = lax.map(one_block, qb)
    return out.transpose(1, 0, 2, 3, 4, 5).reshape(b, s, ATTN_W)


def _spatial_gating(u, v, g_v, w_s, b_s):
    b, s = u.shape[0], u.shape[1]
    nchunk = s // SG_CHUNK
    vn = _rmsnorm(v, g_v).reshape(b, nchunk, SG_CHUNK, N_SG_HEADS, HEAD_DIM)
    mixed = jnp.einsum('hpq,bnqhc->bnphc', w_s, vn) + b_s.T[:, :, None]
    return u * mixed.reshape(b, s, SG_W)


def _fwd_setup_inputs(seed: int = 0) -> dict:
    key = jax.random.key(seed)
    ks = jax.random.split(key, 16)
    f32 = jnp.float32
    nrm = lambda k, shape, s: jax.random.normal(k, shape, f32) * s
    res_scale = (2.0 * DEPTH) ** -0.5
    return {
        "x": jax.random.normal(ks[0], (BATCH, SEQ, D_MODEL), f32),
        "norm1_g": 1.0 + nrm(ks[1], (DEPTH, D_MODEL), 0.02),
        "w_in": nrm(ks[2], (DEPTH, D_MODEL, PROJ_W), D_MODEL ** -0.5),
        "q_norm_g": 1.0 + nrm(ks[3], (DEPTH, HEAD_DIM), 0.02),
        "k_norm_g": 1.0 + nrm(ks[4], (DEPTH, HEAD_DIM), 0.02),
        "sg_norm_g": 1.0 + nrm(ks[5], (DEPTH, SG_W), 0.02),
        "sg_w": nrm(ks[6], (DEPTH, N_SG_HEADS, SG_CHUNK, SG_CHUNK), 0.5 * SG_CHUNK ** -0.5),
        "sg_b": 1.0 + nrm(ks[7], (DEPTH, N_SG_HEADS, SG_CHUNK), 0.1),
        "conv_w": nrm(ks[8], (DEPTH, CONV_K, CONV_W), CONV_K ** -0.5),
        "w_out": nrm(ks[9], (DEPTH, D_MIX, D_MODEL), D_MIX ** -0.5 * res_scale),
        "norm2_g": 1.0 + nrm(ks[10], (DEPTH, D_MODEL), 0.02),
        "ffn_w_up": nrm(ks[11], (DEPTH, D_MODEL, 2 * D_FF), D_MODEL ** -0.5),
        "ffn_conv_w": nrm(ks[12], (DEPTH, CONV_K, 2 * D_FF), CONV_K ** -0.5),
        "ffn_w_down": nrm(ks[13], (DEPTH, D_FF, D_MODEL), D_FF ** -0.5 * res_scale),
    }


def _fwd_reference(x, norm1_g, w_in, q_norm_g, k_norm_g, sg_norm_g, sg_w, sg_b, conv_w,
              w_out, norm2_g, ffn_w_up, ffn_conv_w, ffn_w_down):
    b, s, _ = x.shape
    ang_r, ang_c = _axial_angles(s)
    dt = x.dtype
    tabs = (jnp.cos(ang_r)[:, None, :].astype(dt), jnp.sin(ang_r)[:, None, :].astype(dt),
            jnp.cos(ang_c)[:, None, :].astype(dt), jnp.sin(ang_c)[:, None, :].astype(dt))
    split_at = np.cumsum(PROJ_SPLITS)[:-1].tolist()

    for l in range(DEPTH):
        h = _rmsnorm(x, norm1_g[l])
        p = h @ w_in[l]
        q, k, v, su, sv, cb, cc, cx = jnp.split(p, split_at, axis=-1)

        q = _rmsnorm(q.reshape(b, s, N_Q_HEADS, HEAD_DIM), q_norm_g[l])
        k = _rmsnorm(k.reshape(b, s, N_KV_HEADS, HEAD_DIM), k_norm_g[l])
        q = _axial_rope(q, tabs)
        k = _axial_rope(k, tabs)
        a_out = _block_attention(q, k, v.reshape(b, s, N_KV_HEADS, HEAD_DIM))

        g_out = _spatial_gating(su, sv, sg_norm_g[l], sg_w[l], sg_b[l])

        c_out = cb * _dwconv3(cc * cx, conv_w[l])

        mix = jnp.concatenate([a_out, g_out, c_out], axis=-1)
        x = x + mix @ w_out[l]

        h2 = _rmsnorm(x, norm2_g[l])
        up = _dwconv3(h2 @ ffn_w_up[l], ffn_conv_w[l])
        gate, val = jnp.split(up, 2, axis=-1)
        x = x + (jax.nn.silu(gate) * val) @ ffn_w_down[l]
    return x


import jax as _jax
import jax.numpy as _jnp

TWIN_FORMAT = 'train_step'
FWD_PARAMS = ['x', 'norm1_g', 'w_in', 'q_norm_g', 'k_norm_g', 'sg_norm_g', 'sg_w', 'sg_b', 'conv_w', 'w_out', 'norm2_g', 'ffn_w_up', 'ffn_conv_w', 'ffn_w_down']
TWIN_WEIGHTS = ['norm1_g', 'w_in', 'q_norm_g', 'k_norm_g', 'sg_norm_g', 'sg_w', 'sg_b', 'conv_w', 'w_out', 'norm2_g', 'ffn_w_up', 'ffn_conv_w', 'ffn_w_down']
TWIN_DIFF_INPUT = 'x'
TWIN_INPUTS = ['x', 'norm1_g', 'w_in', 'q_norm_g', 'k_norm_g', 'sg_norm_g', 'sg_w', 'sg_b', 'conv_w', 'w_out', 'norm2_g', 'ffn_w_up', 'ffn_conv_w', 'ffn_w_down', 'loss_target', 'm_norm1_g', 'm_w_in', 'm_q_norm_g', 'm_k_norm_g', 'm_sg_norm_g', 'm_sg_w', 'm_sg_b', 'm_conv_w', 'm_w_out', 'm_norm2_g', 'm_ffn_w_up', 'm_ffn_conv_w', 'm_ffn_w_down', 'v_norm1_g', 'v_w_in', 'v_q_norm_g', 'v_k_norm_g', 'v_sg_norm_g', 'v_sg_w', 'v_sg_b', 'v_conv_w', 'v_w_out', 'v_norm2_g', 'v_ffn_w_up', 'v_ffn_conv_w', 'v_ffn_w_down']
TWIN_OUTPUTS = ['loss', 'grad_x', 'grad_norm1_g', 'grad_w_in', 'grad_q_norm_g', 'grad_k_norm_g', 'grad_sg_norm_g', 'grad_sg_w', 'grad_sg_b', 'grad_conv_w', 'grad_w_out', 'grad_norm2_g', 'grad_ffn_w_up', 'grad_ffn_conv_w', 'grad_ffn_w_down', 'delta_norm1_g', 'delta_w_in', 'delta_q_norm_g', 'delta_k_norm_g', 'delta_sg_norm_g', 'delta_sg_w', 'delta_sg_b', 'delta_conv_w', 'delta_w_out', 'delta_norm2_g', 'delta_ffn_w_up', 'delta_ffn_conv_w', 'delta_ffn_w_down', 'new_m_norm1_g', 'new_m_w_in', 'new_m_q_norm_g', 'new_m_k_norm_g', 'new_m_sg_norm_g', 'new_m_sg_w', 'new_m_sg_b', 'new_m_conv_w', 'new_m_w_out', 'new_m_norm2_g', 'new_m_ffn_w_up', 'new_m_ffn_conv_w', 'new_m_ffn_w_down', 'new_v_norm1_g', 'new_v_w_in', 'new_v_q_norm_g', 'new_v_k_norm_g', 'new_v_sg_norm_g', 'new_v_sg_w', 'new_v_sg_b', 'new_v_conv_w', 'new_v_w_out', 'new_v_norm2_g', 'new_v_ffn_w_up', 'new_v_ffn_conv_w', 'new_v_ffn_w_down']
TWIN_LEAF_KINDS = {'loss': 'loss', 'grad_x': 'grad_x', 'grad_norm1_g': 'grad_w', 'grad_w_in': 'grad_w', 'grad_q_norm_g': 'grad_w', 'grad_k_norm_g': 'grad_w', 'grad_sg_norm_g': 'grad_w', 'grad_sg_w': 'grad_w', 'grad_sg_b': 'grad_w', 'grad_conv_w': 'grad_w', 'grad_w_out': 'grad_w', 'grad_norm2_g': 'grad_w', 'grad_ffn_w_up': 'grad_w', 'grad_ffn_conv_w': 'grad_w', 'grad_ffn_w_down': 'grad_w', 'delta_norm1_g': 'delta_w', 'delta_w_in': 'delta_w', 'delta_q_norm_g': 'delta_w', 'delta_k_norm_g': 'delta_w', 'delta_sg_norm_g': 'delta_w', 'delta_sg_w': 'delta_w', 'delta_sg_b': 'delta_w', 'delta_conv_w': 'delta_w', 'delta_w_out': 'delta_w', 'delta_norm2_g': 'delta_w', 'delta_ffn_w_up': 'delta_w', 'delta_ffn_conv_w': 'delta_w', 'delta_ffn_w_down': 'delta_w', 'new_m_norm1_g': 'new_m', 'new_m_w_in': 'new_m', 'new_m_q_norm_g': 'new_m', 'new_m_k_norm_g': 'new_m', 'new_m_sg_norm_g': 'new_m', 'new_m_sg_w': 'new_m', 'new_m_sg_b': 'new_m', 'new_m_conv_w': 'new_m', 'new_m_w_out': 'new_m', 'new_m_norm2_g': 'new_m', 'new_m_ffn_w_up': 'new_m', 'new_m_ffn_conv_w': 'new_m', 'new_m_ffn_w_down': 'new_m', 'new_v_norm1_g': 'new_v', 'new_v_w_in': 'new_v', 'new_v_q_norm_g': 'new_v', 'new_v_k_norm_g': 'new_v', 'new_v_sg_norm_g': 'new_v', 'new_v_sg_w': 'new_v', 'new_v_sg_b': 'new_v', 'new_v_conv_w': 'new_v', 'new_v_w_out': 'new_v', 'new_v_norm2_g': 'new_v', 'new_v_ffn_w_up': 'new_v', 'new_v_ffn_conv_w': 'new_v', 'new_v_ffn_w_down': 'new_v'}


def _forward(args):
    return _fwd_reference(*[args[k] for k in FWD_PARAMS])


def _output_shape():
    def fwd():
        inp = _fwd_setup_inputs(0)
        return _fwd_reference(*[inp[k] for k in FWD_PARAMS])
    out = _jax.eval_shape(fwd)
    return out.shape, out.dtype

N_MICROBATCH = 1
ADAM_LR = 0.001
ADAM_B1 = 0.9
ADAM_B2 = 0.999
ADAM_EPS = 1e-08
ADAM_WD = 0.01
ADAM_STEP = 10
PER_EXAMPLE_BATCH_AXIS = {'x': 0, 'loss_target': 0}
SHARED_INPUTS = []
_WEIGHT_DTYPES = {'norm1_g': _jnp.float32, 'w_in': _jnp.float32, 'q_norm_g': _jnp.float32, 'k_norm_g': _jnp.float32, 'sg_norm_g': _jnp.float32, 'sg_w': _jnp.float32, 'sg_b': _jnp.float32, 'conv_w': _jnp.float32, 'w_out': _jnp.float32, 'norm2_g': _jnp.float32, 'ffn_w_up': _jnp.float32, 'ffn_conv_w': _jnp.float32, 'ffn_w_down': _jnp.float32}
MOMENT_SCALE = {'norm1_g': 8.318421e+00, 'w_in': 2.752011e-01, 'q_norm_g': 1.199849e-01, 'k_norm_g': 1.199706e-01, 'sg_norm_g': 1.966843e+00, 'sg_w': 1.625087e-01, 'sg_b': 3.940942e+00, 'conv_w': 4.482769e+00, 'w_out': 9.970445e-01, 'norm2_g': 6.508521e+00, 'ffn_w_up': 8.187886e-02, 'ffn_conv_w': 8.628795e-01, 'ffn_w_down': 3.857441e-01}


def _to_microbatches(a, axis):
    t = _jnp.moveaxis(a, axis, 0)
    t = t.reshape((N_MICROBATCH, t.shape[0] // N_MICROBATCH) + t.shape[1:])
    return _jnp.moveaxis(t, 1, axis + 1)


def setup_inputs(seed: int = 0) -> dict:
    inp = _fwd_setup_inputs(seed)
    key = _jax.random.fold_in(_jax.random.key(seed), 7919)
    shape, _ = _output_shape()
    out = dict(inp)
    out["loss_target"] = _jax.random.normal(_jax.random.fold_in(key, 0), shape, _jnp.float32)
    for i, name in enumerate(TWIN_WEIGHTS):
        w = inp[name].astype(_jnp.float32)
        if MOMENT_SCALE is None:
            s = _jnp.sqrt(_jnp.mean(_jnp.square(w)) + 1e-30)
        else:
            s = MOMENT_SCALE[name]
        km, kv = _jax.random.split(_jax.random.fold_in(key, i + 1))
        out[name] = w
        out["m_" + name] = s * _jax.random.normal(km, w.shape, _jnp.float32)
        out["v_" + name] = (s * s) * _jax.random.uniform(kv, w.shape, _jnp.float32, 0.5, 1.5)
    if N_MICROBATCH > 1:
        for name, axis in PER_EXAMPLE_BATCH_AXIS.items():
            out[name] = _to_microbatches(out[name], axis)
    return {'x': out['x'], 'norm1_g': out['norm1_g'], 'w_in': out['w_in'], 'q_norm_g': out['q_norm_g'], 'k_norm_g': out['k_norm_g'], 'sg_norm_g': out['sg_norm_g'], 'sg_w': out['sg_w'], 'sg_b': out['sg_b'], 'conv_w': out['conv_w'], 'w_out': out['w_out'], 'norm2_g': out['norm2_g'], 'ffn_w_up': out['ffn_w_up'], 'ffn_conv_w': out['ffn_conv_w'], 'ffn_w_down': out['ffn_w_down'], 'loss_target': out['loss_target'], 'm_norm1_g': out['m_norm1_g'], 'm_w_in': out['m_w_in'], 'm_q_norm_g': out['m_q_norm_g'], 'm_k_norm_g': out['m_k_norm_g'], 'm_sg_norm_g': out['m_sg_norm_g'], 'm_sg_w': out['m_sg_w'], 'm_sg_b': out['m_sg_b'], 'm_conv_w': out['m_conv_w'], 'm_w_out': out['m_w_out'], 'm_norm2_g': out['m_norm2_g'], 'm_ffn_w_up': out['m_ffn_w_up'], 'm_ffn_conv_w': out['m_ffn_conv_w'], 'm_ffn_w_down': out['m_ffn_w_down'], 'v_norm1_g': out['v_norm1_g'], 'v_w_in': out['v_w_in'], 'v_q_norm_g': out['v_q_norm_g'], 'v_k_norm_g': out['v_k_norm_g'], 'v_sg_norm_g': out['v_sg_norm_g'], 'v_sg_w': out['v_sg_w'], 'v_sg_b': out['v_sg_b'], 'v_conv_w': out['v_conv_w'], 'v_w_out': out['v_w_out'], 'v_norm2_g': out['v_norm2_g'], 'v_ffn_w_up': out['v_ffn_w_up'], 'v_ffn_conv_w': out['v_ffn_conv_w'], 'v_ffn_w_down': out['v_ffn_w_down']}


def _loss(weights, diff, rest, loss_target):
    with _jax.named_scope("forward"):
        args = {**rest, TWIN_DIFF_INPUT: diff, **{k: w.astype(_WEIGHT_DTYPES[k]) for k, w in weights.items()}}
        y = _forward(args)
    with _jax.named_scope("loss_head"):
        err = _jnp.square(y.astype(_jnp.float32) - loss_target)
        return 0.5 * _jnp.sum(_jnp.mean(err, axis=-1)) if err.ndim else 0.5 * err


def _adamw(w, g, m, v):
    m = ADAM_B1 * m + (1.0 - ADAM_B1) * g
    v = ADAM_B2 * v + (1.0 - ADAM_B2) * _jnp.square(g)
    m_hat = m / (1.0 - ADAM_B1 ** ADAM_STEP)
    v_hat = v / (1.0 - ADAM_B2 ** ADAM_STEP)
    delta = -ADAM_LR * (m_hat / (_jnp.sqrt(v_hat) + ADAM_EPS) + ADAM_WD * w)
    return delta, m, v


def reference(x, norm1_g, w_in, q_norm_g, k_norm_g, sg_norm_g, sg_w, sg_b, conv_w, w_out, norm2_g, ffn_w_up, ffn_conv_w, ffn_w_down, loss_target, m_norm1_g, m_w_in, m_q_norm_g, m_k_norm_g, m_sg_norm_g, m_sg_w, m_sg_b, m_conv_w, m_w_out, m_norm2_g, m_ffn_w_up, m_ffn_conv_w, m_ffn_w_down, v_norm1_g, v_w_in, v_q_norm_g, v_k_norm_g, v_sg_norm_g, v_sg_w, v_sg_b, v_conv_w, v_w_out, v_norm2_g, v_ffn_w_up, v_ffn_conv_w, v_ffn_w_down):
    given = dict(x=x, norm1_g=norm1_g, w_in=w_in, q_norm_g=q_norm_g, k_norm_g=k_norm_g, sg_norm_g=sg_norm_g, sg_w=sg_w, sg_b=sg_b, conv_w=conv_w, w_out=w_out, norm2_g=norm2_g, ffn_w_up=ffn_w_up, ffn_conv_w=ffn_conv_w, ffn_w_down=ffn_w_down, loss_target=loss_target, m_norm1_g=m_norm1_g, m_w_in=m_w_in, m_q_norm_g=m_q_norm_g, m_k_norm_g=m_k_norm_g, m_sg_norm_g=m_sg_norm_g, m_sg_w=m_sg_w, m_sg_b=m_sg_b, m_conv_w=m_conv_w, m_w_out=m_w_out, m_norm2_g=m_norm2_g, m_ffn_w_up=m_ffn_w_up, m_ffn_conv_w=m_ffn_conv_w, m_ffn_w_down=m_ffn_w_down, v_norm1_g=v_norm1_g, v_w_in=v_w_in, v_q_norm_g=v_q_norm_g, v_k_norm_g=v_k_norm_g, v_sg_norm_g=v_sg_norm_g, v_sg_w=v_sg_w, v_sg_b=v_sg_b, v_conv_w=v_conv_w, v_w_out=v_w_out, v_norm2_g=v_norm2_g, v_ffn_w_up=v_ffn_w_up, v_ffn_conv_w=v_ffn_conv_w, v_ffn_w_down=v_ffn_w_down)
    weights = {n: given[n] for n in TWIN_WEIGHTS}
    shared = {n: given[n] for n in SHARED_INPUTS}
    per_example = {n: given[n] for n in ['x']}
    grad_fn = _jax.value_and_grad(_loss, argnums=(0, 1))

    def one_microbatch(ex, loss_target):
        ex = dict(ex)
        diff = ex.pop(TWIN_DIFF_INPUT)
        return grad_fn(weights, diff, {**shared, **ex}, loss_target)

    if N_MICROBATCH == 1:
        loss, (grad_w, grad_x) = one_microbatch(per_example, given["loss_target"])
    else:
        def body(carry, xs):
            loss_sum, grad_sum = carry
            l_k, (gw_k, gx_k) = one_microbatch(xs[0], xs[1])
            with _jax.named_scope("update"):
                return (loss_sum + l_k, _jax.tree.map(_jnp.add, grad_sum, gw_k)), gx_k

        init = (_jnp.zeros((), _jnp.float32), _jax.tree.map(_jnp.zeros_like, weights))
        (loss, grad_w), grad_x = _jax.lax.scan(body, init, (per_example, given["loss_target"]))
    with _jax.named_scope("update"):
        delta_w, new_m, new_v = {}, {}, {}
        for n in TWIN_WEIGHTS:
            delta_w[n], new_m[n], new_v[n] = _adamw(weights[n], grad_w[n], given["m_" + n], given["v_" + n])
    return (loss, grad_x, *[grad_w[n] for n in TWIN_WEIGHTS], *[delta_w[n] for n in TWIN_WEIGHTS],
            *[new_m[n] for n in TWIN_WEIGHTS], *[new_v[n] for n in TWIN_WEIGHTS])
```

```python
import functools

import jax
import jax.numpy as jnp
from jax import lax
from jax.experimental import pallas as pl
from jax.experimental.pallas import tpu as pltpu

F32 = jnp.float32
BF16 = jnp.bfloat16

DEPTH = 4
D_MODEL = 1024
HEAD_DIM = 64
ATTN_W = 512
KV_W = 128
SG_W = 256
CONV_W = 256
SG_CHUNK = 128
D_FF = 2816
PROJ_W = 2048
GRID_W = 64
ROPE_THETA = 10000.0
AXIS_DIM = HEAD_DIM // 2
EPS = 1e-6
N_CHIPS = 4

ADAM_LR = 0.001
ADAM_B1 = 0.9
ADAM_B2 = 0.999
ADAM_EPS = 1e-08
ADAM_WD = 0.01
ADAM_STEP = 10

_ROW_TILE = 512
_FFN_ROW_TILE = 256
LANES = 128
HALO = 8
VMEM_LIMIT_BYTES = 56 * 1024 * 1024
MESH = pl.DeviceIdType.MESH
ANY = pl.BlockSpec(memory_space=pl.ANY)


def _cp(*sem):
    return pltpu.CompilerParams(dimension_semantics=sem if sem else None,
                                vmem_limit_bytes=VMEM_LIMIT_BYTES)


def _sds(shape, dtype):
    return jax.ShapeDtypeStruct(shape, dtype)


def _dot(a, b):
    return jnp.dot(a, b, preferred_element_type=F32)


def _dot_nt(a, b):
    return lax.dot_general(a, b, (((1,), (1,)), ((), ())), preferred_element_type=F32)


def _dot_tn(a, b):
    return lax.dot_general(a, b, (((0,), (0,)), ((), ())), preferred_element_type=F32)


def _norm_fwd(x, g):
    T, Dm = x.shape
    tm = min(T, _ROW_TILE)

    def body(x_ref, g_ref, o_ref):
        xv = x_ref[...]
        r = lax.rsqrt(jnp.mean(xv * xv, axis=-1, keepdims=True) + EPS)
        o_ref[...] = ((xv * r) * g_ref[...]).astype(BF16)

    return pl.pallas_call(
        body, name="norm_fwd", grid=(T // tm,),
        in_specs=[pl.BlockSpec((tm, Dm), lambda i: (i, 0)), pl.BlockSpec((1, Dm), lambda i: (0, 0))],
        out_specs=pl.BlockSpec((tm, Dm), lambda i: (i, 0)),
        out_shape=_sds((T, Dm), BF16), compiler_params=_cp("parallel"))(x, g)


def _norm_bwd(x, dh, dres, g):
    T, Dm = x.shape
    tm = min(T, _ROW_TILE)

    def body(x_ref, dh_ref, dr_ref, g_ref, dx_ref, dg_ref):
        i = pl.program_id(0)
        xv = x_ref[...]
        r = lax.rsqrt(jnp.mean(xv * xv, axis=-1, keepdims=True) + EPS)
        xh = xv * r
        dhv = dh_ref[...]
        dxh = dhv * g_ref[...]
        dx_ref[...] = dr_ref[...] + r * (dxh - xh * jnp.mean(dxh * xh, axis=-1, keepdims=True))
        part = jnp.sum(dhv * xh, axis=0, keepdims=True)

        @pl.when(i == 0)
        def _():
            dg_ref[...] = part

        @pl.when(i > 0)
        def _():
            dg_ref[...] += part

    row = pl.BlockSpec((tm, Dm), lambda i: (i, 0))
    vec = pl.BlockSpec((1, Dm), lambda i: (0, 0))
    return pl.pallas_call(
        body, name="norm_bwd", grid=(T // tm,),
        in_specs=[row, row, row, vec], out_specs=[row, vec],
        out_shape=[_sds((T, Dm), F32), _sds((1, Dm), F32)], compiler_params=_cp("arbitrary"))(x, dh, dres, g)


def _w_spec(w, rows, tn, row_of, col_of):
    if w.ndim == 2:
        return pl.BlockSpec((rows, tn), lambda *g: (row_of(*g), col_of(*g)))
    bps = w.shape[2] // tn
    return pl.BlockSpec((None, rows, tn), lambda *g: (col_of(*g) // bps, row_of(*g), col_of(*g) % bps))


def _mm_nn(a, w, *, tn, out_dtype, name, res=None, tm=512):
    M, K = a.shape
    N = w.shape[-1] if w.ndim == 2 else w.shape[0] * w.shape[2]
    tm = min(M, _ROW_TILE)
    has_res = res is not None

    def body(*refs):
        a_ref, w_ref = refs[0], refs[1]
        o_ref = refs[-1]
        acc = _dot(a_ref[...].astype(BF16), w_ref[...])
        if has_res:
            acc = acc + refs[2][...]
        o_ref[...] = acc.astype(out_dtype)

    in_specs = [pl.BlockSpec((tm, K), lambda j, i: (i, 0)),
                _w_spec(w, K, tn, lambda j, i: 0, lambda j, i: j)]
    args = [a, w]
    if has_res:
        in_specs.append(pl.BlockSpec((tm, tn), lambda j, i: (i, j)))
        args.append(res)
    return pl.pallas_call(
        body, name=name, grid=(N // tn, M // tm), in_specs=in_specs,
        out_specs=pl.BlockSpec((tm, tn), lambda j, i: (i, j)),
        out_shape=_sds((M, N), out_dtype), compiler_params=_cp("parallel", "parallel"))(*args)


def _a_spec(a, tm, tn, row_of, col_of):
    if a.ndim == 2:
        return pl.BlockSpec((tm, tn), lambda *g: (row_of(*g), col_of(*g)))
    bph = a.shape[2] // tn
    return pl.BlockSpec((None, tm, tn), lambda *g: (col_of(*g) // bph, row_of(*g), col_of(*g) % bph))


def _a_cols(a):
    return a.shape[1] if a.ndim == 2 else a.shape[0] * a.shape[2]


def _mm_nt(a, w, *, tkw, tn, name, tm=512):
    M = a.shape[-2]
    N = _a_cols(a)
    Kw = w.shape[-2]
    tm = min(M, _ROW_TILE)
    nred = N // tn

    def body(a_ref, w_ref, o_ref, *scratch):
        part = _dot_nt(a_ref[...].astype(BF16), w_ref[...])
        if nred == 1:
            o_ref[...] = part
            return
        acc_ref = scratch[0]
        n = pl.program_id(2)

        @pl.when(n == 0)
        def _():
            acc_ref[...] = part

        @pl.when(n > 0)
        def _():
            acc_ref[...] += part

        @pl.when(n == nred - 1)
        def _():
            o_ref[...] = acc_ref[...]

    return pl.pallas_call(
        body, name=name, grid=(M // tm, Kw // tkw, nred),
        in_specs=[_a_spec(a, tm, tn, lambda i, k, n: i, lambda i, k, n: n),
                  _w_spec(w, tkw, tn, lambda i, k, n: k, lambda i, k, n: n)],
        out_specs=pl.BlockSpec((tm, tkw), lambda i, k, n: (i, k)),
        out_shape=_sds((M, Kw), F32),
        scratch_shapes=[] if nred == 1 else [pltpu.VMEM((tm, tkw), F32)],
        compiler_params=_cp("parallel", "parallel", "arbitrary"))(a, w)


def _mm_tn(a, b, *, tk, tn, name, shards=None, tm=512):
    M, K = a.shape
    N = _a_cols(b)
    tm = min(M, _ROW_TILE)

    def body(a_ref, b_ref, o_ref):
        m = pl.program_id(2)
        part = _dot_tn(a_ref[...].astype(BF16), b_ref[...].astype(BF16))

        @pl.when(m == 0)
        def _():
            o_ref[...] = part

        @pl.when(m > 0)
        def _():
            o_ref[...] += part

    if shards is None:
        out_spec = pl.BlockSpec((tk, tn), lambda k, j, m: (k, j))
        out_shape = _sds((K, N), F32)
    else:
        bps = (N // shards) // tn
        out_spec = pl.BlockSpec((None, tk, tn), lambda k, j, m: (j // bps, k, j % bps))
        out_shape = _sds((shards, K, N // shards), F32)
    return pl.pallas_call(
        body, name=name, grid=(K // tk, N // tn, M // tm),
        in_specs=[pl.BlockSpec((tm, tk), lambda k, j, m: (m, k)),
                  _a_spec(b, tm, tn, lambda k, j, m: m, lambda k, j, m: j)],
        out_specs=out_spec, out_shape=out_shape,
        compiler_params=_cp("parallel", "parallel", "arbitrary"))(a, b)


def _halo_specs(T, tm, cw, ic):
    nb = tm // HALO
    last = T // HALO - 1

    def mk(rows, row_of):
        return pl.BlockSpec((rows, cw), lambda *g: (row_of(ic(*g)[0]), ic(*g)[1]))

    return [mk(HALO, lambda i: jnp.maximum(i * nb - 1, 0)), mk(tm, lambda i: i),
            mk(HALO, lambda i: jnp.minimum((i + 1) * nb, last))]


def _ext(prev_ref, cur_ref, next_ref, i, n):
    p = jnp.where(i > 0, prev_ref[...].astype(F32), 0.0)
    nx = jnp.where(i < n - 1, next_ref[...].astype(F32), 0.0)
    return jnp.concatenate([p, cur_ref[...].astype(F32), nx], axis=0)


def _dn(e):
    return pltpu.roll(e, 1, 0)


def _up(e):
    return pltpu.roll(e, e.shape[0] - 1, 0)


def _mid(e):
    return e[HALO:e.shape[0] - HALO]


def _conv3(e, w):
    return _dn(e) * w[0:1] + e * w[1:2] + _up(e) * w[2:3]


def _conv3_t(e, w):
    return _up(e) * w[0:1] + e * w[1:2] + _dn(e) * w[2:3]


def _conv3_wgrad(d, e):
    return jnp.concatenate([jnp.sum(_mid(d * _dn(e)), axis=0, keepdims=True),
                            jnp.sum(_mid(d * e), axis=0, keepdims=True),
                            jnp.sum(_mid(d * _up(e)), axis=0, keepdims=True)], axis=0)


def _sigmoid(x):
    return 1.0 / (1.0 + jnp.exp(-x))


def _accum(ref, i, part):
    @pl.when(i == 0)
    def _():
        ref[...] = part

    @pl.when(i > 0)
    def _():
        ref[...] += part


def _ffn_act_fwd(up, cw):
    T = up.shape[0]
    tm = min(T, _FFN_ROW_TILE)
    cb = D_FF // 2
    nblk = D_FF // cb
    n = T // tm

    def body(gp, gc, gn, vp, vc, vn, wg_ref, wv_ref, o_ref):
        i = pl.program_id(1)
        gate = _conv3(_ext(gp, gc, gn, i, n), wg_ref[...])
        val = _conv3(_ext(vp, vc, vn, i, n), wv_ref[...])
        o_ref[...] = _mid(gate * _sigmoid(gate) * val).astype(BF16)

    return pl.pallas_call(
        body, name="ffn_act_fwd", grid=(nblk, n),
        in_specs=_halo_specs(T, tm, cb, lambda j, i: (i, j)) + _halo_specs(T, tm, cb, lambda j, i: (i, j + nblk))
        + [pl.BlockSpec((3, cb), lambda j, i: (0, j)), pl.BlockSpec((3, cb), lambda j, i: (0, j + nblk))],
        out_specs=pl.BlockSpec((tm, cb), lambda j, i: (i, j)),
        out_shape=_sds((T, D_FF), BF16), compiler_params=_cp("parallel", "parallel"))(
            up, up, up, up, up, up, cw, cw)


def _ffn_act_bwd(up, dact, cw):
    T = up.shape[0]
    tm = min(T, _FFN_ROW_TILE)
    cb = D_FF // 2
    nblk = D_FF // cb
    n = T // tm

    def body(gp, gc, gn, vp, vc, vn, dp_, dc, dn_, wg_ref, wv_ref, dup_ref, dcw_ref):
        i = pl.program_id(1)
        wg, wv = wg_ref[...], wv_ref[...]
        eg = _ext(gp, gc, gn, i, n)
        ev = _ext(vp, vc, vn, i, n)
        ed = _ext(dp_, dc, dn_, i, n)
        gate = _conv3(eg, wg)
        val = _conv3(ev, wv)
        sg = _sigmoid(gate)
        d_gate = ed * val * (sg * (1.0 + gate * (1.0 - sg)))
        d_val = ed * (gate * sg)
        dup_ref[0] = _mid(_conv3_t(d_gate, wg)).astype(BF16)
        dup_ref[1] = _mid(_conv3_t(d_val, wv)).astype(BF16)
        part = jnp.stack([_conv3_wgrad(d_gate, eg), _conv3_wgrad(d_val, ev)], axis=0)
        _accum(dcw_ref, i, part)

    return pl.pallas_call(
        body, name="ffn_act_bwd", grid=(nblk, n),
        in_specs=_halo_specs(T, tm, cb, lambda j, i: (i, j)) + _halo_specs(T, tm, cb, lambda j, i: (i, j + nblk))
        + _halo_specs(T, tm, cb, lambda j, i: (i, j))
        + [pl.BlockSpec((3, cb), lambda j, i: (0, j)), pl.BlockSpec((3, cb), lambda j, i: (0, j + nblk))],
        out_specs=[pl.BlockSpec((2, tm, cb), lambda j, i: (0, i, j)),
                   pl.BlockSpec((2, 3, cb), lambda j, i: (0, 0, j))],
        out_shape=[_sds((2, T, D_FF), BF16), _sds((2, 3, D_FF), F32)],
        compiler_params=_cp("parallel", "arbitrary"))(up, up, up, up, up, up, dact, dact, dact, cw, cw)


_CB_BLK, _CC_BLK, _CX_BLK = 5, 6, 7


def _convmix_fwd(p, w):
    T = p.shape[0]
    tm = min(T, _ROW_TILE)
    n = T // tm

    def body(cb_ref, ccp, ccc, ccn, cxp, cxc, cxn, w_ref, o_ref):
        i = pl.program_id(0)
        z = _ext(ccp, ccc, ccn, i, n) * _ext(cxp, cxc, cxn, i, n)
        o_ref[...] = (cb_ref[...] * _mid(_conv3(z, w_ref[...]))).astype(BF16)

    return pl.pallas_call(
        body, name="convmix_fwd", grid=(n,),
        in_specs=[pl.BlockSpec((tm, CONV_W), lambda i: (i, _CB_BLK))]
        + _halo_specs(T, tm, CONV_W, lambda i: (i, _CC_BLK)) + _halo_specs(T, tm, CONV_W, lambda i: (i, _CX_BLK))
        + [pl.BlockSpec((3, CONV_W), lambda i: (0, 0))],
        out_specs=pl.BlockSpec((tm, CONV_W), lambda i: (i, 0)),
        out_shape=_sds((T, CONV_W), BF16), compiler_params=_cp("parallel"))(p, p, p, p, p, p, p, w)


def _convmix_bwd(p, dmix, w):
    T = p.shape[0]
    tm = min(T, _ROW_TILE)
    n = T // tm
    dblk = (ATTN_W + SG_W) // CONV_W

    def body(cbp, cbc, cbn, ccp, ccc, ccn, cxp, cxc, cxn, dp_, dc, dn_, w_ref, o_ref, dw_ref):
        i = pl.program_id(0)
        wv = w_ref[...]
        ecb = _ext(cbp, cbc, cbn, i, n)
        ecc = _ext(ccp, ccc, ccn, i, n)
        ecx = _ext(cxp, cxc, cxn, i, n)
        ed = _ext(dp_, dc, dn_, i, n)
        z = ecc * ecx
        d_cz = ed * ecb
        d_z = _conv3_t(d_cz, wv)
        o_ref[...] = jnp.concatenate([_mid(ed * _conv3(z, wv)), _mid(d_z * ecx), _mid(d_z * ecc)],
                                     axis=1).astype(BF16)
        _accum(dw_ref, i, _conv3_wgrad(d_cz, z))

    return pl.pallas_call(
        body, name="convmix_bwd", grid=(n,),
        in_specs=_halo_specs(T, tm, CONV_W, lambda i: (i, _CB_BLK)) + _halo_specs(T, tm, CONV_W, lambda i: (i, _CC_BLK))
        + _halo_specs(T, tm, CONV_W, lambda i: (i, _CX_BLK)) + _halo_specs(T, tm, CONV_W, lambda i: (i, dblk))
        + [pl.BlockSpec((3, CONV_W), lambda i: (0, 0))],
        out_specs=[pl.BlockSpec((tm, 3 * CONV_W), lambda i: (i, 0)), pl.BlockSpec((3, CONV_W), lambda i: (0, 0))],
        out_shape=[_sds((T, 3 * CONV_W), BF16), _sds((3, CONV_W), F32)],
        compiler_params=_cp("arbitrary"))(p, p, p, p, p, p, p, p, p, dmix, dmix, dmix, w)


_SU_BLK, _SV_BLK = 3, 4


def _sg_mixed(vnb, w_ref, bias, ch, pr, lo):
    vp = vnb[ch * SG_CHUNK:(ch + 1) * SG_CHUNK, pr * LANES:(pr + 1) * LANES]
    zero = jnp.zeros_like(vp)
    return (_dot(w_ref[2 * pr], jnp.where(lo, vp, zero)) + _dot(w_ref[2 * pr + 1], jnp.where(lo, zero, vp))
            + bias[:, pr * LANES:(pr + 1) * LANES]), vp


def _sg_fwd(p, g, w, bias):
    T = p.shape[0]
    tm = min(T, _ROW_TILE)

    def body(su_ref, sv_ref, g_ref, w_ref, b_ref, o_ref):
        lo = lax.broadcasted_iota(jnp.int32, (SG_CHUNK, LANES), 1) < HEAD_DIM
        sv = sv_ref[...]
        r = lax.rsqrt(jnp.mean(sv * sv, axis=-1, keepdims=True) + EPS)
        vnb = ((sv * r) * g_ref[...]).astype(BF16)
        bias_v = b_ref[...]
        for ch in range(tm // SG_CHUNK):
            for pr in range(2):
                mixed, _ = _sg_mixed(vnb, w_ref, bias_v, ch, pr, lo)
                rows, cols = pl.ds(ch * SG_CHUNK, SG_CHUNK), pl.ds(pr * LANES, LANES)
                o_ref[rows, cols] = (su_ref[rows, cols] * mixed).astype(BF16)

    return pl.pallas_call(
        body, name="sg_fwd", grid=(T // tm,),
        in_specs=[pl.BlockSpec((tm, SG_W), lambda i: (i, _SU_BLK)), pl.BlockSpec((tm, SG_W), lambda i: (i, _SV_BLK)),
                  pl.BlockSpec((1, SG_W), lambda i: (0, 0)), pl.BlockSpec((4, SG_CHUNK, SG_CHUNK), lambda i: (0, 0, 0)),
                  pl.BlockSpec((SG_CHUNK, SG_W), lambda i: (0, 0))],
        out_specs=pl.BlockSpec((tm, SG_W), lambda i: (i, 0)),
        out_shape=_sds((T, SG_W), BF16), compiler_params=_cp("parallel"))(p, p, g, w, bias)


def _sg_bwd(p, dmix, g, w, wt, bias):
    T = p.shape[0]
    tm = min(T, _ROW_TILE)
    dblk = ATTN_W // SG_W

    def body(su_ref, sv_ref, d_ref, g_ref, w_ref, wt_ref, b_ref, o_ref, dw_ref, db_ref, dg_ref, dvn_ref):
        i = pl.program_id(0)
        lo = lax.broadcasted_iota(jnp.int32, (SG_CHUNK, LANES), 1) < HEAD_DIM
        sv = sv_ref[...]
        gv = g_ref[...]
        r = lax.rsqrt(jnp.mean(sv * sv, axis=-1, keepdims=True) + EPS)
        xh = sv * r
        vnb = (xh * gv).astype(BF16)
        bias_v = b_ref[...]
        dw = [jnp.zeros((SG_CHUNK, SG_CHUNK), F32) for _ in range(4)]
        db = jnp.zeros((SG_CHUNK, SG_W), F32)
        for ch in range(tm // SG_CHUNK):
            dbs = []
            for pr in range(2):
                mixed, vp = _sg_mixed(vnb, w_ref, bias_v, ch, pr, lo)
                rows, cols = pl.ds(ch * SG_CHUNK, SG_CHUNK), pl.ds(pr * LANES, LANES)
                dgo = d_ref[rows, cols]
                o_ref[rows, cols] = (dgo * mixed).astype(BF16)
                dm = dgo * su_ref[rows, cols]
                dmb = dm.astype(BF16)
                zero = jnp.zeros_like(dmb)
                dw[2 * pr] += _dot_nt(jnp.where(lo, dmb, zero), vp)
                dw[2 * pr + 1] += _dot_nt(jnp.where(lo, zero, dmb), vp)
                dvn_ref[rows, cols] = jnp.where(lo, _dot(wt_ref[2 * pr], dmb), _dot(wt_ref[2 * pr + 1], dmb))
                dbs.append(dm)
            db += jnp.concatenate(dbs, axis=1)
        dvn = dvn_ref[...]
        dxh = dvn * gv
        o_ref[:, pl.ds(SG_W, SG_W)] = (r * (dxh - xh * jnp.mean(dxh * xh, axis=-1, keepdims=True))).astype(BF16)
        _accum(dw_ref, i, jnp.stack(dw, axis=0))
        _accum(db_ref, i, db)
        _accum(dg_ref, i, jnp.sum(dvn * xh, axis=0, keepdims=True))

    wspec = pl.BlockSpec((4, SG_CHUNK, SG_CHUNK), lambda i: (0, 0, 0))
    return pl.pallas_call(
        body, name="sg_bwd", grid=(T // tm,),
        in_specs=[pl.BlockSpec((tm, SG_W), lambda i: (i, _SU_BLK)), pl.BlockSpec((tm, SG_W), lambda i: (i, _SV_BLK)),
                  pl.BlockSpec((tm, SG_W), lambda i: (i, dblk)), pl.BlockSpec((1, SG_W), lambda i: (0, 0)),
                  wspec, wspec, pl.BlockSpec((SG_CHUNK, SG_W), lambda i: (0, 0))],
        out_specs=[pl.BlockSpec((tm, 2 * SG_W), lambda i: (i, 0)), wspec,
                   pl.BlockSpec((SG_CHUNK, SG_W), lambda i: (0, 0)), pl.BlockSpec((1, SG_W), lambda i: (0, 0))],
        out_shape=[_sds((T, 2 * SG_W), BF16), _sds((4, SG_CHUNK, SG_CHUNK), F32),
                   _sds((SG_CHUNK, SG_W), F32), _sds((1, SG_W), F32)],
        scratch_shapes=[pltpu.VMEM((tm, SG_W), F32)],
        compiler_params=_cp("arbitrary"))(p, p, dmix, g, w, wt, bias)


_ATTN_TQ = 256
_ATTN_TK = 512


def _head_mean(v, bmat):
    return jnp.dot(v, bmat, preferred_element_type=F32, precision=lax.Precision.HIGHEST)


def _swap16(y):
    lane = lax.broadcasted_iota(jnp.int32, y.shape, 1)
    return jnp.where(lane % 32 < 16, pltpu.roll(y, y.shape[1] - 16, 1), pltpu.roll(y, 16, 1))


def _rope(y, cos, sin):
    return y * cos + _swap16(y) * sin


def _rope_t(dy, cos, sin):
    return dy * cos + _swap16(dy * sin)


def _dup_rows(t, gidx):
    h = t[gidx * HEAD_DIM:(gidx + 1) * HEAD_DIM]
    return jnp.concatenate([h, h], axis=0)


def _qk_prep(p, gq, gk, cos, sin, bmat, tk):
    T = p.shape[0]
    nk = T // tk
    scale = HEAD_DIM ** -0.5

    def body(q_ref, kv_ref, gq_ref, gk_ref, cos_ref, sin_ref, b_ref, qo_ref, kt_ref, kd_ref, vt_ref, vd_ref):
        cosv, sinv, bm = cos_ref[...], sin_ref[...], b_ref[...]
        for pr in range(ATTN_W // LANES):
            cols = pl.ds(pr * LANES, LANES)
            xq = q_ref[:, cols]
            r = lax.rsqrt(_head_mean(xq * xq, bm) + EPS)
            qo_ref[:, cols] = (_rope((xq * r) * gq_ref[:, cols], cosv, sinv) * scale).astype(BF16)
        xk = kv_ref[:, pl.ds(0, LANES)]
        r = lax.rsqrt(_head_mean(xk * xk, bm) + EPS)
        kt = _rope((xk * r) * gk_ref[...], cosv, sinv).T
        vt = kv_ref[:, pl.ds(LANES, LANES)].T
        for gidx in range(2):
            kdup = _dup_rows(kt, gidx)
            vdup = _dup_rows(vt, gidx)
            kt_ref[gidx] = kdup.astype(BF16)
            vt_ref[gidx] = vdup.astype(BF16)
            kd_ref[gidx] = kdup.T.astype(BF16)
            vd_ref[gidx] = vdup.T.astype(BF16)

    tspec = pl.BlockSpec((2, None, LANES, tk), lambda i: (0, i, 0, 0))
    dspec = pl.BlockSpec((2, tk, LANES), lambda i: (0, i, 0))
    tab = pl.BlockSpec((tk, LANES), lambda i: (i, 0))
    return pl.pallas_call(
        body, name="qk_prep", grid=(nk,),
        in_specs=[pl.BlockSpec((tk, ATTN_W), lambda i: (i, 0)), pl.BlockSpec((tk, 2 * KV_W), lambda i: (i, ATTN_W // (2 * KV_W))),
                  pl.BlockSpec((1, ATTN_W), lambda i: (0, 0)), pl.BlockSpec((1, KV_W), lambda i: (0, 0)),
                  tab, tab, pl.BlockSpec((LANES, LANES), lambda i: (0, 0))],
        out_specs=[pl.BlockSpec((tk, ATTN_W), lambda i: (i, 0)), tspec, dspec, tspec, dspec],
        out_shape=[_sds((T, ATTN_W), BF16), _sds((2, nk, LANES, tk), BF16), _sds((2, T, LANES), BF16),
                   _sds((2, nk, LANES, tk), BF16), _sds((2, T, LANES), BF16)],
        compiler_params=_cp("parallel"))(p, p, gq, gk, cos, sin, bmat)


def _stack_heads(t):
    lo = lax.broadcasted_iota(jnp.int32, (t.shape[0], LANES), 1) < HEAD_DIM
    parts = []
    for pr in range(2):
        tp = t[:, pr * LANES:(pr + 1) * LANES]
        zero = jnp.zeros_like(tp)
        parts += [jnp.where(lo, tp, zero), jnp.where(lo, zero, tp)]
    return jnp.concatenate(parts, axis=0)


def _unstack_heads(s, tq):
    lo = lax.broadcasted_iota(jnp.int32, (tq, LANES), 1) < HEAD_DIM
    return jnp.concatenate([jnp.where(lo, s[0:tq], s[tq:2 * tq]),
                            jnp.where(lo, s[2 * tq:3 * tq], s[3 * tq:4 * tq])], axis=1)


def _attn_fwd(q, kt, vd, tq):
    T = q.shape[0]
    nk, tk = kt.shape[1], kt.shape[3]
    nq = T // tq
    sq = 4 * tq

    def body(q_ref, kt_ref, vd_ref, o_ref, lse_ref):
        qs = _stack_heads(q_ref[...])

        def step(j, carry):
            m, l, acc = carry
            s = _dot(qs, kt_ref[j])
            m_new = jnp.maximum(m, jnp.max(s, axis=-1, keepdims=True))
            alpha = jnp.exp(m - m_new)
            pexp = jnp.exp(s - m_new)
            l = alpha * l + jnp.sum(pexp, axis=-1, keepdims=True)
            acc = alpha * acc + _dot(pexp.astype(BF16), vd_ref[pl.ds(pl.multiple_of(j * tk, tk), tk), :])
            return m_new, l, acc

        m, l, acc = lax.fori_loop(0, nk, step, (jnp.full((sq, 1), -jnp.inf, F32), jnp.zeros((sq, 1), F32),
                                                jnp.zeros((sq, LANES), F32)))
        o_ref[...] = _unstack_heads(acc / l, tq).astype(BF16)
        lse_ref[...] = jnp.broadcast_to(m + jnp.log(l), (sq, LANES))

    return pl.pallas_call(
        body, name="attn_fwd", grid=(2, nq),
        in_specs=[pl.BlockSpec((tq, 2 * LANES), lambda g, i: (i, g)),
                  pl.BlockSpec((None, nk, LANES, tk), lambda g, i: (g, 0, 0, 0)),
                  pl.BlockSpec((None, T, LANES), lambda g, i: (g, 0, 0))],
        out_specs=[pl.BlockSpec((tq, 2 * LANES), lambda g, i: (i, g)),
                   pl.BlockSpec((None, None, sq, LANES), lambda g, i: (g, i, 0, 0))],
        out_shape=[_sds((T, ATTN_W), BF16), _sds((2, nq, sq, LANES), F32)],
        compiler_params=_cp("parallel", "parallel"))(q, kt, vd)


def _attn_bwd(q, o, dmix, lse, kt, kd, vt, tq):
    T = q.shape[0]
    nk, tk = kt.shape[1], kt.shape[3]
    nq = T // tq
    sq = 4 * tq
    rep = tk // LANES

    def body(q_ref, o_ref, do_ref, lse_ref, kt_ref, kd_ref, vt_ref, dq_ref, dkt_ref, dvt_ref):
        i = pl.program_id(1)
        qs = _stack_heads(q_ref[...])
        dof = _stack_heads(do_ref[...])
        dos = dof.astype(BF16)
        qst = qs.astype(F32).T.astype(BF16)
        dost = dof.T.astype(BF16)
        o_pair = o_ref[...].astype(F32)
        os_ = jnp.concatenate([o_pair[:, 0:LANES], o_pair[:, 0:LANES], o_pair[:, LANES:], o_pair[:, LANES:]], axis=0)
        delta = jnp.sum(dof * os_, axis=-1, keepdims=True)
        lse_t = jnp.concatenate([lse_ref[...]] * rep, axis=1)

        @pl.when(i == 0)
        def _():
            dkt_ref[...] = jnp.zeros_like(dkt_ref)
            dvt_ref[...] = jnp.zeros_like(dvt_ref)

        def step(j, dq):
            kdb = kd_ref[pl.ds(pl.multiple_of(j * tk, tk), tk), :]
            pexp = jnp.exp(_dot(qs, kt_ref[j]) - lse_t)
            ds = pexp * (_dot(dos, vt_ref[j]) - delta)
            pb = pexp.astype(BF16)
            dsb = ds.astype(BF16)
            dvt_ref[j] += _dot(dost, pb)
            dkt_ref[j] += _dot(qst, dsb)
            return dq + _dot(dsb, kdb)

        dq = lax.fori_loop(0, nk, step, jnp.zeros((sq, LANES), F32))
        dq_ref[...] = _unstack_heads(dq, tq)

    tspec = pl.BlockSpec((None, nk, LANES, tk), lambda g, i: (g, 0, 0, 0))
    qspec = pl.BlockSpec((tq, 2 * LANES), lambda g, i: (i, g))
    return pl.pallas_call(
        body, name="attn_bwd", grid=(2, nq),
        in_specs=[qspec, qspec, qspec, pl.BlockSpec((None, None, sq, LANES), lambda g, i: (g, i, 0, 0)),
                  tspec, pl.BlockSpec((None, T, LANES), lambda g, i: (g, 0, 0)), tspec],
        out_specs=[qspec, tspec, tspec],
        out_shape=[_sds((T, ATTN_W), F32), _sds((2, nk, LANES, tk), F32), _sds((2, nk, LANES, tk), F32)],
        compiler_params=_cp("parallel", "arbitrary"))(q, o, dmix, lse, kt, kd, vt)


def _fold_t(t_ref):
    rows = []
    for gidx in range(2):
        t = t_ref[gidx]
        rows.append(t[0:HEAD_DIM] + t[HEAD_DIM:2 * HEAD_DIM])
    return jnp.concatenate(rows, axis=0).T


def _qk_bwd(p, dq, dkt, dvt, gq, gk, cos, sin, bmat):
    T = p.shape[0]
    nk, tk = dkt.shape[1], dkt.shape[3]
    scale = HEAD_DIM ** -0.5

    def norm_bwd(x, dy, gain, bm):
        r = lax.rsqrt(_head_mean(x * x, bm) + EPS)
        xh = x * r
        dxh = dy * gain
        return r * (dxh - xh * _head_mean(dxh * xh, bm)), jnp.sum(dy * xh, axis=0, keepdims=True)

    def body(q_ref, kv_ref, dq_ref, dkt_ref, dvt_ref, gq_ref, gk_ref, cos_ref, sin_ref, b_ref, o_ref, dgq_ref, dgk_ref):
        i = pl.program_id(0)
        cosv, sinv, bm = cos_ref[...], sin_ref[...], b_ref[...]
        dgq = []
        for pr in range(ATTN_W // LANES):
            cols = pl.ds(pr * LANES, LANES)
            dy = _rope_t(dq_ref[:, cols] * scale, cosv, sinv)
            dx, dg = norm_bwd(q_ref[:, cols], dy, gq_ref[:, cols], bm)
            o_ref[:, cols] = dx.astype(BF16)
            dgq.append(dg)
        dy = _rope_t(_fold_t(dkt_ref), cosv, sinv)
        dx, dgk = norm_bwd(kv_ref[:, pl.ds(0, LANES)], dy, gk_ref[...], bm)
        o_ref[:, pl.ds(ATTN_W, LANES)] = dx.astype(BF16)
        o_ref[:, pl.ds(ATTN_W + LANES, LANES)] = _fold_t(dvt_ref).astype(BF16)
        _accum(dgq_ref, i, jnp.concatenate(dgq, axis=1))
        _accum(dgk_ref, i, dgk)

    tspec = pl.BlockSpec((2, None, LANES, tk), lambda i: (0, i, 0, 0))
    tab = pl.BlockSpec((tk, LANES), lambda i: (i, 0))
    return pl.pallas_call(
        body, name="qk_bwd", grid=(nk,),
        in_specs=[pl.BlockSpec((tk, ATTN_W), lambda i: (i, 0)), pl.BlockSpec((tk, 2 * KV_W), lambda i: (i, ATTN_W // (2 * KV_W))),
                  pl.BlockSpec((tk, ATTN_W), lambda i: (i, 0)), tspec, tspec,
                  pl.BlockSpec((1, ATTN_W), lambda i: (0, 0)), pl.BlockSpec((1, KV_W), lambda i: (0, 0)),
                  tab, tab, pl.BlockSpec((LANES, LANES), lambda i: (0, 0))],
        out_specs=[pl.BlockSpec((tk, ATTN_W + 2 * KV_W), lambda i: (i, 0)),
                   pl.BlockSpec((1, ATTN_W), lambda i: (0, 0)), pl.BlockSpec((1, KV_W), lambda i: (0, 0))],
        out_shape=[_sds((T, ATTN_W + 2 * KV_W), BF16), _sds((1, ATTN_W), F32), _sds((1, KV_W), F32)],
        compiler_params=_cp("arbitrary"))(p, p, dq, dkt, dvt, gq, gk, cos, sin, bmat)


def _loss_head(y, target):
    T, Dm = y.shape
    tm = min(T, _ROW_TILE)

    def body(y_ref, t_ref, dy_ref, l_ref):
        i = pl.program_id(0)
        err = y_ref[...] - t_ref[...]
        dy_ref[...] = err * (1.0 / Dm)
        part = jnp.sum(jnp.sum(err * err, axis=-1, keepdims=True), axis=0, keepdims=True) * (0.5 / Dm)
        _accum(l_ref, i, jnp.broadcast_to(part, (8, LANES)))

    row = pl.BlockSpec((tm, Dm), lambda i: (i, 0))
    return pl.pallas_call(
        body, name="loss_head", grid=(T // tm,), in_specs=[row, row],
        out_specs=[row, pl.BlockSpec((8, LANES), lambda i: (0, 0))],
        out_shape=[_sds((T, Dm), F32), _sds((8, LANES), F32)], compiler_params=_cp("arbitrary"))(y, target)


def _adamw(w, g, m, v, name):
    R, C = w.shape
    tr = R
    for cand in (512, 256, 128, 64, 32, 16, 8):
        if R % cand == 0:
            tr = cand
            break
    c1 = 1.0 - ADAM_B1 ** ADAM_STEP
    c2 = 1.0 - ADAM_B2 ** ADAM_STEP

    def body(w_ref, g_ref, m_ref, v_ref, d_ref, mo_ref, vo_ref):
        gv = g_ref[...]
        mn = ADAM_B1 * m_ref[...] + (1.0 - ADAM_B1) * gv
        vn = ADAM_B2 * v_ref[...] + (1.0 - ADAM_B2) * (gv * gv)
        d_ref[...] = -ADAM_LR * ((mn / c1) / (jnp.sqrt(vn / c2) + ADAM_EPS) + ADAM_WD * w_ref[...])
        mo_ref[...] = mn
        vo_ref[...] = vn

    blk = pl.BlockSpec((tr, C), lambda i: (i, 0))
    return pl.pallas_call(
        body, name=name, grid=(R // tr,), in_specs=[blk] * 4, out_specs=[blk] * 3,
        out_shape=[_sds((R, C), F32)] * 3, compiler_params=_cp("parallel"))(w, g, m, v)


def _cast_bf16(w, name):
    R, C = w.shape
    tr = 512 if R % 512 == 0 else 256

    def body(w_ref, o_ref):
        o_ref[...] = w_ref[...].astype(BF16)

    blk = pl.BlockSpec((tr, C), lambda i: (i, 0))
    return pl.pallas_call(body, name=name, grid=(R // tr,), in_specs=[blk], out_specs=blk,
                          out_shape=_sds((R, C), BF16), compiler_params=_cp("parallel"))(w)


def _add_halves(g, recv, name):
    S, R, C = g.shape
    half = R // 2

    def body(g_ref, r_ref, o_ref):
        c = lax.axis_index("c")
        o_ref[...] = g_ref[pl.ds(pl.multiple_of(c * half, 8), half), :] + r_ref[...]

    return pl.pallas_call(
        body, name=name, grid=(S,),
        in_specs=[pl.BlockSpec((None, R, C), lambda s: (s, 0, 0)), pl.BlockSpec((None, half, C), lambda s: (s, 0, 0))],
        out_specs=pl.BlockSpec((None, half, C), lambda s: (s, 0, 0)),
        out_shape=_sds((S, half, C), F32), compiler_params=_cp("parallel"))(g, recv)


def _sum_chips(parts, name):
    S, R, C = parts.shape
    tr = R
    for cand in (256, 128, 64, 32, 16, 8):
        if R % cand == 0:
            tr = cand
            break

    def body(p_ref, o_ref):
        o_ref[...] = ((p_ref[0] + p_ref[1]) + p_ref[2]) + p_ref[3]

    return pl.pallas_call(
        body, name=name, grid=(R // tr,),
        in_specs=[pl.BlockSpec((S, tr, C), lambda i: (0, i, 0))], out_specs=pl.BlockSpec((tr, C), lambda i: (i, 0)),
        out_shape=_sds((R, C), F32), compiler_params=_cp("parallel"))(parts)


def _position():
    x, y, c = lax.axis_index("x"), lax.axis_index("y"), lax.axis_index("c")
    return x, y, c


def _other_chips(x, y):
    return [(1 - x, y), (x, 1 - y), (1 - x, 1 - y)]


def _gather_shards(shards):
    n = len(shards)

    def body(*refs):
        ins, outs = refs[:n], refs[n:2 * n]
        send_sems, recv_sems, local_sems = refs[2 * n:]
        x, y, c = _position()
        me = 2 * x + y
        chips = _other_chips(x, y)
        copies = []
        for t in range(n):
            own = pltpu.make_async_copy(ins[t], outs[t].at[:, me], local_sems.at[t])
            own.start()
            copies.append(own)
        sends = []
        for t in range(n):
            for k, (px, py) in enumerate(chips):
                cp = pltpu.make_async_remote_copy(
                    src_ref=ins[t], dst_ref=outs[t].at[:, me], send_sem=send_sems.at[t, k], recv_sem=recv_sems.at[t, k],
                    device_id=(px, py, c), device_id_type=MESH)
                cp.start()
                sends.append(cp)
        for t in range(n):
            for k, (px, py) in enumerate(chips):
                pltpu.make_async_remote_copy(
                    src_ref=ins[t], dst_ref=outs[t].at[:, 2 * px + py], send_sem=send_sems.at[t, k],
                    recv_sem=recv_sems.at[t, k], device_id=(px, py, c), device_id_type=MESH).wait_recv()
        for cp in sends:
            cp.wait_send()
        for cp in copies:
            cp.wait()

    return pl.pallas_call(
        body, name="gather_weights",
        in_specs=[ANY] * n, out_specs=[ANY] * n,
        out_shape=[_sds((s.shape[0], N_CHIPS) + s.shape[1:], s.dtype) for s in shards],
        scratch_shapes=[pltpu.SemaphoreType.DMA((n, 3)), pltpu.SemaphoreType.DMA((n, 3)), pltpu.SemaphoreType.DMA((n,))],
        compiler_params=pltpu.CompilerParams(has_side_effects=True))(*shards)


def _send_to_sibling_halves(grads):
    n = len(grads)

    def body(*refs):
        ins, outs = refs[:n], refs[n:2 * n]
        send_sems, recv_sems = refs[2 * n:]
        x, y, c = _position()
        sends = []
        for t in range(n):
            half = ins[t].shape[1] // 2
            src = ins[t].at[:, pl.ds(pl.multiple_of((1 - c) * half, 8), half), :]
            cp = pltpu.make_async_remote_copy(src_ref=src, dst_ref=outs[t], send_sem=send_sems.at[t],
                                              recv_sem=recv_sems.at[t], device_id=(x, y, 1 - c), device_id_type=MESH)
            cp.start()
            sends.append(cp)
        for cp in sends:
            cp.wait()

    return pl.pallas_call(
        body, name="grads_to_sibling",
        in_specs=[ANY] * n, out_specs=[ANY] * n,
        out_shape=[_sds((g.shape[0], g.shape[1] // 2, g.shape[2]), g.dtype) for g in grads],
        scratch_shapes=[pltpu.SemaphoreType.DMA((n,)), pltpu.SemaphoreType.DMA((n,))],
        compiler_params=pltpu.CompilerParams(has_side_effects=True))(*grads)


def _scatter_to_chips(sums):
    n = len(sums)

    def body(*refs):
        ins, outs = refs[:n], refs[n:2 * n]
        send_sems, recv_sems, local_sems = refs[2 * n:]
        x, y, c = _position()
        me = 2 * x + y
        chips = _other_chips(x, y)
        copies = []
        for t in range(n):
            own = pltpu.make_async_copy(ins[t].at[me], outs[t].at[me], local_sems.at[t])
            own.start()
            copies.append(own)
        sends = []
        for t in range(n):
            for k, (px, py) in enumerate(chips):
                cp = pltpu.make_async_remote_copy(
                    src_ref=ins[t].at[2 * px + py], dst_ref=outs[t].at[me], send_sem=send_sems.at[t, k],
                    recv_sem=recv_sems.at[t, k], device_id=(px, py, c), device_id_type=MESH)
                cp.start()
                sends.append(cp)
        for t in range(n):
            for k, (px, py) in enumerate(chips):
                pltpu.make_async_remote_copy(
                    src_ref=ins[t].at[me], dst_ref=outs[t].at[2 * px + py], send_sem=send_sems.at[t, k],
                    recv_sem=recv_sems.at[t, k], device_id=(px, py, c), device_id_type=MESH).wait_recv()
        for cp in sends:
            cp.wait_send()
        for cp in copies:
            cp.wait()

    return pl.pallas_call(
        body, name="grads_to_chips",
        in_specs=[ANY] * n, out_specs=[ANY] * n, out_shape=[_sds(s.shape, s.dtype) for s in sums],
        scratch_shapes=[pltpu.SemaphoreType.DMA((n, 3)), pltpu.SemaphoreType.DMA((n, 3)), pltpu.SemaphoreType.DMA((n,))],
        compiler_params=pltpu.CompilerParams(has_side_effects=True))(*sums)


def _join_halves(halves):
    n = len(halves)

    def body(*refs):
        ins, outs = refs[:n], refs[n:2 * n]
        send_sems, recv_sems, local_sems = refs[2 * n:]
        x, y, c = _position()
        ops = []
        for t in range(n):
            half = ins[t].shape[0]
            mine = pl.ds(pl.multiple_of(c * half, 8), half)
            own = pltpu.make_async_copy(ins[t], outs[t].at[mine, :], local_sems.at[t])
            own.start()
            cp = pltpu.make_async_remote_copy(src_ref=ins[t], dst_ref=outs[t].at[mine, :], send_sem=send_sems.at[t],
                                              recv_sem=recv_sems.at[t], device_id=(x, y, 1 - c), device_id_type=MESH)
            cp.start()
            ops += [own, cp]
        for op in ops:
            op.wait()

    return pl.pallas_call(
        body, name="grads_join_halves",
        in_specs=[ANY] * n, out_specs=[ANY] * n,
        out_shape=[_sds((2 * h.shape[0], h.shape[1]), h.dtype) for h in halves],
        scratch_shapes=[pltpu.SemaphoreType.DMA((n,)), pltpu.SemaphoreType.DMA((n,)), pltpu.SemaphoreType.DMA((n,))],
        compiler_params=pltpu.CompilerParams(has_side_effects=True))(*halves)


def _allreduce_small(packed):
    R, C = packed.shape
    ndev = 2 * N_CHIPS

    def body(in_ref, out_ref, buf, send_sems, recv_sems):
        x, y, c = _position()
        me = 4 * x + 2 * y + c
        buf[me] = in_ref[...]
        sends = []
        for k in range(1, ndev):
            peer = (x ^ (k >> 2), y ^ ((k >> 1) & 1), c ^ (k & 1))
            cp = pltpu.make_async_remote_copy(src_ref=in_ref, dst_ref=buf.at[me], send_sem=send_sems.at[k - 1],
                                              recv_sem=recv_sems.at[k - 1], device_id=peer, device_id_type=MESH)
            cp.start()
            sends.append(cp)
        for k in range(1, ndev):
            peer = (x ^ (k >> 2), y ^ ((k >> 1) & 1), c ^ (k & 1))
            pltpu.make_async_remote_copy(src_ref=in_ref, dst_ref=buf.at[me ^ k], send_sem=send_sems.at[k - 1],
                                         recv_sem=recv_sems.at[k - 1], device_id=peer, device_id_type=MESH).wait_recv()
        for cp in sends:
            cp.wait_send()
        acc = buf[0]
        for d in range(1, ndev):
            acc = acc + buf[d]
        out_ref[...] = acc

    return pl.pallas_call(
        body, name="allreduce_small",
        in_specs=[pl.BlockSpec(memory_space=pltpu.VMEM)], out_specs=pl.BlockSpec(memory_space=pltpu.VMEM),
        out_shape=_sds((R, C), F32),
        scratch_shapes=[pltpu.VMEM((ndev, R, C), F32), pltpu.SemaphoreType.DMA((ndev - 1,)),
                        pltpu.SemaphoreType.DMA((ndev - 1,))],
        compiler_params=pltpu.CompilerParams(vmem_limit_bytes=VMEM_LIMIT_BYTES, has_side_effects=True))(packed)


def _rope_tables(T):
    pos = jnp.arange(T)
    row = (pos // GRID_W).astype(F32)
    col = (pos % GRID_W).astype(F32)
    inv = 1.0 / (ROPE_THETA ** (jnp.arange(AXIS_DIM // 2, dtype=F32) * 2.0 / AXIS_DIM))
    ar, ac = row[:, None] * inv[None, :], col[:, None] * inv[None, :]
    cos = jnp.concatenate([jnp.cos(ar), jnp.cos(ar), jnp.cos(ac), jnp.cos(ac)], axis=-1)
    sin = jnp.concatenate([-jnp.sin(ar), jnp.sin(ar), -jnp.sin(ac), jnp.sin(ac)], axis=-1)
    return jnp.tile(cos, (1, LANES // HEAD_DIM)), jnp.tile(sin, (1, LANES // HEAD_DIM))


def _head_mean_matrix():
    h = jnp.arange(LANES) // HEAD_DIM
    return jnp.where(h[:, None] == h[None, :], 1.0 / HEAD_DIM, 0.0).astype(F32)


def _pack(arrays):
    flat = jnp.concatenate([a.reshape(-1) for a in arrays])
    rows = -(-flat.shape[0] // LANES)
    rows = -(-rows // 256) * 256
    return jnp.pad(flat, (0, rows * LANES - flat.shape[0])).reshape(rows, LANES)


def _unpack(packed, like):
    flat = packed.reshape(-1)
    out, off = [], 0
    for a in like:
        out.append(flat[off:off + a.size].reshape(a.shape))
        off += a.size
    return out


def _layer_fwd(x, lw, consts):
    cos, sin, bmat = consts
    T = x.shape[0]
    tk = min(T, _ATTN_TK)
    tq = min(T, _ATTN_TQ)
    h = _norm_fwd(x, lw["norm1_g"])
    p = _mm_nn(h, lw["w_in"], tn=512, out_dtype=F32, name="mm_p")
    qn, kt, kd, vt, vd = _qk_prep(p, lw["gq"], lw["gk"], cos, sin, bmat, tk)
    o, lse = _attn_fwd(qn, kt, vd, tq)
    go = _sg_fwd(p, lw["sg_norm_g"], lw["sg_w"], lw["sg_bias"])
    co = _convmix_fwd(p, lw["conv_w"])
    mix = jnp.concatenate([o, go, co], axis=1)
    x_mid = _mm_nn(mix, lw["w_out"], tn=D_MODEL, out_dtype=F32, name="mm_out", res=x)
    h2 = _norm_fwd(x_mid, lw["norm2_g"])
    up = _mm_nn(h2, lw["w_up"], tn=D_FF // 2, out_dtype=F32, name="mm_up")
    act = _ffn_act_fwd(up, lw["ffn_conv_w"])
    x_out = _mm_nn(act, lw["w_down"], tn=D_MODEL, out_dtype=F32, name="mm_down", res=x_mid)
    saved = dict(x=x, h=h, p=p, qn=qn, kt=kt, kd=kd, vt=vt, o=o, lse=lse, mix=mix, x_mid=x_mid, h2=h2, up=up, act=act)
    return x_out, saved


def _layer_bwd(dx, s, lw, consts):
    cos, sin, bmat = consts
    T = dx.shape[0]
    tq = min(T, _ATTN_TQ)
    g = {}
    d_act = _mm_nt(dx, lw["w_down"], tkw=D_FF // 2, tn=D_MODEL, name="mm_dact")
    g["w_down"] = _mm_tn(s["act"], dx, tk=D_FF // 2, tn=D_MODEL, name="mm_dwdown")
    d_up, d_cw = _ffn_act_bwd(s["up"], d_act, lw["ffn_conv_w"])
    g["ffn_conv_w"] = d_cw.transpose(1, 0, 2).reshape(3, 2 * D_FF)
    g["w_up"] = _mm_tn(s["h2"], d_up, tk=512, tn=D_FF // 2, name="mm_dwup", shards=N_CHIPS)
    d_h2 = _mm_nt(d_up, lw["w_up"], tkw=D_MODEL, tn=D_FF // 2, name="mm_dh2")
    dx2, g["norm2_g"] = _norm_bwd(s["x_mid"], d_h2, dx, lw["norm2_g"])
    d_mix = _mm_nt(dx2, lw["w_out"], tkw=D_MODEL, tn=D_MODEL, name="mm_dmix")
    g["w_out"] = _mm_tn(s["mix"], dx2, tk=512, tn=D_MODEL, name="mm_dwout")
    dp_c, g["conv_w"] = _convmix_bwd(s["p"], d_mix, lw["conv_w"])
    dp_b, g["sg_w"], d_bias, g["sg_norm_g"] = _sg_bwd(s["p"], d_mix, lw["sg_norm_g"], lw["sg_w"], lw["sg_wt"], lw["sg_bias"])
    g["sg_b"] = d_bias.reshape(SG_CHUNK, SG_W // HEAD_DIM, HEAD_DIM).sum(axis=-1).T
    dq, dkt, dvt = _attn_bwd(s["qn"], s["o"], d_mix, s["lse"], s["kt"], s["kd"], s["vt"], tq)
    dp_a, d_gq, d_gk = _qk_bwd(s["p"], dq, dkt, dvt, lw["gq"], lw["gk"], cos, sin, bmat)
    g["q_norm_g"] = d_gq.reshape(ATTN_W // HEAD_DIM, HEAD_DIM).sum(axis=0)
    g["k_norm_g"] = d_gk.reshape(KV_W // HEAD_DIM, HEAD_DIM).sum(axis=0)
    dp = jnp.concatenate([dp_a, dp_b, dp_c], axis=1)
    g["w_in"] = _mm_tn(s["h"], dp, tk=D_MODEL, tn=512, name="mm_dwin", shards=N_CHIPS)
    d_h = _mm_nt(dp, lw["w_in"], tkw=D_MODEL, tn=512, name="mm_dh")
    dx_in, g["norm1_g"] = _norm_bwd(s["x"], d_h, dx2, lw["norm1_g"])
    return dx_in, g


def _layer_weights(l, full, small):
    sg_w = small["sg_w"][l]
    sg_b = small["sg_b"][l]
    return dict(
        norm1_g=small["norm1_g"][l][None, :], norm2_g=small["norm2_g"][l][None, :],
        gq=jnp.tile(small["q_norm_g"][l], ATTN_W // HEAD_DIM)[None, :],
        gk=jnp.tile(small["k_norm_g"][l], KV_W // HEAD_DIM)[None, :],
        sg_norm_g=small["sg_norm_g"][l][None, :],
        sg_w=sg_w.astype(BF16), sg_wt=sg_w.transpose(0, 2, 1).astype(BF16),
        sg_bias=jnp.repeat(sg_b.T, HEAD_DIM, axis=1),
        conv_w=full["conv_w"][l], ffn_conv_w=full["ffn_conv_w"][l],
        w_in=full["w_in"][l], w_out=full["w_out"][l], w_up=full["w_up"][l], w_down=full["w_down"][l])


_BIG = ("w_in", "w_out", "ffn_w_up", "ffn_w_down")
_SMALL_REPL = ("norm1_g", "q_norm_g", "k_norm_g", "sg_norm_g", "sg_w", "sg_b", "norm2_g")
_SMALL_SHARD = ("conv_w", "ffn_conv_w")
_ORDER = ("norm1_g", "w_in", "q_norm_g", "k_norm_g", "sg_norm_g", "sg_w", "sg_b", "conv_w", "w_out", "norm2_g",
          "ffn_w_up", "ffn_conv_w", "ffn_w_down")


def _reduce_big_grads(layer_grads):
    recv = _send_to_sibling_halves(layer_grads)
    sums = [_add_halves(gr, rc, "add_sibling") for gr, rc in zip(layer_grads, recv)]
    parts = _scatter_to_chips(sums)
    halves = [_sum_chips(pt, "sum_chips") for pt in parts]
    return _join_halves(halves)


def kernel(x, norm1_g, w_in, q_norm_g, k_norm_g, sg_norm_g, sg_w, sg_b, conv_w, w_out, norm2_g, ffn_w_up, ffn_conv_w, ffn_w_down, loss_target, m_norm1_g, m_w_in, m_q_norm_g, m_k_norm_g, m_sg_norm_g, m_sg_w, m_sg_b, m_conv_w, m_w_out, m_norm2_g, m_ffn_w_up, m_ffn_conv_w, m_ffn_w_down, v_norm1_g, v_w_in, v_q_norm_g, v_k_norm_g, v_sg_norm_g, v_sg_w, v_sg_b, v_conv_w, v_w_out, v_norm2_g, v_ffn_w_up, v_ffn_conv_w, v_ffn_w_down):
    w = dict(norm1_g=norm1_g, w_in=w_in, q_norm_g=q_norm_g, k_norm_g=k_norm_g, sg_norm_g=sg_norm_g, sg_w=sg_w,
             sg_b=sg_b, conv_w=conv_w, w_out=w_out, norm2_g=norm2_g, ffn_w_up=ffn_w_up, ffn_conv_w=ffn_conv_w,
             ffn_w_down=ffn_w_down)
    mom = dict(norm1_g=m_norm1_g, w_in=m_w_in, q_norm_g=m_q_norm_g, k_norm_g=m_k_norm_g, sg_norm_g=m_sg_norm_g,
               sg_w=m_sg_w, sg_b=m_sg_b, conv_w=m_conv_w, w_out=m_w_out, norm2_g=m_norm2_g, ffn_w_up=m_ffn_w_up,
               ffn_conv_w=m_ffn_conv_w, ffn_w_down=m_ffn_w_down)
    var = dict(norm1_g=v_norm1_g, w_in=v_w_in, q_norm_g=v_q_norm_g, k_norm_g=v_k_norm_g, sg_norm_g=v_sg_norm_g,
               sg_w=v_sg_w, sg_b=v_sg_b, conv_w=v_conv_w, w_out=v_w_out, norm2_g=v_norm2_g, ffn_w_up=v_ffn_w_up,
               ffn_conv_w=v_ffn_conv_w, ffn_w_down=v_ffn_w_down)
    L = DEPTH
    T = x.shape[1]
    xs = x.reshape(T, D_MODEL)
    target = loss_target.reshape(T, D_MODEL)

    shards = [_cast_bf16(w[n].reshape(-1, w[n].shape[-1]), "cast_" + n).reshape(w[n].shape) for n in _BIG]
    gathered = _gather_shards(shards + [conv_w, ffn_conv_w])
    gw_in, gw_out, gw_up, gw_down, g_conv, g_fconv = gathered
    full = dict(
        w_in=gw_in,
        w_out=gw_out.reshape(L, D_MODEL, D_MODEL),
        w_up=gw_up,
        w_down=gw_down.reshape(L, D_FF, D_MODEL),
        conv_w=g_conv.transpose(0, 2, 1, 3).reshape(L, 3, CONV_W),
        ffn_conv_w=g_fconv.transpose(0, 2, 1, 3).reshape(L, 3, 2 * D_FF))
    consts = _rope_tables(T) + (_head_mean_matrix(),)

    saved, lws = [], []
    act_x = xs
    for l in range(L):
        lw = _layer_weights(l, full, w)
        act_x, s = _layer_fwd(act_x, lw, consts)
        saved.append(s)
        lws.append(lw)
    dx, loss_blk = _loss_head(act_x, target)
    loss = lax.psum(loss_blk[0, 0], ("x", "y", "c"))

    grads = [None] * L
    reduced = [None] * L
    for l in reversed(range(L)):
        dx, g = _layer_bwd(dx, saved[l], lws[l], consts)
        grads[l] = g
        reduced[l] = _reduce_big_grads([g["w_in"], g["w_out"].reshape(N_CHIPS, D_MODEL // N_CHIPS, D_MODEL), g["w_up"],
                                        g["w_down"].reshape(N_CHIPS, D_FF // N_CHIPS, D_MODEL)])
    grad_x = dx.reshape(x.shape)

    small_names = _SMALL_REPL + _SMALL_SHARD
    key = dict(norm1_g="norm1_g", q_norm_g="q_norm_g", k_norm_g="k_norm_g", sg_norm_g="sg_norm_g", sg_w="sg_w",
               sg_b="sg_b", norm2_g="norm2_g", conv_w="conv_w", ffn_conv_w="ffn_conv_w")
    small_local = [jnp.stack([grads[l][key[n]].reshape(-1) for l in range(L)]) for n in small_names]
    small_sum = _unpack(_allreduce_small(_pack(small_local)), small_local)
    grad = {}
    for n, a in zip(small_names, small_sum):
        grad[n] = a
    chip = 2 * lax.axis_index("x") + lax.axis_index("y")
    for n in _SMALL_REPL:
        grad[n] = grad[n].reshape(w[n].shape)
    for n in _SMALL_SHARD:
        full_w = grad[n].reshape(L, 3, -1)
        width = w[n].shape[-1]
        grad[n] = lax.dynamic_slice_in_dim(full_w, chip * width, width, axis=2)
    for i, n in enumerate(_BIG):
        grad[n] = jnp.stack([reduced[l][i] for l in range(L)])

    delta, new_m, new_v = {}, {}, {}
    for n in _BIG:
        shp = w[n].shape
        v2 = lambda a: a.reshape(-1, shp[-1])
        d, mn, vn = _adamw(v2(w[n]), v2(grad[n]), v2(mom[n]), v2(var[n]), "adamw_" + n)
        delta[n], new_m[n], new_v[n] = d.reshape(shp), mn.reshape(shp), vn.reshape(shp)
    for group, gname in ((_SMALL_REPL, "adamw_small"), (_SMALL_SHARD, "adamw_conv")):
        like = [w[n] for n in group]
        outs = _adamw(_pack([w[n] for n in group]), _pack([grad[n] for n in group]), _pack([mom[n] for n in group]),
                      _pack([var[n] for n in group]), gname)
        for res, dst in zip(outs, (delta, new_m, new_v)):
            for n, a in zip(group, _unpack(res, like)):
                dst[n] = a

    return (loss, grad_x, *[grad[n] for n in _ORDER], *[delta[n] for n in _ORDER],
            *[new_m[n] for n in _ORDER], *[new_v[n] for n in _ORDER])
```

```python
import functools

import jax
import jax.numpy as jnp
from jax import lax
from jax.experimental import pallas as pl
from jax.experimental.pallas import tpu as pltpu

F32 = jnp.float32
BF16 = jnp.bfloat16

DEPTH = 4
D_MODEL = 1024
HEAD_DIM = 64
ATTN_W = 512
KV_W = 128
SG_W = 256
CONV_W = 256
SG_CHUNK = 128
D_FF = 2816
PROJ_W = 2048
GRID_W = 64
ROPE_THETA = 10000.0
AXIS_DIM = HEAD_DIM // 2
EPS = 1e-6
N_CHIPS = 4

ADAM_LR = 0.001
ADAM_B1 = 0.9
ADAM_B2 = 0.999
ADAM_EPS = 1e-08
ADAM_WD = 0.01
ADAM_STEP = 10

_ROW_TILE = 512
_FFN_ROW_TILE = 256
LANES = 128
HALO = 8
VMEM_LIMIT_BYTES = 56 * 1024 * 1024
MESH = pl.DeviceIdType.MESH
ANY = pl.BlockSpec(memory_space=pl.ANY)


def _cp(*sem):
    return pltpu.CompilerParams(dimension_semantics=sem if sem else None,
                                vmem_limit_bytes=VMEM_LIMIT_BYTES)


def _sds(shape, dtype):
    return jax.ShapeDtypeStruct(shape, dtype)


def _dot(a, b):
    return jnp.dot(a, b, preferred_element_type=F32)


def _dot_nt(a, b):
    return lax.dot_general(a, b, (((1,), (1,)), ((), ())), preferred_element_type=F32)


def _dot_tn(a, b):
    return lax.dot_general(a, b, (((0,), (0,)), ((), ())), preferred_element_type=F32)


def _norm_fwd(x, g):
    T, Dm = x.shape
    tm = min(T, _ROW_TILE)

    def body(x_ref, g_ref, o_ref):
        xv = x_ref[...]
        r = lax.rsqrt(jnp.mean(xv * xv, axis=-1, keepdims=True) + EPS)
        o_ref[...] = ((xv * r) * g_ref[...]).astype(BF16)

    return pl.pallas_call(
        body, name="norm_fwd", grid=(T // tm,),
        in_specs=[pl.BlockSpec((tm, Dm), lambda i: (i, 0)), pl.BlockSpec((1, Dm), lambda i: (0, 0))],
        out_specs=pl.BlockSpec((tm, Dm), lambda i: (i, 0)),
        out_shape=_sds((T, Dm), BF16), compiler_params=_cp("parallel"))(x, g)


def _norm_bwd(x, dh, dres, g):
    T, Dm = x.shape
    tm = min(T, _ROW_TILE)

    def body(x_ref, dh_ref, dr_ref, g_ref, dx_ref, dg_ref):
        i = pl.program_id(0)
        xv = x_ref[...]
        r = lax.rsqrt(jnp.mean(xv * xv, axis=-1, keepdims=True) + EPS)
        xh = xv * r
        dhv = dh_ref[...]
        dxh = dhv * g_ref[...]
        dx_ref[...] = dr_ref[...] + r * (dxh - xh * jnp.mean(dxh * xh, axis=-1, keepdims=True))
        part = jnp.sum(dhv * xh, axis=0, keepdims=True)

        @pl.when(i == 0)
        def _():
            dg_ref[...] = part

        @pl.when(i > 0)
        def _():
            dg_ref[...] += part

    row = pl.BlockSpec((tm, Dm), lambda i: (i, 0))
    vec = pl.BlockSpec((1, Dm), lambda i: (0, 0))
    return pl.pallas_call(
        body, name="norm_bwd", grid=(T // tm,),
        in_specs=[row, row, row, vec], out_specs=[row, vec],
        out_shape=[_sds((T, Dm), F32), _sds((1, Dm), F32)], compiler_params=_cp("arbitrary"))(x, dh, dres, g)


def _w_spec(w, rows, tn, row_of, col_of):
    if w.ndim == 2:
        return pl.BlockSpec((rows, tn), lambda *g: (row_of(*g), col_of(*g)))
    bps = w.shape[2] // tn
    return pl.BlockSpec((None, rows, tn), lambda *g: (col_of(*g) // bps, row_of(*g), col_of(*g) % bps))


def _mm_nn(a, w, *, tn, out_dtype, name, res=None, tm=512):
    M, K = a.shape
    N = w.shape[-1] if w.ndim == 2 else w.shape[0] * w.shape[2]
    tm = min(M, _ROW_TILE)
    has_res = res is not None

    def body(*refs):
        a_ref, w_ref = refs[0], refs[1]
        o_ref = refs[-1]
        acc = _dot(a_ref[...].astype(BF16), w_ref[...])
        if has_res:
            acc = acc + refs[2][...]
        o_ref[...] = acc.astype(out_dtype)

    in_specs = [pl.BlockSpec((tm, K), lambda j, i: (i, 0)),
                _w_spec(w, K, tn, lambda j, i: 0, lambda j, i: j)]
    args = [a, w]
    if has_res:
        in_specs.append(pl.BlockSpec((tm, tn), lambda j, i: (i, j)))
        args.append(res)
    return pl.pallas_call(
        body, name=name, grid=(N // tn, M // tm), in_specs=in_specs,
        out_specs=pl.BlockSpec((tm, tn), lambda j, i: (i, j)),
        out_shape=_sds((M, N), out_dtype), compiler_params=_cp("parallel", "parallel"))(*args)


def _a_spec(a, tm, tn, row_of, col_of):
    if a.ndim == 2:
        return pl.BlockSpec((tm, tn), lambda *g: (row_of(*g), col_of(*g)))
    bph = a.shape[2] // tn
    return pl.BlockSpec((None, tm, tn), lambda *g: (col_of(*g) // bph, row_of(*g), col_of(*g) % bph))


def _a_cols(a):
    return a.shape[1] if a.ndim == 2 else a.shape[0] * a.shape[2]


def _mm_nt(a, w, *, tkw, tn, name, tm=512):
    M = a.shape[-2]
    N = _a_cols(a)
    Kw = w.shape[-2]
    tm = min(M, _ROW_TILE)
    nred = N // tn

    def body(a_ref, w_ref, o_ref, *scratch):
        part = _dot_nt(a_ref[...].astype(BF16), w_ref[...])
        if nred == 1:
            o_ref[...] = part
            return
        acc_ref = scratch[0]
        n = pl.program_id(2)

        @pl.when(n == 0)
        def _():
            acc_ref[...] = part

        @pl.when(n > 0)
        def _():
            acc_ref[...] += part

        @pl.when(n == nred - 1)
        def _():
            o_ref[...] = acc_ref[...]

    return pl.pallas_call(
        body, name=name, grid=(M // tm, Kw // tkw, nred),
        in_specs=[_a_spec(a, tm, tn, lambda i, k, n: i, lambda i, k, n: n),
                  _w_spec(w, tkw, tn, lambda i, k, n: k, lambda i, k, n: n)],
        out_specs=pl.BlockSpec((tm, tkw), lambda i, k, n: (i, k)),
        out_shape=_sds((M, Kw), F32),
        scratch_shapes=[] if nred == 1 else [pltpu.VMEM((tm, tkw), F32)],
        compiler_params=_cp("parallel", "parallel", "arbitrary"))(a, w)


def _mm_tn(a, b, *, tk, tn, name, shards=None, tm=512):
    M, K = a.shape
    N = _a_cols(b)
    tm = min(M, _ROW_TILE)

    def body(a_ref, b_ref, o_ref):
        m = pl.program_id(2)
        part = _dot_tn(a_ref[...].astype(BF16), b_ref[...].astype(BF16))

        @pl.when(m == 0)
        def _():
            o_ref[...] = part

        @pl.when(m > 0)
        def _():
            o_ref[...] += part

    if shards is None:
        out_spec = pl.BlockSpec((tk, tn), lambda k, j, m: (k, j))
        out_shape = _sds((K, N), F32)
    else:
        bps = (N // shards) // tn
        out_spec = pl.BlockSpec((None, tk, tn), lambda k, j, m: (j // bps, k, j % bps))
        out_shape = _sds((shards, K, N // shards), F32)
    return pl.pallas_call(
        body, name=name, grid=(K // tk, N // tn, M // tm),
        in_specs=[pl.BlockSpec((tm, tk), lambda k, j, m: (m, k)),
                  _a_spec(b, tm, tn, lambda k, j, m: m, lambda k, j, m: j)],
        out_specs=out_spec, out_shape=out_shape,
        compiler_params=_cp("parallel", "parallel", "arbitrary"))(a, b)


def _halo_specs(T, tm, cw, ic):
    nb = tm // HALO
    last = T // HALO - 1

    def mk(rows, row_of):
        return pl.BlockSpec((rows, cw), lambda *g: (row_of(ic(*g)[0]), ic(*g)[1]))

    return [mk(HALO, lambda i: jnp.maximum(i * nb - 1, 0)), mk(tm, lambda i: i),
            mk(HALO, lambda i: jnp.minimum((i + 1) * nb, last))]


def _ext(prev_ref, cur_ref, next_ref, i, n):
    p = jnp.where(i > 0, prev_ref[...].astype(F32), 0.0)
    nx = jnp.where(i < n - 1, next_ref[...].astype(F32), 0.0)
    return jnp.concatenate([p, cur_ref[...].astype(F32), nx], axis=0)


def _dn(e):
    return pltpu.roll(e, 1, 0)


def _up(e):
    return pltpu.roll(e, e.shape[0] - 1, 0)


def _mid(e):
    return e[HALO:e.shape[0] - HALO]


def _conv3(e, w):
    return _dn(e) * w[0:1] + e * w[1:2] + _up(e) * w[2:3]


def _conv3_t(e, w):
    return _up(e) * w[0:1] + e * w[1:2] + _dn(e) * w[2:3]


def _conv3_wgrad(d, e):
    return jnp.concatenate([jnp.sum(_mid(d * _dn(e)), axis=0, keepdims=True),
                            jnp.sum(_mid(d * e), axis=0, keepdims=True),
                            jnp.sum(_mid(d * _up(e)), axis=0, keepdims=True)], axis=0)


def _sigmoid(x):
    return 1.0 / (1.0 + jnp.exp(-x))


def _accum(ref, i, part):
    @pl.when(i == 0)
    def _():
        ref[...] = part

    @pl.when(i > 0)
    def _():
        ref[...] += part


def _ffn_act_fwd(up, cw):
    T = up.shape[0]
    tm = min(T, _FFN_ROW_TILE)
    cb = D_FF // 2
    nblk = D_FF // cb
    n = T // tm

    def body(gp, gc, gn, vp, vc, vn, wg_ref, wv_ref, o_ref):
        i = pl.program_id(1)
        gate = _conv3(_ext(gp, gc, gn, i, n), wg_ref[...])
        val = _conv3(_ext(vp, vc, vn, i, n), wv_ref[...])
        o_ref[...] = _mid(gate * _sigmoid(gate) * val).astype(BF16)

    return pl.pallas_call(
        body, name="ffn_act_fwd", grid=(nblk, n),
        in_specs=_halo_specs(T, tm, cb, lambda j, i: (i, j)) + _halo_specs(T, tm, cb, lambda j, i: (i, j + nblk))
        + [pl.BlockSpec((3, cb), lambda j, i: (0, j)), pl.BlockSpec((3, cb), lambda j, i: (0, j + nblk))],
        out_specs=pl.BlockSpec((tm, cb), lambda j, i: (i, j)),
        out_shape=_sds((T, D_FF), BF16), compiler_params=_cp("parallel", "parallel"))(
            up, up, up, up, up, up, cw, cw)


def _ffn_act_bwd(up, dact, cw):
    T = up.shape[0]
    tm = min(T, _FFN_ROW_TILE)
    cb = D_FF // 2
    nblk = D_FF // cb
    n = T // tm

    def body(gp, gc, gn, vp, vc, vn, dp_, dc, dn_, wg_ref, wv_ref, dup_ref, dcw_ref):
        i = pl.program_id(1)
        wg, wv = wg_ref[...], wv_ref[...]
        eg = _ext(gp, gc, gn, i, n)
        ev = _ext(vp, vc, vn, i, n)
        ed = _ext(dp_, dc, dn_, i, n)
        gate = _conv3(eg, wg)
        val = _conv3(ev, wv)
        sg = _sigmoid(gate)
        d_gate = ed * val * (sg * (1.0 + gate * (1.0 - sg)))
        d_val = ed * (gate * sg)
        dup_ref[0] = _mid(_conv3_t(d_gate, wg)).astype(BF16)
        dup_ref[1] = _mid(_conv3_t(d_val, wv)).astype(BF16)
        part = jnp.stack([_conv3_wgrad(d_gate, eg), _conv3_wgrad(d_val, ev)], axis=0)
        _accum(dcw_ref, i, part)

    return pl.pallas_call(
        body, name="ffn_act_bwd", grid=(nblk, n),
        in_specs=_halo_specs(T, tm, cb, lambda j, i: (i, j)) + _halo_specs(T, tm, cb, lambda j, i: (i, j + nblk))
        + _halo_specs(T, tm, cb, lambda j, i: (i, j))
        + [pl.BlockSpec((3, cb), lambda j, i: (0, j)), pl.BlockSpec((3, cb), lambda j, i: (0, j + nblk))],
        out_specs=[pl.BlockSpec((2, tm, cb), lambda j, i: (0, i, j)),
                   pl.BlockSpec((2, 3, cb), lambda j, i: (0, 0, j))],
        out_shape=[_sds((2, T, D_FF), BF16), _sds((2, 3, D_FF), F32)],
        compiler_params=_cp("parallel", "arbitrary"))(up, up, up, up, up, up, dact, dact, dact, cw, cw)


_CB_BLK, _CC_BLK, _CX_BLK = 5, 6, 7


def _convmix_fwd(p, w):
    T = p.shape[0]
    tm = min(T, _ROW_TILE)
    n = T // tm

    def body(cb_ref, ccp, ccc, ccn, cxp, cxc, cxn, w_ref, o_ref):
        i = pl.program_id(0)
        z = _ext(ccp, ccc, ccn, i, n) * _ext(cxp, cxc, cxn, i, n)
        o_ref[...] = (cb_ref[...] * _mid(_conv3(z, w_ref[...]))).astype(BF16)

    return pl.pallas_call(
        body, name="convmix_fwd", grid=(n,),
        in_specs=[pl.BlockSpec((tm, CONV_W), lambda i: (i, _CB_BLK))]
        + _halo_specs(T, tm, CONV_W, lambda i: (i, _CC_BLK)) + _halo_specs(T, tm, CONV_W, lambda i: (i, _CX_BLK))
        + [pl.BlockSpec((3, CONV_W), lambda i: (0, 0))],
        out_specs=pl.BlockSpec((tm, CONV_W), lambda i: (i, 0)),
        out_shape=_sds((T, CONV_W), BF16), compiler_params=_cp("parallel"))(p, p, p, p, p, p, p, w)


def _convmix_bwd(p, dmix, w):
    T = p.shape[0]
    tm = min(T, _ROW_TILE)
    n = T // tm
    dblk = (ATTN_W + SG_W) // CONV_W

    def body(cbp, cbc, cbn, ccp, ccc, ccn, cxp, cxc, cxn, dp_, dc, dn_, w_ref, o_ref, dw_ref):
        i = pl.program_id(0)
        wv = w_ref[...]
        ecb = _ext(cbp, cbc, cbn, i, n)
        ecc = _ext(ccp, ccc, ccn, i, n)
        ecx = _ext(cxp, cxc, cxn, i, n)
        ed = _ext(dp_, dc, dn_, i, n)
        z = ecc * ecx
        d_cz = ed * ecb
        d_z = _conv3_t(d_cz, wv)
        o_ref[...] = jnp.concatenate([_mid(ed * _conv3(z, wv)), _mid(d_z * ecx), _mid(d_z * ecc)],
                                     axis=1).astype(BF16)
        _accum(dw_ref, i, _conv3_wgrad(d_cz, z))

    return pl.pallas_call(
        body, name="convmix_bwd", grid=(n,),
        in_specs=_halo_specs(T, tm, CONV_W, lambda i: (i, _CB_BLK)) + _halo_specs(T, tm, CONV_W, lambda i: (i, _CC_BLK))
        + _halo_specs(T, tm, CONV_W, lambda i: (i, _CX_BLK)) + _halo_specs(T, tm, CONV_W, lambda i: (i, dblk))
        + [pl.BlockSpec((3, CONV_W), lambda i: (0, 0))],
        out_specs=[pl.BlockSpec((tm, 3 * CONV_W), lambda i: (i, 0)), pl.BlockSpec((3, CONV_W), lambda i: (0, 0))],
        out_shape=[_sds((T, 3 * CONV_W), BF16), _sds((3, CONV_W), F32)],
        compiler_params=_cp("arbitrary"))(p, p, p, p, p, p, p, p, p, dmix, dmix, dmix, w)


_SU_BLK, _SV_BLK = 3, 4


def _sg_mixed(vnb, w_ref, bias, ch, pr, lo):
    vp = vnb[ch * SG_CHUNK:(ch + 1) * SG_CHUNK, pr * LANES:(pr + 1) * LANES]
    zero = jnp.zeros_like(vp)
    return (_dot(w_ref[2 * pr], jnp.where(lo, vp, zero)) + _dot(w_ref[2 * pr + 1], jnp.where(lo, zero, vp))
            + bias[:, pr * LANES:(pr + 1) * LANES]), vp


def _sg_fwd(p, g, w, bias):
    T = p.shape[0]
    tm = min(T, _ROW_TILE)

    def body(su_ref, sv_ref, g_ref, w_ref, b_ref, o_ref):
        lo = lax.broadcasted_iota(jnp.int32, (SG_CHUNK, LANES), 1) < HEAD_DIM
        sv = sv_ref[...]
        r = lax.rsqrt(jnp.mean(sv * sv, axis=-1, keepdims=True) + EPS)
        vnb = ((sv * r) * g_ref[...]).astype(BF16)
        bias_v = b_ref[...]
        for ch in range(tm // SG_CHUNK):
            for pr in range(2):
                mixed, _ = _sg_mixed(vnb, w_ref, bias_v, ch, pr, lo)
                rows, cols = pl.ds(ch * SG_CHUNK, SG_CHUNK), pl.ds(pr * LANES, LANES)
                o_ref[rows, cols] = (su_ref[rows, cols] * mixed).astype(BF16)

    return pl.pallas_call(
        body, name="sg_fwd", grid=(T // tm,),
        in_specs=[pl.BlockSpec((tm, SG_W), lambda i: (i, _SU_BLK)), pl.BlockSpec((tm, SG_W), lambda i: (i, _SV_BLK)),
                  pl.BlockSpec((1, SG_W), lambda i: (0, 0)), pl.BlockSpec((4, SG_CHUNK, SG_CHUNK), lambda i: (0, 0, 0)),
                  pl.BlockSpec((SG_CHUNK, SG_W), lambda i: (0, 0))],
        out_specs=pl.BlockSpec((tm, SG_W), lambda i: (i, 0)),
        out_shape=_sds((T, SG_W), BF16), compiler_params=_cp("parallel"))(p, p, g, w, bias)


def _sg_bwd(p, dmix, g, w, wt, bias):
    T = p.shape[0]
    tm = min(T, _ROW_TILE)
    dblk = ATTN_W // SG_W

    def body(su_ref, sv_ref, d_ref, g_ref, w_ref, wt_ref, b_ref, o_ref, dw_ref, db_ref, dg_ref, dvn_ref):
        i = pl.program_id(0)
        lo = lax.broadcasted_iota(jnp.int32, (SG_CHUNK, LANES), 1) < HEAD_DIM
        sv = sv_ref[...]
        gv = g_ref[...]
        r = lax.rsqrt(jnp.mean(sv * sv, axis=-1, keepdims=True) + EPS)
        xh = sv * r
        vnb = (xh * gv).astype(BF16)
        bias_v = b_ref[...]
        dw = [jnp.zeros((SG_CHUNK, SG_CHUNK), F32) for _ in range(4)]
        db = jnp.zeros((SG_CHUNK, SG_W), F32)
        for ch in range(tm // SG_CHUNK):
            dbs = []
            for pr in range(2):
                mixed, vp = _sg_mixed(vnb, w_ref, bias_v, ch, pr, lo)
                rows, cols = pl.ds(ch * SG_CHUNK, SG_CHUNK), pl.ds(pr * LANES, LANES)
                dgo = d_ref[rows, cols]
                o_ref[rows, cols] = (dgo * mixed).astype(BF16)
                dm = dgo * su_ref[rows, cols]
                dmb = dm.astype(BF16)
                zero = jnp.zeros_like(dmb)
                dw[2 * pr] += _dot_nt(jnp.where(lo, dmb, zero), vp)
                dw[2 * pr + 1] += _dot_nt(jnp.where(lo, zero, dmb), vp)
                dvn_ref[rows, cols] = jnp.where(lo, _dot(wt_ref[2 * pr], dmb), _dot(wt_ref[2 * pr + 1], dmb))
                dbs.append(dm)
            db += jnp.concatenate(dbs, axis=1)
        dvn = dvn_ref[...]
        dxh = dvn * gv
        o_ref[:, pl.ds(SG_W, SG_W)] = (r * (dxh - xh * jnp.mean(dxh * xh, axis=-1, keepdims=True))).astype(BF16)
        _accum(dw_ref, i, jnp.stack(dw, axis=0))
        _accum(db_ref, i, db)
        _accum(dg_ref, i, jnp.sum(dvn * xh, axis=0, keepdims=True))

    wspec = pl.BlockSpec((4, SG_CHUNK, SG_CHUNK), lambda i: (0, 0, 0))
    return pl.pallas_call(
        body, name="sg_bwd", grid=(T // tm,),
        in_specs=[pl.BlockSpec((tm, SG_W), lambda i: (i, _SU_BLK)), pl.BlockSpec((tm, SG_W), lambda i: (i, _SV_BLK)),
                  pl.BlockSpec((tm, SG_W), lambda i: (i, dblk)), pl.BlockSpec((1, SG_W), lambda i: (0, 0)),
                  wspec, wspec, pl.BlockSpec((SG_CHUNK, SG_W), lambda i: (0, 0))],
        out_specs=[pl.BlockSpec((tm, 2 * SG_W), lambda i: (i, 0)), wspec,
                   pl.BlockSpec((SG_CHUNK, SG_W), lambda i: (0, 0)), pl.BlockSpec((1, SG_W), lambda i: (0, 0))],
        out_shape=[_sds((T, 2 * SG_W), BF16), _sds((4, SG_CHUNK, SG_CHUNK), F32),
                   _sds((SG_CHUNK, SG_W), F32), _sds((1, SG_W), F32)],
        scratch_shapes=[pltpu.VMEM((tm, SG_W), F32)],
        compiler_params=_cp("arbitrary"))(p, p, dmix, g, w, wt, bias)


_ATTN_TQ = 256
_ATTN_TK = 512
_SOFTMAX_STRIP = 32


def _head_mean(v, bmat):
    return jnp.dot(v, bmat, preferred_element_type=F32, precision=lax.Precision.HIGHEST)


def _swap16(y):
    lane = lax.broadcasted_iota(jnp.int32, y.shape, 1)
    return jnp.where(lane % 32 < 16, pltpu.roll(y, y.shape[1] - 16, 1), pltpu.roll(y, 16, 1))


def _rope(y, cos, sin):
    return y * cos + _swap16(y) * sin


def _rope_t(dy, cos, sin):
    return dy * cos + _swap16(dy * sin)


def _dup_rows(t, gidx):
    h = t[gidx * HEAD_DIM:(gidx + 1) * HEAD_DIM]
    return jnp.concatenate([h, h], axis=0)


def _qk_prep(p, gq, gk, cos, sin, bmat, tk):
    T = p.shape[0]
    nk = T // tk
    scale = HEAD_DIM ** -0.5

    def body(q_ref, kv_ref, gq_ref, gk_ref, cos_ref, sin_ref, b_ref, qo_ref, kt_ref, kd_ref, vt_ref):
        cosv, sinv, bm = cos_ref[...], sin_ref[...], b_ref[...]
        for pr in range(ATTN_W // LANES):
            cols = pl.ds(pr * LANES, LANES)
            xq = q_ref[:, cols]
            r = lax.rsqrt(_head_mean(xq * xq, bm) + EPS)
            qo_ref[:, cols] = (_rope((xq * r) * gq_ref[:, cols], cosv, sinv) * scale).astype(BF16)
        xk = kv_ref[:, pl.ds(0, LANES)]
        r = lax.rsqrt(_head_mean(xk * xk, bm) + EPS)
        kt = _rope((xk * r) * gk_ref[...], cosv, sinv).T
        vt = kv_ref[:, pl.ds(LANES, LANES)].T
        for gidx in range(2):
            kdup = _dup_rows(kt, gidx)
            kt_ref[gidx] = kdup.astype(BF16)
            vt_ref[gidx] = _dup_rows(vt, gidx).astype(BF16)
            kd_ref[gidx] = kdup.T.astype(BF16)

    tspec = pl.BlockSpec((2, None, LANES, tk), lambda i: (0, i, 0, 0))
    dspec = pl.BlockSpec((2, tk, LANES), lambda i: (0, i, 0))
    tab = pl.BlockSpec((tk, LANES), lambda i: (i, 0))
    return pl.pallas_call(
        body, name="qk_prep", grid=(nk,),
        in_specs=[pl.BlockSpec((tk, ATTN_W), lambda i: (i, 0)), pl.BlockSpec((tk, 2 * KV_W), lambda i: (i, ATTN_W // (2 * KV_W))),
                  pl.BlockSpec((1, ATTN_W), lambda i: (0, 0)), pl.BlockSpec((1, KV_W), lambda i: (0, 0)),
                  tab, tab, pl.BlockSpec((LANES, LANES), lambda i: (0, 0))],
        out_specs=[pl.BlockSpec((tk, ATTN_W), lambda i: (i, 0)), tspec, dspec, tspec],
        out_shape=[_sds((T, ATTN_W), BF16), _sds((2, nk, LANES, tk), BF16), _sds((2, T, LANES), BF16),
                   _sds((2, nk, LANES, tk), BF16)],
        compiler_params=_cp("parallel"))(p, p, gq, gk, cos, sin, bmat)


def _stack_heads(t):
    lo = lax.broadcasted_iota(jnp.int32, (t.shape[0], LANES), 1) < HEAD_DIM
    parts = []
    for pr in range(2):
        tp = t[:, pr * LANES:(pr + 1) * LANES]
        zero = jnp.zeros_like(tp)
        parts += [jnp.where(lo, tp, zero), jnp.where(lo, zero, tp)]
    return jnp.concatenate(parts, axis=0)


def _unstack_heads(s, tq):
    lo = lax.broadcasted_iota(jnp.int32, (tq, LANES), 1) < HEAD_DIM
    return jnp.concatenate([jnp.where(lo, s[0:tq], s[tq:2 * tq]),
                            jnp.where(lo, s[2 * tq:3 * tq], s[3 * tq:4 * tq])], axis=1)


def _rows8_reduce(s, op):
    parts = [s[r:r + 8] for r in range(0, s.shape[0], 8)]
    while len(parts) > 1:
        parts = [op(parts[k], parts[k + 1]) for k in range(0, len(parts) - 1, 2)] + (
            [parts[-1]] if len(parts) % 2 else [])
    return parts[0]


def _attn_fwd(q, kd, vt, tq):
    T = q.shape[0]
    nk, tk = vt.shape[1], vt.shape[3]
    nq = T // tq
    sq = 4 * tq
    strip = _SOFTMAX_STRIP
    assert nk % 2 == 0, "key blocks are taken two per loop trip"

    def body(q_ref, kd_ref, vt_ref, o_ref, lse_ref, qst_ref, sa_ref, sb_ref, pa_ref, pb_ref, m_ref, l_ref, acc_ref):
        qst_ref[...] = _stack_heads(q_ref[...]).astype(F32).T.astype(BF16)
        m_ref[...] = jnp.full((1, sq), -jnp.inf, F32)
        l_ref[...] = jnp.zeros((1, sq), F32)
        acc_ref[...] = jnp.zeros((HEAD_DIM, sq), F32)

        def scores(j):
            return _dot(kd_ref[pl.ds(pl.multiple_of(j * tk, tk), tk), :], qst_ref[...])

        def softmax_pv(s_ref, p_ref, j):
            m8 = None
            for c in range(tk // strip):
                part = _rows8_reduce(s_ref[pl.ds(c * strip, strip), :], jnp.maximum)
                m8 = part if m8 is None else jnp.maximum(m8, part)
            m_old = m_ref[...]
            m_new = jnp.maximum(m_old, jnp.max(m8, axis=0, keepdims=True))
            alpha = jnp.exp(m_old - m_new)
            m_ref[...] = m_new
            l8 = jnp.zeros((8, sq), F32)
            for c in range(tk // strip):
                rows = pl.ds(c * strip, strip)
                pexp = jnp.exp(s_ref[rows, :] - m_new)
                l8 = l8 + _rows8_reduce(pexp, jnp.add)
                p_ref[rows, :] = pexp.astype(BF16)
            l_ref[...] = alpha * l_ref[...] + jnp.sum(l8, axis=0, keepdims=True)
            acc_ref[...] = alpha * acc_ref[...] + _dot(vt_ref[j, pl.ds(0, HEAD_DIM), :], p_ref[...])

        sa_ref[...] = scores(0)

        def pair(t, carry):
            j = 2 * t
            sb_ref[...] = scores(j + 1)
            softmax_pv(sa_ref, pa_ref, j)
            sa_ref[...] = scores(jnp.minimum(j + 2, nk - 1))
            softmax_pv(sb_ref, pb_ref, j + 1)
            return carry

        lax.fori_loop(0, nk // 2, pair, 0)
        l = l_ref[...]
        on = acc_ref[...] / l
        pairs = []
        for pr in range(2):
            two = jnp.concatenate([on[:, (2 * pr) * tq:(2 * pr + 1) * tq], on[:, (2 * pr + 1) * tq:(2 * pr + 2) * tq]],
                                  axis=0)
            pairs.append(two.T)
        o_ref[...] = jnp.concatenate(pairs, axis=1).astype(BF16)
        lse_ref[...] = jnp.broadcast_to(m_ref[...] + jnp.log(l), (LANES, sq)).T

    row = pltpu.VMEM((1, sq), F32)
    return pl.pallas_call(
        body, name="attn_fwd", grid=(2, nq),
        in_specs=[pl.BlockSpec((tq, 2 * LANES), lambda g, i: (i, g)),
                  pl.BlockSpec((None, T, LANES), lambda g, i: (g, 0, 0)),
                  pl.BlockSpec((None, nk, LANES, tk), lambda g, i: (g, 0, 0, 0))],
        out_specs=[pl.BlockSpec((tq, 2 * LANES), lambda g, i: (i, g)),
                   pl.BlockSpec((None, None, sq, LANES), lambda g, i: (g, i, 0, 0))],
        out_shape=[_sds((T, ATTN_W), BF16), _sds((2, nq, sq, LANES), F32)],
        scratch_shapes=[pltpu.VMEM((LANES, sq), BF16), pltpu.VMEM((tk, sq), F32), pltpu.VMEM((tk, sq), F32),
                        pltpu.VMEM((tk, sq), BF16), pltpu.VMEM((tk, sq), BF16), row, row,
                        pltpu.VMEM((HEAD_DIM, sq), F32)],
        compiler_params=_cp("parallel", "parallel"))(q, kd, vt)


def _attn_bwd(q, o, dmix, lse, kt, kd, vt, tq):
    T = q.shape[0]
    nk, tk = kt.shape[1], kt.shape[3]
    nq = T // tq
    sq = 4 * tq
    rep = tk // LANES

    def body(q_ref, o_ref, do_ref, lse_ref, kt_ref, kd_ref, vt_ref, dq_ref, dkt_ref, dvt_ref):
        i = pl.program_id(1)
        qs = _stack_heads(q_ref[...])
        dof = _stack_heads(do_ref[...])
        dos = dof.astype(BF16)
        qst = _add_halves_rows(qs.astype(F32).T).astype(BF16)
        dost = _add_halves_rows(dof.T).astype(BF16)
        o_pair = o_ref[...].astype(F32)
        os_ = jnp.concatenate([o_pair[:, 0:LANES], o_pair[:, 0:LANES], o_pair[:, LANES:], o_pair[:, LANES:]], axis=0)
        delta = jnp.sum(dof * os_, axis=-1, keepdims=True)
        lse_t = jnp.concatenate([lse_ref[...]] * rep, axis=1)

        @pl.when(i == 0)
        def _():
            dkt_ref[...] = jnp.zeros_like(dkt_ref)
            dvt_ref[...] = jnp.zeros_like(dvt_ref)

        def step(j, dq):
            kdb = kd_ref[pl.ds(pl.multiple_of(j * tk, tk), tk), :]
            pexp = jnp.exp(_dot(qs, kt_ref[j]) - lse_t)
            ds = pexp * (_dot(dos, vt_ref[j]) - delta)
            pb = pexp.astype(BF16)
            dsb = ds.astype(BF16)
            dvt_ref[j] += _dot(dost, pb)
            dkt_ref[j] += _dot(qst, dsb)
            return dq + _dot(dsb, kdb)

        dq = lax.fori_loop(0, nk, step, jnp.zeros((sq, LANES), F32))
        dq_ref[...] = _unstack_heads(dq, tq)

    tspec = pl.BlockSpec((None, nk, LANES, tk), lambda g, i: (g, 0, 0, 0))
    gspec = pl.BlockSpec((None, nk, HEAD_DIM, tk), lambda g, i: (g, 0, 0, 0))
    qspec = pl.BlockSpec((tq, 2 * LANES), lambda g, i: (i, g))
    return pl.pallas_call(
        body, name="attn_bwd", grid=(2, nq),
        in_specs=[qspec, qspec, qspec, pl.BlockSpec((None, None, sq, LANES), lambda g, i: (g, i, 0, 0)),
                  tspec, pl.BlockSpec((None, T, LANES), lambda g, i: (g, 0, 0)), tspec],
        out_specs=[qspec, gspec, gspec],
        out_shape=[_sds((T, ATTN_W), F32), _sds((2, nk, HEAD_DIM, tk), F32), _sds((2, nk, HEAD_DIM, tk), F32)],
        compiler_params=_cp("parallel", "arbitrary"))(q, o, dmix, lse, kt, kd, vt)


def _add_halves_rows(t):
    return t[0:HEAD_DIM] + t[HEAD_DIM:2 * HEAD_DIM]


def _fold_t(t_ref):
    return jnp.concatenate([t_ref[0], t_ref[1]], axis=0).T


def _qk_bwd(p, dq, dkt, dvt, gq, gk, cos, sin, bmat):
    T = p.shape[0]
    nk, tk = dkt.shape[1], dkt.shape[3]
    scale = HEAD_DIM ** -0.5

    def norm_bwd(x, dy, gain, bm):
        r = lax.rsqrt(_head_mean(x * x, bm) + EPS)
        xh = x * r
        dxh = dy * gain
        return r * (dxh - xh * _head_mean(dxh * xh, bm)), jnp.sum(dy * xh, axis=0, keepdims=True)

    def body(q_ref, kv_ref, dq_ref, dkt_ref, dvt_ref, gq_ref, gk_ref, cos_ref, sin_ref, b_ref, o_ref, dgq_ref, dgk_ref):
        i = pl.program_id(0)
        cosv, sinv, bm = cos_ref[...], sin_ref[...], b_ref[...]
        dgq = []
        for pr in range(ATTN_W // LANES):
            cols = pl.ds(pr * LANES, LANES)
            dy = _rope_t(dq_ref[:, cols] * scale, cosv, sinv)
            dx, dg = norm_bwd(q_ref[:, cols], dy, gq_ref[:, cols], bm)
            o_ref[:, cols] = dx.astype(BF16)
            dgq.append(dg)
        dy = _rope_t(_fold_t(dkt_ref), cosv, sinv)
        dx, dgk = norm_bwd(kv_ref[:, pl.ds(0, LANES)], dy, gk_ref[...], bm)
        o_ref[:, pl.ds(ATTN_W, LANES)] = dx.astype(BF16)
        o_ref[:, pl.ds(ATTN_W + LANES, LANES)] = _fold_t(dvt_ref).astype(BF16)
        _accum(dgq_ref, i, jnp.concatenate(dgq, axis=1))
        _accum(dgk_ref, i, dgk)

    tspec = pl.BlockSpec((2, None, HEAD_DIM, tk), lambda i: (0, i, 0, 0))
    tab = pl.BlockSpec((tk, LANES), lambda i: (i, 0))
    return pl.pallas_call(
        body, name="qk_bwd", grid=(nk,),
        in_specs=[pl.BlockSpec((tk, ATTN_W), lambda i: (i, 0)), pl.BlockSpec((tk, 2 * KV_W), lambda i: (i, ATTN_W // (2 * KV_W))),
                  pl.BlockSpec((tk, ATTN_W), lambda i: (i, 0)), tspec, tspec,
                  pl.BlockSpec((1, ATTN_W), lambda i: (0, 0)), pl.BlockSpec((1, KV_W), lambda i: (0, 0)),
                  tab, tab, pl.BlockSpec((LANES, LANES), lambda i: (0, 0))],
        out_specs=[pl.BlockSpec((tk, ATTN_W + 2 * KV_W), lambda i: (i, 0)),
                   pl.BlockSpec((1, ATTN_W), lambda i: (0, 0)), pl.BlockSpec((1, KV_W), lambda i: (0, 0))],
        out_shape=[_sds((T, ATTN_W + 2 * KV_W), BF16), _sds((1, ATTN_W), F32), _sds((1, KV_W), F32)],
        compiler_params=_cp("arbitrary"))(p, p, dq, dkt, dvt, gq, gk, cos, sin, bmat)


def _loss_head(y, target):
    T, Dm = y.shape
    tm = min(T, _ROW_TILE)

    def body(y_ref, t_ref, dy_ref, l_ref):
        i = pl.program_id(0)
        err = y_ref[...] - t_ref[...]
        dy_ref[...] = err * (1.0 / Dm)
        part = jnp.sum(jnp.sum(err * err, axis=-1, keepdims=True), axis=0, keepdims=True) * (0.5 / Dm)
        _accum(l_ref, i, jnp.broadcast_to(part, (8, LANES)))

    row = pl.BlockSpec((tm, Dm), lambda i: (i, 0))
    return pl.pallas_call(
        body, name="loss_head", grid=(T // tm,), in_specs=[row, row],
        out_specs=[row, pl.BlockSpec((8, LANES), lambda i: (0, 0))],
        out_shape=[_sds((T, Dm), F32), _sds((8, LANES), F32)], compiler_params=_cp("arbitrary"))(y, target)


def _adamw(w, g, m, v, name):
    R, C = w.shape
    tr = R
    for cand in (512, 256, 128, 64, 32, 16, 8):
        if R % cand == 0:
            tr = cand
            break
    c1 = 1.0 - ADAM_B1 ** ADAM_STEP
    c2 = 1.0 - ADAM_B2 ** ADAM_STEP

    def body(w_ref, g_ref, m_ref, v_ref, d_ref, mo_ref, vo_ref):
        gv = g_ref[...]
        mn = ADAM_B1 * m_ref[...] + (1.0 - ADAM_B1) * gv
        vn = ADAM_B2 * v_ref[...] + (1.0 - ADAM_B2) * (gv * gv)
        d_ref[...] = -ADAM_LR * ((mn / c1) / (jnp.sqrt(vn / c2) + ADAM_EPS) + ADAM_WD * w_ref[...])
        mo_ref[...] = mn
        vo_ref[...] = vn

    blk = pl.BlockSpec((tr, C), lambda i: (i, 0))
    return pl.pallas_call(
        body, name=name, grid=(R // tr,), in_specs=[blk] * 4, out_specs=[blk] * 3,
        out_shape=[_sds((R, C), F32)] * 3, compiler_params=_cp("parallel"))(w, g, m, v)


def _cast_bf16(w, name):
    R, C = w.shape
    tr = 512 if R % 512 == 0 else 256

    def body(w_ref, o_ref):
        o_ref[...] = w_ref[...].astype(BF16)

    blk = pl.BlockSpec((tr, C), lambda i: (i, 0))
    return pl.pallas_call(body, name=name, grid=(R // tr,), in_specs=[blk], out_specs=blk,
                          out_shape=_sds((R, C), BF16), compiler_params=_cp("parallel"))(w)


def _add_halves(g, recv, name):
    S, R, C = g.shape
    half = R // 2

    def body(g_ref, r_ref, o_ref):
        c = lax.axis_index("c")
        o_ref[...] = g_ref[pl.ds(pl.multiple_of(c * half, 8), half), :] + r_ref[...]

    return pl.pallas_call(
        body, name=name, grid=(S,),
        in_specs=[pl.BlockSpec((None, R, C), lambda s: (s, 0, 0)), pl.BlockSpec((None, half, C), lambda s: (s, 0, 0))],
        out_specs=pl.BlockSpec((None, half, C), lambda s: (s, 0, 0)),
        out_shape=_sds((S, half, C), F32), compiler_params=_cp("parallel"))(g, recv)


def _sum_chips(parts, name):
    S, R, C = parts.shape
    tr = R
    for cand in (256, 128, 64, 32, 16, 8):
        if R % cand == 0:
            tr = cand
            break

    def body(p_ref, o_ref):
        o_ref[...] = ((p_ref[0] + p_ref[1]) + p_ref[2]) + p_ref[3]

    return pl.pallas_call(
        body, name=name, grid=(R // tr,),
        in_specs=[pl.BlockSpec((S, tr, C), lambda i: (0, i, 0))], out_specs=pl.BlockSpec((tr, C), lambda i: (i, 0)),
        out_shape=_sds((R, C), F32), compiler_params=_cp("parallel"))(parts)


def _position():
    x, y, c = lax.axis_index("x"), lax.axis_index("y"), lax.axis_index("c")
    return x, y, c


def _other_chips(x, y):
    return [(1 - x, y), (x, 1 - y), (1 - x, 1 - y)]


def _gather_shards(shards):
    n = len(shards)

    def body(*refs):
        ins, outs = refs[:n], refs[n:2 * n]
        send_sems, recv_sems, local_sems = refs[2 * n:]
        x, y, c = _position()
        me = 2 * x + y
        chips = _other_chips(x, y)
        copies = []
        for t in range(n):
            own = pltpu.make_async_copy(ins[t], outs[t].at[:, me], local_sems.at[t])
            own.start()
            copies.append(own)
        sends = []
        for t in range(n):
            for k, (px, py) in enumerate(chips):
                cp = pltpu.make_async_remote_copy(
                    src_ref=ins[t], dst_ref=outs[t].at[:, me], send_sem=send_sems.at[t, k], recv_sem=recv_sems.at[t, k],
                    device_id=(px, py, c), device_id_type=MESH)
                cp.start()
                sends.append(cp)
        for t in range(n):
            for k, (px, py) in enumerate(chips):
                pltpu.make_async_remote_copy(
                    src_ref=ins[t], dst_ref=outs[t].at[:, 2 * px + py], send_sem=send_sems.at[t, k],
                    recv_sem=recv_sems.at[t, k], device_id=(px, py, c), device_id_type=MESH).wait_recv()
        for cp in sends:
            cp.wait_send()
        for cp in copies:
            cp.wait()

    return pl.pallas_call(
        body, name="gather_weights",
        in_specs=[ANY] * n, out_specs=[ANY] * n,
        out_shape=[_sds((s.shape[0], N_CHIPS) + s.shape[1:], s.dtype) for s in shards],
        scratch_shapes=[pltpu.SemaphoreType.DMA((n, 3)), pltpu.SemaphoreType.DMA((n, 3)), pltpu.SemaphoreType.DMA((n,))],
        compiler_params=pltpu.CompilerParams(has_side_effects=True))(*shards)


def _send_to_sibling_halves(grads):
    n = len(grads)

    def body(*refs):
        ins, outs = refs[:n], refs[n:2 * n]
        send_sems, recv_sems = refs[2 * n:]
        x, y, c = _position()
        sends = []
        for t in range(n):
            half = ins[t].shape[1] // 2
            src = ins[t].at[:, pl.ds(pl.multiple_of((1 - c) * half, 8), half), :]
            cp = pltpu.make_async_remote_copy(src_ref=src, dst_ref=outs[t], send_sem=send_sems.at[t],
                                              recv_sem=recv_sems.at[t], device_id=(x, y, 1 - c), device_id_type=MESH)
            cp.start()
            sends.append(cp)
        for cp in sends:
            cp.wait()

    return pl.pallas_call(
        body, name="grads_to_sibling",
        in_specs=[ANY] * n, out_specs=[ANY] * n,
        out_shape=[_sds((g.shape[0], g.shape[1] // 2, g.shape[2]), g.dtype) for g in grads],
        scratch_shapes=[pltpu.SemaphoreType.DMA((n,)), pltpu.SemaphoreType.DMA((n,))],
        compiler_params=pltpu.CompilerParams(has_side_effects=True))(*grads)


def _scatter_to_chips(sums):
    n = len(sums)

    def body(*refs):
        ins, outs = refs[:n], refs[n:2 * n]
        send_sems, recv_sems, local_sems = refs[2 * n:]
        x, y, c = _position()
        me = 2 * x + y
        chips = _other_chips(x, y)
        copies = []
        for t in range(n):
            own = pltpu.make_async_copy(ins[t].at[me], outs[t].at[me], local_sems.at[t])
            own.start()
            copies.append(own)
        sends = []
        for t in range(n):
            for k, (px, py) in enumerate(chips):
                cp = pltpu.make_async_remote_copy(
                    src_ref=ins[t].at[2 * px + py], dst_ref=outs[t].at[me], send_sem=send_sems.at[t, k],
                    recv_sem=recv_sems.at[t, k], device_id=(px, py, c), device_id_type=MESH)
                cp.start()
                sends.append(cp)
        for t in range(n):
            for k, (px, py) in enumerate(chips):
                pltpu.make_async_remote_copy(
                    src_ref=ins[t].at[me], dst_ref=outs[t].at[2 * px + py], send_sem=send_sems.at[t, k],
                    recv_sem=recv_sems.at[t, k], device_id=(px, py, c), device_id_type=MESH).wait_recv()
        for cp in sends:
            cp.wait_send()
        for cp in copies:
            cp.wait()

    return pl.pallas_call(
        body, name="grads_to_chips",
        in_specs=[ANY] * n, out_specs=[ANY] * n, out_shape=[_sds(s.shape, s.dtype) for s in sums],
        scratch_shapes=[pltpu.SemaphoreType.DMA((n, 3)), pltpu.SemaphoreType.DMA((n, 3)), pltpu.SemaphoreType.DMA((n,))],
        compiler_params=pltpu.CompilerParams(has_side_effects=True))(*sums)


def _join_halves(halves):
    n = len(halves)

    def body(*refs):
        ins, outs = refs[:n], refs[n:2 * n]
        send_sems, recv_sems, local_sems = refs[2 * n:]
        x, y, c = _position()
        ops = []
        for t in range(n):
            half = ins[t].shape[0]
            mine = pl.ds(pl.multiple_of(c * half, 8), half)
            own = pltpu.make_async_copy(ins[t], outs[t].at[mine, :], local_sems.at[t])
            own.start()
            cp = pltpu.make_async_remote_copy(src_ref=ins[t], dst_ref=outs[t].at[mine, :], send_sem=send_sems.at[t],
                                              recv_sem=recv_sems.at[t], device_id=(x, y, 1 - c), device_id_type=MESH)
            cp.start()
            ops += [own, cp]
        for op in ops:
            op.wait()

    return pl.pallas_call(
        body, name="grads_join_halves",
        in_specs=[ANY] * n, out_specs=[ANY] * n,
        out_shape=[_sds((2 * h.shape[0], h.shape[1]), h.dtype) for h in halves],
        scratch_shapes=[pltpu.SemaphoreType.DMA((n,)), pltpu.SemaphoreType.DMA((n,)), pltpu.SemaphoreType.DMA((n,))],
        compiler_params=pltpu.CompilerParams(has_side_effects=True))(*halves)


def _allreduce_small(packed):
    R, C = packed.shape
    ndev = 2 * N_CHIPS

    def body(in_ref, out_ref, buf, send_sems, recv_sems):
        x, y, c = _position()
        me = 4 * x + 2 * y + c
        buf[me] = in_ref[...]
        sends = []
        for k in range(1, ndev):
            peer = (x ^ (k >> 2), y ^ ((k >> 1) & 1), c ^ (k & 1))
            cp = pltpu.make_async_remote_copy(src_ref=in_ref, dst_ref=buf.at[me], send_sem=send_sems.at[k - 1],
                                              recv_sem=recv_sems.at[k - 1], device_id=peer, device_id_type=MESH)
            cp.start()
            sends.append(cp)
        for k in range(1, ndev):
            peer = (x ^ (k >> 2), y ^ ((k >> 1) & 1), c ^ (k & 1))
            pltpu.make_async_remote_copy(src_ref=in_ref, dst_ref=buf.at[me ^ k], send_sem=send_sems.at[k - 1],
                                         recv_sem=recv_sems.at[k - 1], device_id=peer, device_id_type=MESH).wait_recv()
        for cp in sends:
            cp.wait_send()
        acc = buf[0]
        for d in range(1, ndev):
            acc = acc + buf[d]
        out_ref[...] = acc

    return pl.pallas_call(
        body, name="allreduce_small",
        in_specs=[pl.BlockSpec(memory_space=pltpu.VMEM)], out_specs=pl.BlockSpec(memory_space=pltpu.VMEM),
        out_shape=_sds((R, C), F32),
        scratch_shapes=[pltpu.VMEM((ndev, R, C), F32), pltpu.SemaphoreType.DMA((ndev - 1,)),
                        pltpu.SemaphoreType.DMA((ndev - 1,))],
        compiler_params=pltpu.CompilerParams(vmem_limit_bytes=VMEM_LIMIT_BYTES, has_side_effects=True))(packed)


def _rope_tables(T):
    pos = jnp.arange(T)
    row = (pos // GRID_W).astype(F32)
    col = (pos % GRID_W).astype(F32)
    inv = 1.0 / (ROPE_THETA ** (jnp.arange(AXIS_DIM // 2, dtype=F32) * 2.0 / AXIS_DIM))
    ar, ac = row[:, None] * inv[None, :], col[:, None] * inv[None, :]
    cos = jnp.concatenate([jnp.cos(ar), jnp.cos(ar), jnp.cos(ac), jnp.cos(ac)], axis=-1)
    sin = jnp.concatenate([-jnp.sin(ar), jnp.sin(ar), -jnp.sin(ac), jnp.sin(ac)], axis=-1)
    return jnp.tile(cos, (1, LANES // HEAD_DIM)), jnp.tile(sin, (1, LANES // HEAD_DIM))


def _head_mean_matrix():
    h = jnp.arange(LANES) // HEAD_DIM
    return jnp.where(h[:, None] == h[None, :], 1.0 / HEAD_DIM, 0.0).astype(F32)


def _pack(arrays):
    flat = jnp.concatenate([a.reshape(-1) for a in arrays])
    rows = -(-flat.shape[0] // LANES)
    rows = -(-rows // 256) * 256
    return jnp.pad(flat, (0, rows * LANES - flat.shape[0])).reshape(rows, LANES)


def _unpack(packed, like):
    flat = packed.reshape(-1)
    out, off = [], 0
    for a in like:
        out.append(flat[off:off + a.size].reshape(a.shape))
        off += a.size
    return out


def _layer_fwd(x, lw, consts):
    cos, sin, bmat = consts
    T = x.shape[0]
    tk = min(T, _ATTN_TK)
    tq = min(T, _ATTN_TQ)
    h = _norm_fwd(x, lw["norm1_g"])
    p = _mm_nn(h, lw["w_in"], tn=512, out_dtype=F32, name="mm_p")
    qn, kt, kd, vt = _qk_prep(p, lw["gq"], lw["gk"], cos, sin, bmat, tk)
    o, lse = _attn_fwd(qn, kd, vt, tq)
    go = _sg_fwd(p, lw["sg_norm_g"], lw["sg_w"], lw["sg_bias"])
    co = _convmix_fwd(p, lw["conv_w"])
    mix = jnp.concatenate([o, go, co], axis=1)
    x_mid = _mm_nn(mix, lw["w_out"], tn=D_MODEL, out_dtype=F32, name="mm_out", res=x)
    h2 = _norm_fwd(x_mid, lw["norm2_g"])
    up = _mm_nn(h2, lw["w_up"], tn=D_FF // 2, out_dtype=F32, name="mm_up")
    act = _ffn_act_fwd(up, lw["ffn_conv_w"])
    x_out = _mm_nn(act, lw["w_down"], tn=D_MODEL, out_dtype=F32, name="mm_down", res=x_mid)
    saved = dict(x=x, h=h, p=p, qn=qn, kt=kt, kd=kd, vt=vt, o=o, lse=lse, mix=mix, x_mid=x_mid, h2=h2, up=up, act=act)
    return x_out, saved


def _layer_bwd(dx, s, lw, consts):
    cos, sin, bmat = consts
    T = dx.shape[0]
    tq = min(T, _ATTN_TQ)
    g = {}
    d_act = _mm_nt(dx, lw["w_down"], tkw=D_FF // 2, tn=D_MODEL, name="mm_dact")
    g["w_down"] = _mm_tn(s["act"], dx, tk=D_FF // 2, tn=D_MODEL, name="mm_dwdown")
    d_up, d_cw = _ffn_act_bwd(s["up"], d_act, lw["ffn_conv_w"])
    g["ffn_conv_w"] = d_cw.transpose(1, 0, 2).reshape(3, 2 * D_FF)
    g["w_up"] = _mm_tn(s["h2"], d_up, tk=512, tn=D_FF // 2, name="mm_dwup", shards=N_CHIPS)
    d_h2 = _mm_nt(d_up, lw["w_up"], tkw=D_MODEL, tn=D_FF // 2, name="mm_dh2")
    dx2, g["norm2_g"] = _norm_bwd(s["x_mid"], d_h2, dx, lw["norm2_g"])
    d_mix = _mm_nt(dx2, lw["w_out"], tkw=D_MODEL, tn=D_MODEL, name="mm_dmix")
    g["w_out"] = _mm_tn(s["mix"], dx2, tk=512, tn=D_MODEL, name="mm_dwout")
    dp_c, g["conv_w"] = _convmix_bwd(s["p"], d_mix, lw["conv_w"])
    dp_b, g["sg_w"], d_bias, g["sg_norm_g"] = _sg_bwd(s["p"], d_mix, lw["sg_norm_g"], lw["sg_w"], lw["sg_wt"], lw["sg_bias"])
    g["sg_b"] = d_bias.reshape(SG_CHUNK, SG_W // HEAD_DIM, HEAD_DIM).sum(axis=-1).T
    dq, dkt, dvt = _attn_bwd(s["qn"], s["o"], d_mix, s["lse"], s["kt"], s["kd"], s["vt"], tq)
    dp_a, d_gq, d_gk = _qk_bwd(s["p"], dq, dkt, dvt, lw["gq"], lw["gk"], cos, sin, bmat)
    g["q_norm_g"] = d_gq.reshape(ATTN_W // HEAD_DIM, HEAD_DIM).sum(axis=0)
    g["k_norm_g"] = d_gk.reshape(KV_W // HEAD_DIM, HEAD_DIM).sum(axis=0)
    dp = jnp.concatenate([dp_a, dp_b, dp_c], axis=1)
    g["w_in"] = _mm_tn(s["h"], dp, tk=D_MODEL, tn=512, name="mm_dwin", shards=N_CHIPS)
    d_h = _mm_nt(dp, lw["w_in"], tkw=D_MODEL, tn=512, name="mm_dh")
    dx_in, g["norm1_g"] = _norm_bwd(s["x"], d_h, dx2, lw["norm1_g"])
    return dx_in, g


def _layer_weights(l, full, small):
    sg_w = small["sg_w"][l]
    sg_b = small["sg_b"][l]
    return dict(
        norm1_g=small["norm1_g"][l][None, :], norm2_g=small["norm2_g"][l][None, :],
        gq=jnp.tile(small["q_norm_g"][l], ATTN_W // HEAD_DIM)[None, :],
        gk=jnp.tile(small["k_norm_g"][l], KV_W // HEAD_DIM)[None, :],
        sg_norm_g=small["sg_norm_g"][l][None, :],
        sg_w=sg_w.astype(BF16), sg_wt=sg_w.transpose(0, 2, 1).astype(BF16),
        sg_bias=jnp.repeat(sg_b.T, HEAD_DIM, axis=1),
        conv_w=full["conv_w"][l], ffn_conv_w=full["ffn_conv_w"][l],
        w_in=full["w_in"][l], w_out=full["w_out"][l], w_up=full["w_up"][l], w_down=full["w_down"][l])


_BIG = ("w_in", "w_out", "ffn_w_up", "ffn_w_down")
_SMALL_REPL = ("norm1_g", "q_norm_g", "k_norm_g", "sg_norm_g", "sg_w", "sg_b", "norm2_g")
_SMALL_SHARD = ("conv_w", "ffn_conv_w")
_ORDER = ("norm1_g", "w_in", "q_norm_g", "k_norm_g", "sg_norm_g", "sg_w", "sg_b", "conv_w", "w_out", "norm2_g",
          "ffn_w_up", "ffn_conv_w", "ffn_w_down")


def _reduce_big_grads(layer_grads):
    recv = _send_to_sibling_halves(layer_grads)
    sums = [_add_halves(gr, rc, "add_sibling") for gr, rc in zip(layer_grads, recv)]
    parts = _scatter_to_chips(sums)
    halves = [_sum_chips(pt, "sum_chips") for pt in parts]
    return _join_halves(halves)


def kernel(x, norm1_g, w_in, q_norm_g, k_norm_g, sg_norm_g, sg_w, sg_b, conv_w, w_out, norm2_g, ffn_w_up, ffn_conv_w, ffn_w_down, loss_target, m_norm1_g, m_w_in, m_q_norm_g, m_k_norm_g, m_sg_norm_g, m_sg_w, m_sg_b, m_conv_w, m_w_out, m_norm2_g, m_ffn_w_up, m_ffn_conv_w, m_ffn_w_down, v_norm1_g, v_w_in, v_q_norm_g, v_k_norm_g, v_sg_norm_g, v_sg_w, v_sg_b, v_conv_w, v_w_out, v_norm2_g, v_ffn_w_up, v_ffn_conv_w, v_ffn_w_down):
    w = dict(norm1_g=norm1_g, w_in=w_in, q_norm_g=q_norm_g, k_norm_g=k_norm_g, sg_norm_g=sg_norm_g, sg_w=sg_w,
             sg_b=sg_b, conv_w=conv_w, w_out=w_out, norm2_g=norm2_g, ffn_w_up=ffn_w_up, ffn_conv_w=ffn_conv_w,
             ffn_w_down=ffn_w_down)
    mom = dict(norm1_g=m_norm1_g, w_in=m_w_in, q_norm_g=m_q_norm_g, k_norm_g=m_k_norm_g, sg_norm_g=m_sg_norm_g,
               sg_w=m_sg_w, sg_b=m_sg_b, conv_w=m_conv_w, w_out=m_w_out, norm2_g=m_norm2_g, ffn_w_up=m_ffn_w_up,
               ffn_conv_w=m_ffn_conv_w, ffn_w_down=m_ffn_w_down)
    var = dict(norm1_g=v_norm1_g, w_in=v_w_in, q_norm_g=v_q_norm_g, k_norm_g=v_k_norm_g, sg_norm_g=v_sg_norm_g,
               sg_w=v_sg_w, sg_b=v_sg_b, conv_w=v_conv_w, w_out=v_w_out, norm2_g=v_norm2_g, ffn_w_up=v_ffn_w_up,
               ffn_conv_w=v_ffn_conv_w, ffn_w_down=v_ffn_w_down)
    L = DEPTH
    T = x.shape[1]
    xs = x.reshape(T, D_MODEL)
    target = loss_target.reshape(T, D_MODEL)

    shards = [_cast_bf16(w[n].reshape(-1, w[n].shape[-1]), "cast_" + n).reshape(w[n].shape) for n in _BIG]
    gathered = _gather_shards(shards + [conv_w, ffn_conv_w])
    gw_in, gw_out, gw_up, gw_down, g_conv, g_fconv = gathered
    full = dict(
        w_in=gw_in,
        w_out=gw_out.reshape(L, D_MODEL, D_MODEL),
        w_up=gw_up,
        w_down=gw_down.reshape(L, D_FF, D_MODEL),
        conv_w=g_conv.transpose(0, 2, 1, 3).reshape(L, 3, CONV_W),
        ffn_conv_w=g_fconv.transpose(0, 2, 1, 3).reshape(L, 3, 2 * D_FF))
    consts = _rope_tables(T) + (_head_mean_matrix(),)

    saved, lws = [], []
    act_x = xs
    for l in range(L):
        lw = _layer_weights(l, full, w)
        act_x, s = _layer_fwd(act_x, lw, consts)
        saved.append(s)
        lws.append(lw)
    dx, loss_blk = _loss_head(act_x, target)
    loss = lax.psum(loss_blk[0, 0], ("x", "y", "c"))

    grads = [None] * L
    reduced = [None] * L
    for l in reversed(range(L)):
        dx, g = _layer_bwd(dx, saved[l], lws[l], consts)
        grads[l] = g
        reduced[l] = _reduce_big_grads([g["w_in"], g["w_out"].reshape(N_CHIPS, D_MODEL // N_CHIPS, D_MODEL), g["w_up"],
                                        g["w_down"].reshape(N_CHIPS, D_FF // N_CHIPS, D_MODEL)])
    grad_x = dx.reshape(x.shape)

    small_names = _SMALL_REPL + _SMALL_SHARD
    key = dict(norm1_g="norm1_g", q_norm_g="q_norm_g", k_norm_g="k_norm_g", sg_norm_g="sg_norm_g", sg_w="sg_w",
               sg_b="sg_b", norm2_g="norm2_g", conv_w="conv_w", ffn_conv_w="ffn_conv_w")
    small_local = [jnp.stack([grads[l][key[n]].reshape(-1) for l in range(L)]) for n in small_names]
    small_sum = _unpack(_allreduce_small(_pack(small_local)), small_local)
    grad = {}
    for n, a in zip(small_names, small_sum):
        grad[n] = a
    chip = 2 * lax.axis_index("x") + lax.axis_index("y")
    for n in _SMALL_REPL:
        grad[n] = grad[n].reshape(w[n].shape)
    for n in _SMALL_SHARD:
        full_w = grad[n].reshape(L, 3, -1)
        width = w[n].shape[-1]
        grad[n] = lax.dynamic_slice_in_dim(full_w, chip * width, width, axis=2)
    for i, n in enumerate(_BIG):
        grad[n] = jnp.stack([reduced[l][i] for l in range(L)])

    delta, new_m, new_v = {}, {}, {}
    for n in _BIG:
        shp = w[n].shape
        v2 = lambda a: a.reshape(-1, shp[-1])
        d, mn, vn = _adamw(v2(w[n]), v2(grad[n]), v2(mom[n]), v2(var[n]), "adamw_" + n)
        delta[n], new_m[n], new_v[n] = d.reshape(shp), mn.reshape(shp), vn.reshape(shp)
    for group, gname in ((_SMALL_REPL, "adamw_small"), (_SMALL_SHARD, "adamw_conv")):
        like = [w[n] for n in group]
        outs = _adamw(_pack([w[n] for n in group]), _pack([grad[n] for n in group]), _pack([mom[n] for n in group]),
                      _pack([var[n] for n in group]), gname)
        for res, dst in zip(outs, (delta, new_m, new_v)):
            for n, a in zip(group, _unpack(res, like)):
                dst[n] = a

    return (loss, grad_x, *[grad[n] for n in _ORDER], *[delta[n] for n in _ORDER],
            *[new_m[n] for n in _ORDER], *[new_v[n] for n in _ORDER])
```

```python
import functools

import jax
import jax.numpy as jnp
from jax import lax
from jax.experimental import pallas as pl
from jax.experimental.pallas import tpu as pltpu

F32 = jnp.float32
BF16 = jnp.bfloat16

DEPTH = 4
D_MODEL = 1024
HEAD_DIM = 64
ATTN_W = 512
KV_W = 128
SG_W = 256
CONV_W = 256
SG_CHUNK = 128
D_FF = 2816
PROJ_W = 2048
GRID_W = 64
ROPE_THETA = 10000.0
AXIS_DIM = HEAD_DIM // 2
EPS = 1e-6
N_CHIPS = 4

ADAM_LR = 0.001
ADAM_B1 = 0.9
ADAM_B2 = 0.999
ADAM_EPS = 1e-08
ADAM_WD = 0.01
ADAM_STEP = 10

_ROW_TILE = 512
_FFN_ROW_TILE = 256
LANES = 128
HALO = 8
VMEM_LIMIT_BYTES = 56 * 1024 * 1024
MESH = pl.DeviceIdType.MESH
ANY = pl.BlockSpec(memory_space=pl.ANY)


def _cp(*sem):
    return pltpu.CompilerParams(dimension_semantics=sem if sem else None,
                                vmem_limit_bytes=VMEM_LIMIT_BYTES)


def _sds(shape, dtype):
    return jax.ShapeDtypeStruct(shape, dtype)


def _dot(a, b):
    return jnp.dot(a, b, preferred_element_type=F32)


def _dot_nt(a, b):
    return lax.dot_general(a, b, (((1,), (1,)), ((), ())), preferred_element_type=F32)


def _dot_tn(a, b):
    return lax.dot_general(a, b, (((0,), (0,)), ((), ())), preferred_element_type=F32)


def _norm_fwd(x, g):
    T, Dm = x.shape
    tm = min(T, _ROW_TILE)

    def body(x_ref, g_ref, o_ref):
        xv = x_ref[...]
        r = lax.rsqrt(jnp.mean(xv * xv, axis=-1, keepdims=True) + EPS)
        o_ref[...] = ((xv * r) * g_ref[...]).astype(BF16)

    return pl.pallas_call(
        body, name="norm_fwd", grid=(T // tm,),
        in_specs=[pl.BlockSpec((tm, Dm), lambda i: (i, 0)), pl.BlockSpec((1, Dm), lambda i: (0, 0))],
        out_specs=pl.BlockSpec((tm, Dm), lambda i: (i, 0)),
        out_shape=_sds((T, Dm), BF16), compiler_params=_cp("parallel"))(x, g)


def _norm_bwd(x, dh, dres, g):
    T, Dm = x.shape
    tm = min(T, _ROW_TILE)

    def body(x_ref, dh_ref, dr_ref, g_ref, dx_ref, dg_ref):
        i = pl.program_id(0)
        xv = x_ref[...]
        r = lax.rsqrt(jnp.mean(xv * xv, axis=-1, keepdims=True) + EPS)
        xh = xv * r
        dhv = dh_ref[...]
        dxh = dhv * g_ref[...]
        dx_ref[...] = dr_ref[...] + r * (dxh - xh * jnp.mean(dxh * xh, axis=-1, keepdims=True))
        part = jnp.sum(dhv * xh, axis=0, keepdims=True)

        @pl.when(i == 0)
        def _():
            dg_ref[...] = part

        @pl.when(i > 0)
        def _():
            dg_ref[...] += part

    row = pl.BlockSpec((tm, Dm), lambda i: (i, 0))
    vec = pl.BlockSpec((1, Dm), lambda i: (0, 0))
    return pl.pallas_call(
        body, name="norm_bwd", grid=(T // tm,),
        in_specs=[row, row, row, vec], out_specs=[row, vec],
        out_shape=[_sds((T, Dm), F32), _sds((1, Dm), F32)], compiler_params=_cp("arbitrary"))(x, dh, dres, g)


def _w_spec(w, rows, tn, row_of, col_of):
    if w.ndim == 2:
        return pl.BlockSpec((rows, tn), lambda *g: (row_of(*g), col_of(*g)))
    bps = w.shape[2] // tn
    return pl.BlockSpec((None, rows, tn), lambda *g: (col_of(*g) // bps, row_of(*g), col_of(*g) % bps))


def _mm_nn(a, w, *, tn, out_dtype, name, res=None, tm=512):
    M, K = a.shape
    N = w.shape[-1] if w.ndim == 2 else w.shape[0] * w.shape[2]
    tm = min(M, _ROW_TILE)
    has_res = res is not None

    def body(*refs):
        a_ref, w_ref = refs[0], refs[1]
        o_ref = refs[-1]
        acc = _dot(a_ref[...].astype(BF16), w_ref[...])
        if has_res:
            acc = acc + refs[2][...]
        o_ref[...] = acc.astype(out_dtype)

    in_specs = [pl.BlockSpec((tm, K), lambda j, i: (i, 0)),
                _w_spec(w, K, tn, lambda j, i: 0, lambda j, i: j)]
    args = [a, w]
    if has_res:
        in_specs.append(pl.BlockSpec((tm, tn), lambda j, i: (i, j)))
        args.append(res)
    return pl.pallas_call(
        body, name=name, grid=(N // tn, M // tm), in_specs=in_specs,
        out_specs=pl.BlockSpec((tm, tn), lambda j, i: (i, j)),
        out_shape=_sds((M, N), out_dtype), compiler_params=_cp("parallel", "parallel"))(*args)


def _a_spec(a, tm, tn, row_of, col_of):
    if a.ndim == 2:
        return pl.BlockSpec((tm, tn), lambda *g: (row_of(*g), col_of(*g)))
    bph = a.shape[2] // tn
    return pl.BlockSpec((None, tm, tn), lambda *g: (col_of(*g) // bph, row_of(*g), col_of(*g) % bph))


def _a_cols(a):
    return a.shape[1] if a.ndim == 2 else a.shape[0] * a.shape[2]


def _mm_nt(a, w, *, tkw, tn, name, tm=512):
    M = a.shape[-2]
    N = _a_cols(a)
    Kw = w.shape[-2]
    tm = min(M, _ROW_TILE)
    nred = N // tn

    def body(a_ref, w_ref, o_ref, *scratch):
        part = _dot_nt(a_ref[...].astype(BF16), w_ref[...])
        if nred == 1:
            o_ref[...] = part
            return
        acc_ref = scratch[0]
        n = pl.program_id(2)

        @pl.when(n == 0)
        def _():
            acc_ref[...] = part

        @pl.when(n > 0)
        def _():
            acc_ref[...] += part

        @pl.when(n == nred - 1)
        def _():
            o_ref[...] = acc_ref[...]

    return pl.pallas_call(
        body, name=name, grid=(M // tm, Kw // tkw, nred),
        in_specs=[_a_spec(a, tm, tn, lambda i, k, n: i, lambda i, k, n: n),
                  _w_spec(w, tkw, tn, lambda i, k, n: k, lambda i, k, n: n)],
        out_specs=pl.BlockSpec((tm, tkw), lambda i, k, n: (i, k)),
        out_shape=_sds((M, Kw), F32),
        scratch_shapes=[] if nred == 1 else [pltpu.VMEM((tm, tkw), F32)],
        compiler_params=_cp("parallel", "parallel", "arbitrary"))(a, w)


def _mm_tn(a, b, *, tk, tn, name, shards=None, tm=512):
    M, K = a.shape
    N = _a_cols(b)
    tm = min(M, _ROW_TILE)

    def body(a_ref, b_ref, o_ref):
        m = pl.program_id(2)
        part = _dot_tn(a_ref[...].astype(BF16), b_ref[...].astype(BF16))

        @pl.when(m == 0)
        def _():
            o_ref[...] = part

        @pl.when(m > 0)
        def _():
            o_ref[...] += part

    if shards is None:
        out_spec = pl.BlockSpec((tk, tn), lambda k, j, m: (k, j))
        out_shape = _sds((K, N), F32)
    else:
        bps = (N // shards) // tn
        out_spec = pl.BlockSpec((None, tk, tn), lambda k, j, m: (j // bps, k, j % bps))
        out_shape = _sds((shards, K, N // shards), F32)
    return pl.pallas_call(
        body, name=name, grid=(K // tk, N // tn, M // tm),
        in_specs=[pl.BlockSpec((tm, tk), lambda k, j, m: (m, k)),
                  _a_spec(b, tm, tn, lambda k, j, m: m, lambda k, j, m: j)],
        out_specs=out_spec, out_shape=out_shape,
        compiler_params=_cp("parallel", "parallel", "arbitrary"))(a, b)


def _halo_specs(T, tm, cw, ic):
    nb = tm // HALO
    last = T // HALO - 1

    def mk(rows, row_of):
        return pl.BlockSpec((rows, cw), lambda *g: (row_of(ic(*g)[0]), ic(*g)[1]))

    return [mk(HALO, lambda i: jnp.maximum(i * nb - 1, 0)), mk(tm, lambda i: i),
            mk(HALO, lambda i: jnp.minimum((i + 1) * nb, last))]


def _ext(prev_ref, cur_ref, next_ref, i, n):
    p = jnp.where(i > 0, prev_ref[...].astype(F32), 0.0)
    nx = jnp.where(i < n - 1, next_ref[...].astype(F32), 0.0)
    return jnp.concatenate([p, cur_ref[...].astype(F32), nx], axis=0)


def _dn(e):
    return pltpu.roll(e, 1, 0)


def _up(e):
    return pltpu.roll(e, e.shape[0] - 1, 0)


def _mid(e):
    return e[HALO:e.shape[0] - HALO]


def _conv3(e, w):
    return _dn(e) * w[0:1] + e * w[1:2] + _up(e) * w[2:3]


def _conv3_t(e, w):
    return _up(e) * w[0:1] + e * w[1:2] + _dn(e) * w[2:3]


def _conv3_wgrad(d, e):
    return jnp.concatenate([jnp.sum(_mid(d * _dn(e)), axis=0, keepdims=True),
                            jnp.sum(_mid(d * e), axis=0, keepdims=True),
                            jnp.sum(_mid(d * _up(e)), axis=0, keepdims=True)], axis=0)


def _sigmoid(x):
    return 1.0 / (1.0 + jnp.exp(-x))


def _accum(ref, i, part):
    @pl.when(i == 0)
    def _():
        ref[...] = part

    @pl.when(i > 0)
    def _():
        ref[...] += part


def _ffn_act_fwd(up, cw):
    T = up.shape[0]
    tm = min(T, _FFN_ROW_TILE)
    cb = D_FF // 2
    nblk = D_FF // cb
    n = T // tm

    def body(gp, gc, gn, vp, vc, vn, wg_ref, wv_ref, o_ref):
        i = pl.program_id(1)
        gate = _conv3(_ext(gp, gc, gn, i, n), wg_ref[...])
        val = _conv3(_ext(vp, vc, vn, i, n), wv_ref[...])
        o_ref[...] = _mid(gate * _sigmoid(gate) * val).astype(BF16)

    return pl.pallas_call(
        body, name="ffn_act_fwd", grid=(nblk, n),
        in_specs=_halo_specs(T, tm, cb, lambda j, i: (i, j)) + _halo_specs(T, tm, cb, lambda j, i: (i, j + nblk))
        + [pl.BlockSpec((3, cb), lambda j, i: (0, j)), pl.BlockSpec((3, cb), lambda j, i: (0, j + nblk))],
        out_specs=pl.BlockSpec((tm, cb), lambda j, i: (i, j)),
        out_shape=_sds((T, D_FF), BF16), compiler_params=_cp("parallel", "parallel"))(
            up, up, up, up, up, up, cw, cw)


def _ffn_act_bwd(up, dact, cw):
    T = up.shape[0]
    tm = min(T, _FFN_ROW_TILE)
    cb = D_FF // 2
    nblk = D_FF // cb
    n = T // tm

    def body(gp, gc, gn, vp, vc, vn, dp_, dc, dn_, wg_ref, wv_ref, dup_ref, dcw_ref):
        i = pl.program_id(1)
        wg, wv = wg_ref[...], wv_ref[...]
        eg = _ext(gp, gc, gn, i, n)
        ev = _ext(vp, vc, vn, i, n)
        ed = _ext(dp_, dc, dn_, i, n)
        gate = _conv3(eg, wg)
        val = _conv3(ev, wv)
        sg = _sigmoid(gate)
        d_gate = ed * val * (sg * (1.0 + gate * (1.0 - sg)))
        d_val = ed * (gate * sg)
        dup_ref[0] = _mid(_conv3_t(d_gate, wg)).astype(BF16)
        dup_ref[1] = _mid(_conv3_t(d_val, wv)).astype(BF16)
        part = jnp.stack([_conv3_wgrad(d_gate, eg), _conv3_wgrad(d_val, ev)], axis=0)
        _accum(dcw_ref, i, part)

    return pl.pallas_call(
        body, name="ffn_act_bwd", grid=(nblk, n),
        in_specs=_halo_specs(T, tm, cb, lambda j, i: (i, j)) + _halo_specs(T, tm, cb, lambda j, i: (i, j + nblk))
        + _halo_specs(T, tm, cb, lambda j, i: (i, j))
        + [pl.BlockSpec((3, cb), lambda j, i: (0, j)), pl.BlockSpec((3, cb), lambda j, i: (0, j + nblk))],
        out_specs=[pl.BlockSpec((2, tm, cb), lambda j, i: (0, i, j)),
                   pl.BlockSpec((2, 3, cb), lambda j, i: (0, 0, j))],
        out_shape=[_sds((2, T, D_FF), BF16), _sds((2, 3, D_FF), F32)],
        compiler_params=_cp("parallel", "arbitrary"))(up, up, up, up, up, up, dact, dact, dact, cw, cw)


_CB_BLK, _CC_BLK, _CX_BLK = 5, 6, 7


def _convmix_fwd(p, w):
    T = p.shape[0]
    tm = min(T, _ROW_TILE)
    n = T // tm

    def body(cb_ref, ccp, ccc, ccn, cxp, cxc, cxn, w_ref, o_ref):
        i = pl.program_id(0)
        z = _ext(ccp, ccc, ccn, i, n) * _ext(cxp, cxc, cxn, i, n)
        o_ref[...] = (cb_ref[...] * _mid(_conv3(z, w_ref[...]))).astype(BF16)

    return pl.pallas_call(
        body, name="convmix_fwd", grid=(n,),
        in_specs=[pl.BlockSpec((tm, CONV_W), lambda i: (i, _CB_BLK))]
        + _halo_specs(T, tm, CONV_W, lambda i: (i, _CC_BLK)) + _halo_specs(T, tm, CONV_W, lambda i: (i, _CX_BLK))
        + [pl.BlockSpec((3, CONV_W), lambda i: (0, 0))],
        out_specs=pl.BlockSpec((tm, CONV_W), lambda i: (i, 0)),
        out_shape=_sds((T, CONV_W), BF16), compiler_params=_cp("parallel"))(p, p, p, p, p, p, p, w)


def _convmix_bwd(p, dmix, w):
    T = p.shape[0]
    tm = min(T, _ROW_TILE)
    n = T // tm
    dblk = (ATTN_W + SG_W) // CONV_W

    def body(cbp, cbc, cbn, ccp, ccc, ccn, cxp, cxc, cxn, dp_, dc, dn_, w_ref, o_ref, dw_ref):
        i = pl.program_id(0)
        wv = w_ref[...]
        ecb = _ext(cbp, cbc, cbn, i, n)
        ecc = _ext(ccp, ccc, ccn, i, n)
        ecx = _ext(cxp, cxc, cxn, i, n)
        ed = _ext(dp_, dc, dn_, i, n)
        z = ecc * ecx
        d_cz = ed * ecb
        d_z = _conv3_t(d_cz, wv)
        o_ref[...] = jnp.concatenate([_mid(ed * _conv3(z, wv)), _mid(d_z * ecx), _mid(d_z * ecc)],
                                     axis=1).astype(BF16)
        _accum(dw_ref, i, _conv3_wgrad(d_cz, z))

    return pl.pallas_call(
        body, name="convmix_bwd", grid=(n,),
        in_specs=_halo_specs(T, tm, CONV_W, lambda i: (i, _CB_BLK)) + _halo_specs(T, tm, CONV_W, lambda i: (i, _CC_BLK))
        + _halo_specs(T, tm, CONV_W, lambda i: (i, _CX_BLK)) + _halo_specs(T, tm, CONV_W, lambda i: (i, dblk))
        + [pl.BlockSpec((3, CONV_W), lambda i: (0, 0))],
        out_specs=[pl.BlockSpec((tm, 3 * CONV_W), lambda i: (i, 0)), pl.BlockSpec((3, CONV_W), lambda i: (0, 0))],
        out_shape=[_sds((T, 3 * CONV_W), BF16), _sds((3, CONV_W), F32)],
        compiler_params=_cp("arbitrary"))(p, p, p, p, p, p, p, p, p, dmix, dmix, dmix, w)


_SU_BLK, _SV_BLK = 3, 4


def _sg_mixed(vnb, w_ref, bias, ch, pr, lo):
    vp = vnb[ch * SG_CHUNK:(ch + 1) * SG_CHUNK, pr * LANES:(pr + 1) * LANES]
    zero = jnp.zeros_like(vp)
    return (_dot(w_ref[2 * pr], jnp.where(lo, vp, zero)) + _dot(w_ref[2 * pr + 1], jnp.where(lo, zero, vp))
            + bias[:, pr * LANES:(pr + 1) * LANES]), vp


def _sg_fwd(p, g, w, bias):
    T = p.shape[0]
    tm = min(T, _ROW_TILE)

    def body(su_ref, sv_ref, g_ref, w_ref, b_ref, o_ref):
        lo = lax.broadcasted_iota(jnp.int32, (SG_CHUNK, LANES), 1) < HEAD_DIM
        sv = sv_ref[...]
        r = lax.rsqrt(jnp.mean(sv * sv, axis=-1, keepdims=True) + EPS)
        vnb = ((sv * r) * g_ref[...]).astype(BF16)
        bias_v = b_ref[...]
        for ch in range(tm // SG_CHUNK):
            for pr in range(2):
                mixed, _ = _sg_mixed(vnb, w_ref, bias_v, ch, pr, lo)
                rows, cols = pl.ds(ch * SG_CHUNK, SG_CHUNK), pl.ds(pr * LANES, LANES)
                o_ref[rows, cols] = (su_ref[rows, cols] * mixed).astype(BF16)

    return pl.pallas_call(
        body, name="sg_fwd", grid=(T // tm,),
        in_specs=[pl.BlockSpec((tm, SG_W), lambda i: (i, _SU_BLK)), pl.BlockSpec((tm, SG_W), lambda i: (i, _SV_BLK)),
                  pl.BlockSpec((1, SG_W), lambda i: (0, 0)), pl.BlockSpec((4, SG_CHUNK, SG_CHUNK), lambda i: (0, 0, 0)),
                  pl.BlockSpec((SG_CHUNK, SG_W), lambda i: (0, 0))],
        out_specs=pl.BlockSpec((tm, SG_W), lambda i: (i, 0)),
        out_shape=_sds((T, SG_W), BF16), compiler_params=_cp("parallel"))(p, p, g, w, bias)


def _sg_bwd(p, dmix, g, w, wt, bias):
    T = p.shape[0]
    tm = min(T, _ROW_TILE)
    dblk = ATTN_W // SG_W

    def body(su_ref, sv_ref, d_ref, g_ref, w_ref, wt_ref, b_ref, o_ref, dw_ref, db_ref, dg_ref, dvn_ref):
        i = pl.program_id(0)
        lo = lax.broadcasted_iota(jnp.int32, (SG_CHUNK, LANES), 1) < HEAD_DIM
        sv = sv_ref[...]
        gv = g_ref[...]
        r = lax.rsqrt(jnp.mean(sv * sv, axis=-1, keepdims=True) + EPS)
        xh = sv * r
        vnb = (xh * gv).astype(BF16)
        bias_v = b_ref[...]
        dw = [jnp.zeros((SG_CHUNK, SG_CHUNK), F32) for _ in range(4)]
        db = jnp.zeros((SG_CHUNK, SG_W), F32)
        for ch in range(tm // SG_CHUNK):
            dbs = []
            for pr in range(2):
                mixed, vp = _sg_mixed(vnb, w_ref, bias_v, ch, pr, lo)
                rows, cols = pl.ds(ch * SG_CHUNK, SG_CHUNK), pl.ds(pr * LANES, LANES)
                dgo = d_ref[rows, cols]
                o_ref[rows, cols] = (dgo * mixed).astype(BF16)
                dm = dgo * su_ref[rows, cols]
                dmb = dm.astype(BF16)
                zero = jnp.zeros_like(dmb)
                dw[2 * pr] += _dot_nt(jnp.where(lo, dmb, zero), vp)
                dw[2 * pr + 1] += _dot_nt(jnp.where(lo, zero, dmb), vp)
                dvn_ref[rows, cols] = jnp.where(lo, _dot(wt_ref[2 * pr], dmb), _dot(wt_ref[2 * pr + 1], dmb))
                dbs.append(dm)
            db += jnp.concatenate(dbs, axis=1)
        dvn = dvn_ref[...]
        dxh = dvn * gv
        o_ref[:, pl.ds(SG_W, SG_W)] = (r * (dxh - xh * jnp.mean(dxh * xh, axis=-1, keepdims=True))).astype(BF16)
        _accum(dw_ref, i, jnp.stack(dw, axis=0))
        _accum(db_ref, i, db)
        _accum(dg_ref, i, jnp.sum(dvn * xh, axis=0, keepdims=True))

    wspec = pl.BlockSpec((4, SG_CHUNK, SG_CHUNK), lambda i: (0, 0, 0))
    return pl.pallas_call(
        body, name="sg_bwd", grid=(T // tm,),
        in_specs=[pl.BlockSpec((tm, SG_W), lambda i: (i, _SU_BLK)), pl.BlockSpec((tm, SG_W), lambda i: (i, _SV_BLK)),
                  pl.BlockSpec((tm, SG_W), lambda i: (i, dblk)), pl.BlockSpec((1, SG_W), lambda i: (0, 0)),
                  wspec, wspec, pl.BlockSpec((SG_CHUNK, SG_W), lambda i: (0, 0))],
        out_specs=[pl.BlockSpec((tm, 2 * SG_W), lambda i: (i, 0)), wspec,
                   pl.BlockSpec((SG_CHUNK, SG_W), lambda i: (0, 0)), pl.BlockSpec((1, SG_W), lambda i: (0, 0))],
        out_shape=[_sds((T, 2 * SG_W), BF16), _sds((4, SG_CHUNK, SG_CHUNK), F32),
                   _sds((SG_CHUNK, SG_W), F32), _sds((1, SG_W), F32)],
        scratch_shapes=[pltpu.VMEM((tm, SG_W), F32)],
        compiler_params=_cp("arbitrary"))(p, p, dmix, g, w, wt, bias)


_ATTN_TQ = 256
_ATTN_TK = 512
_SOFTMAX_STRIP = 32


def _head_mean(v, bmat):
    return jnp.dot(v, bmat, preferred_element_type=F32, precision=lax.Precision.HIGHEST)


def _swap16(y):
    lane = lax.broadcasted_iota(jnp.int32, y.shape, 1)
    return jnp.where(lane % 32 < 16, pltpu.roll(y, y.shape[1] - 16, 1), pltpu.roll(y, 16, 1))


def _rope(y, cos, sin):
    return y * cos + _swap16(y) * sin


def _rope_t(dy, cos, sin):
    return dy * cos + _swap16(dy * sin)


def _dup_rows(t, gidx):
    h = t[gidx * HEAD_DIM:(gidx + 1) * HEAD_DIM]
    return jnp.concatenate([h, h], axis=0)


def _qk_prep(p, gq, gk, cos, sin, bmat, tk):
    T = p.shape[0]
    nk = T // tk
    scale = HEAD_DIM ** -0.5

    def body(q_ref, kv_ref, gq_ref, gk_ref, cos_ref, sin_ref, b_ref, qo_ref, kt_ref, kd_ref, vt_ref):
        cosv, sinv, bm = cos_ref[...], sin_ref[...], b_ref[...]
        for pr in range(ATTN_W // LANES):
            cols = pl.ds(pr * LANES, LANES)
            xq = q_ref[:, cols]
            r = lax.rsqrt(_head_mean(xq * xq, bm) + EPS)
            qo_ref[:, cols] = (_rope((xq * r) * gq_ref[:, cols], cosv, sinv) * scale).astype(BF16)
        xk = kv_ref[:, pl.ds(0, LANES)]
        r = lax.rsqrt(_head_mean(xk * xk, bm) + EPS)
        kt = _rope((xk * r) * gk_ref[...], cosv, sinv).T
        vt = kv_ref[:, pl.ds(LANES, LANES)].T
        for gidx in range(2):
            kdup = _dup_rows(kt, gidx)
            kt_ref[gidx] = kdup.astype(BF16)
            vt_ref[gidx] = _dup_rows(vt, gidx).astype(BF16)
            kd_ref[gidx] = kdup.T.astype(BF16)

    tspec = pl.BlockSpec((2, None, LANES, tk), lambda i: (0, i, 0, 0))
    dspec = pl.BlockSpec((2, tk, LANES), lambda i: (0, i, 0))
    tab = pl.BlockSpec((tk, LANES), lambda i: (i, 0))
    return pl.pallas_call(
        body, name="qk_prep", grid=(nk,),
        in_specs=[pl.BlockSpec((tk, ATTN_W), lambda i: (i, 0)), pl.BlockSpec((tk, 2 * KV_W), lambda i: (i, ATTN_W // (2 * KV_W))),
                  pl.BlockSpec((1, ATTN_W), lambda i: (0, 0)), pl.BlockSpec((1, KV_W), lambda i: (0, 0)),
                  tab, tab, pl.BlockSpec((LANES, LANES), lambda i: (0, 0))],
        out_specs=[pl.BlockSpec((tk, ATTN_W), lambda i: (i, 0)), tspec, dspec, tspec],
        out_shape=[_sds((T, ATTN_W), BF16), _sds((2, nk, LANES, tk), BF16), _sds((2, T, LANES), BF16),
                   _sds((2, nk, LANES, tk), BF16)],
        compiler_params=_cp("parallel"))(p, p, gq, gk, cos, sin, bmat)


def _stack_heads(t):
    lo = lax.broadcasted_iota(jnp.int32, (t.shape[0], LANES), 1) < HEAD_DIM
    parts = []
    for pr in range(2):
        tp = t[:, pr * LANES:(pr + 1) * LANES]
        zero = jnp.zeros_like(tp)
        parts += [jnp.where(lo, tp, zero), jnp.where(lo, zero, tp)]
    return jnp.concatenate(parts, axis=0)


def _unstack_heads(s, tq):
    lo = lax.broadcasted_iota(jnp.int32, (tq, LANES), 1) < HEAD_DIM
    return jnp.concatenate([jnp.where(lo, s[0:tq], s[tq:2 * tq]),
                            jnp.where(lo, s[2 * tq:3 * tq], s[3 * tq:4 * tq])], axis=1)


def _rows8_reduce(s, op):
    parts = [s[r:r + 8] for r in range(0, s.shape[0], 8)]
    while len(parts) > 1:
        parts = [op(parts[k], parts[k + 1]) for k in range(0, len(parts) - 1, 2)] + (
            [parts[-1]] if len(parts) % 2 else [])
    return parts[0]


def _attn_fwd(q, kd, vt, tq):
    T = q.shape[0]
    nk, tk = vt.shape[1], vt.shape[3]
    nq = T // tq
    sq = 4 * tq
    strip = _SOFTMAX_STRIP
    assert nk % 2 == 0, "key blocks are taken two per loop trip"

    def body(q_ref, kd_ref, vt_ref, o_ref, lse_ref, qst_ref, sa_ref, sb_ref, pa_ref, pb_ref, m_ref, l_ref, acc_ref):
        qst_ref[...] = _stack_heads(q_ref[...]).astype(F32).T.astype(BF16)
        m_ref[...] = jnp.full((1, sq), -jnp.inf, F32)
        l_ref[...] = jnp.zeros((1, sq), F32)
        acc_ref[...] = jnp.zeros((HEAD_DIM, sq), F32)

        def scores(j):
            return _dot(kd_ref[pl.ds(pl.multiple_of(j * tk, tk), tk), :], qst_ref[...])

        def softmax_pv(s_ref, p_ref, j):
            m8 = None
            for c in range(tk // strip):
                part = _rows8_reduce(s_ref[pl.ds(c * strip, strip), :], jnp.maximum)
                m8 = part if m8 is None else jnp.maximum(m8, part)
            m_old = m_ref[...]
            m_new = jnp.maximum(m_old, jnp.max(m8, axis=0, keepdims=True))
            alpha = jnp.exp(m_old - m_new)
            m_ref[...] = m_new
            l8 = jnp.zeros((8, sq), F32)
            for c in range(tk // strip):
                rows = pl.ds(c * strip, strip)
                pexp = jnp.exp(s_ref[rows, :] - m_new)
                l8 = l8 + _rows8_reduce(pexp, jnp.add)
                p_ref[rows, :] = pexp.astype(BF16)
            l_ref[...] = alpha * l_ref[...] + jnp.sum(l8, axis=0, keepdims=True)
            acc_ref[...] = alpha * acc_ref[...] + _dot(vt_ref[j, pl.ds(0, HEAD_DIM), :], p_ref[...])

        sa_ref[...] = scores(0)

        def pair(t, carry):
            j = 2 * t
            sb_ref[...] = scores(j + 1)
            softmax_pv(sa_ref, pa_ref, j)
            sa_ref[...] = scores(jnp.minimum(j + 2, nk - 1))
            softmax_pv(sb_ref, pb_ref, j + 1)
            return carry

        lax.fori_loop(0, nk // 2, pair, 0)
        l = l_ref[...]
        on = acc_ref[...] / l
        pairs = []
        for pr in range(2):
            two = jnp.concatenate([on[:, (2 * pr) * tq:(2 * pr + 1) * tq], on[:, (2 * pr + 1) * tq:(2 * pr + 2) * tq]],
                                  axis=0)
            pairs.append(two.T)
        o_ref[...] = jnp.concatenate(pairs, axis=1).astype(BF16)
        lse_ref[...] = jnp.broadcast_to(m_ref[...] + jnp.log(l), (LANES, sq)).T

    row = pltpu.VMEM((1, sq), F32)
    return pl.pallas_call(
        body, name="attn_fwd", grid=(2, nq),
        in_specs=[pl.BlockSpec((tq, 2 * LANES), lambda g, i: (i, g)),
                  pl.BlockSpec((None, T, LANES), lambda g, i: (g, 0, 0)),
                  pl.BlockSpec((None, nk, LANES, tk), lambda g, i: (g, 0, 0, 0))],
        out_specs=[pl.BlockSpec((tq, 2 * LANES), lambda g, i: (i, g)),
                   pl.BlockSpec((None, None, sq, LANES), lambda g, i: (g, i, 0, 0))],
        out_shape=[_sds((T, ATTN_W), BF16), _sds((2, nq, sq, LANES), F32)],
        scratch_shapes=[pltpu.VMEM((LANES, sq), BF16), pltpu.VMEM((tk, sq), F32), pltpu.VMEM((tk, sq), F32),
                        pltpu.VMEM((tk, sq), BF16), pltpu.VMEM((tk, sq), BF16), row, row,
                        pltpu.VMEM((HEAD_DIM, sq), F32)],
        compiler_params=_cp("parallel", "parallel"))(q, kd, vt)


def _attn_bwd(q, o, dmix, lse, kt, kd, vt, tq):
    T = q.shape[0]
    nk, tk = kt.shape[1], kt.shape[3]
    nq = T // tq
    sq = 4 * tq
    rep = tk // LANES

    def body(q_ref, o_ref, do_ref, lse_ref, kt_ref, kd_ref, vt_ref, dq_ref, dkt_ref, dvt_ref):
        i = pl.program_id(1)
        qs = _stack_heads(q_ref[...])
        dof = _stack_heads(do_ref[...])
        dos = dof.astype(BF16)
        qst = qs.astype(F32).T.astype(BF16)
        dost = dof.T.astype(BF16)
        o_pair = o_ref[...].astype(F32)
        os_ = jnp.concatenate([o_pair[:, 0:LANES], o_pair[:, 0:LANES], o_pair[:, LANES:], o_pair[:, LANES:]], axis=0)
        delta = jnp.sum(dof * os_, axis=-1, keepdims=True)
        lse_t = jnp.concatenate([lse_ref[...]] * rep, axis=1)

        @pl.when(i == 0)
        def _():
            dkt_ref[...] = jnp.zeros_like(dkt_ref)
            dvt_ref[...] = jnp.zeros_like(dvt_ref)

        def step(j, dq):
            kdb = kd_ref[pl.ds(pl.multiple_of(j * tk, tk), tk), :]
            pexp = jnp.exp(_dot(qs, kt_ref[j]) - lse_t)
            ds = pexp * (_dot(dos, vt_ref[j]) - delta)
            pb = pexp.astype(BF16)
            dsb = ds.astype(BF16)
            dvt_ref[j] += _dot(dost, pb)
            dkt_ref[j] += _dot(qst, dsb)
            return dq + _dot(dsb, kdb)

        dq = lax.fori_loop(0, nk, step, jnp.zeros((sq, LANES), F32))
        dq_ref[...] = _unstack_heads(dq, tq)

    tspec = pl.BlockSpec((None, nk, LANES, tk), lambda g, i: (g, 0, 0, 0))
    qspec = pl.BlockSpec((tq, 2 * LANES), lambda g, i: (i, g))
    return pl.pallas_call(
        body, name="attn_bwd", grid=(2, nq),
        in_specs=[qspec, qspec, qspec, pl.BlockSpec((None, None, sq, LANES), lambda g, i: (g, i, 0, 0)),
                  tspec, pl.BlockSpec((None, T, LANES), lambda g, i: (g, 0, 0)), tspec],
        out_specs=[qspec, tspec, tspec],
        out_shape=[_sds((T, ATTN_W), F32), _sds((2, nk, LANES, tk), F32), _sds((2, nk, LANES, tk), F32)],
        compiler_params=_cp("parallel", "arbitrary"))(q, o, dmix, lse, kt, kd, vt)


def _fold_t(t_ref):
    rows = []
    for gidx in range(2):
        t = t_ref[gidx]
        rows.append(t[0:HEAD_DIM] + t[HEAD_DIM:2 * HEAD_DIM])
    return jnp.concatenate(rows, axis=0).T


def _qk_bwd(p, dq, dkt, dvt, gq, gk, cos, sin, bmat):
    T = p.shape[0]
    nk, tk = dkt.shape[1], dkt.shape[3]
    scale = HEAD_DIM ** -0.5

    def norm_bwd(x, dy, gain, bm):
        r = lax.rsqrt(_head_mean(x * x, bm) + EPS)
        xh = x * r
        dxh = dy * gain
        return r * (dxh - xh * _head_mean(dxh * xh, bm)), jnp.sum(dy * xh, axis=0, keepdims=True)

    def body(q_ref, kv_ref, dq_ref, dkt_ref, dvt_ref, gq_ref, gk_ref, cos_ref, sin_ref, b_ref, o_ref, dgq_ref, dgk_ref):
        i = pl.program_id(0)
        cosv, sinv, bm = cos_ref[...], sin_ref[...], b_ref[...]
        dgq = []
        for pr in range(ATTN_W // LANES):
            cols = pl.ds(pr * LANES, LANES)
            dy = _rope_t(dq_ref[:, cols] * scale, cosv, sinv)
            dx, dg = norm_bwd(q_ref[:, cols], dy, gq_ref[:, cols], bm)
            o_ref[:, cols] = dx.astype(BF16)
            dgq.append(dg)
        dy = _rope_t(_fold_t(dkt_ref), cosv, sinv)
        dx, dgk = norm_bwd(kv_ref[:, pl.ds(0, LANES)], dy, gk_ref[...], bm)
        o_ref[:, pl.ds(ATTN_W, LANES)] = dx.astype(BF16)
        o_ref[:, pl.ds(ATTN_W + LANES, LANES)] = _fold_t(dvt_ref).astype(BF16)
        _accum(dgq_ref, i, jnp.concatenate(dgq, axis=1))
        _accum(dgk_ref, i, dgk)

    tspec = pl.BlockSpec((2, None, LANES, tk), lambda i: (0, i, 0, 0))
    tab = pl.BlockSpec((tk, LANES), lambda i: (i, 0))
    return pl.pallas_call(
        body, name="qk_bwd", grid=(nk,),
        in_specs=[pl.BlockSpec((tk, ATTN_W), lambda i: (i, 0)), pl.BlockSpec((tk, 2 * KV_W), lambda i: (i, ATTN_W // (2 * KV_W))),
                  pl.BlockSpec((tk, ATTN_W), lambda i: (i, 0)), tspec, tspec,
                  pl.BlockSpec((1, ATTN_W), lambda i: (0, 0)), pl.BlockSpec((1, KV_W), lambda i: (0, 0)),
                  tab, tab, pl.BlockSpec((LANES, LANES), lambda i: (0, 0))],
        out_specs=[pl.BlockSpec((tk, ATTN_W + 2 * KV_W), lambda i: (i, 0)),
                   pl.BlockSpec((1, ATTN_W), lambda i: (0, 0)), pl.BlockSpec((1, KV_W), lambda i: (0, 0))],
        out_shape=[_sds((T, ATTN_W + 2 * KV_W), BF16), _sds((1, ATTN_W), F32), _sds((1, KV_W), F32)],
        compiler_params=_cp("arbitrary"))(p, p, dq, dkt, dvt, gq, gk, cos, sin, bmat)


def _loss_head(y, target):
    T, Dm = y.shape
    tm = min(T, _ROW_TILE)

    def body(y_ref, t_ref, dy_ref, l_ref):
        i = pl.program_id(0)
        err = y_ref[...] - t_ref[...]
        dy_ref[...] = err * (1.0 / Dm)
        part = jnp.sum(jnp.sum(err * err, axis=-1, keepdims=True), axis=0, keepdims=True) * (0.5 / Dm)
        _accum(l_ref, i, jnp.broadcast_to(part, (8, LANES)))

    row = pl.BlockSpec((tm, Dm), lambda i: (i, 0))
    return pl.pallas_call(
        body, name="loss_head", grid=(T // tm,), in_specs=[row, row],
        out_specs=[row, pl.BlockSpec((8, LANES), lambda i: (0, 0))],
        out_shape=[_sds((T, Dm), F32), _sds((8, LANES), F32)], compiler_params=_cp("arbitrary"))(y, target)


def _adamw(w, g, m, v, name):
    R, C = w.shape
    tr = R
    for cand in (512, 256, 128, 64, 32, 16, 8):
        if R % cand == 0:
            tr = cand
            break
    c1 = 1.0 - ADAM_B1 ** ADAM_STEP
    c2 = 1.0 - ADAM_B2 ** ADAM_STEP

    def body(w_ref, g_ref, m_ref, v_ref, d_ref, mo_ref, vo_ref):
        gv = g_ref[...]
        mn = ADAM_B1 * m_ref[...] + (1.0 - ADAM_B1) * gv
        vn = ADAM_B2 * v_ref[...] + (1.0 - ADAM_B2) * (gv * gv)
        d_ref[...] = -ADAM_LR * ((mn / c1) / (jnp.sqrt(vn / c2) + ADAM_EPS) + ADAM_WD * w_ref[...])
        mo_ref[...] = mn
        vo_ref[...] = vn

    blk = pl.BlockSpec((tr, C), lambda i: (i, 0))
    return pl.pallas_call(
        body, name=name, grid=(R // tr,), in_specs=[blk] * 4, out_specs=[blk] * 3,
        out_shape=[_sds((R, C), F32)] * 3, compiler_params=_cp("parallel"))(w, g, m, v)


def _cast_bf16(w, name):
    R, C = w.shape
    tr = 512 if R % 512 == 0 else 256

    def body(w_ref, o_ref):
        o_ref[...] = w_ref[...].astype(BF16)

    blk = pl.BlockSpec((tr, C), lambda i: (i, 0))
    return pl.pallas_call(body, name=name, grid=(R // tr,), in_specs=[blk], out_specs=blk,
                          out_shape=_sds((R, C), BF16), compiler_params=_cp("parallel"))(w)


def _add_halves(g, recv, name):
    S, R, C = g.shape
    half = R // 2

    def body(g_ref, r_ref, o_ref):
        c = lax.axis_index("c")
        o_ref[...] = g_ref[pl.ds(pl.multiple_of(c * half, 8), half), :] + r_ref[...]

    return pl.pallas_call(
        body, name=name, grid=(S,),
        in_specs=[pl.BlockSpec((None, R, C), lambda s: (s, 0, 0)), pl.BlockSpec((None, half, C), lambda s: (s, 0, 0))],
        out_specs=pl.BlockSpec((None, half, C), lambda s: (s, 0, 0)),
        out_shape=_sds((S, half, C), F32), compiler_params=_cp("parallel"))(g, recv)


def _sum_chips(parts, name):
    S, R, C = parts.shape
    tr = R
    for cand in (256, 128, 64, 32, 16, 8):
        if R % cand == 0:
            tr = cand
            break

    def body(p_ref, o_ref):
        o_ref[...] = ((p_ref[0] + p_ref[1]) + p_ref[2]) + p_ref[3]

    return pl.pallas_call(
        body, name=name, grid=(R // tr,),
        in_specs=[pl.BlockSpec((S, tr, C), lambda i: (0, i, 0))], out_specs=pl.BlockSpec((tr, C), lambda i: (i, 0)),
        out_shape=_sds((R, C), F32), compiler_params=_cp("parallel"))(parts)


def _position():
    x, y, c = lax.axis_index("x"), lax.axis_index("y"), lax.axis_index("c")
    return x, y, c


def _other_chips(x, y):
    return [(1 - x, y), (x, 1 - y), (1 - x, 1 - y)]


def _gather_shards(shards):
    n = len(shards)

    def body(*refs):
        ins, outs = refs[:n], refs[n:2 * n]
        send_sems, recv_sems, local_sems = refs[2 * n:]
        x, y, c = _position()
        me = 2 * x + y
        chips = _other_chips(x, y)
        copies = []
        for t in range(n):
            own = pltpu.make_async_copy(ins[t], outs[t].at[:, me], local_sems.at[t])
            own.start()
            copies.append(own)
        sends = []
        for t in range(n):
            for k, (px, py) in enumerate(chips):
                cp = pltpu.make_async_remote_copy(
                    src_ref=ins[t], dst_ref=outs[t].at[:, me], send_sem=send_sems.at[t, k], recv_sem=recv_sems.at[t, k],
                    device_id=(px, py, c), device_id_type=MESH)
                cp.start()
                sends.append(cp)
        for t in range(n):
            for k, (px, py) in enumerate(chips):
                pltpu.make_async_remote_copy(
                    src_ref=ins[t], dst_ref=outs[t].at[:, 2 * px + py], send_sem=send_sems.at[t, k],
                    recv_sem=recv_sems.at[t, k], device_id=(px, py, c), device_id_type=MESH).wait_recv()
        for cp in sends:
            cp.wait_send()
        for cp in copies:
            cp.wait()

    return pl.pallas_call(
        body, name="gather_weights",
        in_specs=[ANY] * n, out_specs=[ANY] * n,
        out_shape=[_sds((s.shape[0], N_CHIPS) + s.shape[1:], s.dtype) for s in shards],
        scratch_shapes=[pltpu.SemaphoreType.DMA((n, 3)), pltpu.SemaphoreType.DMA((n, 3)), pltpu.SemaphoreType.DMA((n,))],
        compiler_params=pltpu.CompilerParams(has_side_effects=True))(*shards)


def _send_to_sibling_halves(grads):
    n = len(grads)

    def body(*refs):
        ins, outs = refs[:n], refs[n:2 * n]
        send_sems, recv_sems = refs[2 * n:]
        x, y, c = _position()
        sends = []
        for t in range(n):
            half = ins[t].shape[1] // 2
            src = ins[t].at[:, pl.ds(pl.multiple_of((1 - c) * half, 8), half), :]
            cp = pltpu.make_async_remote_copy(src_ref=src, dst_ref=outs[t], send_sem=send_sems.at[t],
                                              recv_sem=recv_sems.at[t], device_id=(x, y, 1 - c), device_id_type=MESH)
            cp.start()
            sends.append(cp)
        for cp in sends:
            cp.wait()

    return pl.pallas_call(
        body, name="grads_to_sibling",
        in_specs=[ANY] * n, out_specs=[ANY] * n,
        out_shape=[_sds((g.shape[0], g.shape[1] // 2, g.shape[2]), g.dtype) for g in grads],
        scratch_shapes=[pltpu.SemaphoreType.DMA((n,)), pltpu.SemaphoreType.DMA((n,))],
        compiler_params=pltpu.CompilerParams(has_side_effects=True))(*grads)


def _scatter_to_chips(sums):
    n = len(sums)

    def body(*refs):
        ins, outs = refs[:n], refs[n:2 * n]
        send_sems, recv_sems, local_sems = refs[2 * n:]
        x, y, c = _position()
        me = 2 * x + y
        chips = _other_chips(x, y)
        copies = []
        for t in range(n):
            own = pltpu.make_async_copy(ins[t].at[me], outs[t].at[me], local_sems.at[t])
            own.start()
            copies.append(own)
        sends = []
        for t in range(n):
            for k, (px, py) in enumerate(chips):
                cp = pltpu.make_async_remote_copy(
                    src_ref=ins[t].at[2 * px + py], dst_ref=outs[t].at[me], send_sem=send_sems.at[t, k],
                    recv_sem=recv_sems.at[t, k], device_id=(px, py, c), device_id_type=MESH)
                cp.start()
                sends.append(cp)
        for t in range(n):
            for k, (px, py) in enumerate(chips):
                pltpu.make_async_remote_copy(
                    src_ref=ins[t].at[me], dst_ref=outs[t].at[2 * px + py], send_sem=send_sems.at[t, k],
                    recv_sem=recv_sems.at[t, k], device_id=(px, py, c), device_id_type=MESH).wait_recv()
        for cp in sends:
            cp.wait_send()
        for cp in copies:
            cp.wait()

    return pl.pallas_call(
        body, name="grads_to_chips",
        in_specs=[ANY] * n, out_specs=[ANY] * n, out_shape=[_sds(s.shape, s.dtype) for s in sums],
        scratch_shapes=[pltpu.SemaphoreType.DMA((n, 3)), pltpu.SemaphoreType.DMA((n, 3)), pltpu.SemaphoreType.DMA((n,))],
        compiler_params=pltpu.CompilerParams(has_side_effects=True))(*sums)


def _join_halves(halves):
    n = len(halves)

    def body(*refs):
        ins, outs = refs[:n], refs[n:2 * n]
        send_sems, recv_sems, local_sems = refs[2 * n:]
        x, y, c = _position()
        ops = []
        for t in range(n):
            half = ins[t].shape[0]
            mine = pl.ds(pl.multiple_of(c * half, 8), half)
            own = pltpu.make_async_copy(ins[t], outs[t].at[mine, :], local_sems.at[t])
            own.start()
            cp = pltpu.make_async_remote_copy(src_ref=ins[t], dst_ref=outs[t].at[mine, :], send_sem=send_sems.at[t],
                                              recv_sem=recv_sems.at[t], device_id=(x, y, 1 - c), device_id_type=MESH)
            cp.start()
            ops += [own, cp]
        for op in ops:
            op.wait()

    return pl.pallas_call(
        body, name="grads_join_halves",
        in_specs=[ANY] * n, out_specs=[ANY] * n,
        out_shape=[_sds((2 * h.shape[0], h.shape[1]), h.dtype) for h in halves],
        scratch_shapes=[pltpu.SemaphoreType.DMA((n,)), pltpu.SemaphoreType.DMA((n,)), pltpu.SemaphoreType.DMA((n,))],
        compiler_params=pltpu.CompilerParams(has_side_effects=True))(*halves)


_HBM = pl.BlockSpec(memory_space=pltpu.HBM)
_SEM = pl.BlockSpec(memory_space=pltpu.SEMAPHORE)
_EFFECT = pltpu.SideEffectType.DATAFLOW_SIDE_EFFECTING


def _chip_copies(srcs, lands, send_sems, recv_sems, per_chip, arriving):
    x, y, c = _position()
    me = 2 * x + y
    copies = []
    for t, (src, land) in enumerate(zip(srcs, lands)):
        for k, (px, py) in enumerate(_other_chips(x, y)):
            peer = 2 * px + py
            copies.append(pltpu.make_async_remote_copy(
                src_ref=src.at[peer] if per_chip else src, dst_ref=land.at[peer if arriving else me],
                send_sem=send_sems[3 * t + k], recv_sem=recv_sems[3 * t + k],
                device_id=(px, py, c), device_id_type=MESH))
    return copies


def _chips_start(srcs, per_chip, name):
    n = len(srcs)
    slab = [s.shape[1:] if per_chip else s.shape for s in srcs]
    lands = [lax.empty((N_CHIPS,) + sh, s.dtype) for sh, s in zip(slab, srcs)]

    ns = 3 * n

    def body(*refs):
        ins = refs[:2 * n]
        send_sems, recv_sems = refs[2 * n:2 * n + ns], refs[2 * n + ns:2 * n + 2 * ns]
        token = refs[-1]
        for cp in _chip_copies(ins[:n], ins[n:], send_sems, recv_sems, per_chip, False):
            cp.start()
        token[...] = jnp.zeros_like(token)

    args = [pltpu.with_memory_space_constraint(a, pltpu.HBM) for a in list(srcs) + lands]
    outs = pl.pallas_call(
        body, name=name,
        out_shape=[pltpu.SemaphoreType.DMA(())] * (2 * ns)
        + [pltpu.HBM(a.shape, a.dtype) for a in args] + [_sds((8, LANES), F32)],
        in_specs=[_HBM] * (2 * n),
        out_specs=[_SEM] * (2 * ns) + [_HBM] * (2 * n) + [pl.BlockSpec(memory_space=pltpu.VMEM)],
        input_output_aliases={i: 2 * ns + i for i in range(2 * n)},
        compiler_params=pltpu.CompilerParams(has_side_effects=_EFFECT))(*args)
    sems, rest = outs[:2 * ns], outs[2 * ns:]
    return sems[:ns], sems[ns:], rest[:n], rest[n:2 * n], rest[-1]


def _chips_wait(handle, after, per_chip, name):
    send_sems, recv_sems, srcs, lands, _ = handle
    n = len(srcs)
    ns = 3 * n

    def body(*refs):
        ins = refs[:2 * n]
        s_sems, r_sems = refs[2 * n:2 * n + ns], refs[2 * n + ns:2 * n + 2 * ns]
        for cp in _chip_copies(ins[:n], ins[n:], s_sems, r_sems, per_chip, False):
            cp.wait_send()
        for cp in _chip_copies(ins[:n], ins[n:], s_sems, r_sems, per_chip, True):
            cp.wait_recv()

    outs = pl.pallas_call(
        body, name=name,
        out_shape=[pltpu.HBM(a.shape, a.dtype) for a in list(srcs) + list(lands)],
        in_specs=[_HBM] * (2 * n) + [_SEM] * (2 * ns) + [ANY],
        out_specs=[_HBM] * (2 * n),
        input_output_aliases={i: i for i in range(2 * n)},
        compiler_params=pltpu.CompilerParams(has_side_effects=_EFFECT))(*srcs, *lands, *send_sems, *recv_sems, after)
    return outs[:n], outs[n:]


def _sum_chips_own(land, own, name):
    S, R, C = land.shape
    tr = R
    for cand in (256, 128, 64, 32, 16, 8):
        if R % cand == 0:
            tr = cand
            break

    def body(l_ref, o_ref, out_ref):
        x, y, _ = _position()
        me = 2 * x + y
        mine = o_ref[me]
        acc = None
        for k in range(S):
            part = jnp.where(me == k, mine, l_ref[k])
            acc = part if acc is None else acc + part
        out_ref[...] = acc

    blk = pl.BlockSpec((S, tr, C), lambda i: (0, i, 0))
    return pl.pallas_call(
        body, name=name, grid=(R // tr,), in_specs=[blk, blk], out_specs=pl.BlockSpec((tr, C), lambda i: (i, 0)),
        out_shape=_sds((R, C), F32), compiler_params=_cp("parallel"))(land, own)


def _exchange_sibling(arrays):
    n = len(arrays)

    def body(*refs):
        ins, outs = refs[:n], refs[n:2 * n]
        send_sems, recv_sems = refs[2 * n:]
        x, y, c = _position()
        sends = []
        for t in range(n):
            cp = pltpu.make_async_remote_copy(src_ref=ins[t], dst_ref=outs[t], send_sem=send_sems.at[t],
                                              recv_sem=recv_sems.at[t], device_id=(x, y, 1 - c), device_id_type=MESH)
            cp.start()
            sends.append(cp)
        for cp in sends:
            cp.wait()

    return pl.pallas_call(
        body, name="grads_to_sibling",
        in_specs=[ANY] * n, out_specs=[ANY] * n, out_shape=[_sds(a.shape, a.dtype) for a in arrays],
        scratch_shapes=[pltpu.SemaphoreType.DMA((n,)), pltpu.SemaphoreType.DMA((n,))],
        compiler_params=pltpu.CompilerParams(has_side_effects=True))(*arrays)


def _adamw_sum(w, ga, gb, m, v, name):
    R, C = w.shape
    tr = next(t for t in (512, 256, 128, 64) if R % t == 0 and t * C * 4 <= (1 << 20))
    c1 = 1.0 - ADAM_B1 ** ADAM_STEP
    c2 = 1.0 - ADAM_B2 ** ADAM_STEP

    def body(w_ref, ga_ref, gb_ref, m_ref, v_ref, g_ref, d_ref, mo_ref, vo_ref):
        gv = ga_ref[...] + gb_ref[...]
        mn = ADAM_B1 * m_ref[...] + (1.0 - ADAM_B1) * gv
        vn = ADAM_B2 * v_ref[...] + (1.0 - ADAM_B2) * (gv * gv)
        g_ref[...] = gv
        d_ref[...] = -ADAM_LR * ((mn / c1) / (jnp.sqrt(vn / c2) + ADAM_EPS) + ADAM_WD * w_ref[...])
        mo_ref[...] = mn
        vo_ref[...] = vn

    blk = pl.BlockSpec((tr, C), lambda i: (i, 0))
    return pl.pallas_call(
        body, name=name, grid=(R // tr,), in_specs=[blk] * 5, out_specs=[blk] * 4,
        out_shape=[_sds((R, C), F32)] * 4, compiler_params=_cp("parallel"))(w, ga, gb, m, v)


def _allreduce_small(packed):
    R, C = packed.shape
    ndev = 2 * N_CHIPS

    def body(in_ref, out_ref, buf, send_sems, recv_sems):
        x, y, c = _position()
        me = 4 * x + 2 * y + c
        buf[me] = in_ref[...]
        sends = []
        for k in range(1, ndev):
            peer = (x ^ (k >> 2), y ^ ((k >> 1) & 1), c ^ (k & 1))
            cp = pltpu.make_async_remote_copy(src_ref=in_ref, dst_ref=buf.at[me], send_sem=send_sems.at[k - 1],
                                              recv_sem=recv_sems.at[k - 1], device_id=peer, device_id_type=MESH)
            cp.start()
            sends.append(cp)
        for k in range(1, ndev):
            peer = (x ^ (k >> 2), y ^ ((k >> 1) & 1), c ^ (k & 1))
            pltpu.make_async_remote_copy(src_ref=in_ref, dst_ref=buf.at[me ^ k], send_sem=send_sems.at[k - 1],
                                         recv_sem=recv_sems.at[k - 1], device_id=peer, device_id_type=MESH).wait_recv()
        for cp in sends:
            cp.wait_send()
        acc = buf[0]
        for d in range(1, ndev):
            acc = acc + buf[d]
        out_ref[...] = acc

    return pl.pallas_call(
        body, name="allreduce_small",
        in_specs=[pl.BlockSpec(memory_space=pltpu.VMEM)], out_specs=pl.BlockSpec(memory_space=pltpu.VMEM),
        out_shape=_sds((R, C), F32),
        scratch_shapes=[pltpu.VMEM((ndev, R, C), F32), pltpu.SemaphoreType.DMA((ndev - 1,)),
                        pltpu.SemaphoreType.DMA((ndev - 1,))],
        compiler_params=pltpu.CompilerParams(vmem_limit_bytes=VMEM_LIMIT_BYTES, has_side_effects=True))(packed)


def _rope_tables(T):
    pos = jnp.arange(T)
    row = (pos // GRID_W).astype(F32)
    col = (pos % GRID_W).astype(F32)
    inv = 1.0 / (ROPE_THETA ** (jnp.arange(AXIS_DIM // 2, dtype=F32) * 2.0 / AXIS_DIM))
    ar, ac = row[:, None] * inv[None, :], col[:, None] * inv[None, :]
    cos = jnp.concatenate([jnp.cos(ar), jnp.cos(ar), jnp.cos(ac), jnp.cos(ac)], axis=-1)
    sin = jnp.concatenate([-jnp.sin(ar), jnp.sin(ar), -jnp.sin(ac), jnp.sin(ac)], axis=-1)
    return jnp.tile(cos, (1, LANES // HEAD_DIM)), jnp.tile(sin, (1, LANES // HEAD_DIM))


def _head_mean_matrix():
    h = jnp.arange(LANES) // HEAD_DIM
    return jnp.where(h[:, None] == h[None, :], 1.0 / HEAD_DIM, 0.0).astype(F32)


def _pack(arrays):
    flat = jnp.concatenate([a.reshape(-1) for a in arrays])
    rows = -(-flat.shape[0] // LANES)
    rows = -(-rows // 256) * 256
    return jnp.pad(flat, (0, rows * LANES - flat.shape[0])).reshape(rows, LANES)


def _unpack(packed, like):
    flat = packed.reshape(-1)
    out, off = [], 0
    for a in like:
        out.append(flat[off:off + a.size].reshape(a.shape))
        off += a.size
    return out


def _layer_fwd(x, lw, consts):
    cos, sin, bmat = consts
    T = x.shape[0]
    tk = min(T, _ATTN_TK)
    tq = min(T, _ATTN_TQ)
    h = _norm_fwd(x, lw["norm1_g"])
    p = _mm_nn(h, lw["w_in"], tn=512, out_dtype=F32, name="mm_p")
    qn, kt, kd, vt = _qk_prep(p, lw["gq"], lw["gk"], cos, sin, bmat, tk)
    o, lse = _attn_fwd(qn, kd, vt, tq)
    go = _sg_fwd(p, lw["sg_norm_g"], lw["sg_w"], lw["sg_bias"])
    co = _convmix_fwd(p, lw["conv_w"])
    mix = jnp.concatenate([o, go, co], axis=1)
    x_mid = _mm_nn(mix, lw["w_out"], tn=D_MODEL, out_dtype=F32, name="mm_out", res=x)
    h2 = _norm_fwd(x_mid, lw["norm2_g"])
    up = _mm_nn(h2, lw["w_up"], tn=D_FF // 2, out_dtype=F32, name="mm_up")
    act = _ffn_act_fwd(up, lw["ffn_conv_w"])
    x_out = _mm_nn(act, lw["w_down"], tn=D_MODEL, out_dtype=F32, name="mm_down", res=x_mid)
    saved = dict(x=x, h=h, p=p, qn=qn, kt=kt, kd=kd, vt=vt, o=o, lse=lse, mix=mix, x_mid=x_mid, h2=h2, up=up, act=act)
    return x_out, saved


def _layer_bwd(dx, s, lw, consts, send):
    cos, sin, bmat = consts
    T = dx.shape[0]
    tq = min(T, _ATTN_TQ)
    g = {}
    d_act = _mm_nt(dx, lw["w_down"], tkw=D_FF // 2, tn=D_MODEL, name="mm_dact")
    g_down = _mm_tn(s["act"], dx, tk=D_FF // 2, tn=D_MODEL, name="mm_dwdown")
    tok = send("w_down", g_down.reshape(N_CHIPS, D_FF // N_CHIPS, D_MODEL))
    d_up, d_cw = _ffn_act_bwd(s["up"], d_act, lw["ffn_conv_w"] + tok)
    g["ffn_conv_w"] = d_cw.transpose(1, 0, 2).reshape(3, 2 * D_FF)
    tok = send("w_up", _mm_tn(s["h2"], d_up, tk=512, tn=D_FF // 2, name="mm_dwup", shards=N_CHIPS))
    d_h2 = _mm_nt(d_up, lw["w_up"], tkw=D_MODEL, tn=D_FF // 2, name="mm_dh2")
    dx2, g["norm2_g"] = _norm_bwd(s["x_mid"], d_h2, dx, lw["norm2_g"] + tok)
    d_mix = _mm_nt(dx2, lw["w_out"], tkw=D_MODEL, tn=D_MODEL, name="mm_dmix")
    g_out = _mm_tn(s["mix"], dx2, tk=512, tn=D_MODEL, name="mm_dwout")
    tok = send("w_out", g_out.reshape(N_CHIPS, D_MODEL // N_CHIPS, D_MODEL))
    dp_c, g["conv_w"] = _convmix_bwd(s["p"], d_mix, lw["conv_w"] + tok)
    dp_b, g["sg_w"], d_bias, g["sg_norm_g"] = _sg_bwd(s["p"], d_mix, lw["sg_norm_g"], lw["sg_w"], lw["sg_wt"], lw["sg_bias"])
    g["sg_b"] = d_bias.reshape(SG_CHUNK, SG_W // HEAD_DIM, HEAD_DIM).sum(axis=-1).T
    dq, dkt, dvt = _attn_bwd(s["qn"], s["o"], d_mix, s["lse"], s["kt"], s["kd"], s["vt"], tq)
    dp_a, d_gq, d_gk = _qk_bwd(s["p"], dq, dkt, dvt, lw["gq"], lw["gk"], cos, sin, bmat)
    g["q_norm_g"] = d_gq.reshape(ATTN_W // HEAD_DIM, HEAD_DIM).sum(axis=0)
    g["k_norm_g"] = d_gk.reshape(KV_W // HEAD_DIM, HEAD_DIM).sum(axis=0)
    dp = jnp.concatenate([dp_a, dp_b, dp_c], axis=1)
    tok = send("w_in", _mm_tn(s["h"], dp, tk=D_MODEL, tn=512, name="mm_dwin", shards=N_CHIPS))
    d_h = _mm_nt(dp, lw["w_in"], tkw=D_MODEL, tn=512, name="mm_dh")
    dx_in, g["norm1_g"] = _norm_bwd(s["x"], d_h, dx2, lw["norm1_g"] + tok)
    return dx_in, g


def _layer_weights(l, full, small):
    sg_w = small["sg_w"][l]
    sg_b = small["sg_b"][l]
    return dict(
        norm1_g=small["norm1_g"][l][None, :], norm2_g=small["norm2_g"][l][None, :],
        gq=jnp.tile(small["q_norm_g"][l], ATTN_W // HEAD_DIM)[None, :],
        gk=jnp.tile(small["k_norm_g"][l], KV_W // HEAD_DIM)[None, :],
        sg_norm_g=small["sg_norm_g"][l][None, :],
        sg_w=sg_w.astype(BF16), sg_wt=sg_w.transpose(0, 2, 1).astype(BF16),
        sg_bias=jnp.repeat(sg_b.T, HEAD_DIM, axis=1),
        conv_w=full["conv_w"], ffn_conv_w=full["ffn_conv_w"],
        w_in=full["w_in"], w_out=full["w_out"], w_up=full["w_up"], w_down=full["w_down"])


_BIG = ("w_in", "w_out", "ffn_w_up", "ffn_w_down")
_SMALL_REPL = ("norm1_g", "q_norm_g", "k_norm_g", "sg_norm_g", "sg_w", "sg_b", "norm2_g")
_SMALL_SHARD = ("conv_w", "ffn_conv_w")
_ORDER = ("norm1_g", "w_in", "q_norm_g", "k_norm_g", "sg_norm_g", "sg_w", "sg_b", "conv_w", "w_out", "norm2_g",
          "ffn_w_up", "ffn_conv_w", "ffn_w_down")


def _reduce_big_grads(layer_grads):
    recv = _send_to_sibling_halves(layer_grads)
    sums = [_add_halves(gr, rc, "add_sibling") for gr, rc in zip(layer_grads, recv)]
    parts = _scatter_to_chips(sums)
    halves = [_sum_chips(pt, "sum_chips") for pt in parts]
    return _join_halves(halves)


def kernel(x, norm1_g, w_in, q_norm_g, k_norm_g, sg_norm_g, sg_w, sg_b, conv_w, w_out, norm2_g, ffn_w_up, ffn_conv_w, ffn_w_down, loss_target, m_norm1_g, m_w_in, m_q_norm_g, m_k_norm_g, m_sg_norm_g, m_sg_w, m_sg_b, m_conv_w, m_w_out, m_norm2_g, m_ffn_w_up, m_ffn_conv_w, m_ffn_w_down, v_norm1_g, v_w_in, v_q_norm_g, v_k_norm_g, v_sg_norm_g, v_sg_w, v_sg_b, v_conv_w, v_w_out, v_norm2_g, v_ffn_w_up, v_ffn_conv_w, v_ffn_w_down):
    w = dict(norm1_g=norm1_g, w_in=w_in, q_norm_g=q_norm_g, k_norm_g=k_norm_g, sg_norm_g=sg_norm_g, sg_w=sg_w,
             sg_b=sg_b, conv_w=conv_w, w_out=w_out, norm2_g=norm2_g, ffn_w_up=ffn_w_up, ffn_conv_w=ffn_conv_w,
             ffn_w_down=ffn_w_down)
    mom = dict(norm1_g=m_norm1_g, w_in=m_w_in, q_norm_g=m_q_norm_g, k_norm_g=m_k_norm_g, sg_norm_g=m_sg_norm_g,
               sg_w=m_sg_w, sg_b=m_sg_b, conv_w=m_conv_w, w_out=m_w_out, norm2_g=m_norm2_g, ffn_w_up=m_ffn_w_up,
               ffn_conv_w=m_ffn_conv_w, ffn_w_down=m_ffn_w_down)
    var = dict(norm1_g=v_norm1_g, w_in=v_w_in, q_norm_g=v_q_norm_g, k_norm_g=v_k_norm_g, sg_norm_g=v_sg_norm_g,
               sg_w=v_sg_w, sg_b=v_sg_b, conv_w=v_conv_w, w_out=v_w_out, norm2_g=v_norm2_g, ffn_w_up=v_ffn_w_up,
               ffn_conv_w=v_ffn_conv_w, ffn_w_down=v_ffn_w_down)
    L = DEPTH
    T = x.shape[1]
    xs = x.reshape(T, D_MODEL)
    target = loss_target.reshape(T, D_MODEL)

    chip = 2 * lax.axis_index("x") + lax.axis_index("y")

    shards = [_cast_bf16(w[n].reshape(-1, w[n].shape[-1]), "cast_" + n).reshape(w[n].shape) for n in _BIG]
    shards += [conv_w, ffn_conv_w]
    gathers = [_chips_start([a[l] for a in shards], False, "gather_start_%d" % l) for l in range(L)]
    start_token = sum(h[4][0, 0] for h in gathers)
    consts = _rope_tables(T) + (_head_mean_matrix(),)

    saved, lws = [], []
    act_x = xs
    for l in range(L):
        after = act_x if l else gathers[-1][4]
        own, lands = _chips_wait(gathers[l], after, False, "gather_wait_%d" % l)
        g_in, g_out, g_up, g_down, g_conv, g_fconv = [
            lax.dynamic_update_slice(ld, o[None], (chip,) + (jnp.int32(0),) * o.ndim) for ld, o in zip(lands, own)]
        fw = dict(w_in=g_in, w_out=g_out.reshape(D_MODEL, D_MODEL), w_up=g_up, w_down=g_down.reshape(D_FF, D_MODEL),
                  conv_w=g_conv.transpose(1, 0, 2).reshape(3, CONV_W),
                  ffn_conv_w=g_fconv.transpose(1, 0, 2).reshape(3, 2 * D_FF))
        lw = _layer_weights(l, fw, w)
        if l == 0:
            lw["norm1_g"] = lw["norm1_g"] + start_token
        act_x, s = _layer_fwd(act_x, lw, consts)
        saved.append(s)
        lws.append(lw)
    dx, loss_blk = _loss_head(act_x, target)
    loss = lax.psum(loss_blk[0, 0], ("x", "y", "c"))

    grads = [None] * L
    partial = [None] * L

    def collect(pending, after, l):
        sums = {}
        for name, handle in pending:
            own, lands = _chips_wait(handle, after, True, "grad_wait_%d_%s" % (l, name))
            sums[name] = _sum_chips_own(lands[0], own[0], "sum_chips_" + name)
        return sums

    pending_prev = None
    for l in reversed(range(L)):
        pending = []

        def send(name, g4, l=l, pending=pending):
            handle = _chips_start([g4], True, "grad_start_%d_%s" % (l, name))
            pending.append((name, handle))
            return handle[4][0, 0]

        dx, g = _layer_bwd(dx, saved[l], lws[l], consts, send)
        grads[l] = g
        if pending_prev is not None:
            partial[l + 1] = collect(pending_prev, dx, l + 1)
        pending_prev = pending
    partial[0] = collect(pending_prev, dx, 0)
    grad_x = dx.reshape(x.shape)

    small_names = _SMALL_REPL + _SMALL_SHARD
    key = dict(norm1_g="norm1_g", q_norm_g="q_norm_g", k_norm_g="k_norm_g", sg_norm_g="sg_norm_g", sg_w="sg_w",
               sg_b="sg_b", norm2_g="norm2_g", conv_w="conv_w", ffn_conv_w="ffn_conv_w")
    small_local = [jnp.stack([grads[l][key[n]].reshape(-1) for l in range(L)]) for n in small_names]
    small_sum = _unpack(_allreduce_small(_pack(small_local)), small_local)
    grad = {}
    for n, a in zip(small_names, small_sum):
        grad[n] = a
    for n in _SMALL_REPL:
        grad[n] = grad[n].reshape(w[n].shape)
    for n in _SMALL_SHARD:
        full_w = grad[n].reshape(L, 3, -1)
        width = w[n].shape[-1]
        grad[n] = lax.dynamic_slice_in_dim(full_w, chip * width, width, axis=2)

    short = dict(w_in="w_in", w_out="w_out", ffn_w_up="w_up", ffn_w_down="w_down")
    mine = [jnp.stack([partial[l][short[n]] for l in range(L)]) for n in _BIG]
    theirs = _exchange_sibling(mine)
    delta, new_m, new_v = {}, {}, {}
    for n, ga, gb in zip(_BIG, mine, theirs):
        shp = w[n].shape
        v2 = lambda a: a.reshape(-1, shp[-1])
        gsum, d, mn, vn = _adamw_sum(v2(w[n]), v2(ga), v2(gb), v2(mom[n]), v2(var[n]), "adamw_" + n)
        grad[n], delta[n], new_m[n], new_v[n] = gsum.reshape(shp), d.reshape(shp), mn.reshape(shp), vn.reshape(shp)
    for group, gname in ((_SMALL_REPL, "adamw_small"), (_SMALL_SHARD, "adamw_conv")):
        like = [w[n] for n in group]
        outs = _adamw(_pack([w[n] for n in group]), _pack([grad[n] for n in group]), _pack([mom[n] for n in group]),
                      _pack([var[n] for n in group]), gname)
        for res, dst in zip(outs, (delta, new_m, new_v)):
            for n, a in zip(group, _unpack(res, like)):
                dst[n] = a

    return (loss, grad_x, *[grad[n] for n in _ORDER], *[delta[n] for n in _ORDER],
            *[new_m[n] for n in _ORDER], *[new_v[n] for n in _ORDER])
```

```python
import functools

import jax
import jax.numpy as jnp
from jax import lax
from jax.experimental import pallas as pl
from jax.experimental.pallas import tpu as pltpu

F32 = jnp.float32
BF16 = jnp.bfloat16

DEPTH = 4
D_MODEL = 1024
HEAD_DIM = 64
ATTN_W = 512
KV_W = 128
SG_W = 256
CONV_W = 256
SG_CHUNK = 128
D_FF = 2816
PROJ_W = 2048
GRID_W = 64
ROPE_THETA = 10000.0
AXIS_DIM = HEAD_DIM // 2
EPS = 1e-6
N_CHIPS = 4

ADAM_LR = 0.001
ADAM_B1 = 0.9
ADAM_B2 = 0.999
ADAM_EPS = 1e-08
ADAM_WD = 0.01
ADAM_STEP = 10

_ROW_TILE = 512
_FFN_ROW_TILE = 256
_WGRAD_ROWS = 2048
LANES = 128
HALO = 8
VMEM_LIMIT_BYTES = 56 * 1024 * 1024
MESH = pl.DeviceIdType.MESH
ANY = pl.BlockSpec(memory_space=pl.ANY)


def _cp(*sem):
    return pltpu.CompilerParams(dimension_semantics=sem if sem else None,
                                vmem_limit_bytes=VMEM_LIMIT_BYTES)


def _sds(shape, dtype):
    return jax.ShapeDtypeStruct(shape, dtype)


def _dot(a, b):
    return jnp.dot(a, b, preferred_element_type=F32)


def _dot_nt(a, b):
    return lax.dot_general(a, b, (((1,), (1,)), ((), ())), preferred_element_type=F32)


def _dot_tn(a, b):
    return lax.dot_general(a, b, (((0,), (0,)), ((), ())), preferred_element_type=F32)


def _norm_fwd(x, g):
    T, Dm = x.shape
    tm = min(T, _ROW_TILE)

    def body(x_ref, g_ref, o_ref):
        xv = x_ref[...]
        r = lax.rsqrt(jnp.mean(xv * xv, axis=-1, keepdims=True) + EPS)
        o_ref[...] = ((xv * r) * g_ref[...]).astype(BF16)

    return pl.pallas_call(
        body, name="norm_fwd", grid=(T // tm,),
        in_specs=[pl.BlockSpec((tm, Dm), lambda i: (i, 0)), pl.BlockSpec((1, Dm), lambda i: (0, 0))],
        out_specs=pl.BlockSpec((tm, Dm), lambda i: (i, 0)),
        out_shape=_sds((T, Dm), BF16), compiler_params=_cp("parallel"))(x, g)


def _norm_bwd(x, dh, dres, g):
    T, Dm = x.shape
    tm = min(T, _ROW_TILE)

    def body(x_ref, dh_ref, dr_ref, g_ref, dx_ref, dg_ref):
        i = pl.program_id(0)
        xv = x_ref[...]
        r = lax.rsqrt(jnp.mean(xv * xv, axis=-1, keepdims=True) + EPS)
        xh = xv * r
        dhv = dh_ref[...]
        dxh = dhv * g_ref[...]
        dx_ref[...] = dr_ref[...] + r * (dxh - xh * jnp.mean(dxh * xh, axis=-1, keepdims=True))
        part = jnp.sum(dhv * xh, axis=0, keepdims=True)

        @pl.when(i == 0)
        def _():
            dg_ref[...] = part

        @pl.when(i > 0)
        def _():
            dg_ref[...] += part

    row = pl.BlockSpec((tm, Dm), lambda i: (i, 0))
    vec = pl.BlockSpec((1, Dm), lambda i: (0, 0))
    return pl.pallas_call(
        body, name="norm_bwd", grid=(T // tm,),
        in_specs=[row, row, row, vec], out_specs=[row, vec],
        out_shape=[_sds((T, Dm), F32), _sds((1, Dm), F32)], compiler_params=_cp("arbitrary"))(x, dh, dres, g)


def _whole(w):
    return pl.BlockSpec(w.shape, lambda *g: (0,) * w.ndim)


def _mm_nn(a, w, *, out_dtype, name, res=None, tm=None):
    M, K = a.shape
    N = w.shape[-1] if w.ndim == 2 else w.shape[0] * w.shape[2]
    tm = min(M, tm or _ROW_TILE)
    has_res = res is not None

    def body(*refs):
        a_ref, w_ref = refs[0], refs[1]
        o_ref = refs[-1]
        av = a_ref[...].astype(BF16)
        parts = [w_ref[...]] if w.ndim == 2 else [w_ref[s] for s in range(w.shape[0])]
        ns = N // len(parts)
        for s, wv in enumerate(parts):
            cols = pl.ds(s * ns, ns)
            acc = _dot(av, wv)
            if has_res:
                acc = acc + refs[2][:, cols]
            o_ref[:, cols] = acc.astype(out_dtype)

    in_specs = [pl.BlockSpec((tm, K), lambda i: (i, 0)), _whole(w)]
    args = [a, w]
    if has_res:
        in_specs.append(pl.BlockSpec((tm, N), lambda i: (i, 0)))
        args.append(res)
    return pl.pallas_call(
        body, name=name, grid=(M // tm,), in_specs=in_specs, out_specs=pl.BlockSpec((tm, N), lambda i: (i, 0)),
        out_shape=_sds((M, N), out_dtype), compiler_params=_cp("parallel"))(*args)


def _a_spec(a, tm, tn, row_of, col_of):
    if a.ndim == 2:
        return pl.BlockSpec((tm, tn), lambda *g: (row_of(*g), col_of(*g)))
    bph = a.shape[2] // tn
    return pl.BlockSpec((None, tm, tn), lambda *g: (col_of(*g) // bph, row_of(*g), col_of(*g) % bph))


def _a_cols(a):
    return a.shape[1] if a.ndim == 2 else a.shape[0] * a.shape[2]


def _mm_nt(a, w, *, name, tm=None):
    M = a.shape[-2]
    Kw = w.shape[-2]
    tm = min(M, tm or _ROW_TILE)

    def body(a_ref, w_ref, o_ref):
        if w.ndim == 2:
            o_ref[...] = _dot_nt(a_ref[...].astype(BF16), w_ref[...])
            return
        S, ns = w.shape[0], w.shape[2]
        acc = None
        for s in range(S):
            if a.ndim == 2:
                piece = a_ref[:, pl.ds(s * ns, ns)]
            else:
                per_half = S // 2
                piece = a_ref[s // per_half, :, pl.ds((s % per_half) * ns, ns)]
            part = _dot_nt(piece.astype(BF16), w_ref[s])
            acc = part if acc is None else acc + part
        o_ref[...] = acc

    a_spec = (pl.BlockSpec((tm, a.shape[1]), lambda i: (i, 0)) if a.ndim == 2
              else pl.BlockSpec((2, tm, a.shape[2]), lambda i: (0, i, 0)))
    return pl.pallas_call(
        body, name=name, grid=(M // tm,), in_specs=[a_spec, _whole(w)],
        out_specs=pl.BlockSpec((tm, Kw), lambda i: (i, 0)), out_shape=_sds((M, Kw), F32),
        compiler_params=_cp("parallel"))(a, w)


def _mm_tn(a, b, *, tk, tn, name, shards=None, tm=None):
    M, K = a.shape
    N = _a_cols(b)
    tm = min(M, tm or _ROW_TILE)

    def body(a_ref, b_ref, o_ref):
        m = pl.program_id(2)
        part = _dot_tn(a_ref[...].astype(BF16), b_ref[...].astype(BF16))

        @pl.when(m == 0)
        def _():
            o_ref[...] = part

        @pl.when(m > 0)
        def _():
            o_ref[...] += part

    if shards is None:
        out_spec = pl.BlockSpec((tk, tn), lambda k, j, m: (k, j))
        out_shape = _sds((K, N), F32)
    else:
        bps = (N // shards) // tn
        out_spec = pl.BlockSpec((None, tk, tn), lambda k, j, m: (j // bps, k, j % bps))
        out_shape = _sds((shards, K, N // shards), F32)
    return pl.pallas_call(
        body, name=name, grid=(K // tk, N // tn, M // tm),
        in_specs=[pl.BlockSpec((tm, tk), lambda k, j, m: (m, k)),
                  _a_spec(b, tm, tn, lambda k, j, m: m, lambda k, j, m: j)],
        out_specs=out_spec, out_shape=out_shape,
        compiler_params=_cp("parallel", "parallel", "arbitrary"))(a, b)


def _halo_specs(T, tm, cw, ic):
    nb = tm // HALO
    last = T // HALO - 1

    def mk(rows, row_of):
        return pl.BlockSpec((rows, cw), lambda *g: (row_of(ic(*g)[0]), ic(*g)[1]))

    return [mk(HALO, lambda i: jnp.maximum(i * nb - 1, 0)), mk(tm, lambda i: i),
            mk(HALO, lambda i: jnp.minimum((i + 1) * nb, last))]


def _ext(prev_ref, cur_ref, next_ref, i, n):
    p = jnp.where(i > 0, prev_ref[...].astype(F32), 0.0)
    nx = jnp.where(i < n - 1, next_ref[...].astype(F32), 0.0)
    return jnp.concatenate([p, cur_ref[...].astype(F32), nx], axis=0)


def _dn(e):
    return pltpu.roll(e, 1, 0)


def _up(e):
    return pltpu.roll(e, e.shape[0] - 1, 0)


def _mid(e):
    return e[HALO:e.shape[0] - HALO]


def _conv3(e, w):
    return _dn(e) * w[0:1] + e * w[1:2] + _up(e) * w[2:3]


def _conv3_t(e, w):
    return _up(e) * w[0:1] + e * w[1:2] + _dn(e) * w[2:3]


def _conv3_wgrad(d, e):
    return jnp.concatenate([jnp.sum(_mid(d * _dn(e)), axis=0, keepdims=True),
                            jnp.sum(_mid(d * e), axis=0, keepdims=True),
                            jnp.sum(_mid(d * _up(e)), axis=0, keepdims=True)], axis=0)


def _sigmoid(x):
    return 1.0 / (1.0 + jnp.exp(-x))


def _accum(ref, i, part):
    @pl.when(i == 0)
    def _():
        ref[...] = part

    @pl.when(i > 0)
    def _():
        ref[...] += part


def _ffn_act_fwd(up, cw):
    T = up.shape[0]
    tm = min(T, _FFN_ROW_TILE)
    cb = D_FF // 2
    nblk = D_FF // cb
    n = T // tm

    def body(gp, gc, gn, vp, vc, vn, wg_ref, wv_ref, o_ref):
        i = pl.program_id(1)
        gate = _conv3(_ext(gp, gc, gn, i, n), wg_ref[...])
        val = _conv3(_ext(vp, vc, vn, i, n), wv_ref[...])
        o_ref[...] = _mid(gate * _sigmoid(gate) * val).astype(BF16)

    return pl.pallas_call(
        body, name="ffn_act_fwd", grid=(nblk, n),
        in_specs=_halo_specs(T, tm, cb, lambda j, i: (i, j)) + _halo_specs(T, tm, cb, lambda j, i: (i, j + nblk))
        + [pl.BlockSpec((3, cb), lambda j, i: (0, j)), pl.BlockSpec((3, cb), lambda j, i: (0, j + nblk))],
        out_specs=pl.BlockSpec((tm, cb), lambda j, i: (i, j)),
        out_shape=_sds((T, D_FF), BF16), compiler_params=_cp("parallel", "parallel"))(
            up, up, up, up, up, up, cw, cw)


def _ffn_act_bwd(up, dact, cw):
    T = up.shape[0]
    tm = min(T, _FFN_ROW_TILE)
    cb = D_FF // 2
    nblk = D_FF // cb
    n = T // tm

    def body(gp, gc, gn, vp, vc, vn, dp_, dc, dn_, wg_ref, wv_ref, dup_ref, dcw_ref):
        i = pl.program_id(1)
        wg, wv = wg_ref[...], wv_ref[...]
        eg = _ext(gp, gc, gn, i, n)
        ev = _ext(vp, vc, vn, i, n)
        ed = _ext(dp_, dc, dn_, i, n)
        gate = _conv3(eg, wg)
        val = _conv3(ev, wv)
        sg = _sigmoid(gate)
        d_gate = ed * val * (sg * (1.0 + gate * (1.0 - sg)))
        d_val = ed * (gate * sg)
        dup_ref[0] = _mid(_conv3_t(d_gate, wg)).astype(BF16)
        dup_ref[1] = _mid(_conv3_t(d_val, wv)).astype(BF16)
        part = jnp.stack([_conv3_wgrad(d_gate, eg), _conv3_wgrad(d_val, ev)], axis=0)
        _accum(dcw_ref, i, part)

    return pl.pallas_call(
        body, name="ffn_act_bwd", grid=(nblk, n),
        in_specs=_halo_specs(T, tm, cb, lambda j, i: (i, j)) + _halo_specs(T, tm, cb, lambda j, i: (i, j + nblk))
        + _halo_specs(T, tm, cb, lambda j, i: (i, j))
        + [pl.BlockSpec((3, cb), lambda j, i: (0, j)), pl.BlockSpec((3, cb), lambda j, i: (0, j + nblk))],
        out_specs=[pl.BlockSpec((2, tm, cb), lambda j, i: (0, i, j)),
                   pl.BlockSpec((2, 3, cb), lambda j, i: (0, 0, j))],
        out_shape=[_sds((2, T, D_FF), BF16), _sds((2, 3, D_FF), F32)],
        compiler_params=_cp("parallel", "arbitrary"))(up, up, up, up, up, up, dact, dact, dact, cw, cw)


_CB_BLK, _CC_BLK, _CX_BLK = 5, 6, 7


def _convmix_fwd(p, w):
    T = p.shape[0]
    tm = min(T, _ROW_TILE)
    n = T // tm

    def body(cb_ref, ccp, ccc, ccn, cxp, cxc, cxn, w_ref, o_ref):
        i = pl.program_id(0)
        z = _ext(ccp, ccc, ccn, i, n) * _ext(cxp, cxc, cxn, i, n)
        o_ref[...] = (cb_ref[...] * _mid(_conv3(z, w_ref[...]))).astype(BF16)

    return pl.pallas_call(
        body, name="convmix_fwd", grid=(n,),
        in_specs=[pl.BlockSpec((tm, CONV_W), lambda i: (i, _CB_BLK))]
        + _halo_specs(T, tm, CONV_W, lambda i: (i, _CC_BLK)) + _halo_specs(T, tm, CONV_W, lambda i: (i, _CX_BLK))
        + [pl.BlockSpec((3, CONV_W), lambda i: (0, 0))],
        out_specs=pl.BlockSpec((tm, CONV_W), lambda i: (i, 0)),
        out_shape=_sds((T, CONV_W), BF16), compiler_params=_cp("parallel"))(p, p, p, p, p, p, p, w)


def _convmix_bwd(p, dmix, w):
    T = p.shape[0]
    tm = min(T, _ROW_TILE)
    n = T // tm
    dblk = (ATTN_W + SG_W) // CONV_W

    def body(cbp, cbc, cbn, ccp, ccc, ccn, cxp, cxc, cxn, dp_, dc, dn_, w_ref, o_ref, dw_ref):
        i = pl.program_id(0)
        wv = w_ref[...]
        ecb = _ext(cbp, cbc, cbn, i, n)
        ecc = _ext(ccp, ccc, ccn, i, n)
        ecx = _ext(cxp, cxc, cxn, i, n)
        ed = _ext(dp_, dc, dn_, i, n)
        z = ecc * ecx
        d_cz = ed * ecb
        d_z = _conv3_t(d_cz, wv)
        o_ref[...] = jnp.concatenate([_mid(ed * _conv3(z, wv)), _mid(d_z * ecx), _mid(d_z * ecc)],
                                     axis=1).astype(BF16)
        _accum(dw_ref, i, _conv3_wgrad(d_cz, z))

    return pl.pallas_call(
        body, name="convmix_bwd", grid=(n,),
        in_specs=_halo_specs(T, tm, CONV_W, lambda i: (i, _CB_BLK)) + _halo_specs(T, tm, CONV_W, lambda i: (i, _CC_BLK))
        + _halo_specs(T, tm, CONV_W, lambda i: (i, _CX_BLK)) + _halo_specs(T, tm, CONV_W, lambda i: (i, dblk))
        + [pl.BlockSpec((3, CONV_W), lambda i: (0, 0))],
        out_specs=[pl.BlockSpec((tm, 3 * CONV_W), lambda i: (i, 0)), pl.BlockSpec((3, CONV_W), lambda i: (0, 0))],
        out_shape=[_sds((T, 3 * CONV_W), BF16), _sds((3, CONV_W), F32)],
        compiler_params=_cp("arbitrary"))(p, p, p, p, p, p, p, p, p, dmix, dmix, dmix, w)


_SU_BLK, _SV_BLK = 3, 4


def _sg_mixed(vnb, w_ref, bias, ch, pr, lo):
    vp = vnb[ch * SG_CHUNK:(ch + 1) * SG_CHUNK, pr * LANES:(pr + 1) * LANES]
    zero = jnp.zeros_like(vp)
    return (_dot(w_ref[2 * pr], jnp.where(lo, vp, zero)) + _dot(w_ref[2 * pr + 1], jnp.where(lo, zero, vp))
            + bias[:, pr * LANES:(pr + 1) * LANES]), vp


def _sg_fwd(p, g, w, bias):
    T = p.shape[0]
    tm = min(T, _ROW_TILE)

    def body(su_ref, sv_ref, g_ref, w_ref, b_ref, o_ref):
        lo = lax.broadcasted_iota(jnp.int32, (SG_CHUNK, LANES), 1) < HEAD_DIM
        sv = sv_ref[...]
        r = lax.rsqrt(jnp.mean(sv * sv, axis=-1, keepdims=True) + EPS)
        vnb = ((sv * r) * g_ref[...]).astype(BF16)
        bias_v = b_ref[...]
        for ch in range(tm // SG_CHUNK):
            for pr in range(2):
                mixed, _ = _sg_mixed(vnb, w_ref, bias_v, ch, pr, lo)
                rows, cols = pl.ds(ch * SG_CHUNK, SG_CHUNK), pl.ds(pr * LANES, LANES)
                o_ref[rows, cols] = (su_ref[rows, cols] * mixed).astype(BF16)

    return pl.pallas_call(
        body, name="sg_fwd", grid=(T // tm,),
        in_specs=[pl.BlockSpec((tm, SG_W), lambda i: (i, _SU_BLK)), pl.BlockSpec((tm, SG_W), lambda i: (i, _SV_BLK)),
                  pl.BlockSpec((1, SG_W), lambda i: (0, 0)), pl.BlockSpec((4, SG_CHUNK, SG_CHUNK), lambda i: (0, 0, 0)),
                  pl.BlockSpec((SG_CHUNK, SG_W), lambda i: (0, 0))],
        out_specs=pl.BlockSpec((tm, SG_W), lambda i: (i, 0)),
        out_shape=_sds((T, SG_W), BF16), compiler_params=_cp("parallel"))(p, p, g, w, bias)


def _sg_bwd(p, dmix, g, w, wt, bias):
    T = p.shape[0]
    tm = min(T, _ROW_TILE)
    dblk = ATTN_W // SG_W

    def body(su_ref, sv_ref, d_ref, g_ref, w_ref, wt_ref, b_ref, o_ref, dw_ref, db_ref, dg_ref, dvn_ref):
        i = pl.program_id(0)
        lo = lax.broadcasted_iota(jnp.int32, (SG_CHUNK, LANES), 1) < HEAD_DIM
        sv = sv_ref[...]
        gv = g_ref[...]
        r = lax.rsqrt(jnp.mean(sv * sv, axis=-1, keepdims=True) + EPS)
        xh = sv * r
        vnb = (xh * gv).astype(BF16)
        bias_v = b_ref[...]
        dw = [jnp.zeros((SG_CHUNK, SG_CHUNK), F32) for _ in range(4)]
        db = jnp.zeros((SG_CHUNK, SG_W), F32)
        for ch in range(tm // SG_CHUNK):
            dbs = []
            for pr in range(2):
                mixed, vp = _sg_mixed(vnb, w_ref, bias_v, ch, pr, lo)
                rows, cols = pl.ds(ch * SG_CHUNK, SG_CHUNK), pl.ds(pr * LANES, LANES)
                dgo = d_ref[rows, cols]
                o_ref[rows, cols] = (dgo * mixed).astype(BF16)
                dm = dgo * su_ref[rows, cols]
                dmb = dm.astype(BF16)
                zero = jnp.zeros_like(dmb)
                dw[2 * pr] += _dot_nt(jnp.where(lo, dmb, zero), vp)
                dw[2 * pr + 1] += _dot_nt(jnp.where(lo, zero, dmb), vp)
                dvn_ref[rows, cols] = jnp.where(lo, _dot(wt_ref[2 * pr], dmb), _dot(wt_ref[2 * pr + 1], dmb))
                dbs.append(dm)
            db += jnp.concatenate(dbs, axis=1)
        dvn = dvn_ref[...]
        dxh = dvn * gv
        o_ref[:, pl.ds(SG_W, SG_W)] = (r * (dxh - xh * jnp.mean(dxh * xh, axis=-1, keepdims=True))).astype(BF16)
        _accum(dw_ref, i, jnp.stack(dw, axis=0))
        _accum(db_ref, i, db)
        _accum(dg_ref, i, jnp.sum(dvn * xh, axis=0, keepdims=True))

    wspec = pl.BlockSpec((4, SG_CHUNK, SG_CHUNK), lambda i: (0, 0, 0))
    return pl.pallas_call(
        body, name="sg_bwd", grid=(T // tm,),
        in_specs=[pl.BlockSpec((tm, SG_W), lambda i: (i, _SU_BLK)), pl.BlockSpec((tm, SG_W), lambda i: (i, _SV_BLK)),
                  pl.BlockSpec((tm, SG_W), lambda i: (i, dblk)), pl.BlockSpec((1, SG_W), lambda i: (0, 0)),
                  wspec, wspec, pl.BlockSpec((SG_CHUNK, SG_W), lambda i: (0, 0))],
        out_specs=[pl.BlockSpec((tm, 2 * SG_W), lambda i: (i, 0)), wspec,
                   pl.BlockSpec((SG_CHUNK, SG_W), lambda i: (0, 0)), pl.BlockSpec((1, SG_W), lambda i: (0, 0))],
        out_shape=[_sds((T, 2 * SG_W), BF16), _sds((4, SG_CHUNK, SG_CHUNK), F32),
                   _sds((SG_CHUNK, SG_W), F32), _sds((1, SG_W), F32)],
        scratch_shapes=[pltpu.VMEM((tm, SG_W), F32)],
        compiler_params=_cp("arbitrary"))(p, p, dmix, g, w, wt, bias)


_ATTN_TQ = 256
_ATTN_TK = 512
_SOFTMAX_STRIP = 32
_ONES_ROWS = 16


def _head_mean(v, bmat):
    return jnp.dot(v, bmat, preferred_element_type=F32, precision=lax.Precision.HIGHEST)


def _swap16(y):
    lane = lax.broadcasted_iota(jnp.int32, y.shape, 1)
    return jnp.where(lane % 32 < 16, pltpu.roll(y, y.shape[1] - 16, 1), pltpu.roll(y, 16, 1))


def _rope(y, cos, sin):
    return y * cos + _swap16(y) * sin


def _rope_t(dy, cos, sin):
    return dy * cos + _swap16(dy * sin)


def _dup_rows(t, gidx):
    h = t[gidx * HEAD_DIM:(gidx + 1) * HEAD_DIM]
    return jnp.concatenate([h, h], axis=0)


def _qk_prep(p, gq, gk, cos, sin, bmat, tk):
    T = p.shape[0]
    nk = T // tk
    scale = HEAD_DIM ** -0.5

    def body(q_ref, kv_ref, gq_ref, gk_ref, cos_ref, sin_ref, b_ref, qo_ref, kt_ref, kd_ref, vt_ref, v1_ref):
        cosv, sinv, bm = cos_ref[...], sin_ref[...], b_ref[...]
        for pr in range(ATTN_W // LANES):
            cols = pl.ds(pr * LANES, LANES)
            xq = q_ref[:, cols]
            r = lax.rsqrt(_head_mean(xq * xq, bm) + EPS)
            qo_ref[:, cols] = (_rope((xq * r) * gq_ref[:, cols], cosv, sinv) * scale).astype(BF16)
        xk = kv_ref[:, pl.ds(0, LANES)]
        r = lax.rsqrt(_head_mean(xk * xk, bm) + EPS)
        kt = _rope((xk * r) * gk_ref[...], cosv, sinv).T
        vt = kv_ref[:, pl.ds(LANES, LANES)].T
        for gidx in range(2):
            kdup = _dup_rows(kt, gidx)
            kt_ref[gidx] = kdup.astype(BF16)
            vt_ref[gidx] = _dup_rows(vt, gidx).astype(BF16)
            v1_ref[gidx] = jnp.concatenate([vt[gidx * HEAD_DIM:(gidx + 1) * HEAD_DIM],
                                            jnp.ones((_ONES_ROWS, tk), F32)], axis=0).astype(BF16)
            kd_ref[gidx] = kdup.T.astype(BF16)

    tspec = pl.BlockSpec((2, None, LANES, tk), lambda i: (0, i, 0, 0))
    dspec = pl.BlockSpec((2, tk, LANES), lambda i: (0, i, 0))
    tab = pl.BlockSpec((tk, LANES), lambda i: (i, 0))
    return pl.pallas_call(
        body, name="qk_prep", grid=(nk,),
        in_specs=[pl.BlockSpec((tk, ATTN_W), lambda i: (i, 0)), pl.BlockSpec((tk, 2 * KV_W), lambda i: (i, ATTN_W // (2 * KV_W))),
                  pl.BlockSpec((1, ATTN_W), lambda i: (0, 0)), pl.BlockSpec((1, KV_W), lambda i: (0, 0)),
                  tab, tab, pl.BlockSpec((LANES, LANES), lambda i: (0, 0))],
        out_specs=[pl.BlockSpec((tk, ATTN_W), lambda i: (i, 0)), tspec, dspec, tspec,
                   pl.BlockSpec((2, None, HEAD_DIM + _ONES_ROWS, tk), lambda i: (0, i, 0, 0))],
        out_shape=[_sds((T, ATTN_W), BF16), _sds((2, nk, LANES, tk), BF16), _sds((2, T, LANES), BF16),
                   _sds((2, nk, LANES, tk), BF16), _sds((2, nk, HEAD_DIM + _ONES_ROWS, tk), BF16)],
        compiler_params=_cp("parallel"))(p, p, gq, gk, cos, sin, bmat)


def _stack_heads(t):
    lo = lax.broadcasted_iota(jnp.int32, (t.shape[0], LANES), 1) < HEAD_DIM
    parts = []
    for pr in range(2):
        tp = t[:, pr * LANES:(pr + 1) * LANES]
        zero = jnp.zeros_like(tp)
        parts += [jnp.where(lo, tp, zero), jnp.where(lo, zero, tp)]
    return jnp.concatenate(parts, axis=0)


def _unstack_heads(s, tq):
    lo = lax.broadcasted_iota(jnp.int32, (tq, LANES), 1) < HEAD_DIM
    return jnp.concatenate([jnp.where(lo, s[0:tq], s[tq:2 * tq]),
                            jnp.where(lo, s[2 * tq:3 * tq], s[3 * tq:4 * tq])], axis=1)


def _rows8_reduce(s, op):
    parts = [s[r:r + 8] for r in range(0, s.shape[0], 8)]
    while len(parts) > 1:
        parts = [op(parts[k], parts[k + 1]) for k in range(0, len(parts) - 1, 2)] + (
            [parts[-1]] if len(parts) % 2 else [])
    return parts[0]


def _attn_fwd(q, kd, v1, tq):
    T = q.shape[0]
    nk, tk = v1.shape[1], v1.shape[3]
    vrows = v1.shape[2]
    nq = T // tq
    sq = 4 * tq
    strip = _SOFTMAX_STRIP
    assert nk % 2 == 0, "key blocks are taken two per loop trip"

    def body(q_ref, kd_ref, v1_ref, o_ref, lse_ref, qst_ref, sa_ref, sb_ref, pa_ref, pb_ref, m_ref, acc_ref):
        qst_ref[...] = _stack_heads(q_ref[...]).astype(F32).T.astype(BF16)
        m_ref[...] = jnp.full((1, sq), -jnp.inf, F32)
        acc_ref[...] = jnp.zeros((vrows, sq), F32)

        def scores(j):
            return _dot(kd_ref[pl.ds(pl.multiple_of(j * tk, tk), tk), :], qst_ref[...])

        def block_max(s_ref):
            m8 = None
            for c in range(tk // strip):
                part = _rows8_reduce(s_ref[pl.ds(c * strip, strip), :], jnp.maximum)
                m8 = part if m8 is None else jnp.maximum(m8, part)
            return m8

        def exp_pass(s_ref, p_ref, m8):
            m_old = m_ref[...]
            m_new = jnp.maximum(m_old, jnp.max(m8, axis=0, keepdims=True))
            m_ref[...] = m_new
            for c in range(tk // strip):
                rows = pl.ds(c * strip, strip)
                p_ref[rows, :] = jnp.exp(s_ref[rows, :] - m_new).astype(BF16)
            return jnp.exp(m_old - m_new)

        def apply(p_ref, alpha, j):
            acc_ref[...] = alpha * acc_ref[...] + _dot(v1_ref[j], p_ref[...])

        sa_ref[...] = scores(0)

        def pair(t, max_a):
            j = 2 * t
            sb_ref[...] = scores(j + 1)
            alpha = exp_pass(sa_ref, pa_ref, max_a)
            max_b = block_max(sb_ref)
            apply(pa_ref, alpha, j)
            sa_ref[...] = scores(jnp.minimum(j + 2, nk - 1))
            alpha = exp_pass(sb_ref, pb_ref, max_b)
            max_a = block_max(sa_ref)
            apply(pb_ref, alpha, j + 1)
            return max_a

        lax.fori_loop(0, nk // 2, pair, block_max(sa_ref))
        l = acc_ref[pl.ds(HEAD_DIM, 1), :]
        on = acc_ref[pl.ds(0, HEAD_DIM), :] / l
        pairs = []
        for pr in range(2):
            two = jnp.concatenate([on[:, (2 * pr) * tq:(2 * pr + 1) * tq], on[:, (2 * pr + 1) * tq:(2 * pr + 2) * tq]],
                                  axis=0)
            pairs.append(two.T)
        o_ref[...] = jnp.concatenate(pairs, axis=1).astype(BF16)
        lse_ref[...] = jnp.broadcast_to(m_ref[...] + jnp.log(l), (LANES, sq)).T

    row = pltpu.VMEM((1, sq), F32)
    return pl.pallas_call(
        body, name="attn_fwd", grid=(2, nq),
        in_specs=[pl.BlockSpec((tq, 2 * LANES), lambda g, i: (i, g)),
                  pl.BlockSpec((None, T, LANES), lambda g, i: (g, 0, 0)),
                  pl.BlockSpec((None, nk, vrows, tk), lambda g, i: (g, 0, 0, 0))],
        out_specs=[pl.BlockSpec((tq, 2 * LANES), lambda g, i: (i, g)),
                   pl.BlockSpec((None, None, sq, LANES), lambda g, i: (g, i, 0, 0))],
        out_shape=[_sds((T, ATTN_W), BF16), _sds((2, nq, sq, LANES), F32)],
        scratch_shapes=[pltpu.VMEM((LANES, sq), BF16), pltpu.VMEM((tk, sq), F32), pltpu.VMEM((tk, sq), F32),
                        pltpu.VMEM((tk, sq), BF16), pltpu.VMEM((tk, sq), BF16), row,
                        pltpu.VMEM((vrows, sq), F32)],
        compiler_params=_cp("parallel", "parallel"))(q, kd, v1)


def _attn_bwd(q, o, dmix, lse, kt, kd, vt, tq):
    T = q.shape[0]
    nk, tk = kt.shape[1], kt.shape[3]
    nq = T // tq
    sq = 4 * tq
    rep = tk // LANES

    def body(q_ref, o_ref, do_ref, lse_ref, kt_ref, kd_ref, vt_ref, dq_ref, dkt_ref, dvt_ref):
        i = pl.program_id(1)
        qs = _stack_heads(q_ref[...])
        dof = _stack_heads(do_ref[...])
        dos = dof.astype(BF16)
        qst = qs.astype(F32).T.astype(BF16)
        dost = dof.T.astype(BF16)
        o_pair = o_ref[...].astype(F32)
        os_ = jnp.concatenate([o_pair[:, 0:LANES], o_pair[:, 0:LANES], o_pair[:, LANES:], o_pair[:, LANES:]], axis=0)
        delta = jnp.sum(dof * os_, axis=-1, keepdims=True)
        lse_t = jnp.concatenate([lse_ref[...]] * rep, axis=1)

        @pl.when(i == 0)
        def _():
            dkt_ref[...] = jnp.zeros_like(dkt_ref)
            dvt_ref[...] = jnp.zeros_like(dvt_ref)

        def step(j, dq):
            kdb = kd_ref[pl.ds(pl.multiple_of(j * tk, tk), tk), :]
            pexp = jnp.exp(_dot(qs, kt_ref[j]) - lse_t)
            ds = pexp * (_dot(dos, vt_ref[j]) - delta)
            pb = pexp.astype(BF16)
            dsb = ds.astype(BF16)
            dvt_ref[j] += _dot(dost, pb)
            dkt_ref[j] += _dot(qst, dsb)
            return dq + _dot(dsb, kdb)

        dq = lax.fori_loop(0, nk, step, jnp.zeros((sq, LANES), F32))
        dq_ref[...] = _unstack_heads(dq, tq)

    tspec = pl.BlockSpec((None, nk, LANES, tk), lambda g, i: (g, 0, 0, 0))
    qspec = pl.BlockSpec((tq, 2 * LANES), lambda g, i: (i, g))
    return pl.pallas_call(
        body, name="attn_bwd", grid=(2, nq),
        in_specs=[qspec, qspec, qspec, pl.BlockSpec((None, None, sq, LANES), lambda g, i: (g, i, 0, 0)),
                  tspec, pl.BlockSpec((None, T, LANES), lambda g, i: (g, 0, 0)), tspec],
        out_specs=[qspec, tspec, tspec],
        out_shape=[_sds((T, ATTN_W), F32), _sds((2, nk, LANES, tk), F32), _sds((2, nk, LANES, tk), F32)],
        compiler_params=_cp("parallel", "arbitrary"))(q, o, dmix, lse, kt, kd, vt)


def _fold_t(t_ref):
    rows = []
    for gidx in range(2):
        t = t_ref[gidx]
        rows.append(t[0:HEAD_DIM] + t[HEAD_DIM:2 * HEAD_DIM])
    return jnp.concatenate(rows, axis=0).T


def _qk_bwd(p, dq, dkt, dvt, gq, gk, cos, sin, bmat):
    T = p.shape[0]
    nk, tk = dkt.shape[1], dkt.shape[3]
    scale = HEAD_DIM ** -0.5

    def norm_bwd(x, dy, gain, bm):
        r = lax.rsqrt(_head_mean(x * x, bm) + EPS)
        xh = x * r
        dxh = dy * gain
        return r * (dxh - xh * _head_mean(dxh * xh, bm)), jnp.sum(dy * xh, axis=0, keepdims=True)

    def body(q_ref, kv_ref, dq_ref, dkt_ref, dvt_ref, gq_ref, gk_ref, cos_ref, sin_ref, b_ref, o_ref, dgq_ref, dgk_ref):
        i = pl.program_id(0)
        cosv, sinv, bm = cos_ref[...], sin_ref[...], b_ref[...]
        dgq = []
        for pr in range(ATTN_W // LANES):
            cols = pl.ds(pr * LANES, LANES)
            dy = _rope_t(dq_ref[:, cols] * scale, cosv, sinv)
            dx, dg = norm_bwd(q_ref[:, cols], dy, gq_ref[:, cols], bm)
            o_ref[:, cols] = dx.astype(BF16)
            dgq.append(dg)
        dy = _rope_t(_fold_t(dkt_ref), cosv, sinv)
        dx, dgk = norm_bwd(kv_ref[:, pl.ds(0, LANES)], dy, gk_ref[...], bm)
        o_ref[:, pl.ds(ATTN_W, LANES)] = dx.astype(BF16)
        o_ref[:, pl.ds(ATTN_W + LANES, LANES)] = _fold_t(dvt_ref).astype(BF16)
        _accum(dgq_ref, i, jnp.concatenate(dgq, axis=1))
        _accum(dgk_ref, i, dgk)

    tspec = pl.BlockSpec((2, None, LANES, tk), lambda i: (0, i, 0, 0))
    tab = pl.BlockSpec((tk, LANES), lambda i: (i, 0))
    return pl.pallas_call(
        body, name="qk_bwd", grid=(nk,),
        in_specs=[pl.BlockSpec((tk, ATTN_W), lambda i: (i, 0)), pl.BlockSpec((tk, 2 * KV_W), lambda i: (i, ATTN_W // (2 * KV_W))),
                  pl.BlockSpec((tk, ATTN_W), lambda i: (i, 0)), tspec, tspec,
                  pl.BlockSpec((1, ATTN_W), lambda i: (0, 0)), pl.BlockSpec((1, KV_W), lambda i: (0, 0)),
                  tab, tab, pl.BlockSpec((LANES, LANES), lambda i: (0, 0))],
        out_specs=[pl.BlockSpec((tk, ATTN_W + 2 * KV_W), lambda i: (i, 0)),
                   pl.BlockSpec((1, ATTN_W), lambda i: (0, 0)), pl.BlockSpec((1, KV_W), lambda i: (0, 0))],
        out_shape=[_sds((T, ATTN_W + 2 * KV_W), BF16), _sds((1, ATTN_W), F32), _sds((1, KV_W), F32)],
        compiler_params=_cp("arbitrary"))(p, p, dq, dkt, dvt, gq, gk, cos, sin, bmat)


def _loss_head(y, target):
    T, Dm = y.shape
    tm = min(T, _ROW_TILE)

    def body(y_ref, t_ref, dy_ref, l_ref):
        i = pl.program_id(0)
        err = y_ref[...] - t_ref[...]
        dy_ref[...] = err * (1.0 / Dm)
        part = jnp.sum(jnp.sum(err * err, axis=-1, keepdims=True), axis=0, keepdims=True) * (0.5 / Dm)
        _accum(l_ref, i, jnp.broadcast_to(part, (8, LANES)))

    row = pl.BlockSpec((tm, Dm), lambda i: (i, 0))
    return pl.pallas_call(
        body, name="loss_head", grid=(T // tm,), in_specs=[row, row],
        out_specs=[row, pl.BlockSpec((8, LANES), lambda i: (0, 0))],
        out_shape=[_sds((T, Dm), F32), _sds((8, LANES), F32)], compiler_params=_cp("arbitrary"))(y, target)


def _adamw(w, g, m, v, name):
    R, C = w.shape
    tr = R
    for cand in (512, 256, 128, 64, 32, 16, 8):
        if R % cand == 0:
            tr = cand
            break
    c1 = 1.0 - ADAM_B1 ** ADAM_STEP
    c2 = 1.0 - ADAM_B2 ** ADAM_STEP

    def body(w_ref, g_ref, m_ref, v_ref, d_ref, mo_ref, vo_ref):
        gv = g_ref[...]
        mn = ADAM_B1 * m_ref[...] + (1.0 - ADAM_B1) * gv
        vn = ADAM_B2 * v_ref[...] + (1.0 - ADAM_B2) * (gv * gv)
        d_ref[...] = -ADAM_LR * ((mn / c1) / (jnp.sqrt(vn / c2) + ADAM_EPS) + ADAM_WD * w_ref[...])
        mo_ref[...] = mn
        vo_ref[...] = vn

    blk = pl.BlockSpec((tr, C), lambda i: (i, 0))
    return pl.pallas_call(
        body, name=name, grid=(R // tr,), in_specs=[blk] * 4, out_specs=[blk] * 3,
        out_shape=[_sds((R, C), F32)] * 3, compiler_params=_cp("parallel"))(w, g, m, v)


def _cast_bf16(w, name):
    R, C = w.shape
    tr = 512 if R % 512 == 0 else 256

    def body(w_ref, o_ref):
        o_ref[...] = w_ref[...].astype(BF16)

    blk = pl.BlockSpec((tr, C), lambda i: (i, 0))
    return pl.pallas_call(body, name=name, grid=(R // tr,), in_specs=[blk], out_specs=blk,
                          out_shape=_sds((R, C), BF16), compiler_params=_cp("parallel"))(w)


def _add_halves(g, recv, name):
    S, R, C = g.shape
    half = R // 2

    def body(g_ref, r_ref, o_ref):
        c = lax.axis_index("c")
        o_ref[...] = g_ref[pl.ds(pl.multiple_of(c * half, 8), half), :] + r_ref[...]

    return pl.pallas_call(
        body, name=name, grid=(S,),
        in_specs=[pl.BlockSpec((None, R, C), lambda s: (s, 0, 0)), pl.BlockSpec((None, half, C), lambda s: (s, 0, 0))],
        out_specs=pl.BlockSpec((None, half, C), lambda s: (s, 0, 0)),
        out_shape=_sds((S, half, C), F32), compiler_params=_cp("parallel"))(g, recv)


def _sum_chips(parts, name):
    S, R, C = parts.shape
    tr = R
    for cand in (256, 128, 64, 32, 16, 8):
        if R % cand == 0:
            tr = cand
            break

    def body(p_ref, o_ref):
        o_ref[...] = ((p_ref[0] + p_ref[1]) + p_ref[2]) + p_ref[3]

    return pl.pallas_call(
        body, name=name, grid=(R // tr,),
        in_specs=[pl.BlockSpec((S, tr, C), lambda i: (0, i, 0))], out_specs=pl.BlockSpec((tr, C), lambda i: (i, 0)),
        out_shape=_sds((R, C), F32), compiler_params=_cp("parallel"))(parts)


def _position():
    x, y, c = lax.axis_index("x"), lax.axis_index("y"), lax.axis_index("c")
    return x, y, c


def _other_chips(x, y):
    return [(1 - x, y), (x, 1 - y), (1 - x, 1 - y)]


def _gather_shards(shards):
    n = len(shards)

    def body(*refs):
        ins, outs = refs[:n], refs[n:2 * n]
        send_sems, recv_sems, local_sems = refs[2 * n:]
        x, y, c = _position()
        me = 2 * x + y
        chips = _other_chips(x, y)
        copies = []
        for t in range(n):
            own = pltpu.make_async_copy(ins[t], outs[t].at[:, me], local_sems.at[t])
            own.start()
            copies.append(own)
        sends = []
        for t in range(n):
            for k, (px, py) in enumerate(chips):
                cp = pltpu.make_async_remote_copy(
                    src_ref=ins[t], dst_ref=outs[t].at[:, me], send_sem=send_sems.at[t, k], recv_sem=recv_sems.at[t, k],
                    device_id=(px, py, c), device_id_type=MESH)
                cp.start()
                sends.append(cp)
        for t in range(n):
            for k, (px, py) in enumerate(chips):
                pltpu.make_async_remote_copy(
                    src_ref=ins[t], dst_ref=outs[t].at[:, 2 * px + py], send_sem=send_sems.at[t, k],
                    recv_sem=recv_sems.at[t, k], device_id=(px, py, c), device_id_type=MESH).wait_recv()
        for cp in sends:
            cp.wait_send()
        for cp in copies:
            cp.wait()

    return pl.pallas_call(
        body, name="gather_weights",
        in_specs=[ANY] * n, out_specs=[ANY] * n,
        out_shape=[_sds((s.shape[0], N_CHIPS) + s.shape[1:], s.dtype) for s in shards],
        scratch_shapes=[pltpu.SemaphoreType.DMA((n, 3)), pltpu.SemaphoreType.DMA((n, 3)), pltpu.SemaphoreType.DMA((n,))],
        compiler_params=pltpu.CompilerParams(has_side_effects=True))(*shards)


def _send_to_sibling_halves(grads):
    n = len(grads)

    def body(*refs):
        ins, outs = refs[:n], refs[n:2 * n]
        send_sems, recv_sems = refs[2 * n:]
        x, y, c = _position()
        sends = []
        for t in range(n):
            half = ins[t].shape[1] // 2
            src = ins[t].at[:, pl.ds(pl.multiple_of((1 - c) * half, 8), half), :]
            cp = pltpu.make_async_remote_copy(src_ref=src, dst_ref=outs[t], send_sem=send_sems.at[t],
                                              recv_sem=recv_sems.at[t], device_id=(x, y, 1 - c), device_id_type=MESH)
            cp.start()
            sends.append(cp)
        for cp in sends:
            cp.wait()

    return pl.pallas_call(
        body, name="grads_to_sibling",
        in_specs=[ANY] * n, out_specs=[ANY] * n,
        out_shape=[_sds((g.shape[0], g.shape[1] // 2, g.shape[2]), g.dtype) for g in grads],
        scratch_shapes=[pltpu.SemaphoreType.DMA((n,)), pltpu.SemaphoreType.DMA((n,))],
        compiler_params=pltpu.CompilerParams(has_side_effects=True))(*grads)


def _scatter_to_chips(sums):
    n = len(sums)

    def body(*refs):
        ins, outs = refs[:n], refs[n:2 * n]
        send_sems, recv_sems, local_sems = refs[2 * n:]
        x, y, c = _position()
        me = 2 * x + y
        chips = _other_chips(x, y)
        copies = []
        for t in range(n):
            own = pltpu.make_async_copy(ins[t].at[me], outs[t].at[me], local_sems.at[t])
            own.start()
            copies.append(own)
        sends = []
        for t in range(n):
            for k, (px, py) in enumerate(chips):
                cp = pltpu.make_async_remote_copy(
                    src_ref=ins[t].at[2 * px + py], dst_ref=outs[t].at[me], send_sem=send_sems.at[t, k],
                    recv_sem=recv_sems.at[t, k], device_id=(px, py, c), device_id_type=MESH)
                cp.start()
                sends.append(cp)
        for t in range(n):
            for k, (px, py) in enumerate(chips):
                pltpu.make_async_remote_copy(
                    src_ref=ins[t].at[me], dst_ref=outs[t].at[2 * px + py], send_sem=send_sems.at[t, k],
                    recv_sem=recv_sems.at[t, k], device_id=(px, py, c), device_id_type=MESH).wait_recv()
        for cp in sends:
            cp.wait_send()
        for cp in copies:
            cp.wait()

    return pl.pallas_call(
        body, name="grads_to_chips",
        in_specs=[ANY] * n, out_specs=[ANY] * n, out_shape=[_sds(s.shape, s.dtype) for s in sums],
        scratch_shapes=[pltpu.SemaphoreType.DMA((n, 3)), pltpu.SemaphoreType.DMA((n, 3)), pltpu.SemaphoreType.DMA((n,))],
        compiler_params=pltpu.CompilerParams(has_side_effects=True))(*sums)


def _join_halves(halves):
    n = len(halves)

    def body(*refs):
        ins, outs = refs[:n], refs[n:2 * n]
        send_sems, recv_sems, local_sems = refs[2 * n:]
        x, y, c = _position()
        ops = []
        for t in range(n):
            half = ins[t].shape[0]
            mine = pl.ds(pl.multiple_of(c * half, 8), half)
            own = pltpu.make_async_copy(ins[t], outs[t].at[mine, :], local_sems.at[t])
            own.start()
            cp = pltpu.make_async_remote_copy(src_ref=ins[t], dst_ref=outs[t].at[mine, :], send_sem=send_sems.at[t],
                                              recv_sem=recv_sems.at[t], device_id=(x, y, 1 - c), device_id_type=MESH)
            cp.start()
            ops += [own, cp]
        for op in ops:
            op.wait()

    return pl.pallas_call(
        body, name="grads_join_halves",
        in_specs=[ANY] * n, out_specs=[ANY] * n,
        out_shape=[_sds((2 * h.shape[0], h.shape[1]), h.dtype) for h in halves],
        scratch_shapes=[pltpu.SemaphoreType.DMA((n,)), pltpu.SemaphoreType.DMA((n,)), pltpu.SemaphoreType.DMA((n,))],
        compiler_params=pltpu.CompilerParams(has_side_effects=True))(*halves)


_HBM = pl.BlockSpec(memory_space=pltpu.HBM)
_SEM = pl.BlockSpec(memory_space=pltpu.SEMAPHORE)
_EFFECT = pltpu.SideEffectType.DATAFLOW_SIDE_EFFECTING


def _chip_copies(srcs, lands, send_sems, recv_sems, per_chip, arriving):
    x, y, c = _position()
    me = 2 * x + y
    copies = []
    for t, (src, land) in enumerate(zip(srcs, lands)):
        for k, (px, py) in enumerate(_other_chips(x, y)):
            peer = 2 * px + py
            copies.append(pltpu.make_async_remote_copy(
                src_ref=src.at[peer] if per_chip else src, dst_ref=land.at[peer if arriving else me],
                send_sem=send_sems[3 * t + k], recv_sem=recv_sems[3 * t + k],
                device_id=(px, py, c), device_id_type=MESH))
    return copies


def _chips_start(srcs, per_chip, name):
    n = len(srcs)
    slab = [s.shape[1:] if per_chip else s.shape for s in srcs]
    lands = [lax.empty((N_CHIPS,) + sh, s.dtype) for sh, s in zip(slab, srcs)]

    ns = 3 * n

    def body(*refs):
        ins = refs[:2 * n]
        send_sems, recv_sems = refs[2 * n:2 * n + ns], refs[2 * n + ns:2 * n + 2 * ns]
        token = refs[-1]
        for cp in _chip_copies(ins[:n], ins[n:], send_sems, recv_sems, per_chip, False):
            cp.start()
        token[...] = jnp.zeros_like(token)

    args = [pltpu.with_memory_space_constraint(a, pltpu.HBM) for a in list(srcs) + lands]
    outs = pl.pallas_call(
        body, name=name,
        out_shape=[pltpu.SemaphoreType.DMA(())] * (2 * ns)
        + [pltpu.HBM(a.shape, a.dtype) for a in args] + [_sds((8, LANES), F32)],
        in_specs=[_HBM] * (2 * n),
        out_specs=[_SEM] * (2 * ns) + [_HBM] * (2 * n) + [pl.BlockSpec(memory_space=pltpu.VMEM)],
        input_output_aliases={i: 2 * ns + i for i in range(2 * n)},
        compiler_params=pltpu.CompilerParams(has_side_effects=_EFFECT))(*args)
    sems, rest = outs[:2 * ns], outs[2 * ns:]
    return sems[:ns], sems[ns:], rest[:n], rest[n:2 * n], rest[-1]


def _chips_wait(handle, after, per_chip, name):
    send_sems, recv_sems, srcs, lands, _ = handle
    n = len(srcs)
    ns = 3 * n

    def body(*refs):
        ins = refs[:2 * n]
        s_sems, r_sems = refs[2 * n:2 * n + ns], refs[2 * n + ns:2 * n + 2 * ns]
        for cp in _chip_copies(ins[:n], ins[n:], s_sems, r_sems, per_chip, False):
            cp.wait_send()
        for cp in _chip_copies(ins[:n], ins[n:], s_sems, r_sems, per_chip, True):
            cp.wait_recv()

    outs = pl.pallas_call(
        body, name=name,
        out_shape=[pltpu.HBM(a.shape, a.dtype) for a in list(srcs) + list(lands)],
        in_specs=[_HBM] * (2 * n) + [_SEM] * (2 * ns) + [ANY],
        out_specs=[_HBM] * (2 * n),
        input_output_aliases={i: i for i in range(2 * n)},
        compiler_params=pltpu.CompilerParams(has_side_effects=_EFFECT))(*srcs, *lands, *send_sems, *recv_sems, after)
    return outs[:n], outs[n:]


def _sum_chips_own(land, own, name):
    S, R, C = land.shape
    tr = R
    for cand in (256, 128, 64, 32, 16, 8):
        if R % cand == 0:
            tr = cand
            break

    def body(l_ref, o_ref, out_ref):
        x, y, _ = _position()
        me = 2 * x + y
        mine = o_ref[me]
        acc = None
        for k in range(S):
            part = jnp.where(me == k, mine, l_ref[k])
            acc = part if acc is None else acc + part
        out_ref[...] = acc

    blk = pl.BlockSpec((S, tr, C), lambda i: (0, i, 0))
    return pl.pallas_call(
        body, name=name, grid=(R // tr,), in_specs=[blk, blk], out_specs=pl.BlockSpec((tr, C), lambda i: (i, 0)),
        out_shape=_sds((R, C), F32), compiler_params=_cp("parallel"))(land, own)


def _exchange_sibling(arrays):
    n = len(arrays)

    def body(*refs):
        ins, outs = refs[:n], refs[n:2 * n]
        send_sems, recv_sems = refs[2 * n:]
        x, y, c = _position()
        sends = []
        for t in range(n):
            cp = pltpu.make_async_remote_copy(src_ref=ins[t], dst_ref=outs[t], send_sem=send_sems.at[t],
                                              recv_sem=recv_sems.at[t], device_id=(x, y, 1 - c), device_id_type=MESH)
            cp.start()
            sends.append(cp)
        for cp in sends:
            cp.wait()

    return pl.pallas_call(
        body, name="grads_to_sibling",
        in_specs=[ANY] * n, out_specs=[ANY] * n, out_shape=[_sds(a.shape, a.dtype) for a in arrays],
        scratch_shapes=[pltpu.SemaphoreType.DMA((n,)), pltpu.SemaphoreType.DMA((n,))],
        compiler_params=pltpu.CompilerParams(has_side_effects=True))(*arrays)


def _adamw_sum(w, ga, gb, m, v, name):
    R, C = w.shape
    tr = next(t for t in (512, 256, 128, 64) if R % t == 0 and t * C * 4 <= (1 << 20))
    c1 = 1.0 - ADAM_B1 ** ADAM_STEP
    c2 = 1.0 - ADAM_B2 ** ADAM_STEP

    def body(w_ref, ga_ref, gb_ref, m_ref, v_ref, g_ref, d_ref, mo_ref, vo_ref):
        gv = ga_ref[...] + gb_ref[...]
        mn = ADAM_B1 * m_ref[...] + (1.0 - ADAM_B1) * gv
        vn = ADAM_B2 * v_ref[...] + (1.0 - ADAM_B2) * (gv * gv)
        g_ref[...] = gv
        d_ref[...] = -ADAM_LR * ((mn / c1) / (jnp.sqrt(vn / c2) + ADAM_EPS) + ADAM_WD * w_ref[...])
        mo_ref[...] = mn
        vo_ref[...] = vn

    blk = pl.BlockSpec((tr, C), lambda i: (i, 0))
    return pl.pallas_call(
        body, name=name, grid=(R // tr,), in_specs=[blk] * 5, out_specs=[blk] * 4,
        out_shape=[_sds((R, C), F32)] * 4, compiler_params=_cp("parallel"))(w, ga, gb, m, v)


def _allreduce_small(packed):
    R, C = packed.shape
    ndev = 2 * N_CHIPS

    def body(in_ref, out_ref, buf, send_sems, recv_sems):
        x, y, c = _position()
        me = 4 * x + 2 * y + c
        buf[me] = in_ref[...]
        sends = []
        for k in range(1, ndev):
            peer = (x ^ (k >> 2), y ^ ((k >> 1) & 1), c ^ (k & 1))
            cp = pltpu.make_async_remote_copy(src_ref=in_ref, dst_ref=buf.at[me], send_sem=send_sems.at[k - 1],
                                              recv_sem=recv_sems.at[k - 1], device_id=peer, device_id_type=MESH)
            cp.start()
            sends.append(cp)
        for k in range(1, ndev):
            peer = (x ^ (k >> 2), y ^ ((k >> 1) & 1), c ^ (k & 1))
            pltpu.make_async_remote_copy(src_ref=in_ref, dst_ref=buf.at[me ^ k], send_sem=send_sems.at[k - 1],
                                         recv_sem=recv_sems.at[k - 1], device_id=peer, device_id_type=MESH).wait_recv()
        for cp in sends:
            cp.wait_send()
        acc = buf[0]
        for d in range(1, ndev):
            acc = acc + buf[d]
        out_ref[...] = acc

    return pl.pallas_call(
        body, name="allreduce_small",
        in_specs=[pl.BlockSpec(memory_space=pltpu.VMEM)], out_specs=pl.BlockSpec(memory_space=pltpu.VMEM),
        out_shape=_sds((R, C), F32),
        scratch_shapes=[pltpu.VMEM((ndev, R, C), F32), pltpu.SemaphoreType.DMA((ndev - 1,)),
                        pltpu.SemaphoreType.DMA((ndev - 1,))],
        compiler_params=pltpu.CompilerParams(vmem_limit_bytes=VMEM_LIMIT_BYTES, has_side_effects=True))(packed)


def _rope_tables(T):
    pos = jnp.arange(T)
    row = (pos // GRID_W).astype(F32)
    col = (pos % GRID_W).astype(F32)
    inv = 1.0 / (ROPE_THETA ** (jnp.arange(AXIS_DIM // 2, dtype=F32) * 2.0 / AXIS_DIM))
    ar, ac = row[:, None] * inv[None, :], col[:, None] * inv[None, :]
    cos = jnp.concatenate([jnp.cos(ar), jnp.cos(ar), jnp.cos(ac), jnp.cos(ac)], axis=-1)
    sin = jnp.concatenate([-jnp.sin(ar), jnp.sin(ar), -jnp.sin(ac), jnp.sin(ac)], axis=-1)
    return jnp.tile(cos, (1, LANES // HEAD_DIM)), jnp.tile(sin, (1, LANES // HEAD_DIM))


def _head_mean_matrix():
    h = jnp.arange(LANES) // HEAD_DIM
    return jnp.where(h[:, None] == h[None, :], 1.0 / HEAD_DIM, 0.0).astype(F32)


def _pack(arrays):
    flat = jnp.concatenate([a.reshape(-1) for a in arrays])
    rows = -(-flat.shape[0] // LANES)
    rows = -(-rows // 256) * 256
    return jnp.pad(flat, (0, rows * LANES - flat.shape[0])).reshape(rows, LANES)


def _unpack(packed, like):
    flat = packed.reshape(-1)
    out, off = [], 0
    for a in like:
        out.append(flat[off:off + a.size].reshape(a.shape))
        off += a.size
    return out


def _layer_fwd(x, lw, consts):
    cos, sin, bmat = consts
    T = x.shape[0]
    tk = min(T, _ATTN_TK)
    tq = min(T, _ATTN_TQ)
    h = _norm_fwd(x, lw["norm1_g"])
    p = _mm_nn(h, lw["w_in"], out_dtype=F32, name="mm_p")
    qn, kt, kd, vt, v1 = _qk_prep(p, lw["gq"], lw["gk"], cos, sin, bmat, tk)
    o, lse = _attn_fwd(qn, kd, v1, tq)
    go = _sg_fwd(p, lw["sg_norm_g"], lw["sg_w"], lw["sg_bias"])
    co = _convmix_fwd(p, lw["conv_w"])
    mix = jnp.concatenate([o, go, co], axis=1)
    x_mid = _mm_nn(mix, lw["w_out"], out_dtype=F32, name="mm_out", res=x)
    h2 = _norm_fwd(x_mid, lw["norm2_g"])
    up = _mm_nn(h2, lw["w_up"], out_dtype=F32, name="mm_up", tm=_FFN_ROW_TILE)
    act = _ffn_act_fwd(up, lw["ffn_conv_w"])
    x_out = _mm_nn(act, lw["w_down"], out_dtype=F32, name="mm_down", res=x_mid)
    saved = dict(x=x, h=h, p=p, qn=qn, kt=kt, kd=kd, vt=vt, o=o, lse=lse, mix=mix, x_mid=x_mid, h2=h2, up=up, act=act)
    return x_out, saved


def _layer_bwd(dx, s, lw, consts, send):
    cos, sin, bmat = consts
    T = dx.shape[0]
    tq = min(T, _ATTN_TQ)
    g = {}
    d_act = _mm_nt(dx, lw["w_down"], name="mm_dact")
    g_down = _mm_tn(s["act"], dx, tk=D_FF // 2, tn=D_MODEL, name="mm_dwdown", tm=_WGRAD_ROWS // 2)
    tok = send("w_down", g_down.reshape(N_CHIPS, D_FF // N_CHIPS, D_MODEL))
    d_up, d_cw = _ffn_act_bwd(s["up"], d_act, lw["ffn_conv_w"] + tok)
    g["ffn_conv_w"] = d_cw.transpose(1, 0, 2).reshape(3, 2 * D_FF)
    tok = send("w_up", _mm_tn(s["h2"], d_up, tk=512, tn=D_FF // 2, name="mm_dwup", shards=N_CHIPS, tm=_WGRAD_ROWS))
    d_h2 = _mm_nt(d_up, lw["w_up"], name="mm_dh2")
    dx2, g["norm2_g"] = _norm_bwd(s["x_mid"], d_h2, dx, lw["norm2_g"] + tok)
    d_mix = _mm_nt(dx2, lw["w_out"], name="mm_dmix")
    g_out = _mm_tn(s["mix"], dx2, tk=512, tn=D_MODEL, name="mm_dwout", tm=_WGRAD_ROWS // 2)
    tok = send("w_out", g_out.reshape(N_CHIPS, D_MODEL // N_CHIPS, D_MODEL))
    dp_c, g["conv_w"] = _convmix_bwd(s["p"], d_mix, lw["conv_w"] + tok)
    dp_b, g["sg_w"], d_bias, g["sg_norm_g"] = _sg_bwd(s["p"], d_mix, lw["sg_norm_g"], lw["sg_w"], lw["sg_wt"], lw["sg_bias"])
    g["sg_b"] = d_bias.reshape(SG_CHUNK, SG_W // HEAD_DIM, HEAD_DIM).sum(axis=-1).T
    dq, dkt, dvt = _attn_bwd(s["qn"], s["o"], d_mix, s["lse"], s["kt"], s["kd"], s["vt"], tq)
    dp_a, d_gq, d_gk = _qk_bwd(s["p"], dq, dkt, dvt, lw["gq"], lw["gk"], cos, sin, bmat)
    g["q_norm_g"] = d_gq.reshape(ATTN_W // HEAD_DIM, HEAD_DIM).sum(axis=0)
    g["k_norm_g"] = d_gk.reshape(KV_W // HEAD_DIM, HEAD_DIM).sum(axis=0)
    dp = jnp.concatenate([dp_a, dp_b, dp_c], axis=1)
    tok = send("w_in", _mm_tn(s["h"], dp, tk=D_MODEL, tn=512, name="mm_dwin", shards=N_CHIPS, tm=_WGRAD_ROWS))
    d_h = _mm_nt(dp, lw["w_in"], name="mm_dh")
    dx_in, g["norm1_g"] = _norm_bwd(s["x"], d_h, dx2, lw["norm1_g"] + tok)
    return dx_in, g


def _layer_weights(l, full, small):
    sg_w = small["sg_w"][l]
    sg_b = small["sg_b"][l]
    return dict(
        norm1_g=small["norm1_g"][l][None, :], norm2_g=small["norm2_g"][l][None, :],
        gq=jnp.tile(small["q_norm_g"][l], ATTN_W // HEAD_DIM)[None, :],
        gk=jnp.tile(small["k_norm_g"][l], KV_W // HEAD_DIM)[None, :],
        sg_norm_g=small["sg_norm_g"][l][None, :],
        sg_w=sg_w.astype(BF16), sg_wt=sg_w.transpose(0, 2, 1).astype(BF16),
        sg_bias=jnp.repeat(sg_b.T, HEAD_DIM, axis=1),
        conv_w=full["conv_w"], ffn_conv_w=full["ffn_conv_w"],
        w_in=full["w_in"], w_out=full["w_out"], w_up=full["w_up"], w_down=full["w_down"])


_BIG = ("w_in", "w_out", "ffn_w_up", "ffn_w_down")
_SMALL_REPL = ("norm1_g", "q_norm_g", "k_norm_g", "sg_norm_g", "sg_w", "sg_b", "norm2_g")
_SMALL_SHARD = ("conv_w", "ffn_conv_w")
_ORDER = ("norm1_g", "w_in", "q_norm_g", "k_norm_g", "sg_norm_g", "sg_w", "sg_b", "conv_w", "w_out", "norm2_g",
          "ffn_w_up", "ffn_conv_w", "ffn_w_down")


def _reduce_big_grads(layer_grads):
    recv = _send_to_sibling_halves(layer_grads)
    sums = [_add_halves(gr, rc, "add_sibling") for gr, rc in zip(layer_grads, recv)]
    parts = _scatter_to_chips(sums)
    halves = [_sum_chips(pt, "sum_chips") for pt in parts]
    return _join_halves(halves)


def kernel(x, norm1_g, w_in, q_norm_g, k_norm_g, sg_norm_g, sg_w, sg_b, conv_w, w_out, norm2_g, ffn_w_up, ffn_conv_w, ffn_w_down, loss_target, m_norm1_g, m_w_in, m_q_norm_g, m_k_norm_g, m_sg_norm_g, m_sg_w, m_sg_b, m_conv_w, m_w_out, m_norm2_g, m_ffn_w_up, m_ffn_conv_w, m_ffn_w_down, v_norm1_g, v_w_in, v_q_norm_g, v_k_norm_g, v_sg_norm_g, v_sg_w, v_sg_b, v_conv_w, v_w_out, v_norm2_g, v_ffn_w_up, v_ffn_conv_w, v_ffn_w_down):
    w = dict(norm1_g=norm1_g, w_in=w_in, q_norm_g=q_norm_g, k_norm_g=k_norm_g, sg_norm_g=sg_norm_g, sg_w=sg_w,
             sg_b=sg_b, conv_w=conv_w, w_out=w_out, norm2_g=norm2_g, ffn_w_up=ffn_w_up, ffn_conv_w=ffn_conv_w,
             ffn_w_down=ffn_w_down)
    mom = dict(norm1_g=m_norm1_g, w_in=m_w_in, q_norm_g=m_q_norm_g, k_norm_g=m_k_norm_g, sg_norm_g=m_sg_norm_g,
               sg_w=m_sg_w, sg_b=m_sg_b, conv_w=m_conv_w, w_out=m_w_out, norm2_g=m_norm2_g, ffn_w_up=m_ffn_w_up,
               ffn_conv_w=m_ffn_conv_w, ffn_w_down=m_ffn_w_down)
    var = dict(norm1_g=v_norm1_g, w_in=v_w_in, q_norm_g=v_q_norm_g, k_norm_g=v_k_norm_g, sg_norm_g=v_sg_norm_g,
               sg_w=v_sg_w, sg_b=v_sg_b, conv_w=v_conv_w, w_out=v_w_out, norm2_g=v_norm2_g, ffn_w_up=v_ffn_w_up,
               ffn_conv_w=v_ffn_conv_w, ffn_w_down=v_ffn_w_down)
    L = DEPTH
    T = x.shape[1]
    xs = x.reshape(T, D_MODEL)
    target = loss_target.reshape(T, D_MODEL)

    chip = 2 * lax.axis_index("x") + lax.axis_index("y")

    shards = [_cast_bf16(w[n].reshape(-1, w[n].shape[-1]), "cast_" + n).reshape(w[n].shape) for n in _BIG]
    shards += [conv_w, ffn_conv_w]
    gathers = [_chips_start([a[l] for a in shards], False, "gather_start_%d" % l) for l in range(L)]
    start_token = sum(h[4][0, 0] for h in gathers)
    consts = _rope_tables(T) + (_head_mean_matrix(),)

    saved, lws = [], []
    act_x = xs
    for l in range(L):
        after = act_x if l else gathers[-1][4]
        own, lands = _chips_wait(gathers[l], after, False, "gather_wait_%d" % l)
        g_in, g_out, g_up, g_down, g_conv, g_fconv = [
            lax.dynamic_update_slice(ld, o[None], (chip,) + (jnp.int32(0),) * o.ndim) for ld, o in zip(lands, own)]
        fw = dict(w_in=g_in, w_out=g_out.reshape(D_MODEL, D_MODEL), w_up=g_up, w_down=g_down.reshape(D_FF, D_MODEL),
                  conv_w=g_conv.transpose(1, 0, 2).reshape(3, CONV_W),
                  ffn_conv_w=g_fconv.transpose(1, 0, 2).reshape(3, 2 * D_FF))
        lw = _layer_weights(l, fw, w)
        if l == 0:
            lw["norm1_g"] = lw["norm1_g"] + start_token
        act_x, s = _layer_fwd(act_x, lw, consts)
        saved.append(s)
        lws.append(lw)
    dx, loss_blk = _loss_head(act_x, target)
    loss = lax.psum(loss_blk[0, 0], ("x", "y", "c"))

    grads = [None] * L
    partial = [None] * L

    def collect(pending, after, l):
        sums = {}
        for name, handle in pending:
            own, lands = _chips_wait(handle, after, True, "grad_wait_%d_%s" % (l, name))
            sums[name] = _sum_chips_own(lands[0], own[0], "sum_chips_" + name)
        return sums

    pending_prev = None
    for l in reversed(range(L)):
        pending = []

        def send(name, g4, l=l, pending=pending):
            handle = _chips_start([g4], True, "grad_start_%d_%s" % (l, name))
            pending.append((name, handle))
            return handle[4][0, 0]

        dx, g = _layer_bwd(dx, saved[l], lws[l], consts, send)
        grads[l] = g
        if pending_prev is not None:
            partial[l + 1] = collect(pending_prev, dx, l + 1)
        pending_prev = pending
    partial[0] = collect(pending_prev, dx, 0)
    grad_x = dx.reshape(x.shape)

    small_names = _SMALL_REPL + _SMALL_SHARD
    key = dict(norm1_g="norm1_g", q_norm_g="q_norm_g", k_norm_g="k_norm_g", sg_norm_g="sg_norm_g", sg_w="sg_w",
               sg_b="sg_b", norm2_g="norm2_g", conv_w="conv_w", ffn_conv_w="ffn_conv_w")
    small_local = [jnp.stack([grads[l][key[n]].reshape(-1) for l in range(L)]) for n in small_names]
    small_sum = _unpack(_allreduce_small(_pack(small_local)), small_local)
    grad = {}
    for n, a in zip(small_names, small_sum):
        grad[n] = a
    for n in _SMALL_REPL:
        grad[n] = grad[n].reshape(w[n].shape)
    for n in _SMALL_SHARD:
        full_w = grad[n].reshape(L, 3, -1)
        width = w[n].shape[-1]
        grad[n] = lax.dynamic_slice_in_dim(full_w, chip * width, width, axis=2)

    short = dict(w_in="w_in", w_out="w_out", ffn_w_up="w_up", ffn_w_down="w_down")
    mine = [jnp.stack([partial[l][short[n]] for l in range(L)]) for n in _BIG]
    theirs = _exchange_sibling(mine)
    delta, new_m, new_v = {}, {}, {}
    for n, ga, gb in zip(_BIG, mine, theirs):
        shp = w[n].shape
        v2 = lambda a: a.reshape(-1, shp[-1])
        gsum, d, mn, vn = _adamw_sum(v2(w[n]), v2(ga), v2(gb), v2(mom[n]), v2(var[n]), "adamw_" + n)
        grad[n], delta[n], new_m[n], new_v[n] = gsum.reshape(shp), d.reshape(shp), mn.reshape(shp), vn.reshape(shp)
    for group, gname in ((_SMALL_REPL, "adamw_small"), (_SMALL_SHARD, "adamw_conv")):
        like = [w[n] for n in group]
        outs = _adamw(_pack([w[n] for n in group]), _pack([grad[n] for n in group]), _pack([mom[n] for n in group]),
                      _pack([var[n] for n in group]), gname)
        for res, dst in zip(outs, (delta, new_m, new_v)):
            for n, a in zip(group, _unpack(res, like)):
                dst[n] = a

    return (loss, grad_x, *[grad[n] for n in _ORDER], *[delta[n] for n in _ORDER],
            *[new_m[n] for n in _ORDER], *[new_v[n] for n in _ORDER])
```

```python
import jax
import jax.numpy as jnp
from jax import lax
from jax.experimental import pallas as pl
from jax.experimental.pallas import tpu as pltpu

F32 = jnp.float32
BF16 = jnp.bfloat16

DEPTH = 4
D_MODEL = 1024
HEAD_DIM = 64
ATTN_W = 512
KV_W = 128
SG_W = 256
CONV_W = 256
SG_CHUNK = 128
D_FF = 2816
PROJ_W = 2048
GRID_W = 64
ROPE_THETA = 10000.0
AXIS_DIM = HEAD_DIM // 2
EPS = 1e-6
N_CHIPS = 4

ADAM_LR = 0.001
ADAM_B1 = 0.9
ADAM_B2 = 0.999
ADAM_EPS = 1e-08
ADAM_WD = 0.01
ADAM_STEP = 10

_ROW_TILE = 512
_FFN_ROW_TILE = 256
_WGRAD_ROWS = 2048
LANES = 128
HALO = 8
VMEM_LIMIT_BYTES = 56 * 1024 * 1024
MESH = pl.DeviceIdType.MESH
ANY = pl.BlockSpec(memory_space=pl.ANY)


def _cp(*sem):
    return pltpu.CompilerParams(dimension_semantics=sem if sem else None,
                                vmem_limit_bytes=VMEM_LIMIT_BYTES)


def _sds(shape, dtype):
    return jax.ShapeDtypeStruct(shape, dtype)


def _dot(a, b):
    return jnp.dot(a, b, preferred_element_type=F32)


def _dot_nt(a, b):
    return lax.dot_general(a, b, (((1,), (1,)), ((), ())), preferred_element_type=F32)


def _dot_tn(a, b):
    return lax.dot_general(a, b, (((0,), (0,)), ((), ())), preferred_element_type=F32)


def _norm_fwd(x, g):
    T, Dm = x.shape
    tm = min(T, _ROW_TILE)

    def body(x_ref, g_ref, o_ref):
        xv = x_ref[...]
        r = lax.rsqrt(jnp.mean(xv * xv, axis=-1, keepdims=True) + EPS)
        o_ref[...] = ((xv * r) * g_ref[...]).astype(BF16)

    return pl.pallas_call(
        body, name="norm_fwd", grid=(T // tm,),
        in_specs=[pl.BlockSpec((tm, Dm), lambda i: (i, 0)), pl.BlockSpec((1, Dm), lambda i: (0, 0))],
        out_specs=pl.BlockSpec((tm, Dm), lambda i: (i, 0)),
        out_shape=_sds((T, Dm), BF16), compiler_params=_cp("parallel"))(x, g)


def _norm_bwd(x, dh, dres, g):
    T, Dm = x.shape
    tm = min(T, _ROW_TILE)

    def body(x_ref, dh_ref, dr_ref, g_ref, dx_ref, dg_ref):
        i = pl.program_id(0)
        xv = x_ref[...]
        r = lax.rsqrt(jnp.mean(xv * xv, axis=-1, keepdims=True) + EPS)
        xh = xv * r
        dhv = dh_ref[...]
        dxh = dhv * g_ref[...]
        dx_ref[...] = dr_ref[...] + r * (dxh - xh * jnp.mean(dxh * xh, axis=-1, keepdims=True))
        part = jnp.sum(dhv * xh, axis=0, keepdims=True)

        @pl.when(i == 0)
        def _():
            dg_ref[...] = part

        @pl.when(i > 0)
        def _():
            dg_ref[...] += part

    row = pl.BlockSpec((tm, Dm), lambda i: (i, 0))
    vec = pl.BlockSpec((1, Dm), lambda i: (0, 0))
    return pl.pallas_call(
        body, name="norm_bwd", grid=(T // tm,),
        in_specs=[row, row, row, vec], out_specs=[row, vec],
        out_shape=[_sds((T, Dm), F32), _sds((1, Dm), F32)], compiler_params=_cp("arbitrary"))(x, dh, dres, g)


def _whole(w):
    return pl.BlockSpec(w.shape, lambda *g: (0,) * w.ndim)


def _mm_nn(a, w, *, out_dtype, name, res=None, tm=None):
    M, K = a.shape
    N = w.shape[-1] if w.ndim == 2 else w.shape[0] * w.shape[2]
    tm = min(M, tm or _ROW_TILE)
    has_res = res is not None

    def body(*refs):
        a_ref, w_ref = refs[0], refs[1]
        o_ref = refs[-1]
        av = a_ref[...].astype(BF16)
        parts = [w_ref[...]] if w.ndim == 2 else [w_ref[s] for s in range(w.shape[0])]
        ns = N // len(parts)
        for s, wv in enumerate(parts):
            cols = pl.ds(s * ns, ns)
            acc = _dot(av, wv)
            if has_res:
                acc = acc + refs[2][:, cols]
            o_ref[:, cols] = acc.astype(out_dtype)

    in_specs = [pl.BlockSpec((tm, K), lambda i: (i, 0)), _whole(w)]
    args = [a, w]
    if has_res:
        in_specs.append(pl.BlockSpec((tm, N), lambda i: (i, 0)))
        args.append(res)
    return pl.pallas_call(
        body, name=name, grid=(M // tm,), in_specs=in_specs, out_specs=pl.BlockSpec((tm, N), lambda i: (i, 0)),
        out_shape=_sds((M, N), out_dtype), compiler_params=_cp("parallel"))(*args)


def _a_spec(a, tm, tn, row_of, col_of):
    if a.ndim == 2:
        return pl.BlockSpec((tm, tn), lambda *g: (row_of(*g), col_of(*g)))
    bph = a.shape[2] // tn
    return pl.BlockSpec((None, tm, tn), lambda *g: (col_of(*g) // bph, row_of(*g), col_of(*g) % bph))


def _a_cols(a):
    return a.shape[1] if a.ndim == 2 else a.shape[0] * a.shape[2]


def _mm_nt(a, w, *, name, tm=None):
    M = a.shape[-2]
    Kw = w.shape[-2]
    tm = min(M, tm or _ROW_TILE)

    def body(a_ref, w_ref, o_ref):
        if w.ndim == 2:
            o_ref[...] = _dot_nt(a_ref[...].astype(BF16), w_ref[...])
            return
        S, ns = w.shape[0], w.shape[2]
        acc = None
        for s in range(S):
            if a.ndim == 2:
                piece = a_ref[:, pl.ds(s * ns, ns)]
            else:
                per_half = S // 2
                piece = a_ref[s // per_half, :, pl.ds((s % per_half) * ns, ns)]
            part = _dot_nt(piece.astype(BF16), w_ref[s])
            acc = part if acc is None else acc + part
        o_ref[...] = acc

    a_spec = (pl.BlockSpec((tm, a.shape[1]), lambda i: (i, 0)) if a.ndim == 2
              else pl.BlockSpec((2, tm, a.shape[2]), lambda i: (0, i, 0)))
    return pl.pallas_call(
        body, name=name, grid=(M // tm,), in_specs=[a_spec, _whole(w)],
        out_specs=pl.BlockSpec((tm, Kw), lambda i: (i, 0)), out_shape=_sds((M, Kw), F32),
        compiler_params=_cp("parallel"))(a, w)


def _mm_tn(a, b, *, tk, tn, name, shards=None, tm=None):
    M, K = a.shape
    N = _a_cols(b)
    tm = min(M, tm or _ROW_TILE)

    def body(a_ref, b_ref, o_ref):
        m = pl.program_id(2)
        part = _dot_tn(a_ref[...].astype(BF16), b_ref[...].astype(BF16))

        @pl.when(m == 0)
        def _():
            o_ref[...] = part

        @pl.when(m > 0)
        def _():
            o_ref[...] += part

    if shards is None:
        out_spec = pl.BlockSpec((tk, tn), lambda k, j, m: (k, j))
        out_shape = _sds((K, N), F32)
    else:
        bps = (N // shards) // tn
        out_spec = pl.BlockSpec((None, tk, tn), lambda k, j, m: (j // bps, k, j % bps))
        out_shape = _sds((shards, K, N // shards), F32)
    return pl.pallas_call(
        body, name=name, grid=(K // tk, N // tn, M // tm),
        in_specs=[pl.BlockSpec((tm, tk), lambda k, j, m: (m, k)),
                  _a_spec(b, tm, tn, lambda k, j, m: m, lambda k, j, m: j)],
        out_specs=out_spec, out_shape=out_shape,
        compiler_params=_cp("parallel", "parallel", "arbitrary"))(a, b)


def _halo_specs(T, tm, cw, ic):
    nb = tm // HALO
    last = T // HALO - 1

    def mk(rows, row_of):
        return pl.BlockSpec((rows, cw), lambda *g: (row_of(ic(*g)[0]), ic(*g)[1]))

    return [mk(HALO, lambda i: jnp.maximum(i * nb - 1, 0)), mk(tm, lambda i: i),
            mk(HALO, lambda i: jnp.minimum((i + 1) * nb, last))]


def _ext(prev_ref, cur_ref, next_ref, i, n):
    p = jnp.where(i > 0, prev_ref[...].astype(F32), 0.0)
    nx = jnp.where(i < n - 1, next_ref[...].astype(F32), 0.0)
    return jnp.concatenate([p, cur_ref[...].astype(F32), nx], axis=0)


def _dn(e):
    return pltpu.roll(e, 1, 0)


def _up(e):
    return pltpu.roll(e, e.shape[0] - 1, 0)


def _mid(e):
    return e[HALO:e.shape[0] - HALO]


def _taps(e):
    return _dn(e), e, _up(e)


def _conv3(taps, w):
    return taps[0] * w[0:1] + taps[1] * w[1:2] + taps[2] * w[2:3]


def _conv3_t(e, w):
    return _up(e) * w[0:1] + e * w[1:2] + _dn(e) * w[2:3]


def _conv3_wgrad(d, taps):
    return jnp.concatenate([jnp.sum(_mid(d * tap), axis=0, keepdims=True) for tap in taps], axis=0)


def _sigmoid(x):
    return 1.0 / (1.0 + jnp.exp(-x))


def _accum(ref, i, part):
    @pl.when(i == 0)
    def _():
        ref[...] = part

    @pl.when(i > 0)
    def _():
        ref[...] += part


def _ffn_act_fwd(up, cw):
    T = up.shape[0]
    tm = min(T, _FFN_ROW_TILE)
    cb = D_FF // 2
    nblk = D_FF // cb
    n = T // tm

    def body(gp, gc, gn, vp, vc, vn, wg_ref, wv_ref, o_ref):
        i = pl.program_id(1)
        gate = _conv3(_taps(_ext(gp, gc, gn, i, n)), wg_ref[...])
        val = _conv3(_taps(_ext(vp, vc, vn, i, n)), wv_ref[...])
        o_ref[...] = _mid(gate * _sigmoid(gate) * val).astype(BF16)

    return pl.pallas_call(
        body, name="ffn_act_fwd", grid=(nblk, n),
        in_specs=_halo_specs(T, tm, cb, lambda j, i: (i, j)) + _halo_specs(T, tm, cb, lambda j, i: (i, j + nblk))
        + [pl.BlockSpec((3, cb), lambda j, i: (0, j)), pl.BlockSpec((3, cb), lambda j, i: (0, j + nblk))],
        out_specs=pl.BlockSpec((tm, cb), lambda j, i: (i, j)),
        out_shape=_sds((T, D_FF), BF16), compiler_params=_cp("parallel", "parallel"))(
            up, up, up, up, up, up, cw, cw)


def _ffn_act_bwd(up, dact, cw):
    T = up.shape[0]
    tm = min(T, _FFN_ROW_TILE)
    cb = D_FF // 2
    nblk = D_FF // cb
    n = T // tm

    def body(gp, gc, gn, vp, vc, vn, dp_, dc, dn_, wg_ref, wv_ref, dup_ref, dcw_ref):
        i = pl.program_id(1)
        wg, wv = wg_ref[...], wv_ref[...]
        eg = _taps(_ext(gp, gc, gn, i, n))
        ev = _taps(_ext(vp, vc, vn, i, n))
        ed = _ext(dp_, dc, dn_, i, n)
        gate = _conv3(eg, wg)
        val = _conv3(ev, wv)
        sg = _sigmoid(gate)
        d_gate = ed * val * (sg * (1.0 + gate * (1.0 - sg)))
        d_val = ed * (gate * sg)
        dup_ref[0] = _mid(_conv3_t(d_gate, wg)).astype(BF16)
        dup_ref[1] = _mid(_conv3_t(d_val, wv)).astype(BF16)
        part = jnp.stack([_conv3_wgrad(d_gate, eg), _conv3_wgrad(d_val, ev)], axis=0)
        _accum(dcw_ref, i, part)

    return pl.pallas_call(
        body, name="ffn_act_bwd", grid=(nblk, n),
        in_specs=_halo_specs(T, tm, cb, lambda j, i: (i, j)) + _halo_specs(T, tm, cb, lambda j, i: (i, j + nblk))
        + _halo_specs(T, tm, cb, lambda j, i: (i, j))
        + [pl.BlockSpec((3, cb), lambda j, i: (0, j)), pl.BlockSpec((3, cb), lambda j, i: (0, j + nblk))],
        out_specs=[pl.BlockSpec((2, tm, cb), lambda j, i: (0, i, j)),
                   pl.BlockSpec((2, 3, cb), lambda j, i: (0, 0, j))],
        out_shape=[_sds((2, T, D_FF), BF16), _sds((2, 3, D_FF), F32)],
        compiler_params=_cp("parallel", "arbitrary"))(up, up, up, up, up, up, dact, dact, dact, cw, cw)


_CB_BLK, _CC_BLK, _CX_BLK = 5, 6, 7


def _convmix_fwd(p, w):
    T = p.shape[0]
    tm = min(T, _ROW_TILE)
    n = T // tm

    def body(cb_ref, ccp, ccc, ccn, cxp, cxc, cxn, w_ref, o_ref):
        i = pl.program_id(0)
        z = _ext(ccp, ccc, ccn, i, n) * _ext(cxp, cxc, cxn, i, n)
        o_ref[...] = (cb_ref[...] * _mid(_conv3(_taps(z), w_ref[...]))).astype(BF16)

    return pl.pallas_call(
        body, name="convmix_fwd", grid=(n,),
        in_specs=[pl.BlockSpec((tm, CONV_W), lambda i: (i, _CB_BLK))]
        + _halo_specs(T, tm, CONV_W, lambda i: (i, _CC_BLK)) + _halo_specs(T, tm, CONV_W, lambda i: (i, _CX_BLK))
        + [pl.BlockSpec((3, CONV_W), lambda i: (0, 0))],
        out_specs=pl.BlockSpec((tm, CONV_W), lambda i: (i, 0)),
        out_shape=_sds((T, CONV_W), BF16), compiler_params=_cp("parallel"))(p, p, p, p, p, p, p, w)


def _convmix_bwd(p, dmix, w):
    T = p.shape[0]
    tm = min(T, _ROW_TILE)
    n = T // tm
    dblk = (ATTN_W + SG_W) // CONV_W

    def body(cbp, cbc, cbn, ccp, ccc, ccn, cxp, cxc, cxn, dp_, dc, dn_, w_ref, o_ref, dw_ref):
        i = pl.program_id(0)
        wv = w_ref[...]
        ecb = _ext(cbp, cbc, cbn, i, n)
        ecc = _ext(ccp, ccc, ccn, i, n)
        ecx = _ext(cxp, cxc, cxn, i, n)
        ed = _ext(dp_, dc, dn_, i, n)
        z = _taps(ecc * ecx)
        d_cz = ed * ecb
        d_z = _conv3_t(d_cz, wv)
        o_ref[...] = jnp.concatenate([_mid(ed * _conv3(z, wv)), _mid(d_z * ecx), _mid(d_z * ecc)],
                                     axis=1).astype(BF16)
        _accum(dw_ref, i, _conv3_wgrad(d_cz, z))

    return pl.pallas_call(
        body, name="convmix_bwd", grid=(n,),
        in_specs=_halo_specs(T, tm, CONV_W, lambda i: (i, _CB_BLK)) + _halo_specs(T, tm, CONV_W, lambda i: (i, _CC_BLK))
        + _halo_specs(T, tm, CONV_W, lambda i: (i, _CX_BLK)) + _halo_specs(T, tm, CONV_W, lambda i: (i, dblk))
        + [pl.BlockSpec((3, CONV_W), lambda i: (0, 0))],
        out_specs=[pl.BlockSpec((tm, 3 * CONV_W), lambda i: (i, 0)), pl.BlockSpec((3, CONV_W), lambda i: (0, 0))],
        out_shape=[_sds((T, 3 * CONV_W), BF16), _sds((3, CONV_W), F32)],
        compiler_params=_cp("arbitrary"))(p, p, p, p, p, p, p, p, p, dmix, dmix, dmix, w)


_SU_BLK, _SV_BLK = 3, 4


def _sg_mixed(vnb, w_ref, bias, ch, pr, lo):
    vp = vnb[ch * SG_CHUNK:(ch + 1) * SG_CHUNK, pr * LANES:(pr + 1) * LANES]
    zero = jnp.zeros_like(vp)
    return (_dot(w_ref[2 * pr], jnp.where(lo, vp, zero)) + _dot(w_ref[2 * pr + 1], jnp.where(lo, zero, vp))
            + bias[:, pr * LANES:(pr + 1) * LANES]), vp


def _sg_fwd(p, g, w, bias):
    T = p.shape[0]
    tm = min(T, _ROW_TILE)

    def body(su_ref, sv_ref, g_ref, w_ref, b_ref, o_ref):
        lo = lax.broadcasted_iota(jnp.int32, (SG_CHUNK, LANES), 1) < HEAD_DIM
        sv = sv_ref[...]
        r = lax.rsqrt(jnp.mean(sv * sv, axis=-1, keepdims=True) + EPS)
        vnb = ((sv * r) * g_ref[...]).astype(BF16)
        bias_v = b_ref[...]
        for ch in range(tm // SG_CHUNK):
            for pr in range(2):
                mixed, _ = _sg_mixed(vnb, w_ref, bias_v, ch, pr, lo)
                rows, cols = pl.ds(ch * SG_CHUNK, SG_CHUNK), pl.ds(pr * LANES, LANES)
                o_ref[rows, cols] = (su_ref[rows, cols] * mixed).astype(BF16)

    return pl.pallas_call(
        body, name="sg_fwd", grid=(T // tm,),
        in_specs=[pl.BlockSpec((tm, SG_W), lambda i: (i, _SU_BLK)), pl.BlockSpec((tm, SG_W), lambda i: (i, _SV_BLK)),
                  pl.BlockSpec((1, SG_W), lambda i: (0, 0)), pl.BlockSpec((4, SG_CHUNK, SG_CHUNK), lambda i: (0, 0, 0)),
                  pl.BlockSpec((SG_CHUNK, SG_W), lambda i: (0, 0))],
        out_specs=pl.BlockSpec((tm, SG_W), lambda i: (i, 0)),
        out_shape=_sds((T, SG_W), BF16), compiler_params=_cp("parallel"))(p, p, g, w, bias)


def _sg_bwd(p, dmix, g, w, wt, bias):
    T = p.shape[0]
    tm = min(T, _ROW_TILE)
    dblk = ATTN_W // SG_W

    def body(su_ref, sv_ref, d_ref, g_ref, w_ref, wt_ref, b_ref, o_ref, dw_ref, db_ref, dg_ref, dvn_ref):
        i = pl.program_id(0)
        lo = lax.broadcasted_iota(jnp.int32, (SG_CHUNK, LANES), 1) < HEAD_DIM
        sv = sv_ref[...]
        gv = g_ref[...]
        r = lax.rsqrt(jnp.mean(sv * sv, axis=-1, keepdims=True) + EPS)
        xh = sv * r
        vnb = (xh * gv).astype(BF16)
        bias_v = b_ref[...]
        dw = [jnp.zeros((SG_CHUNK, SG_CHUNK), F32) for _ in range(4)]
        db = jnp.zeros((SG_CHUNK, SG_W), F32)
        for ch in range(tm // SG_CHUNK):
            dbs = []
            for pr in range(2):
                mixed, vp = _sg_mixed(vnb, w_ref, bias_v, ch, pr, lo)
                rows, cols = pl.ds(ch * SG_CHUNK, SG_CHUNK), pl.ds(pr * LANES, LANES)
                dgo = d_ref[rows, cols]
                o_ref[rows, cols] = (dgo * mixed).astype(BF16)
                dm = dgo * su_ref[rows, cols]
                dmb = dm.astype(BF16)
                zero = jnp.zeros_like(dmb)
                dw[2 * pr] += _dot_nt(jnp.where(lo, dmb, zero), vp)
                dw[2 * pr + 1] += _dot_nt(jnp.where(lo, zero, dmb), vp)
                dvn_ref[rows, cols] = jnp.where(lo, _dot(wt_ref[2 * pr], dmb), _dot(wt_ref[2 * pr + 1], dmb))
                dbs.append(dm)
            db += jnp.concatenate(dbs, axis=1)
        dvn = dvn_ref[...]
        dxh = dvn * gv
        o_ref[:, pl.ds(SG_W, SG_W)] = (r * (dxh - xh * jnp.mean(dxh * xh, axis=-1, keepdims=True))).astype(BF16)
        _accum(dw_ref, i, jnp.stack(dw, axis=0))
        _accum(db_ref, i, db)
        _accum(dg_ref, i, jnp.sum(dvn * xh, axis=0, keepdims=True))

    wspec = pl.BlockSpec((4, SG_CHUNK, SG_CHUNK), lambda i: (0, 0, 0))
    return pl.pallas_call(
        body, name="sg_bwd", grid=(T // tm,),
        in_specs=[pl.BlockSpec((tm, SG_W), lambda i: (i, _SU_BLK)), pl.BlockSpec((tm, SG_W), lambda i: (i, _SV_BLK)),
                  pl.BlockSpec((tm, SG_W), lambda i: (i, dblk)), pl.BlockSpec((1, SG_W), lambda i: (0, 0)),
                  wspec, wspec, pl.BlockSpec((SG_CHUNK, SG_W), lambda i: (0, 0))],
        out_specs=[pl.BlockSpec((tm, 2 * SG_W), lambda i: (i, 0)), wspec,
                   pl.BlockSpec((SG_CHUNK, SG_W), lambda i: (0, 0)), pl.BlockSpec((1, SG_W), lambda i: (0, 0))],
        out_shape=[_sds((T, 2 * SG_W), BF16), _sds((4, SG_CHUNK, SG_CHUNK), F32),
                   _sds((SG_CHUNK, SG_W), F32), _sds((1, SG_W), F32)],
        scratch_shapes=[pltpu.VMEM((tm, SG_W), F32)],
        compiler_params=_cp("arbitrary"))(p, p, dmix, g, w, wt, bias)


_ATTN_TQ = 256
_ATTN_TK = 512
_SOFTMAX_STRIP = 32
_ONES_ROWS = 16


def _head_mean(v, bmat):
    return jnp.dot(v, bmat, preferred_element_type=F32, precision=lax.Precision.HIGHEST)


def _swap16(y):
    lane = lax.broadcasted_iota(jnp.int32, y.shape, 1)
    return jnp.where(lane % 32 < 16, pltpu.roll(y, y.shape[1] - 16, 1), pltpu.roll(y, 16, 1))


def _rope(y, cos, sin):
    return y * cos + _swap16(y) * sin


def _rope_t(dy, cos, sin):
    return dy * cos + _swap16(dy * sin)


def _dup_rows(t, gidx):
    h = t[gidx * HEAD_DIM:(gidx + 1) * HEAD_DIM]
    return jnp.concatenate([h, h], axis=0)


def _qk_prep(p, gq, gk, cos, sin, bmat, tk):
    T = p.shape[0]
    nk = T // tk
    scale = HEAD_DIM ** -0.5

    def body(q_ref, kv_ref, gq_ref, gk_ref, cos_ref, sin_ref, b_ref, qo_ref, kt_ref, kd_ref, vt_ref, v1_ref):
        cosv, sinv, bm = cos_ref[...], sin_ref[...], b_ref[...]
        for pr in range(ATTN_W // LANES):
            cols = pl.ds(pr * LANES, LANES)
            xq = q_ref[:, cols]
            r = lax.rsqrt(_head_mean(xq * xq, bm) + EPS)
            qo_ref[:, cols] = (_rope((xq * r) * gq_ref[:, cols], cosv, sinv) * scale).astype(BF16)
        xk = kv_ref[:, pl.ds(0, LANES)]
        r = lax.rsqrt(_head_mean(xk * xk, bm) + EPS)
        kt = _rope((xk * r) * gk_ref[...], cosv, sinv).T
        vt = kv_ref[:, pl.ds(LANES, LANES)].T
        for gidx in range(2):
            kdup = _dup_rows(kt, gidx)
            kt_ref[gidx] = kdup.astype(BF16)
            vt_ref[gidx] = _dup_rows(vt, gidx).astype(BF16)
            v1_ref[gidx] = jnp.concatenate([vt[gidx * HEAD_DIM:(gidx + 1) * HEAD_DIM],
                                            jnp.ones((_ONES_ROWS, tk), F32)], axis=0).astype(BF16)
            kd_ref[gidx] = kdup.T.astype(BF16)

    tspec = pl.BlockSpec((2, None, LANES, tk), lambda i: (0, i, 0, 0))
    dspec = pl.BlockSpec((2, tk, LANES), lambda i: (0, i, 0))
    tab = pl.BlockSpec((tk, LANES), lambda i: (i, 0))
    return pl.pallas_call(
        body, name="qk_prep", grid=(nk,),
        in_specs=[pl.BlockSpec((tk, ATTN_W), lambda i: (i, 0)), pl.BlockSpec((tk, 2 * KV_W), lambda i: (i, ATTN_W // (2 * KV_W))),
                  pl.BlockSpec((1, ATTN_W), lambda i: (0, 0)), pl.BlockSpec((1, KV_W), lambda i: (0, 0)),
                  tab, tab, pl.BlockSpec((LANES, LANES), lambda i: (0, 0))],
        out_specs=[pl.BlockSpec((tk, ATTN_W), lambda i: (i, 0)), tspec, dspec, tspec,
                   pl.BlockSpec((2, None, HEAD_DIM + _ONES_ROWS, tk), lambda i: (0, i, 0, 0))],
        out_shape=[_sds((T, ATTN_W), BF16), _sds((2, nk, LANES, tk), BF16), _sds((2, T, LANES), BF16),
                   _sds((2, nk, LANES, tk), BF16), _sds((2, nk, HEAD_DIM + _ONES_ROWS, tk), BF16)],
        compiler_params=_cp("parallel"))(p, p, gq, gk, cos, sin, bmat)


def _stack_heads(t):
    lo = lax.broadcasted_iota(jnp.int32, (t.shape[0], LANES), 1) < HEAD_DIM
    parts = []
    for pr in range(2):
        tp = t[:, pr * LANES:(pr + 1) * LANES]
        zero = jnp.zeros_like(tp)
        parts += [jnp.where(lo, tp, zero), jnp.where(lo, zero, tp)]
    return jnp.concatenate(parts, axis=0)


def _rows8_reduce(s, op):
    parts = [s[r:r + 8] for r in range(0, s.shape[0], 8)]
    while len(parts) > 1:
        parts = [op(parts[k], parts[k + 1]) for k in range(0, len(parts) - 1, 2)] + (
            [parts[-1]] if len(parts) % 2 else [])
    return parts[0]


def _attn_fwd(q, kd, v1, tq):
    T = q.shape[0]
    nk, tk = v1.shape[1], v1.shape[3]
    vrows = v1.shape[2]
    nq = T // tq
    sq = 4 * tq
    strip = _SOFTMAX_STRIP
    assert nk % 2 == 0, "key blocks are taken two per loop trip"

    def body(q_ref, kd_ref, v1_ref, o_ref, lse_ref, qst_ref, sa_ref, sb_ref, pa_ref, pb_ref, m_ref, acc_ref):
        qst_ref[...] = _stack_heads(q_ref[...]).astype(F32).T.astype(BF16)
        m_ref[...] = jnp.full((1, sq), -jnp.inf, F32)
        acc_ref[...] = jnp.zeros((vrows, sq), F32)

        def scores(j):
            return _dot(kd_ref[pl.ds(pl.multiple_of(j * tk, tk), tk), :], qst_ref[...])

        def block_max(s_ref):
            m8 = None
            for c in range(tk // strip):
                part = _rows8_reduce(s_ref[pl.ds(c * strip, strip), :], jnp.maximum)
                m8 = part if m8 is None else jnp.maximum(m8, part)
            return m8

        def exp_pass(s_ref, p_ref, m8):
            m_old = m_ref[...]
            m_new = jnp.maximum(m_old, jnp.max(m8, axis=0, keepdims=True))
            m_ref[...] = m_new
            for c in range(tk // strip):
                rows = pl.ds(c * strip, strip)
                p_ref[rows, :] = jnp.exp(s_ref[rows, :] - m_new).astype(BF16)
            return jnp.exp(m_old - m_new)

        def apply(p_ref, alpha, j):
            acc_ref[...] = alpha * acc_ref[...] + _dot(v1_ref[j], p_ref[...])

        sa_ref[...] = scores(0)

        def pair(t, max_a):
            j = 2 * t
            sb_ref[...] = scores(j + 1)
            alpha = exp_pass(sa_ref, pa_ref, max_a)
            max_b = block_max(sb_ref)
            apply(pa_ref, alpha, j)
            sa_ref[...] = scores(jnp.minimum(j + 2, nk - 1))
            alpha = exp_pass(sb_ref, pb_ref, max_b)
            max_a = block_max(sa_ref)
            apply(pb_ref, alpha, j + 1)
            return max_a

        lax.fori_loop(0, nk // 2, pair, block_max(sa_ref))
        l = acc_ref[pl.ds(HEAD_DIM, 1), :]
        on = acc_ref[pl.ds(0, HEAD_DIM), :] / l
        pairs = []
        for pr in range(2):
            two = jnp.concatenate([on[:, (2 * pr) * tq:(2 * pr + 1) * tq], on[:, (2 * pr + 1) * tq:(2 * pr + 2) * tq]],
                                  axis=0)
            pairs.append(two.T)
        o_ref[...] = jnp.concatenate(pairs, axis=1).astype(BF16)
        lse_ref[...] = jnp.broadcast_to(m_ref[...] + jnp.log(l), (LANES, sq)).T

    row = pltpu.VMEM((1, sq), F32)
    return pl.pallas_call(
        body, name="attn_fwd", grid=(2, nq),
        in_specs=[pl.BlockSpec((tq, 2 * LANES), lambda g, i: (i, g)),
                  pl.BlockSpec((None, T, LANES), lambda g, i: (g, 0, 0)),
                  pl.BlockSpec((None, nk, vrows, tk), lambda g, i: (g, 0, 0, 0))],
        out_specs=[pl.BlockSpec((tq, 2 * LANES), lambda g, i: (i, g)),
                   pl.BlockSpec((None, None, sq, LANES), lambda g, i: (g, i, 0, 0))],
        out_shape=[_sds((T, ATTN_W), BF16), _sds((2, nq, sq, LANES), F32)],
        scratch_shapes=[pltpu.VMEM((LANES, sq), BF16), pltpu.VMEM((tk, sq), F32), pltpu.VMEM((tk, sq), F32),
                        pltpu.VMEM((tk, sq), BF16), pltpu.VMEM((tk, sq), BF16), row,
                        pltpu.VMEM((vrows, sq), F32)],
        compiler_params=_cp("parallel", "parallel"))(q, kd, v1)


def _attn_bwd(q, o, dmix, lse, kt, vt, tq):
    T = q.shape[0]
    nk, tk = kt.shape[1], kt.shape[3]
    nq = T // tq
    sq = 4 * tq
    rep = tk // LANES

    def body(q_ref, o_ref, do_ref, lse_ref, kt_ref, vt_ref, dq_ref, dkt_ref, dvt_ref):
        i = pl.program_id(1)
        qs = _stack_heads(q_ref[...])
        dof = _stack_heads(do_ref[...])
        dos = dof.astype(BF16)
        qst = qs.astype(F32).T.astype(BF16)
        dost = dof.T.astype(BF16)
        o_pair = o_ref[...].astype(F32)
        os_ = jnp.concatenate([o_pair[:, 0:LANES], o_pair[:, 0:LANES], o_pair[:, LANES:], o_pair[:, LANES:]], axis=0)
        delta = jnp.sum(dof * os_, axis=-1, keepdims=True)
        lse_t = jnp.concatenate([lse_ref[...]] * rep, axis=1)

        @pl.when(i == 0)
        def _():
            dkt_ref[...] = jnp.zeros_like(dkt_ref)
            dvt_ref[...] = jnp.zeros_like(dvt_ref)

        def step(j, dqt):
            ktb = kt_ref[j]
            pexp = jnp.exp(_dot(qs, ktb) - lse_t)
            ds = pexp * (_dot(dos, vt_ref[j]) - delta)
            pb = pexp.astype(BF16)
            dsb = ds.astype(BF16)
            dvt_ref[j] += _dot(dost, pb)
            dkt_ref[j] += _dot(qst, dsb)
            return dqt + _dot_nt(ktb, dsb)

        dqt = lax.fori_loop(0, nk, step, jnp.zeros((LANES, sq), F32))
        pairs = []
        for pr in range(2):
            two = jnp.concatenate([dqt[0:HEAD_DIM, (2 * pr) * tq:(2 * pr + 1) * tq],
                                   dqt[0:HEAD_DIM, (2 * pr + 1) * tq:(2 * pr + 2) * tq]], axis=0)
            pairs.append(two.T)
        dq_ref[...] = jnp.concatenate(pairs, axis=1)

    tspec = pl.BlockSpec((None, nk, LANES, tk), lambda g, i: (g, 0, 0, 0))
    qspec = pl.BlockSpec((tq, 2 * LANES), lambda g, i: (i, g))
    return pl.pallas_call(
        body, name="attn_bwd", grid=(2, nq),
        in_specs=[qspec, qspec, qspec, pl.BlockSpec((None, None, sq, LANES), lambda g, i: (g, i, 0, 0)),
                  tspec, tspec],
        out_specs=[qspec, tspec, tspec],
        out_shape=[_sds((T, ATTN_W), F32), _sds((2, nk, LANES, tk), F32), _sds((2, nk, LANES, tk), F32)],
        compiler_params=_cp("parallel", "arbitrary"))(q, o, dmix, lse, kt, vt)


def _fold_t(t_ref):
    rows = []
    for gidx in range(2):
        t = t_ref[gidx]
        rows.append(t[0:HEAD_DIM] + t[HEAD_DIM:2 * HEAD_DIM])
    return jnp.concatenate(rows, axis=0).T


def _qk_bwd(p, dq, dkt, dvt, gq, gk, cos, sin, bmat):
    T = p.shape[0]
    nk, tk = dkt.shape[1], dkt.shape[3]
    scale = HEAD_DIM ** -0.5

    def norm_bwd(x, dy, gain, bm):
        r = lax.rsqrt(_head_mean(x * x, bm) + EPS)
        xh = x * r
        dxh = dy * gain
        return r * (dxh - xh * _head_mean(dxh * xh, bm)), jnp.sum(dy * xh, axis=0, keepdims=True)

    def body(q_ref, kv_ref, dq_ref, dkt_ref, dvt_ref, gq_ref, gk_ref, cos_ref, sin_ref, b_ref, o_ref, dgq_ref, dgk_ref):
        i = pl.program_id(0)
        cosv, sinv, bm = cos_ref[...], sin_ref[...], b_ref[...]
        dgq = []
        for pr in range(ATTN_W // LANES):
            cols = pl.ds(pr * LANES, LANES)
            dy = _rope_t(dq_ref[:, cols] * scale, cosv, sinv)
            dx, dg = norm_bwd(q_ref[:, cols], dy, gq_ref[:, cols], bm)
            o_ref[:, cols] = dx.astype(BF16)
            dgq.append(dg)
        dy = _rope_t(_fold_t(dkt_ref), cosv, sinv)
        dx, dgk = norm_bwd(kv_ref[:, pl.ds(0, LANES)], dy, gk_ref[...], bm)
        o_ref[:, pl.ds(ATTN_W, LANES)] = dx.astype(BF16)
        o_ref[:, pl.ds(ATTN_W + LANES, LANES)] = _fold_t(dvt_ref).astype(BF16)
        _accum(dgq_ref, i, jnp.concatenate(dgq, axis=1))
        _accum(dgk_ref, i, dgk)

    tspec = pl.BlockSpec((2, None, LANES, tk), lambda i: (0, i, 0, 0))
    tab = pl.BlockSpec((tk, LANES), lambda i: (i, 0))
    return pl.pallas_call(
        body, name="qk_bwd", grid=(nk,),
        in_specs=[pl.BlockSpec((tk, ATTN_W), lambda i: (i, 0)), pl.BlockSpec((tk, 2 * KV_W), lambda i: (i, ATTN_W // (2 * KV_W))),
                  pl.BlockSpec((tk, ATTN_W), lambda i: (i, 0)), tspec, tspec,
                  pl.BlockSpec((1, ATTN_W), lambda i: (0, 0)), pl.BlockSpec((1, KV_W), lambda i: (0, 0)),
                  tab, tab, pl.BlockSpec((LANES, LANES), lambda i: (0, 0))],
        out_specs=[pl.BlockSpec((tk, ATTN_W + 2 * KV_W), lambda i: (i, 0)),
                   pl.BlockSpec((1, ATTN_W), lambda i: (0, 0)), pl.BlockSpec((1, KV_W), lambda i: (0, 0))],
        out_shape=[_sds((T, ATTN_W + 2 * KV_W), BF16), _sds((1, ATTN_W), F32), _sds((1, KV_W), F32)],
        compiler_params=_cp("arbitrary"))(p, p, dq, dkt, dvt, gq, gk, cos, sin, bmat)


def _loss_head(y, target):
    T, Dm = y.shape
    tm = min(T, _ROW_TILE)

    def body(y_ref, t_ref, dy_ref, l_ref):
        i = pl.program_id(0)
        err = y_ref[...] - t_ref[...]
        dy_ref[...] = err * (1.0 / Dm)
        part = jnp.sum(jnp.sum(err * err, axis=-1, keepdims=True), axis=0, keepdims=True) * (0.5 / Dm)
        _accum(l_ref, i, jnp.broadcast_to(part, (8, LANES)))

    row = pl.BlockSpec((tm, Dm), lambda i: (i, 0))
    return pl.pallas_call(
        body, name="loss_head", grid=(T // tm,), in_specs=[row, row],
        out_specs=[row, pl.BlockSpec((8, LANES), lambda i: (0, 0))],
        out_shape=[_sds((T, Dm), F32), _sds((8, LANES), F32)], compiler_params=_cp("arbitrary"))(y, target)


def _adamw(w, g, m, v, name):
    R, C = w.shape
    tr = R
    for cand in (512, 256, 128, 64, 32, 16, 8):
        if R % cand == 0:
            tr = cand
            break
    c1 = 1.0 - ADAM_B1 ** ADAM_STEP
    c2 = 1.0 - ADAM_B2 ** ADAM_STEP

    def body(w_ref, g_ref, m_ref, v_ref, d_ref, mo_ref, vo_ref):
        gv = g_ref[...]
        mn = ADAM_B1 * m_ref[...] + (1.0 - ADAM_B1) * gv
        vn = ADAM_B2 * v_ref[...] + (1.0 - ADAM_B2) * (gv * gv)
        d_ref[...] = -ADAM_LR * ((mn / c1) / (jnp.sqrt(vn / c2) + ADAM_EPS) + ADAM_WD * w_ref[...])
        mo_ref[...] = mn
        vo_ref[...] = vn

    blk = pl.BlockSpec((tr, C), lambda i: (i, 0))
    return pl.pallas_call(
        body, name=name, grid=(R // tr,), in_specs=[blk] * 4, out_specs=[blk] * 3,
        out_shape=[_sds((R, C), F32)] * 3, compiler_params=_cp("parallel"))(w, g, m, v)


def _cast_bf16(w, name):
    R, C = w.shape
    tr = 512 if R % 512 == 0 else 256

    def body(w_ref, o_ref):
        o_ref[...] = w_ref[...].astype(BF16)

    blk = pl.BlockSpec((tr, C), lambda i: (i, 0))
    return pl.pallas_call(body, name=name, grid=(R // tr,), in_specs=[blk], out_specs=blk,
                          out_shape=_sds((R, C), BF16), compiler_params=_cp("parallel"))(w)


def _position():
    x, y, c = lax.axis_index("x"), lax.axis_index("y"), lax.axis_index("c")
    return x, y, c


def _other_chips(x, y):
    return [(1 - x, y), (x, 1 - y), (1 - x, 1 - y)]


_HBM = pl.BlockSpec(memory_space=pltpu.HBM)
_SEM = pl.BlockSpec(memory_space=pltpu.SEMAPHORE)
_EFFECT = pltpu.SideEffectType.DATAFLOW_SIDE_EFFECTING


def _chip_copies(srcs, lands, send_sems, recv_sems, per_chip, arriving):
    x, y, c = _position()
    me = 2 * x + y
    copies = []
    for t, (src, land) in enumerate(zip(srcs, lands)):
        for k, (px, py) in enumerate(_other_chips(x, y)):
            peer = 2 * px + py
            copies.append(pltpu.make_async_remote_copy(
                src_ref=src.at[peer] if per_chip else src, dst_ref=land.at[peer if arriving else me],
                send_sem=send_sems[3 * t + k], recv_sem=recv_sems[3 * t + k],
                device_id=(px, py, c), device_id_type=MESH))
    return copies


def _chips_start(srcs, per_chip, name):
    n = len(srcs)
    slab = [s.shape[1:] if per_chip else s.shape for s in srcs]
    lands = [lax.empty((N_CHIPS,) + sh, s.dtype) for sh, s in zip(slab, srcs)]

    ns = 3 * n

    def body(*refs):
        ins = refs[:2 * n]
        send_sems, recv_sems = refs[2 * n:2 * n + ns], refs[2 * n + ns:2 * n + 2 * ns]
        token = refs[-1]
        for cp in _chip_copies(ins[:n], ins[n:], send_sems, recv_sems, per_chip, False):
            cp.start()
        token[...] = jnp.zeros_like(token)

    args = [pltpu.with_memory_space_constraint(a, pltpu.HBM) for a in list(srcs) + lands]
    outs = pl.pallas_call(
        body, name=name,
        out_shape=[pltpu.SemaphoreType.DMA(())] * (2 * ns)
        + [pltpu.HBM(a.shape, a.dtype) for a in args] + [_sds((8, LANES), F32)],
        in_specs=[_HBM] * (2 * n),
        out_specs=[_SEM] * (2 * ns) + [_HBM] * (2 * n) + [pl.BlockSpec(memory_space=pltpu.VMEM)],
        input_output_aliases={i: 2 * ns + i for i in range(2 * n)},
        compiler_params=pltpu.CompilerParams(has_side_effects=_EFFECT))(*args)
    sems, rest = outs[:2 * ns], outs[2 * ns:]
    return sems[:ns], sems[ns:], rest[:n], rest[n:2 * n], rest[-1]


def _chips_wait(handle, after, per_chip, name):
    send_sems, recv_sems, srcs, lands, _ = handle
    n = len(srcs)
    ns = 3 * n

    def body(*refs):
        ins = refs[:2 * n]
        s_sems, r_sems = refs[2 * n:2 * n + ns], refs[2 * n + ns:2 * n + 2 * ns]
        for cp in _chip_copies(ins[:n], ins[n:], s_sems, r_sems, per_chip, False):
            cp.wait_send()
        for cp in _chip_copies(ins[:n], ins[n:], s_sems, r_sems, per_chip, True):
            cp.wait_recv()

    outs = pl.pallas_call(
        body, name=name,
        out_shape=[pltpu.HBM(a.shape, a.dtype) for a in list(srcs) + list(lands)],
        in_specs=[_HBM] * (2 * n) + [_SEM] * (2 * ns) + [ANY],
        out_specs=[_HBM] * (2 * n),
        input_output_aliases={i: i for i in range(2 * n)},
        compiler_params=pltpu.CompilerParams(has_side_effects=_EFFECT))(*srcs, *lands, *send_sems, *recv_sems, after)
    return outs[:n], outs[n:]


def _sum_chips_own(land, own, name):
    S, R, C = land.shape
    tr = R
    for cand in (256, 128, 64, 32, 16, 8):
        if R % cand == 0:
            tr = cand
            break

    def body(l_ref, o_ref, out_ref):
        x, y, _ = _position()
        me = 2 * x + y
        mine = o_ref[me]
        acc = None
        for k in range(S):
            part = jnp.where(me == k, mine, l_ref[k])
            acc = part if acc is None else acc + part
        out_ref[...] = acc

    blk = pl.BlockSpec((S, tr, C), lambda i: (0, i, 0))
    return pl.pallas_call(
        body, name=name, grid=(R // tr,), in_specs=[blk, blk], out_specs=pl.BlockSpec((tr, C), lambda i: (i, 0)),
        out_shape=_sds((R, C), F32), compiler_params=_cp("parallel"))(land, own)


def _exchange_sibling(arrays):
    n = len(arrays)

    def body(*refs):
        ins, outs = refs[:n], refs[n:2 * n]
        send_sems, recv_sems = refs[2 * n:]
        x, y, c = _position()
        sends = []
        for t in range(n):
            cp = pltpu.make_async_remote_copy(src_ref=ins[t], dst_ref=outs[t], send_sem=send_sems.at[t],
                                              recv_sem=recv_sems.at[t], device_id=(x, y, 1 - c), device_id_type=MESH)
            cp.start()
            sends.append(cp)
        for cp in sends:
            cp.wait()

    return pl.pallas_call(
        body, name="grads_to_sibling",
        in_specs=[ANY] * n, out_specs=[ANY] * n, out_shape=[_sds(a.shape, a.dtype) for a in arrays],
        scratch_shapes=[pltpu.SemaphoreType.DMA((n,)), pltpu.SemaphoreType.DMA((n,))],
        compiler_params=pltpu.CompilerParams(has_side_effects=True))(*arrays)


def _adamw_sum(w, ga, gb, m, v, name):
    R, C = w.shape
    tr = next(t for t in (512, 256, 128, 64) if R % t == 0 and t * C * 4 <= (1 << 20))
    c1 = 1.0 - ADAM_B1 ** ADAM_STEP
    c2 = 1.0 - ADAM_B2 ** ADAM_STEP

    def body(w_ref, ga_ref, gb_ref, m_ref, v_ref, g_ref, d_ref, mo_ref, vo_ref):
        gv = ga_ref[...] + gb_ref[...]
        mn = ADAM_B1 * m_ref[...] + (1.0 - ADAM_B1) * gv
        vn = ADAM_B2 * v_ref[...] + (1.0 - ADAM_B2) * (gv * gv)
        g_ref[...] = gv
        d_ref[...] = -ADAM_LR * ((mn / c1) / (jnp.sqrt(vn / c2) + ADAM_EPS) + ADAM_WD * w_ref[...])
        mo_ref[...] = mn
        vo_ref[...] = vn

    blk = pl.BlockSpec((tr, C), lambda i: (i, 0))
    return pl.pallas_call(
        body, name=name, grid=(R // tr,), in_specs=[blk] * 5, out_specs=[blk] * 4,
        out_shape=[_sds((R, C), F32)] * 4, compiler_params=_cp("parallel"))(w, ga, gb, m, v)


def _allreduce_small(packed):
    R, C = packed.shape
    ndev = 2 * N_CHIPS

    def body(in_ref, out_ref, buf, send_sems, recv_sems):
        x, y, c = _position()
        me = 4 * x + 2 * y + c
        buf[me] = in_ref[...]
        sends = []
        for k in range(1, ndev):
            peer = (x ^ (k >> 2), y ^ ((k >> 1) & 1), c ^ (k & 1))
            cp = pltpu.make_async_remote_copy(src_ref=in_ref, dst_ref=buf.at[me], send_sem=send_sems.at[k - 1],
                                              recv_sem=recv_sems.at[k - 1], device_id=peer, device_id_type=MESH)
            cp.start()
            sends.append(cp)
        for k in range(1, ndev):
            peer = (x ^ (k >> 2), y ^ ((k >> 1) & 1), c ^ (k & 1))
            pltpu.make_async_remote_copy(src_ref=in_ref, dst_ref=buf.at[me ^ k], send_sem=send_sems.at[k - 1],
                                         recv_sem=recv_sems.at[k - 1], device_id=peer, device_id_type=MESH).wait_recv()
        for cp in sends:
            cp.wait_send()
        acc = buf[0]
        for d in range(1, ndev):
            acc = acc + buf[d]
        out_ref[...] = acc

    return pl.pallas_call(
        body, name="allreduce_small",
        in_specs=[pl.BlockSpec(memory_space=pltpu.VMEM)], out_specs=pl.BlockSpec(memory_space=pltpu.VMEM),
        out_shape=_sds((R, C), F32),
        scratch_shapes=[pltpu.VMEM((ndev, R, C), F32), pltpu.SemaphoreType.DMA((ndev - 1,)),
                        pltpu.SemaphoreType.DMA((ndev - 1,))],
        compiler_params=pltpu.CompilerParams(vmem_limit_bytes=VMEM_LIMIT_BYTES, has_side_effects=True))(packed)


def _rope_tables(T):
    pos = jnp.arange(T)
    row = (pos // GRID_W).astype(F32)
    col = (pos % GRID_W).astype(F32)
    inv = 1.0 / (ROPE_THETA ** (jnp.arange(AXIS_DIM // 2, dtype=F32) * 2.0 / AXIS_DIM))
    ar, ac = row[:, None] * inv[None, :], col[:, None] * inv[None, :]
    cos = jnp.concatenate([jnp.cos(ar), jnp.cos(ar), jnp.cos(ac), jnp.cos(ac)], axis=-1)
    sin = jnp.concatenate([-jnp.sin(ar), jnp.sin(ar), -jnp.sin(ac), jnp.sin(ac)], axis=-1)
    return jnp.tile(cos, (1, LANES // HEAD_DIM)), jnp.tile(sin, (1, LANES // HEAD_DIM))


def _head_mean_matrix():
    h = jnp.arange(LANES) // HEAD_DIM
    return jnp.where(h[:, None] == h[None, :], 1.0 / HEAD_DIM, 0.0).astype(F32)


def _pack(arrays):
    flat = jnp.concatenate([a.reshape(-1) for a in arrays])
    rows = -(-flat.shape[0] // LANES)
    rows = -(-rows // 256) * 256
    return jnp.pad(flat, (0, rows * LANES - flat.shape[0])).reshape(rows, LANES)


def _unpack(packed, like):
    flat = packed.reshape(-1)
    out, off = [], 0
    for a in like:
        out.append(flat[off:off + a.size].reshape(a.shape))
        off += a.size
    return out


def _layer_fwd(x, lw, consts, ffn_weights):
    cos, sin, bmat = consts
    T = x.shape[0]
    tk = min(T, _ATTN_TK)
    tq = min(T, _ATTN_TQ)
    h = _norm_fwd(x, lw["norm1_g"])
    p = _mm_nn(h, lw["w_in"], out_dtype=F32, name="mm_p")
    qn, kt, kd, vt, v1 = _qk_prep(p, lw["gq"], lw["gk"], cos, sin, bmat, tk)
    o, lse = _attn_fwd(qn, kd, v1, tq)
    go = _sg_fwd(p, lw["sg_norm_g"], lw["sg_w"], lw["sg_bias"])
    co = _convmix_fwd(p, lw["conv_w"])
    mix = jnp.concatenate([o, go, co], axis=1)
    x_mid = _mm_nn(mix, lw["w_out"], out_dtype=F32, name="mm_out", res=x)
    lw.update(ffn_weights(x_mid))
    h2 = _norm_fwd(x_mid, lw["norm2_g"])
    up = _mm_nn(h2, lw["w_up"], out_dtype=F32, name="mm_up", tm=_FFN_ROW_TILE)
    act = _ffn_act_fwd(up, lw["ffn_conv_w"])
    x_out = _mm_nn(act, lw["w_down"], out_dtype=F32, name="mm_down", res=x_mid)
    saved = dict(x=x, h=h, p=p, qn=qn, kt=kt, kd=kd, vt=vt, o=o, lse=lse, mix=mix, x_mid=x_mid, h2=h2, up=up, act=act)
    return x_out, saved


def _layer_bwd(dx, s, lw, consts, send):
    cos, sin, bmat = consts
    T = dx.shape[0]
    tq = min(T, _ATTN_TQ)
    g = {}
    d_act = _mm_nt(dx, lw["w_down"], name="mm_dact")
    g_down = _mm_tn(s["act"], dx, tk=D_FF // 2, tn=D_MODEL, name="mm_dwdown", tm=_WGRAD_ROWS // 2)
    tok = send("w_down", g_down.reshape(N_CHIPS, D_FF // N_CHIPS, D_MODEL))
    d_up, d_cw = _ffn_act_bwd(s["up"], d_act, lw["ffn_conv_w"] + tok)
    g["ffn_conv_w"] = d_cw.transpose(1, 0, 2).reshape(3, 2 * D_FF)
    tok = send("w_up", _mm_tn(s["h2"], d_up, tk=512, tn=D_FF // 2, name="mm_dwup", shards=N_CHIPS, tm=_WGRAD_ROWS))
    d_h2 = _mm_nt(d_up, lw["w_up"], name="mm_dh2")
    dx2, g["norm2_g"] = _norm_bwd(s["x_mid"], d_h2, dx, lw["norm2_g"] + tok)
    d_mix = _mm_nt(dx2, lw["w_out"], name="mm_dmix")
    g_out = _mm_tn(s["mix"], dx2, tk=512, tn=D_MODEL, name="mm_dwout", tm=_WGRAD_ROWS // 2)
    tok = send("w_out", g_out.reshape(N_CHIPS, D_MODEL // N_CHIPS, D_MODEL))
    dp_c, g["conv_w"] = _convmix_bwd(s["p"], d_mix, lw["conv_w"] + tok)
    dp_b, g["sg_w"], d_bias, g["sg_norm_g"] = _sg_bwd(s["p"], d_mix, lw["sg_norm_g"], lw["sg_w"], lw["sg_wt"], lw["sg_bias"])
    g["sg_b"] = d_bias.reshape(SG_CHUNK, SG_W // HEAD_DIM, HEAD_DIM).sum(axis=-1).T
    dq, dkt, dvt = _attn_bwd(s["qn"], s["o"], d_mix, s["lse"], s["kt"], s["vt"], tq)
    dp_a, d_gq, d_gk = _qk_bwd(s["p"], dq, dkt, dvt, lw["gq"], lw["gk"], cos, sin, bmat)
    g["q_norm_g"] = d_gq.reshape(ATTN_W // HEAD_DIM, HEAD_DIM).sum(axis=0)
    g["k_norm_g"] = d_gk.reshape(KV_W // HEAD_DIM, HEAD_DIM).sum(axis=0)
    dp = jnp.concatenate([dp_a, dp_b, dp_c], axis=1)
    tok = send("w_in", _mm_tn(s["h"], dp, tk=D_MODEL, tn=512, name="mm_dwin", shards=N_CHIPS, tm=_WGRAD_ROWS))
    d_h = _mm_nt(dp, lw["w_in"], name="mm_dh")
    dx_in, g["norm1_g"] = _norm_bwd(s["x"], d_h, dx2, lw["norm1_g"] + tok)
    return dx_in, g


def _layer_weights(l, full, small):
    sg_w = small["sg_w"][l]
    sg_b = small["sg_b"][l]
    return dict(
        norm1_g=small["norm1_g"][l][None, :], norm2_g=small["norm2_g"][l][None, :],
        gq=jnp.tile(small["q_norm_g"][l], ATTN_W // HEAD_DIM)[None, :],
        gk=jnp.tile(small["k_norm_g"][l], KV_W // HEAD_DIM)[None, :],
        sg_norm_g=small["sg_norm_g"][l][None, :],
        sg_w=sg_w.astype(BF16), sg_wt=sg_w.transpose(0, 2, 1).astype(BF16),
        sg_bias=jnp.repeat(sg_b.T, HEAD_DIM, axis=1),
        **full)


_BIG = ("w_in", "w_out", "ffn_w_up", "ffn_w_down")
_SMALL_REPL = ("norm1_g", "q_norm_g", "k_norm_g", "sg_norm_g", "sg_w", "sg_b", "norm2_g")
_SMALL_SHARD = ("conv_w", "ffn_conv_w")
_ORDER = ("norm1_g", "w_in", "q_norm_g", "k_norm_g", "sg_norm_g", "sg_w", "sg_b", "conv_w", "w_out", "norm2_g",
          "ffn_w_up", "ffn_conv_w", "ffn_w_down")


def kernel(x, norm1_g, w_in, q_norm_g, k_norm_g, sg_norm_g, sg_w, sg_b, conv_w, w_out, norm2_g, ffn_w_up, ffn_conv_w, ffn_w_down, loss_target, m_norm1_g, m_w_in, m_q_norm_g, m_k_norm_g, m_sg_norm_g, m_sg_w, m_sg_b, m_conv_w, m_w_out, m_norm2_g, m_ffn_w_up, m_ffn_conv_w, m_ffn_w_down, v_norm1_g, v_w_in, v_q_norm_g, v_k_norm_g, v_sg_norm_g, v_sg_w, v_sg_b, v_conv_w, v_w_out, v_norm2_g, v_ffn_w_up, v_ffn_conv_w, v_ffn_w_down):
    w = dict(norm1_g=norm1_g, w_in=w_in, q_norm_g=q_norm_g, k_norm_g=k_norm_g, sg_norm_g=sg_norm_g, sg_w=sg_w,
             sg_b=sg_b, conv_w=conv_w, w_out=w_out, norm2_g=norm2_g, ffn_w_up=ffn_w_up, ffn_conv_w=ffn_conv_w,
             ffn_w_down=ffn_w_down)
    mom = dict(norm1_g=m_norm1_g, w_in=m_w_in, q_norm_g=m_q_norm_g, k_norm_g=m_k_norm_g, sg_norm_g=m_sg_norm_g,
               sg_w=m_sg_w, sg_b=m_sg_b, conv_w=m_conv_w, w_out=m_w_out, norm2_g=m_norm2_g, ffn_w_up=m_ffn_w_up,
               ffn_conv_w=m_ffn_conv_w, ffn_w_down=m_ffn_w_down)
    var = dict(norm1_g=v_norm1_g, w_in=v_w_in, q_norm_g=v_q_norm_g, k_norm_g=v_k_norm_g, sg_norm_g=v_sg_norm_g,
               sg_w=v_sg_w, sg_b=v_sg_b, conv_w=v_conv_w, w_out=v_w_out, norm2_g=v_norm2_g, ffn_w_up=v_ffn_w_up,
               ffn_conv_w=v_ffn_conv_w, ffn_w_down=v_ffn_w_down)
    L = DEPTH
    T = x.shape[1]
    xs = x.reshape(T, D_MODEL)
    target = loss_target.reshape(T, D_MODEL)

    chip = 2 * lax.axis_index("x") + lax.axis_index("y")

    shards = [_cast_bf16(w[n].reshape(-1, w[n].shape[-1]), "cast_" + n).reshape(w[n].shape) for n in _BIG]
    shards += [conv_w, ffn_conv_w]
    w_in_s, w_out_s, w_up_s, w_down_s, conv_s, fconv_s = shards
    gathers = []
    for l in range(L):
        gathers.append((_chips_start([w_in_s[l], w_out_s[l], conv_s[l]], False, "gather_start_%da" % l),
                        _chips_start([w_up_s[l], w_down_s[l], fconv_s[l]], False, "gather_start_%db" % l)))
    start_token = sum(h[4][0, 0] for pair in gathers for h in pair)
    consts = _rope_tables(T) + (_head_mean_matrix(),)

    def gathered(handle, after, name):
        own, lands = _chips_wait(handle, after, False, name)
        return [lax.dynamic_update_slice(ld, o[None], (chip,) + (jnp.int32(0),) * o.ndim) for ld, o in zip(lands, own)]

    saved, lws = [], []
    act_x = xs
    for l in range(L):
        g_in, g_out, g_conv = gathered(gathers[l][0], act_x if l else gathers[-1][1][4], "gather_wait_%da" % l)
        lw = _layer_weights(l, dict(w_in=g_in, w_out=g_out.reshape(D_MODEL, D_MODEL),
                                    conv_w=g_conv.transpose(1, 0, 2).reshape(3, CONV_W)), w)
        if l == 0:
            lw["norm1_g"] = lw["norm1_g"] + start_token

        def ffn_weights(after, l=l):
            g_up, g_down, g_fconv = gathered(gathers[l][1], after, "gather_wait_%db" % l)
            return dict(w_up=g_up, w_down=g_down.reshape(D_FF, D_MODEL),
                        ffn_conv_w=g_fconv.transpose(1, 0, 2).reshape(3, 2 * D_FF))

        act_x, s = _layer_fwd(act_x, lw, consts, ffn_weights)
        saved.append(s)
        lws.append(lw)
    dx, loss_blk = _loss_head(act_x, target)
    loss = lax.psum(loss_blk[0, 0], ("x", "y", "c"))

    grads = [None] * L
    partial = [None] * L

    def collect(pending, after, l):
        sums = {}
        for name, handle in pending:
            own, lands = _chips_wait(handle, after, True, "grad_wait_%d_%s" % (l, name))
            sums[name] = _sum_chips_own(lands[0], own[0], "sum_chips_" + name)
        return sums

    pending_prev = None
    for l in reversed(range(L)):
        pending = []

        def send(name, g4, l=l, pending=pending):
            handle = _chips_start([g4], True, "grad_start_%d_%s" % (l, name))
            pending.append((name, handle))
            return handle[4][0, 0]

        dx, g = _layer_bwd(dx, saved[l], lws[l], consts, send)
        grads[l] = g
        if pending_prev is not None:
            partial[l + 1] = collect(pending_prev, dx, l + 1)
        pending_prev = pending
    partial[0] = collect(pending_prev, dx, 0)
    grad_x = dx.reshape(x.shape)

    small_names = _SMALL_REPL + _SMALL_SHARD
    key = dict(norm1_g="norm1_g", q_norm_g="q_norm_g", k_norm_g="k_norm_g", sg_norm_g="sg_norm_g", sg_w="sg_w",
               sg_b="sg_b", norm2_g="norm2_g", conv_w="conv_w", ffn_conv_w="ffn_conv_w")
    small_local = [jnp.stack([grads[l][key[n]].reshape(-1) for l in range(L)]) for n in small_names]
    small_sum = _unpack(_allreduce_small(_pack(small_local)), small_local)
    grad = {}
    for n, a in zip(small_names, small_sum):
        grad[n] = a
    for n in _SMALL_REPL:
        grad[n] = grad[n].reshape(w[n].shape)
    for n in _SMALL_SHARD:
        full_w = grad[n].reshape(L, 3, -1)
        width = w[n].shape[-1]
        grad[n] = lax.dynamic_slice_in_dim(full_w, chip * width, width, axis=2)

    short = dict(w_in="w_in", w_out="w_out", ffn_w_up="w_up", ffn_w_down="w_down")
    mine = [jnp.stack([partial[l][short[n]] for l in range(L)]) for n in _BIG]
    theirs = _exchange_sibling(mine)
    delta, new_m, new_v = {}, {}, {}
    for n, ga, gb in zip(_BIG, mine, theirs):
        shp = w[n].shape
        v2 = lambda a: a.reshape(-1, shp[-1])
        gsum, d, mn, vn = _adamw_sum(v2(w[n]), v2(ga), v2(gb), v2(mom[n]), v2(var[n]), "adamw_" + n)
        grad[n], delta[n], new_m[n], new_v[n] = gsum.reshape(shp), d.reshape(shp), mn.reshape(shp), vn.reshape(shp)
    for group, gname in ((_SMALL_REPL, "adamw_small"), (_SMALL_SHARD, "adamw_conv")):
        like = [w[n] for n in group]
        outs = _adamw(_pack([w[n] for n in group]), _pack([grad[n] for n in group]), _pack([mom[n] for n in group]),
                      _pack([var[n] for n in group]), gname)
        for res, dst in zip(outs, (delta, new_m, new_v)):
            for n, a in zip(group, _unpack(res, like)):
                dst[n] = a

    return (loss, grad_x, *[grad[n] for n in _ORDER], *[delta[n] for n in _ORDER],
            *[new_m[n] for n in _ORDER], *[new_v[n] for n in _ORDER])
```

```python
import jax
import jax.numpy as jnp
from jax import lax
from jax.experimental import pallas as pl
from jax.experimental.pallas import tpu as pltpu

F32 = jnp.float32
BF16 = jnp.bfloat16

DEPTH = 4
D_MODEL = 1024
HEAD_DIM = 64
ATTN_W = 512
KV_W = 128
SG_W = 256
CONV_W = 256
SG_CHUNK = 128
D_FF = 2816
PROJ_W = 2048
GRID_W = 64
ROPE_THETA = 10000.0
AXIS_DIM = HEAD_DIM // 2
EPS = 1e-6
N_CHIPS = 4

ADAM_LR = 0.001
ADAM_B1 = 0.9
ADAM_B2 = 0.999
ADAM_EPS = 1e-08
ADAM_WD = 0.01
ADAM_STEP = 10

_ROW_TILE = 512
_FFN_ROW_TILE = 256
_WGRAD_ROWS = 2048
LANES = 128
HALO = 8
VMEM_LIMIT_BYTES = 56 * 1024 * 1024
MESH = pl.DeviceIdType.MESH
ANY = pl.BlockSpec(memory_space=pl.ANY)


def _cp(*sem):
    return pltpu.CompilerParams(dimension_semantics=sem if sem else None,
                                vmem_limit_bytes=VMEM_LIMIT_BYTES)


def _sds(shape, dtype):
    return jax.ShapeDtypeStruct(shape, dtype)


def _dot(a, b):
    return jnp.dot(a, b, preferred_element_type=F32)


def _dot_nt(a, b):
    return lax.dot_general(a, b, (((1,), (1,)), ((), ())), preferred_element_type=F32)


def _dot_tn(a, b):
    return lax.dot_general(a, b, (((0,), (0,)), ((), ())), preferred_element_type=F32)


def _norm_fwd(x, g):
    T, Dm = x.shape
    tm = min(T, _ROW_TILE)

    def body(x_ref, g_ref, o_ref):
        xv = x_ref[...]
        r = lax.rsqrt(jnp.mean(xv * xv, axis=-1, keepdims=True) + EPS)
        o_ref[...] = ((xv * r) * g_ref[...]).astype(BF16)

    return pl.pallas_call(
        body, name="norm_fwd", grid=(T // tm,),
        in_specs=[pl.BlockSpec((tm, Dm), lambda i: (i, 0)), pl.BlockSpec((1, Dm), lambda i: (0, 0))],
        out_specs=pl.BlockSpec((tm, Dm), lambda i: (i, 0)),
        out_shape=_sds((T, Dm), BF16), compiler_params=_cp("parallel"))(x, g)


def _norm_bwd(x, dh, dres, g):
    T, Dm = x.shape
    tm = min(T, _ROW_TILE)

    def body(x_ref, dh_ref, dr_ref, g_ref, dx_ref, dg_ref):
        i = pl.program_id(0)
        xv = x_ref[...]
        r = lax.rsqrt(jnp.mean(xv * xv, axis=-1, keepdims=True) + EPS)
        xh = xv * r
        dhv = dh_ref[...]
        dxh = dhv * g_ref[...]
        dx_ref[...] = dr_ref[...] + r * (dxh - xh * jnp.mean(dxh * xh, axis=-1, keepdims=True))
        part = jnp.sum(dhv * xh, axis=0, keepdims=True)

        @pl.when(i == 0)
        def _():
            dg_ref[...] = part

        @pl.when(i > 0)
        def _():
            dg_ref[...] += part

    row = pl.BlockSpec((tm, Dm), lambda i: (i, 0))
    vec = pl.BlockSpec((1, Dm), lambda i: (0, 0))
    return pl.pallas_call(
        body, name="norm_bwd", grid=(T // tm,),
        in_specs=[row, row, row, vec], out_specs=[row, vec],
        out_shape=[_sds((T, Dm), F32), _sds((1, Dm), F32)], compiler_params=_cp("arbitrary"))(x, dh, dres, g)


def _whole(w):
    return pl.BlockSpec(w.shape, lambda *g: (0,) * w.ndim)


def _mm_nn(a, w, *, out_dtype, name, res=None, tm=None):
    M, K = a.shape
    N = w.shape[-1] if w.ndim == 2 else w.shape[0] * w.shape[2]
    tm = min(M, tm or _ROW_TILE)
    has_res = res is not None

    def body(*refs):
        a_ref, w_ref = refs[0], refs[1]
        o_ref = refs[-1]
        av = a_ref[...].astype(BF16)
        parts = [w_ref[...]] if w.ndim == 2 else [w_ref[s] for s in range(w.shape[0])]
        ns = N // len(parts)
        for s, wv in enumerate(parts):
            cols = pl.ds(s * ns, ns)
            acc = _dot(av, wv)
            if has_res:
                acc = acc + refs[2][:, cols]
            o_ref[:, cols] = acc.astype(out_dtype)

    in_specs = [pl.BlockSpec((tm, K), lambda i: (i, 0)), _whole(w)]
    args = [a, w]
    if has_res:
        in_specs.append(pl.BlockSpec((tm, N), lambda i: (i, 0)))
        args.append(res)
    return pl.pallas_call(
        body, name=name, grid=(M // tm,), in_specs=in_specs, out_specs=pl.BlockSpec((tm, N), lambda i: (i, 0)),
        out_shape=_sds((M, N), out_dtype), compiler_params=_cp("parallel"))(*args)


def _a_spec(a, tm, tn, row_of, col_of):
    if a.ndim == 2:
        return pl.BlockSpec((tm, tn), lambda *g: (row_of(*g), col_of(*g)))
    bph = a.shape[2] // tn
    return pl.BlockSpec((None, tm, tn), lambda *g: (col_of(*g) // bph, row_of(*g), col_of(*g) % bph))


def _a_cols(a):
    return a.shape[1] if a.ndim == 2 else a.shape[0] * a.shape[2]


def _mm_nt(a, w, *, name, tm=None):
    M = a.shape[-2]
    Kw = w.shape[-2]
    tm = min(M, tm or _ROW_TILE)

    def body(a_ref, w_ref, o_ref):
        if w.ndim == 2:
            o_ref[...] = _dot_nt(a_ref[...].astype(BF16), w_ref[...])
            return
        S, ns = w.shape[0], w.shape[2]
        acc = None
        for s in range(S):
            if a.ndim == 2:
                piece = a_ref[:, pl.ds(s * ns, ns)]
            else:
                per_half = S // 2
                piece = a_ref[s // per_half, :, pl.ds((s % per_half) * ns, ns)]
            part = _dot_nt(piece.astype(BF16), w_ref[s])
            acc = part if acc is None else acc + part
        o_ref[...] = acc

    a_spec = (pl.BlockSpec((tm, a.shape[1]), lambda i: (i, 0)) if a.ndim == 2
              else pl.BlockSpec((2, tm, a.shape[2]), lambda i: (0, i, 0)))
    return pl.pallas_call(
        body, name=name, grid=(M // tm,), in_specs=[a_spec, _whole(w)],
        out_specs=pl.BlockSpec((tm, Kw), lambda i: (i, 0)), out_shape=_sds((M, Kw), F32),
        compiler_params=_cp("parallel"))(a, w)


def _mm_tn(a, b, *, tk, tn, name, shards=None, tm=None):
    M, K = a.shape
    N = _a_cols(b)
    tm = min(M, tm or _ROW_TILE)

    def body(a_ref, b_ref, o_ref):
        m = pl.program_id(2)
        part = _dot_tn(a_ref[...].astype(BF16), b_ref[...].astype(BF16))

        @pl.when(m == 0)
        def _():
            o_ref[...] = part

        @pl.when(m > 0)
        def _():
            o_ref[...] += part

    if shards is None:
        out_spec = pl.BlockSpec((tk, tn), lambda k, j, m: (k, j))
        out_shape = _sds((K, N), F32)
    else:
        bps = (N // shards) // tn
        out_spec = pl.BlockSpec((None, tk, tn), lambda k, j, m: (j // bps, k, j % bps))
        out_shape = _sds((shards, K, N // shards), F32)
    return pl.pallas_call(
        body, name=name, grid=(K // tk, N // tn, M // tm),
        in_specs=[pl.BlockSpec((tm, tk), lambda k, j, m: (m, k)),
                  _a_spec(b, tm, tn, lambda k, j, m: m, lambda k, j, m: j)],
        out_specs=out_spec, out_shape=out_shape,
        compiler_params=_cp("parallel", "parallel", "arbitrary"))(a, b)


def _halo_specs(T, tm, cw, ic):
    nb = tm // HALO
    last = T // HALO - 1

    def mk(rows, row_of):
        return pl.BlockSpec((rows, cw), lambda *g: (row_of(ic(*g)[0]), ic(*g)[1]))

    return [mk(HALO, lambda i: jnp.maximum(i * nb - 1, 0)), mk(tm, lambda i: i),
            mk(HALO, lambda i: jnp.minimum((i + 1) * nb, last))]


def _ext(prev_ref, cur_ref, next_ref, i, n):
    p = jnp.where(i > 0, prev_ref[...].astype(F32), 0.0)
    nx = jnp.where(i < n - 1, next_ref[...].astype(F32), 0.0)
    return jnp.concatenate([p, cur_ref[...].astype(F32), nx], axis=0)


def _dn(e):
    return pltpu.roll(e, 1, 0)


def _up(e):
    return pltpu.roll(e, e.shape[0] - 1, 0)


def _mid(e):
    return e[HALO:e.shape[0] - HALO]


def _taps(e):
    return _dn(e), e, _up(e)


def _conv3(taps, w):
    return taps[0] * w[0:1] + taps[1] * w[1:2] + taps[2] * w[2:3]


def _conv3_t(e, w):
    return _up(e) * w[0:1] + e * w[1:2] + _dn(e) * w[2:3]


def _conv3_wgrad(d, taps):
    return jnp.concatenate([jnp.sum(_mid(d * tap), axis=0, keepdims=True) for tap in taps], axis=0)


def _sigmoid(x):
    return 1.0 / (1.0 + jnp.exp(-x))


def _accum(ref, i, part):
    @pl.when(i == 0)
    def _():
        ref[...] = part

    @pl.when(i > 0)
    def _():
        ref[...] += part


def _ffn_act_fwd(up, cw):
    T = up.shape[0]
    tm = min(T, _FFN_ROW_TILE)
    cb = D_FF // 2
    nblk = D_FF // cb
    n = T // tm

    def body(gp, gc, gn, vp, vc, vn, wg_ref, wv_ref, o_ref):
        i = pl.program_id(1)
        gate = _conv3(_taps(_ext(gp, gc, gn, i, n)), wg_ref[...])
        val = _conv3(_taps(_ext(vp, vc, vn, i, n)), wv_ref[...])
        o_ref[...] = _mid(gate * _sigmoid(gate) * val).astype(BF16)

    return pl.pallas_call(
        body, name="ffn_act_fwd", grid=(nblk, n),
        in_specs=_halo_specs(T, tm, cb, lambda j, i: (i, j)) + _halo_specs(T, tm, cb, lambda j, i: (i, j + nblk))
        + [pl.BlockSpec((3, cb), lambda j, i: (0, j)), pl.BlockSpec((3, cb), lambda j, i: (0, j + nblk))],
        out_specs=pl.BlockSpec((tm, cb), lambda j, i: (i, j)),
        out_shape=_sds((T, D_FF), BF16), compiler_params=_cp("parallel", "parallel"))(
            up, up, up, up, up, up, cw, cw)


def _ffn_act_bwd(up, dact, cw):
    T = up.shape[0]
    tm = min(T, _FFN_ROW_TILE)
    cb = D_FF // 2
    nblk = D_FF // cb
    n = T // tm

    def body(gp, gc, gn, vp, vc, vn, dp_, dc, dn_, wg_ref, wv_ref, dup_ref, dcw_ref):
        i = pl.program_id(1)
        wg, wv = wg_ref[...], wv_ref[...]
        eg = _taps(_ext(gp, gc, gn, i, n))
        ev = _taps(_ext(vp, vc, vn, i, n))
        ed = _ext(dp_, dc, dn_, i, n)
        gate = _conv3(eg, wg)
        val = _conv3(ev, wv)
        sg = _sigmoid(gate)
        d_gate = ed * val * (sg * (1.0 + gate * (1.0 - sg)))
        d_val = ed * (gate * sg)
        dup_ref[0] = _mid(_conv3_t(d_gate, wg)).astype(BF16)
        dup_ref[1] = _mid(_conv3_t(d_val, wv)).astype(BF16)
        part = jnp.stack([_conv3_wgrad(d_gate, eg), _conv3_wgrad(d_val, ev)], axis=0)
        _accum(dcw_ref, i, part)

    return pl.pallas_call(
        body, name="ffn_act_bwd", grid=(nblk, n),
        in_specs=_halo_specs(T, tm, cb, lambda j, i: (i, j)) + _halo_specs(T, tm, cb, lambda j, i: (i, j + nblk))
        + _halo_specs(T, tm, cb, lambda j, i: (i, j))
        + [pl.BlockSpec((3, cb), lambda j, i: (0, j)), pl.BlockSpec((3, cb), lambda j, i: (0, j + nblk))],
        out_specs=[pl.BlockSpec((2, tm, cb), lambda j, i: (0, i, j)),
                   pl.BlockSpec((2, 3, cb), lambda j, i: (0, 0, j))],
        out_shape=[_sds((2, T, D_FF), BF16), _sds((2, 3, D_FF), F32)],
        compiler_params=_cp("parallel", "arbitrary"))(up, up, up, up, up, up, dact, dact, dact, cw, cw)


_CB_BLK, _CC_BLK, _CX_BLK = 5, 6, 7


def _convmix_fwd(p, w):
    T = p.shape[0]
    tm = min(T, _ROW_TILE)
    n = T // tm

    def body(cb_ref, ccp, ccc, ccn, cxp, cxc, cxn, w_ref, o_ref):
        i = pl.program_id(0)
        z = _ext(ccp, ccc, ccn, i, n) * _ext(cxp, cxc, cxn, i, n)
        o_ref[...] = (cb_ref[...] * _mid(_conv3(_taps(z), w_ref[...]))).astype(BF16)

    return pl.pallas_call(
        body, name="convmix_fwd", grid=(n,),
        in_specs=[pl.BlockSpec((tm, CONV_W), lambda i: (i, _CB_BLK))]
        + _halo_specs(T, tm, CONV_W, lambda i: (i, _CC_BLK)) + _halo_specs(T, tm, CONV_W, lambda i: (i, _CX_BLK))
        + [pl.BlockSpec((3, CONV_W), lambda i: (0, 0))],
        out_specs=pl.BlockSpec((tm, CONV_W), lambda i: (i, 0)),
        out_shape=_sds((T, CONV_W), BF16), compiler_params=_cp("parallel"))(p, p, p, p, p, p, p, w)


def _convmix_bwd(p, dmix, w):
    T = p.shape[0]
    tm = min(T, _ROW_TILE)
    n = T // tm
    dblk = (ATTN_W + SG_W) // CONV_W

    def body(cbp, cbc, cbn, ccp, ccc, ccn, cxp, cxc, cxn, dp_, dc, dn_, w_ref, o_ref, dw_ref):
        i = pl.program_id(0)
        wv = w_ref[...]
        ecb = _ext(cbp, cbc, cbn, i, n)
        ecc = _ext(ccp, ccc, ccn, i, n)
        ecx = _ext(cxp, cxc, cxn, i, n)
        ed = _ext(dp_, dc, dn_, i, n)
        z = _taps(ecc * ecx)
        d_cz = ed * ecb
        d_z = _conv3_t(d_cz, wv)
        o_ref[...] = jnp.concatenate([_mid(ed * _conv3(z, wv)), _mid(d_z * ecx), _mid(d_z * ecc)],
                                     axis=1).astype(BF16)
        _accum(dw_ref, i, _conv3_wgrad(d_cz, z))

    return pl.pallas_call(
        body, name="convmix_bwd", grid=(n,),
        in_specs=_halo_specs(T, tm, CONV_W, lambda i: (i, _CB_BLK)) + _halo_specs(T, tm, CONV_W, lambda i: (i, _CC_BLK))
        + _halo_specs(T, tm, CONV_W, lambda i: (i, _CX_BLK)) + _halo_specs(T, tm, CONV_W, lambda i: (i, dblk))
        + [pl.BlockSpec((3, CONV_W), lambda i: (0, 0))],
        out_specs=[pl.BlockSpec((tm, 3 * CONV_W), lambda i: (i, 0)), pl.BlockSpec((3, CONV_W), lambda i: (0, 0))],
        out_shape=[_sds((T, 3 * CONV_W), BF16), _sds((3, CONV_W), F32)],
        compiler_params=_cp("arbitrary"))(p, p, p, p, p, p, p, p, p, dmix, dmix, dmix, w)


_SU_BLK, _SV_BLK = 3, 4


def _sg_mixed(vnb, w_ref, bias, ch, pr, lo):
    vp = vnb[ch * SG_CHUNK:(ch + 1) * SG_CHUNK, pr * LANES:(pr + 1) * LANES]
    zero = jnp.zeros_like(vp)
    return (_dot(w_ref[2 * pr], jnp.where(lo, vp, zero)) + _dot(w_ref[2 * pr + 1], jnp.where(lo, zero, vp))
            + bias[:, pr * LANES:(pr + 1) * LANES]), vp


def _sg_fwd(p, g, w, bias):
    T = p.shape[0]
    tm = min(T, _ROW_TILE)

    def body(su_ref, sv_ref, g_ref, w_ref, b_ref, o_ref):
        lo = lax.broadcasted_iota(jnp.int32, (SG_CHUNK, LANES), 1) < HEAD_DIM
        sv = sv_ref[...]
        r = lax.rsqrt(jnp.mean(sv * sv, axis=-1, keepdims=True) + EPS)
        vnb = ((sv * r) * g_ref[...]).astype(BF16)
        bias_v = b_ref[...]
        for ch in range(tm // SG_CHUNK):
            for pr in range(2):
                mixed, _ = _sg_mixed(vnb, w_ref, bias_v, ch, pr, lo)
                rows, cols = pl.ds(ch * SG_CHUNK, SG_CHUNK), pl.ds(pr * LANES, LANES)
                o_ref[rows, cols] = (su_ref[rows, cols] * mixed).astype(BF16)

    return pl.pallas_call(
        body, name="sg_fwd", grid=(T // tm,),
        in_specs=[pl.BlockSpec((tm, SG_W), lambda i: (i, _SU_BLK)), pl.BlockSpec((tm, SG_W), lambda i: (i, _SV_BLK)),
                  pl.BlockSpec((1, SG_W), lambda i: (0, 0)), pl.BlockSpec((4, SG_CHUNK, SG_CHUNK), lambda i: (0, 0, 0)),
                  pl.BlockSpec((SG_CHUNK, SG_W), lambda i: (0, 0))],
        out_specs=pl.BlockSpec((tm, SG_W), lambda i: (i, 0)),
        out_shape=_sds((T, SG_W), BF16), compiler_params=_cp("parallel"))(p, p, g, w, bias)


def _sg_bwd(p, dmix, g, w, wt, bias):
    T = p.shape[0]
    tm = min(T, _ROW_TILE)
    dblk = ATTN_W // SG_W

    def body(su_ref, sv_ref, d_ref, g_ref, w_ref, wt_ref, b_ref, o_ref, dw_ref, db_ref, dg_ref, dvn_ref):
        i = pl.program_id(0)
        lo = lax.broadcasted_iota(jnp.int32, (SG_CHUNK, LANES), 1) < HEAD_DIM
        sv = sv_ref[...]
        gv = g_ref[...]
        r = lax.rsqrt(jnp.mean(sv * sv, axis=-1, keepdims=True) + EPS)
        xh = sv * r
        vnb = (xh * gv).astype(BF16)
        bias_v = b_ref[...]
        dw = [jnp.zeros((SG_CHUNK, SG_CHUNK), F32) for _ in range(4)]
        db = jnp.zeros((SG_CHUNK, SG_W), F32)
        for ch in range(tm // SG_CHUNK):
            dbs = []
            for pr in range(2):
                mixed, vp = _sg_mixed(vnb, w_ref, bias_v, ch, pr, lo)
                rows, cols = pl.ds(ch * SG_CHUNK, SG_CHUNK), pl.ds(pr * LANES, LANES)
                dgo = d_ref[rows, cols]
                o_ref[rows, cols] = (dgo * mixed).astype(BF16)
                dm = dgo * su_ref[rows, cols]
                dmb = dm.astype(BF16)
                zero = jnp.zeros_like(dmb)
                dw[2 * pr] += _dot_nt(jnp.where(lo, dmb, zero), vp)
                dw[2 * pr + 1] += _dot_nt(jnp.where(lo, zero, dmb), vp)
                dvn_ref[rows, cols] = jnp.where(lo, _dot(wt_ref[2 * pr], dmb), _dot(wt_ref[2 * pr + 1], dmb))
                dbs.append(dm)
            db += jnp.concatenate(dbs, axis=1)
        dvn = dvn_ref[...]
        dxh = dvn * gv
        o_ref[:, pl.ds(SG_W, SG_W)] = (r * (dxh - xh * jnp.mean(dxh * xh, axis=-1, keepdims=True))).astype(BF16)
        _accum(dw_ref, i, jnp.stack(dw, axis=0))
        _accum(db_ref, i, db)
        _accum(dg_ref, i, jnp.sum(dvn * xh, axis=0, keepdims=True))

    wspec = pl.BlockSpec((4, SG_CHUNK, SG_CHUNK), lambda i: (0, 0, 0))
    return pl.pallas_call(
        body, name="sg_bwd", grid=(T // tm,),
        in_specs=[pl.BlockSpec((tm, SG_W), lambda i: (i, _SU_BLK)), pl.BlockSpec((tm, SG_W), lambda i: (i, _SV_BLK)),
                  pl.BlockSpec((tm, SG_W), lambda i: (i, dblk)), pl.BlockSpec((1, SG_W), lambda i: (0, 0)),
                  wspec, wspec, pl.BlockSpec((SG_CHUNK, SG_W), lambda i: (0, 0))],
        out_specs=[pl.BlockSpec((tm, 2 * SG_W), lambda i: (i, 0)), wspec,
                   pl.BlockSpec((SG_CHUNK, SG_W), lambda i: (0, 0)), pl.BlockSpec((1, SG_W), lambda i: (0, 0))],
        out_shape=[_sds((T, 2 * SG_W), BF16), _sds((4, SG_CHUNK, SG_CHUNK), F32),
                   _sds((SG_CHUNK, SG_W), F32), _sds((1, SG_W), F32)],
        scratch_shapes=[pltpu.VMEM((tm, SG_W), F32)],
        compiler_params=_cp("arbitrary"))(p, p, dmix, g, w, wt, bias)


_ATTN_TQ = 256
_ATTN_TK = 1024
_ATTN_FWD_TK = 512
_SOFTMAX_STRIP = 32
_ONES_ROWS = 16


def _head_mean(v, bmat):
    return jnp.dot(v, bmat, preferred_element_type=F32, precision=lax.Precision.HIGHEST)


def _swap16(y):
    lane = lax.broadcasted_iota(jnp.int32, y.shape, 1)
    return jnp.where(lane % 32 < 16, pltpu.roll(y, y.shape[1] - 16, 1), pltpu.roll(y, 16, 1))


def _rope(y, cos, sin):
    return y * cos + _swap16(y) * sin


def _rope_t(dy, cos, sin):
    return dy * cos + _swap16(dy * sin)


def _dup_rows(t, gidx):
    h = t[gidx * HEAD_DIM:(gidx + 1) * HEAD_DIM]
    return jnp.concatenate([h, h], axis=0)


def _qk_prep(p, gq, gk, cos, sin, bmat, tk, tkf):
    T = p.shape[0]
    nk = T // tk
    sub = tk // tkf
    scale = HEAD_DIM ** -0.5

    def body(q_ref, kv_ref, gq_ref, gk_ref, cos_ref, sin_ref, b_ref, qo_ref, kt_ref, kd_ref, vt_ref, v1_ref):
        cosv, sinv, bm = cos_ref[...], sin_ref[...], b_ref[...]
        for pr in range(ATTN_W // LANES):
            cols = pl.ds(pr * LANES, LANES)
            xq = q_ref[:, cols]
            r = lax.rsqrt(_head_mean(xq * xq, bm) + EPS)
            qo_ref[:, cols] = (_rope((xq * r) * gq_ref[:, cols], cosv, sinv) * scale).astype(BF16)
        xk = kv_ref[:, pl.ds(0, LANES)]
        r = lax.rsqrt(_head_mean(xk * xk, bm) + EPS)
        kt = _rope((xk * r) * gk_ref[...], cosv, sinv).T
        vt = kv_ref[:, pl.ds(LANES, LANES)].T
        for gidx in range(2):
            kdup = _dup_rows(kt, gidx)
            kt_ref[gidx] = kdup.astype(BF16)
            vt_ref[gidx] = _dup_rows(vt, gidx).astype(BF16)
            v1 = jnp.concatenate([vt[gidx * HEAD_DIM:(gidx + 1) * HEAD_DIM],
                                  jnp.ones((_ONES_ROWS, tk), F32)], axis=0).astype(BF16)
            for b in range(sub):
                v1_ref[gidx, b] = v1[:, b * tkf:(b + 1) * tkf]
            kd_ref[gidx] = kdup.T.astype(BF16)

    tspec = pl.BlockSpec((2, None, LANES, tk), lambda i: (0, i, 0, 0))
    dspec = pl.BlockSpec((2, tk, LANES), lambda i: (0, i, 0))
    tab = pl.BlockSpec((tk, LANES), lambda i: (i, 0))
    return pl.pallas_call(
        body, name="qk_prep", grid=(nk,),
        in_specs=[pl.BlockSpec((tk, ATTN_W), lambda i: (i, 0)), pl.BlockSpec((tk, 2 * KV_W), lambda i: (i, ATTN_W // (2 * KV_W))),
                  pl.BlockSpec((1, ATTN_W), lambda i: (0, 0)), pl.BlockSpec((1, KV_W), lambda i: (0, 0)),
                  tab, tab, pl.BlockSpec((LANES, LANES), lambda i: (0, 0))],
        out_specs=[pl.BlockSpec((tk, ATTN_W), lambda i: (i, 0)), tspec, dspec, tspec,
                   pl.BlockSpec((2, sub, HEAD_DIM + _ONES_ROWS, tkf), lambda i: (0, i, 0, 0))],
        out_shape=[_sds((T, ATTN_W), BF16), _sds((2, nk, LANES, tk), BF16), _sds((2, T, LANES), BF16),
                   _sds((2, nk, LANES, tk), BF16), _sds((2, nk * sub, HEAD_DIM + _ONES_ROWS, tkf), BF16)],
        compiler_params=_cp("parallel"))(p, p, gq, gk, cos, sin, bmat)


def _stack_heads(t):
    lo = lax.broadcasted_iota(jnp.int32, (t.shape[0], LANES), 1) < HEAD_DIM
    parts = []
    for pr in range(2):
        tp = t[:, pr * LANES:(pr + 1) * LANES]
        zero = jnp.zeros_like(tp)
        parts += [jnp.where(lo, tp, zero), jnp.where(lo, zero, tp)]
    return jnp.concatenate(parts, axis=0)


def _rows8_reduce(s, op):
    parts = [s[r:r + 8] for r in range(0, s.shape[0], 8)]
    while len(parts) > 1:
        parts = [op(parts[k], parts[k + 1]) for k in range(0, len(parts) - 1, 2)] + (
            [parts[-1]] if len(parts) % 2 else [])
    return parts[0]


def _attn_fwd(q, kd, v1, tq):
    T = q.shape[0]
    nk, tk = v1.shape[1], v1.shape[3]
    vrows = v1.shape[2]
    nq = T // tq
    sq = 4 * tq
    strip = _SOFTMAX_STRIP
    depth = 4
    assert nk % depth == 0

    def body(q_ref, kd_ref, v1_ref, o_ref, lse_ref, qst_ref, s0_ref, s1_ref, s2_ref, s3_ref, pa_ref, pb_ref,
             m_ref, acc_ref):
        s_refs = (s0_ref, s1_ref, s2_ref, s3_ref)
        p_refs = (pa_ref, pb_ref)
        qst_ref[...] = _stack_heads(q_ref[...]).astype(F32).T.astype(BF16)
        m_ref[...] = jnp.full((1, sq), -jnp.inf, F32)
        acc_ref[...] = jnp.zeros((vrows, sq), F32)

        def scores(j):
            return _dot(kd_ref[pl.ds(pl.multiple_of(j * tk, tk), tk), :], qst_ref[...])

        def block_max(s_ref):
            m8 = None
            for c in range(tk // strip):
                part = _rows8_reduce(s_ref[pl.ds(c * strip, strip), :], jnp.maximum)
                m8 = part if m8 is None else jnp.maximum(m8, part)
            return m8

        def exp_pass(s_ref, p_ref, m8):
            m_old = m_ref[...]
            m_new = jnp.maximum(m_old, jnp.max(m8, axis=0, keepdims=True))
            m_ref[...] = m_new
            for c in range(tk // strip):
                rows = pl.ds(c * strip, strip)
                p_ref[rows, :] = jnp.exp(s_ref[rows, :] - m_new).astype(BF16)
            return jnp.exp(m_old - m_new)

        def apply(p_ref, alpha, j):
            acc_ref[...] = alpha * acc_ref[...] + _dot(v1_ref[j], p_ref[...])

        s_refs[0][...] = scores(0)
        s_refs[1][...] = scores(1)

        def trip(t, max_cur):
            for u in range(depth):
                j = depth * t + u
                s_refs[(u + 2) % depth][...] = scores(jnp.minimum(j + 2, nk - 1))
                alpha = exp_pass(s_refs[u], p_refs[u % 2], max_cur)
                max_cur = block_max(s_refs[(u + 1) % depth])
                apply(p_refs[u % 2], alpha, j)
            return max_cur

        lax.fori_loop(0, nk // depth, trip, block_max(s_refs[0]))
        l = acc_ref[pl.ds(HEAD_DIM, 1), :]
        on = acc_ref[pl.ds(0, HEAD_DIM), :] / l
        pairs = []
        for pr in range(2):
            two = jnp.concatenate([on[:, (2 * pr) * tq:(2 * pr + 1) * tq], on[:, (2 * pr + 1) * tq:(2 * pr + 2) * tq]],
                                  axis=0)
            pairs.append(two.T)
        o_ref[...] = jnp.concatenate(pairs, axis=1).astype(BF16)
        lse_ref[...] = jnp.broadcast_to(m_ref[...] + jnp.log(l), (LANES, sq)).T

    row = pltpu.VMEM((1, sq), F32)
    return pl.pallas_call(
        body, name="attn_fwd", grid=(2, nq),
        in_specs=[pl.BlockSpec((tq, 2 * LANES), lambda g, i: (i, g)),
                  pl.BlockSpec((None, T, LANES), lambda g, i: (g, 0, 0)),
                  pl.BlockSpec((None, nk, vrows, tk), lambda g, i: (g, 0, 0, 0))],
        out_specs=[pl.BlockSpec((tq, 2 * LANES), lambda g, i: (i, g)),
                   pl.BlockSpec((None, None, sq, LANES), lambda g, i: (g, i, 0, 0))],
        out_shape=[_sds((T, ATTN_W), BF16), _sds((2, nq, sq, LANES), F32)],
        scratch_shapes=[pltpu.VMEM((LANES, sq), BF16)] + [pltpu.VMEM((tk, sq), F32)] * depth
        + [pltpu.VMEM((tk, sq), BF16), pltpu.VMEM((tk, sq), BF16), row, pltpu.VMEM((vrows, sq), F32)],
        compiler_params=_cp("parallel", "parallel"))(q, kd, v1)


def _attn_bwd(q, o, dmix, lse, kt, kd, vt, tq):
    T = q.shape[0]
    nk, tk = kt.shape[1], kt.shape[3]
    nq = T // tq
    sq = 4 * tq
    rep = tk // LANES

    def body(q_ref, o_ref, do_ref, lse_ref, kt_ref, kd_ref, vt_ref, dq_ref, dkt_ref, dvt_ref):
        i = pl.program_id(1)
        qs = _stack_heads(q_ref[...])
        dof = _stack_heads(do_ref[...])
        dos = dof.astype(BF16)
        qst = qs.astype(F32).T.astype(BF16)
        dost = dof.T.astype(BF16)
        o_pair = o_ref[...].astype(F32)
        os_ = jnp.concatenate([o_pair[:, 0:LANES], o_pair[:, 0:LANES], o_pair[:, LANES:], o_pair[:, LANES:]], axis=0)
        delta = jnp.sum(dof * os_, axis=-1, keepdims=True)
        lse_t = jnp.concatenate([lse_ref[...]] * rep, axis=1)

        @pl.when(i == 0)
        def _():
            dkt_ref[...] = jnp.zeros_like(dkt_ref)
            dvt_ref[...] = jnp.zeros_like(dvt_ref)

        def step(j, dq):
            kdb = kd_ref[pl.ds(pl.multiple_of(j * tk, tk), tk), :]
            pexp = jnp.exp(_dot(qs, kt_ref[j]) - lse_t)
            ds = pexp * (_dot(dos, vt_ref[j]) - delta)
            pb = pexp.astype(BF16)
            dsb = ds.astype(BF16)
            dvt_ref[j] += _dot(dost, pb)
            dkt_ref[j] += _dot(qst, dsb)
            return dq + _dot(dsb, kdb)

        dq = lax.fori_loop(0, nk, step, jnp.zeros((sq, LANES), F32))
        lo = lax.broadcasted_iota(jnp.int32, (tq, LANES), 1) < HEAD_DIM
        dq_ref[...] = jnp.concatenate([jnp.where(lo, dq[0:tq], dq[tq:2 * tq]),
                                       jnp.where(lo, dq[2 * tq:3 * tq], dq[3 * tq:4 * tq])], axis=1)

    tspec = pl.BlockSpec((None, nk, LANES, tk), lambda g, i: (g, 0, 0, 0))
    qspec = pl.BlockSpec((tq, 2 * LANES), lambda g, i: (i, g))
    return pl.pallas_call(
        body, name="attn_bwd", grid=(2, nq),
        in_specs=[qspec, qspec, qspec, pl.BlockSpec((None, None, sq, LANES), lambda g, i: (g, i, 0, 0)),
                  tspec, pl.BlockSpec((None, T, LANES), lambda g, i: (g, 0, 0)), tspec],
        out_specs=[qspec, tspec, tspec],
        out_shape=[_sds((T, ATTN_W), F32), _sds((2, nk, LANES, tk), F32), _sds((2, nk, LANES, tk), F32)],
        compiler_params=_cp("parallel", "arbitrary"))(q, o, dmix, lse, kt, kd, vt)


def _fold_t(t_ref):
    rows = []
    for gidx in range(2):
        t = t_ref[gidx]
        rows.append(t[0:HEAD_DIM] + t[HEAD_DIM:2 * HEAD_DIM])
    return jnp.concatenate(rows, axis=0).T


def _qk_bwd(p, dq, dkt, dvt, gq, gk, cos, sin, bmat):
    T = p.shape[0]
    nk, tk = dkt.shape[1], dkt.shape[3]
    scale = HEAD_DIM ** -0.5

    def norm_bwd(x, dy, gain, bm):
        r = lax.rsqrt(_head_mean(x * x, bm) + EPS)
        xh = x * r
        dxh = dy * gain
        return r * (dxh - xh * _head_mean(dxh * xh, bm)), jnp.sum(dy * xh, axis=0, keepdims=True)

    def body(q_ref, kv_ref, dq_ref, dkt_ref, dvt_ref, gq_ref, gk_ref, cos_ref, sin_ref, b_ref, o_ref, dgq_ref, dgk_ref):
        i = pl.program_id(0)
        cosv, sinv, bm = cos_ref[...], sin_ref[...], b_ref[...]
        dgq = []
        for pr in range(ATTN_W // LANES):
            cols = pl.ds(pr * LANES, LANES)
            dy = _rope_t(dq_ref[:, cols] * scale, cosv, sinv)
            dx, dg = norm_bwd(q_ref[:, cols], dy, gq_ref[:, cols], bm)
            o_ref[:, cols] = dx.astype(BF16)
            dgq.append(dg)
        dy = _rope_t(_fold_t(dkt_ref), cosv, sinv)
        dx, dgk = norm_bwd(kv_ref[:, pl.ds(0, LANES)], dy, gk_ref[...], bm)
        o_ref[:, pl.ds(ATTN_W, LANES)] = dx.astype(BF16)
        o_ref[:, pl.ds(ATTN_W + LANES, LANES)] = _fold_t(dvt_ref).astype(BF16)
        _accum(dgq_ref, i, jnp.concatenate(dgq, axis=1))
        _accum(dgk_ref, i, dgk)

    tspec = pl.BlockSpec((2, None, LANES, tk), lambda i: (0, i, 0, 0))
    tab = pl.BlockSpec((tk, LANES), lambda i: (i, 0))
    return pl.pallas_call(
        body, name="qk_bwd", grid=(nk,),
        in_specs=[pl.BlockSpec((tk, ATTN_W), lambda i: (i, 0)), pl.BlockSpec((tk, 2 * KV_W), lambda i: (i, ATTN_W // (2 * KV_W))),
                  pl.BlockSpec((tk, ATTN_W), lambda i: (i, 0)), tspec, tspec,
                  pl.BlockSpec((1, ATTN_W), lambda i: (0, 0)), pl.BlockSpec((1, KV_W), lambda i: (0, 0)),
                  tab, tab, pl.BlockSpec((LANES, LANES), lambda i: (0, 0))],
        out_specs=[pl.BlockSpec((tk, ATTN_W + 2 * KV_W), lambda i: (i, 0)),
                   pl.BlockSpec((1, ATTN_W), lambda i: (0, 0)), pl.BlockSpec((1, KV_W), lambda i: (0, 0))],
        out_shape=[_sds((T, ATTN_W + 2 * KV_W), BF16), _sds((1, ATTN_W), F32), _sds((1, KV_W), F32)],
        compiler_params=_cp("arbitrary"))(p, p, dq, dkt, dvt, gq, gk, cos, sin, bmat)


def _loss_head(y, target):
    T, Dm = y.shape
    tm = min(T, _ROW_TILE)

    def body(y_ref, t_ref, dy_ref, l_ref):
        i = pl.program_id(0)
        err = y_ref[...] - t_ref[...]
        dy_ref[...] = err * (1.0 / Dm)
        part = jnp.sum(jnp.sum(err * err, axis=-1, keepdims=True), axis=0, keepdims=True) * (0.5 / Dm)
        _accum(l_ref, i, jnp.broadcast_to(part, (8, LANES)))

    row = pl.BlockSpec((tm, Dm), lambda i: (i, 0))
    return pl.pallas_call(
        body, name="loss_head", grid=(T // tm,), in_specs=[row, row],
        out_specs=[row, pl.BlockSpec((8, LANES), lambda i: (0, 0))],
        out_shape=[_sds((T, Dm), F32), _sds((8, LANES), F32)], compiler_params=_cp("arbitrary"))(y, target)


def _adamw(w, g, m, v, name):
    R, C = w.shape
    tr = R
    for cand in (512, 256, 128, 64, 32, 16, 8):
        if R % cand == 0:
            tr = cand
            break
    c1 = 1.0 - ADAM_B1 ** ADAM_STEP
    c2 = 1.0 - ADAM_B2 ** ADAM_STEP

    def body(w_ref, g_ref, m_ref, v_ref, d_ref, mo_ref, vo_ref):
        gv = g_ref[...]
        mn = ADAM_B1 * m_ref[...] + (1.0 - ADAM_B1) * gv
        vn = ADAM_B2 * v_ref[...] + (1.0 - ADAM_B2) * (gv * gv)
        d_ref[...] = -ADAM_LR * ((mn / c1) / (jnp.sqrt(vn / c2) + ADAM_EPS) + ADAM_WD * w_ref[...])
        mo_ref[...] = mn
        vo_ref[...] = vn

    blk = pl.BlockSpec((tr, C), lambda i: (i, 0))
    return pl.pallas_call(
        body, name=name, grid=(R // tr,), in_specs=[blk] * 4, out_specs=[blk] * 3,
        out_shape=[_sds((R, C), F32)] * 3, compiler_params=_cp("parallel"))(w, g, m, v)


def _cast_bf16(w, name):
    R, C = w.shape
    tr = 512 if R % 512 == 0 else 256

    def body(w_ref, o_ref):
        o_ref[...] = w_ref[...].astype(BF16)

    blk = pl.BlockSpec((tr, C), lambda i: (i, 0))
    return pl.pallas_call(body, name=name, grid=(R // tr,), in_specs=[blk], out_specs=blk,
                          out_shape=_sds((R, C), BF16), compiler_params=_cp("parallel"))(w)


def _position():
    x, y, c = lax.axis_index("x"), lax.axis_index("y"), lax.axis_index("c")
    return x, y, c


def _other_chips(x, y):
    return [(1 - x, y), (x, 1 - y), (1 - x, 1 - y)]


_HBM = pl.BlockSpec(memory_space=pltpu.HBM)
_SEM = pl.BlockSpec(memory_space=pltpu.SEMAPHORE)
_EFFECT = pltpu.SideEffectType.DATAFLOW_SIDE_EFFECTING


def _chip_copies(srcs, lands, send_sems, recv_sems, per_chip, arriving):
    x, y, c = _position()
    me = 2 * x + y
    copies = []
    for t, (src, land) in enumerate(zip(srcs, lands)):
        for k, (px, py) in enumerate(_other_chips(x, y)):
            peer = 2 * px + py
            copies.append(pltpu.make_async_remote_copy(
                src_ref=src.at[peer] if per_chip else src, dst_ref=land.at[peer if arriving else me],
                send_sem=send_sems[3 * t + k], recv_sem=recv_sems[3 * t + k],
                device_id=(px, py, c), device_id_type=MESH))
    return copies


def _chips_start(srcs, per_chip, name):
    n = len(srcs)
    slab = [s.shape[1:] if per_chip else s.shape for s in srcs]
    lands = [lax.empty((N_CHIPS,) + sh, s.dtype) for sh, s in zip(slab, srcs)]

    ns = 3 * n

    def body(*refs):
        ins = refs[:2 * n]
        send_sems, recv_sems = refs[2 * n:2 * n + ns], refs[2 * n + ns:2 * n + 2 * ns]
        token = refs[-1]
        for cp in _chip_copies(ins[:n], ins[n:], send_sems, recv_sems, per_chip, False):
            cp.start()
        token[...] = jnp.zeros_like(token)

    args = [pltpu.with_memory_space_constraint(a, pltpu.HBM) for a in list(srcs) + lands]
    outs = pl.pallas_call(
        body, name=name,
        out_shape=[pltpu.SemaphoreType.DMA(())] * (2 * ns)
        + [pltpu.HBM(a.shape, a.dtype) for a in args] + [_sds((8, LANES), F32)],
        in_specs=[_HBM] * (2 * n),
        out_specs=[_SEM] * (2 * ns) + [_HBM] * (2 * n) + [pl.BlockSpec(memory_space=pltpu.VMEM)],
        input_output_aliases={i: 2 * ns + i for i in range(2 * n)},
        compiler_params=pltpu.CompilerParams(has_side_effects=_EFFECT))(*args)
    sems, rest = outs[:2 * ns], outs[2 * ns:]
    return sems[:ns], sems[ns:], rest[:n], rest[n:2 * n], rest[-1]


def _chips_wait(handle, after, per_chip, name):
    send_sems, recv_sems, srcs, lands, _ = handle
    n = len(srcs)
    ns = 3 * n

    def body(*refs):
        ins = refs[:2 * n]
        s_sems, r_sems = refs[2 * n:2 * n + ns], refs[2 * n + ns:2 * n + 2 * ns]
        for cp in _chip_copies(ins[:n], ins[n:], s_sems, r_sems, per_chip, False):
            cp.wait_send()
        for cp in _chip_copies(ins[:n], ins[n:], s_sems, r_sems, per_chip, True):
            cp.wait_recv()

    outs = pl.pallas_call(
        body, name=name,
        out_shape=[pltpu.HBM(a.shape, a.dtype) for a in list(srcs) + list(lands)],
        in_specs=[_HBM] * (2 * n) + [_SEM] * (2 * ns) + [ANY],
        out_specs=[_HBM] * (2 * n),
        input_output_aliases={i: i for i in range(2 * n)},
        compiler_params=pltpu.CompilerParams(has_side_effects=_EFFECT))(*srcs, *lands, *send_sems, *recv_sems, after)
    return outs[:n], outs[n:]


def _sum_chips_own(land, own, name):
    S, R, C = land.shape
    tr = R
    for cand in (256, 128, 64, 32, 16, 8):
        if R % cand == 0:
            tr = cand
            break

    def body(l_ref, o_ref, out_ref):
        x, y, _ = _position()
        me = 2 * x + y
        mine = o_ref[me]
        acc = None
        for k in range(S):
            part = jnp.where(me == k, mine, l_ref[k])
            acc = part if acc is None else acc + part
        out_ref[...] = acc

    blk = pl.BlockSpec((S, tr, C), lambda i: (0, i, 0))
    return pl.pallas_call(
        body, name=name, grid=(R // tr,), in_specs=[blk, blk], out_specs=pl.BlockSpec((tr, C), lambda i: (i, 0)),
        out_shape=_sds((R, C), F32), compiler_params=_cp("parallel"))(land, own)


def _exchange_sibling(arrays):
    n = len(arrays)

    def body(*refs):
        ins, outs = refs[:n], refs[n:2 * n]
        send_sems, recv_sems = refs[2 * n:]
        x, y, c = _position()
        sends = []
        for t in range(n):
            cp = pltpu.make_async_remote_copy(src_ref=ins[t], dst_ref=outs[t], send_sem=send_sems.at[t],
                                              recv_sem=recv_sems.at[t], device_id=(x, y, 1 - c), device_id_type=MESH)
            cp.start()
            sends.append(cp)
        for cp in sends:
            cp.wait()

    return pl.pallas_call(
        body, name="grads_to_sibling",
        in_specs=[ANY] * n, out_specs=[ANY] * n, out_shape=[_sds(a.shape, a.dtype) for a in arrays],
        scratch_shapes=[pltpu.SemaphoreType.DMA((n,)), pltpu.SemaphoreType.DMA((n,))],
        compiler_params=pltpu.CompilerParams(has_side_effects=True))(*arrays)


def _adamw_sum(w, ga, gb, m, v, name):
    R, C = w.shape
    tr = next(t for t in (512, 256, 128, 64) if R % t == 0 and t * C * 4 <= (1 << 20))
    c1 = 1.0 - ADAM_B1 ** ADAM_STEP
    c2 = 1.0 - ADAM_B2 ** ADAM_STEP

    def body(w_ref, ga_ref, gb_ref, m_ref, v_ref, g_ref, d_ref, mo_ref, vo_ref):
        gv = ga_ref[...] + gb_ref[...]
        mn = ADAM_B1 * m_ref[...] + (1.0 - ADAM_B1) * gv
        vn = ADAM_B2 * v_ref[...] + (1.0 - ADAM_B2) * (gv * gv)
        g_ref[...] = gv
        d_ref[...] = -ADAM_LR * ((mn / c1) / (jnp.sqrt(vn / c2) + ADAM_EPS) + ADAM_WD * w_ref[...])
        mo_ref[...] = mn
        vo_ref[...] = vn

    blk = pl.BlockSpec((tr, C), lambda i: (i, 0))
    return pl.pallas_call(
        body, name=name, grid=(R // tr,), in_specs=[blk] * 5, out_specs=[blk] * 4,
        out_shape=[_sds((R, C), F32)] * 4, compiler_params=_cp("parallel"))(w, ga, gb, m, v)


def _allreduce_small(packed):
    R, C = packed.shape
    ndev = 2 * N_CHIPS

    def body(in_ref, out_ref, buf, send_sems, recv_sems):
        x, y, c = _position()
        me = 4 * x + 2 * y + c
        buf[me] = in_ref[...]
        sends = []
        for k in range(1, ndev):
            peer = (x ^ (k >> 2), y ^ ((k >> 1) & 1), c ^ (k & 1))
            cp = pltpu.make_async_remote_copy(src_ref=in_ref, dst_ref=buf.at[me], send_sem=send_sems.at[k - 1],
                                              recv_sem=recv_sems.at[k - 1], device_id=peer, device_id_type=MESH)
            cp.start()
            sends.append(cp)
        for k in range(1, ndev):
            peer = (x ^ (k >> 2), y ^ ((k >> 1) & 1), c ^ (k & 1))
            pltpu.make_async_remote_copy(src_ref=in_ref, dst_ref=buf.at[me ^ k], send_sem=send_sems.at[k - 1],
                                         recv_sem=recv_sems.at[k - 1], device_id=peer, device_id_type=MESH).wait_recv()
        for cp in sends:
            cp.wait_send()
        acc = buf[0]
        for d in range(1, ndev):
            acc = acc + buf[d]
        out_ref[...] = acc

    return pl.pallas_call(
        body, name="allreduce_small",
        in_specs=[pl.BlockSpec(memory_space=pltpu.VMEM)], out_specs=pl.BlockSpec(memory_space=pltpu.VMEM),
        out_shape=_sds((R, C), F32),
        scratch_shapes=[pltpu.VMEM((ndev, R, C), F32), pltpu.SemaphoreType.DMA((ndev - 1,)),
                        pltpu.SemaphoreType.DMA((ndev - 1,))],
        compiler_params=pltpu.CompilerParams(vmem_limit_bytes=VMEM_LIMIT_BYTES, has_side_effects=True))(packed)


def _rope_tables(T):
    pos = jnp.arange(T)
    row = (pos // GRID_W).astype(F32)
    col = (pos % GRID_W).astype(F32)
    inv = 1.0 / (ROPE_THETA ** (jnp.arange(AXIS_DIM // 2, dtype=F32) * 2.0 / AXIS_DIM))
    ar, ac = row[:, None] * inv[None, :], col[:, None] * inv[None, :]
    cos = jnp.concatenate([jnp.cos(ar), jnp.cos(ar), jnp.cos(ac), jnp.cos(ac)], axis=-1)
    sin = jnp.concatenate([-jnp.sin(ar), jnp.sin(ar), -jnp.sin(ac), jnp.sin(ac)], axis=-1)
    return jnp.tile(cos, (1, LANES // HEAD_DIM)), jnp.tile(sin, (1, LANES // HEAD_DIM))


def _head_mean_matrix():
    h = jnp.arange(LANES) // HEAD_DIM
    return jnp.where(h[:, None] == h[None, :], 1.0 / HEAD_DIM, 0.0).astype(F32)


def _pack(arrays):
    flat = jnp.concatenate([a.reshape(-1) for a in arrays])
    rows = -(-flat.shape[0] // LANES)
    rows = -(-rows // 256) * 256
    return jnp.pad(flat, (0, rows * LANES - flat.shape[0])).reshape(rows, LANES)


def _unpack(packed, like):
    flat = packed.reshape(-1)
    out, off = [], 0
    for a in like:
        out.append(flat[off:off + a.size].reshape(a.shape))
        off += a.size
    return out


def _layer_fwd(x, lw, consts, ffn_weights):
    cos, sin, bmat = consts
    T = x.shape[0]
    tk = min(T, _ATTN_TK)
    tq = min(T, _ATTN_TQ)
    h = _norm_fwd(x, lw["norm1_g"])
    p = _mm_nn(h, lw["w_in"], out_dtype=F32, name="mm_p")
    qn, kt, kd, vt, v1 = _qk_prep(p, lw["gq"], lw["gk"], cos, sin, bmat, tk, min(tk, _ATTN_FWD_TK))
    o, lse = _attn_fwd(qn, kd, v1, tq)
    go = _sg_fwd(p, lw["sg_norm_g"], lw["sg_w"], lw["sg_bias"])
    co = _convmix_fwd(p, lw["conv_w"])
    mix = jnp.concatenate([o, go, co], axis=1)
    x_mid = _mm_nn(mix, lw["w_out"], out_dtype=F32, name="mm_out", res=x)
    lw.update(ffn_weights(x_mid))
    h2 = _norm_fwd(x_mid, lw["norm2_g"])
    up = _mm_nn(h2, lw["w_up"], out_dtype=F32, name="mm_up", tm=_FFN_ROW_TILE)
    act = _ffn_act_fwd(up, lw["ffn_conv_w"])
    x_out = _mm_nn(act, lw["w_down"], out_dtype=F32, name="mm_down", res=x_mid)
    saved = dict(x=x, h=h, p=p, qn=qn, kt=kt, kd=kd, vt=vt, o=o, lse=lse, mix=mix, x_mid=x_mid, h2=h2, up=up, act=act)
    return x_out, saved


def _layer_bwd(dx, s, lw, consts, send):
    cos, sin, bmat = consts
    T = dx.shape[0]
    tq = min(T, _ATTN_TQ)
    g = {}
    d_act = _mm_nt(dx, lw["w_down"], name="mm_dact")
    g_down = _mm_tn(s["act"], dx, tk=D_FF // 2, tn=D_MODEL, name="mm_dwdown", tm=_WGRAD_ROWS // 2)
    tok = send("w_down", g_down.reshape(N_CHIPS, D_FF // N_CHIPS, D_MODEL))
    d_up, d_cw = _ffn_act_bwd(s["up"], d_act, lw["ffn_conv_w"] + tok)
    g["ffn_conv_w"] = d_cw.transpose(1, 0, 2).reshape(3, 2 * D_FF)
    tok = send("w_up", _mm_tn(s["h2"], d_up, tk=512, tn=D_FF // 2, name="mm_dwup", shards=N_CHIPS, tm=_WGRAD_ROWS))
    d_h2 = _mm_nt(d_up, lw["w_up"], name="mm_dh2")
    dx2, g["norm2_g"] = _norm_bwd(s["x_mid"], d_h2, dx, lw["norm2_g"] + tok)
    d_mix = _mm_nt(dx2, lw["w_out"], name="mm_dmix")
    g_out = _mm_tn(s["mix"], dx2, tk=512, tn=D_MODEL, name="mm_dwout", tm=_WGRAD_ROWS // 2)
    tok = send("w_out", g_out.reshape(N_CHIPS, D_MODEL // N_CHIPS, D_MODEL))
    dp_c, g["conv_w"] = _convmix_bwd(s["p"], d_mix, lw["conv_w"] + tok)
    dp_b, g["sg_w"], d_bias, g["sg_norm_g"] = _sg_bwd(s["p"], d_mix, lw["sg_norm_g"], lw["sg_w"], lw["sg_wt"], lw["sg_bias"])
    g["sg_b"] = d_bias.reshape(SG_CHUNK, SG_W // HEAD_DIM, HEAD_DIM).sum(axis=-1).T
    dq, dkt, dvt = _attn_bwd(s["qn"], s["o"], d_mix, s["lse"], s["kt"], s["kd"], s["vt"], tq)
    dp_a, d_gq, d_gk = _qk_bwd(s["p"], dq, dkt, dvt, lw["gq"], lw["gk"], cos, sin, bmat)
    g["q_norm_g"] = d_gq.reshape(ATTN_W // HEAD_DIM, HEAD_DIM).sum(axis=0)
    g["k_norm_g"] = d_gk.reshape(KV_W // HEAD_DIM, HEAD_DIM).sum(axis=0)
    dp = jnp.concatenate([dp_a, dp_b, dp_c], axis=1)
    tok = send("w_in", _mm_tn(s["h"], dp, tk=D_MODEL, tn=512, name="mm_dwin", shards=N_CHIPS, tm=_WGRAD_ROWS))
    d_h = _mm_nt(dp, lw["w_in"], name="mm_dh")
    dx_in, g["norm1_g"] = _norm_bwd(s["x"], d_h, dx2, lw["norm1_g"] + tok)
    return dx_in, g


def _layer_weights(l, full, small):
    sg_w = small["sg_w"][l]
    sg_b = small["sg_b"][l]
    return dict(
        norm1_g=small["norm1_g"][l][None, :], norm2_g=small["norm2_g"][l][None, :],
        gq=jnp.tile(small["q_norm_g"][l], ATTN_W // HEAD_DIM)[None, :],
        gk=jnp.tile(small["k_norm_g"][l], KV_W // HEAD_DIM)[None, :],
        sg_norm_g=small["sg_norm_g"][l][None, :],
        sg_w=sg_w.astype(BF16), sg_wt=sg_w.transpose(0, 2, 1).astype(BF16),
        sg_bias=jnp.repeat(sg_b.T, HEAD_DIM, axis=1),
        **full)


_BIG = ("w_in", "w_out", "ffn_w_up", "ffn_w_down")
_SMALL_REPL = ("norm1_g", "q_norm_g", "k_norm_g", "sg_norm_g", "sg_w", "sg_b", "norm2_g")
_SMALL_SHARD = ("conv_w", "ffn_conv_w")
_ORDER = ("norm1_g", "w_in", "q_norm_g", "k_norm_g", "sg_norm_g", "sg_w", "sg_b", "conv_w", "w_out", "norm2_g",
          "ffn_w_up", "ffn_conv_w", "ffn_w_down")


def kernel(x, norm1_g, w_in, q_norm_g, k_norm_g, sg_norm_g, sg_w, sg_b, conv_w, w_out, norm2_g, ffn_w_up, ffn_conv_w, ffn_w_down, loss_target, m_norm1_g, m_w_in, m_q_norm_g, m_k_norm_g, m_sg_norm_g, m_sg_w, m_sg_b, m_conv_w, m_w_out, m_norm2_g, m_ffn_w_up, m_ffn_conv_w, m_ffn_w_down, v_norm1_g, v_w_in, v_q_norm_g, v_k_norm_g, v_sg_norm_g, v_sg_w, v_sg_b, v_conv_w, v_w_out, v_norm2_g, v_ffn_w_up, v_ffn_conv_w, v_ffn_w_down):
    w = dict(norm1_g=norm1_g, w_in=w_in, q_norm_g=q_norm_g, k_norm_g=k_norm_g, sg_norm_g=sg_norm_g, sg_w=sg_w,
             sg_b=sg_b, conv_w=conv_w, w_out=w_out, norm2_g=norm2_g, ffn_w_up=ffn_w_up, ffn_conv_w=ffn_conv_w,
             ffn_w_down=ffn_w_down)
    mom = dict(norm1_g=m_norm1_g, w_in=m_w_in, q_norm_g=m_q_norm_g, k_norm_g=m_k_norm_g, sg_norm_g=m_sg_norm_g,
               sg_w=m_sg_w, sg_b=m_sg_b, conv_w=m_conv_w, w_out=m_w_out, norm2_g=m_norm2_g, ffn_w_up=m_ffn_w_up,
               ffn_conv_w=m_ffn_conv_w, ffn_w_down=m_ffn_w_down)
    var = dict(norm1_g=v_norm1_g, w_in=v_w_in, q_norm_g=v_q_norm_g, k_norm_g=v_k_norm_g, sg_norm_g=v_sg_norm_g,
               sg_w=v_sg_w, sg_b=v_sg_b, conv_w=v_conv_w, w_out=v_w_out, norm2_g=v_norm2_g, ffn_w_up=v_ffn_w_up,
               ffn_conv_w=v_ffn_conv_w, ffn_w_down=v_ffn_w_down)
    L = DEPTH
    T = x.shape[1]
    xs = x.reshape(T, D_MODEL)
    target = loss_target.reshape(T, D_MODEL)

    chip = 2 * lax.axis_index("x") + lax.axis_index("y")

    shards = [_cast_bf16(w[n].reshape(-1, w[n].shape[-1]), "cast_" + n).reshape(w[n].shape) for n in _BIG]
    shards += [conv_w, ffn_conv_w]
    w_in_s, w_out_s, w_up_s, w_down_s, conv_s, fconv_s = shards
    gathers = []
    for l in range(L):
        gathers.append((_chips_start([w_in_s[l], w_out_s[l], conv_s[l]], False, "gather_start_%da" % l),
                        _chips_start([w_up_s[l], w_down_s[l], fconv_s[l]], False, "gather_start_%db" % l)))
    start_token = sum(h[4][0, 0] for pair in gathers for h in pair)
    consts = _rope_tables(T) + (_head_mean_matrix(),)

    def gathered(handle, after, name):
        own, lands = _chips_wait(handle, after, False, name)
        return [lax.dynamic_update_slice(ld, o[None], (chip,) + (jnp.int32(0),) * o.ndim) for ld, o in zip(lands, own)]

    saved, lws = [], []
    act_x = xs
    for l in range(L):
        g_in, g_out, g_conv = gathered(gathers[l][0], act_x if l else gathers[-1][1][4], "gather_wait_%da" % l)
        lw = _layer_weights(l, dict(w_in=g_in, w_out=g_out.reshape(D_MODEL, D_MODEL),
                                    conv_w=g_conv.transpose(1, 0, 2).reshape(3, CONV_W)), w)
        if l == 0:
            lw["norm1_g"] = lw["norm1_g"] + start_token

        def ffn_weights(after, l=l):
            g_up, g_down, g_fconv = gathered(gathers[l][1], after, "gather_wait_%db" % l)
            return dict(w_up=g_up, w_down=g_down.reshape(D_FF, D_MODEL),
                        ffn_conv_w=g_fconv.transpose(1, 0, 2).reshape(3, 2 * D_FF))

        act_x, s = _layer_fwd(act_x, lw, consts, ffn_weights)
        saved.append(s)
        lws.append(lw)
    dx, loss_blk = _loss_head(act_x, target)
    loss = lax.psum(loss_blk[0, 0], ("x", "y", "c"))

    grads = [None] * L
    partial = [None] * L

    def collect(pending, after, l):
        sums = {}
        for name, handle in pending:
            own, lands = _chips_wait(handle, after, True, "grad_wait_%d_%s" % (l, name))
            sums[name] = _sum_chips_own(lands[0], own[0], "sum_chips_" + name)
        return sums

    pending_prev = None
    for l in reversed(range(L)):
        pending = []

        def send(name, g4, l=l, pending=pending):
            handle = _chips_start([g4], True, "grad_start_%d_%s" % (l, name))
            pending.append((name, handle))
            return handle[4][0, 0]

        dx, g = _layer_bwd(dx, saved[l], lws[l], consts, send)
        grads[l] = g
        if pending_prev is not None:
            partial[l + 1] = collect(pending_prev, dx, l + 1)
        pending_prev = pending
    partial[0] = collect(pending_prev, dx, 0)
    grad_x = dx.reshape(x.shape)

    small_names = _SMALL_REPL + _SMALL_SHARD
    key = dict(norm1_g="norm1_g", q_norm_g="q_norm_g", k_norm_g="k_norm_g", sg_norm_g="sg_norm_g", sg_w="sg_w",
               sg_b="sg_b", norm2_g="norm2_g", conv_w="conv_w", ffn_conv_w="ffn_conv_w")
    small_local = [jnp.stack([grads[l][key[n]].reshape(-1) for l in range(L)]) for n in small_names]
    small_sum = _unpack(_allreduce_small(_pack(small_local)), small_local)
    grad = {}
    for n, a in zip(small_names, small_sum):
        grad[n] = a
    for n in _SMALL_REPL:
        grad[n] = grad[n].reshape(w[n].shape)
    for n in _SMALL_SHARD:
        full_w = grad[n].reshape(L, 3, -1)
        width = w[n].shape[-1]
        grad[n] = lax.dynamic_slice_in_dim(full_w, chip * width, width, axis=2)

    short = dict(w_in="w_in", w_out="w_out", ffn_w_up="w_up", ffn_w_down="w_down")
    mine = [jnp.stack([partial[l][short[n]] for l in range(L)]) for n in _BIG]
    theirs = _exchange_sibling(mine)
    delta, new_m, new_v = {}, {}, {}
    for n, ga, gb in zip(_BIG, mine, theirs):
        shp = w[n].shape
        v2 = lambda a: a.reshape(-1, shp[-1])
        gsum, d, mn, vn = _adamw_sum(v2(w[n]), v2(ga), v2(gb), v2(mom[n]), v2(var[n]), "adamw_" + n)
        grad[n], delta[n], new_m[n], new_v[n] = gsum.reshape(shp), d.reshape(shp), mn.reshape(shp), vn.reshape(shp)
    for group, gname in ((_SMALL_REPL, "adamw_small"), (_SMALL_SHARD, "adamw_conv")):
        like = [w[n] for n in group]
        outs = _adamw(_pack([w[n] for n in group]), _pack([grad[n] for n in group]), _pack([mom[n] for n in group]),
                      _pack([var[n] for n in group]), gname)
        for res, dst in zip(outs, (delta, new_m, new_v)):
            for n, a in zip(group, _unpack(res, like)):
                dst[n] = a

    return (loss, grad_x, *[grad[n] for n in _ORDER], *[delta[n] for n in _ORDER],
            *[new_m[n] for n in _ORDER], *[new_v[n] for n in _ORDER])
```

```python
import jax
import jax.numpy as jnp
from jax import lax
from jax.experimental import pallas as pl
from jax.experimental.pallas import tpu as pltpu

F32 = jnp.float32
BF16 = jnp.bfloat16

DEPTH = 4
D_MODEL = 1024
HEAD_DIM = 64
ATTN_W = 512
KV_W = 128
SG_W = 256
CONV_W = 256
SG_CHUNK = 128
D_FF = 2816
PROJ_W = 2048
GRID_W = 64
ROPE_THETA = 10000.0
AXIS_DIM = HEAD_DIM // 2
EPS = 1e-6
N_CHIPS = 4

ADAM_LR = 0.001
ADAM_B1 = 0.9
ADAM_B2 = 0.999
ADAM_EPS = 1e-08
ADAM_WD = 0.01
ADAM_STEP = 10

_ROW_TILE = 512
_FFN_ROW_TILE = 256
_WGRAD_ROWS = 2048
LANES = 128
HALO = 8
VMEM_LIMIT_BYTES = 56 * 1024 * 1024
MESH = pl.DeviceIdType.MESH
ANY = pl.BlockSpec(memory_space=pl.ANY)


def _cp(*sem):
    return pltpu.CompilerParams(dimension_semantics=sem if sem else None,
                                vmem_limit_bytes=VMEM_LIMIT_BYTES)


def _sds(shape, dtype):
    return jax.ShapeDtypeStruct(shape, dtype)


def _dot(a, b):
    return jnp.dot(a, b, preferred_element_type=F32)


def _dot_nt(a, b):
    return lax.dot_general(a, b, (((1,), (1,)), ((), ())), preferred_element_type=F32)


def _dot_tn(a, b):
    return lax.dot_general(a, b, (((0,), (0,)), ((), ())), preferred_element_type=F32)


def _norm_fwd(x, g):
    T, Dm = x.shape
    tm = min(T, _ROW_TILE)

    def body(x_ref, g_ref, o_ref):
        xv = x_ref[...]
        r = lax.rsqrt(jnp.mean(xv * xv, axis=-1, keepdims=True) + EPS)
        o_ref[...] = ((xv * r) * g_ref[...]).astype(BF16)

    return pl.pallas_call(
        body, name="norm_fwd", grid=(T // tm,),
        in_specs=[pl.BlockSpec((tm, Dm), lambda i: (i, 0)), pl.BlockSpec((1, Dm), lambda i: (0, 0))],
        out_specs=pl.BlockSpec((tm, Dm), lambda i: (i, 0)),
        out_shape=_sds((T, Dm), BF16), compiler_params=_cp("parallel"))(x, g)


def _whole(w):
    return pl.BlockSpec(w.shape, lambda *g: (0,) * w.ndim)


def _mm_nn(a, w, *, out_dtype, name, res=None, tm=None):
    M, K = a.shape
    N = w.shape[-1] if w.ndim == 2 else w.shape[0] * w.shape[2]
    tm = min(M, tm or _ROW_TILE)
    has_res = res is not None

    def body(*refs):
        a_ref, w_ref = refs[0], refs[1]
        o_ref = refs[-1]
        av = a_ref[...].astype(BF16)
        parts = [w_ref[...]] if w.ndim == 2 else [w_ref[s] for s in range(w.shape[0])]
        ns = N // len(parts)
        for s, wv in enumerate(parts):
            cols = pl.ds(s * ns, ns)
            acc = _dot(av, wv)
            if has_res:
                acc = acc + refs[2][:, cols]
            o_ref[:, cols] = acc.astype(out_dtype)

    in_specs = [pl.BlockSpec((tm, K), lambda i: (i, 0)), _whole(w)]
    args = [a, w]
    if has_res:
        in_specs.append(pl.BlockSpec((tm, N), lambda i: (i, 0)))
        args.append(res)
    return pl.pallas_call(
        body, name=name, grid=(M // tm,), in_specs=in_specs, out_specs=pl.BlockSpec((tm, N), lambda i: (i, 0)),
        out_shape=_sds((M, N), out_dtype), compiler_params=_cp("parallel"))(*args)


def _a_spec(a, tm, tn, row_of, col_of):
    if a.ndim == 2:
        return pl.BlockSpec((tm, tn), lambda *g: (row_of(*g), col_of(*g)))
    bph = a.shape[2] // tn
    return pl.BlockSpec((None, tm, tn), lambda *g: (col_of(*g) // bph, row_of(*g), col_of(*g) % bph))


def _a_cols(a):
    return a.shape[1] if a.ndim == 2 else a.shape[0] * a.shape[2]


def _mm_nt(a, w, *, name, tm=None, norm=None):
    M = a.shape[-2]
    Kw = w.shape[-2]
    tm = min(M, tm or _ROW_TILE)

    def product(a_ref, w_ref):
        if w.ndim == 2:
            return _dot_nt(a_ref[...].astype(BF16), w_ref[...])
        S, ns = w.shape[0], w.shape[2]
        acc = None
        for s in range(S):
            if a.ndim == 2:
                piece = a_ref[:, pl.ds(s * ns, ns)]
            else:
                per_half = S // 2
                piece = a_ref[s // per_half, :, pl.ds((s % per_half) * ns, ns)]
            part = _dot_nt(piece.astype(BF16), w_ref[s])
            acc = part if acc is None else acc + part
        return acc

    def body(a_ref, w_ref, o_ref):
        o_ref[...] = product(a_ref, w_ref)

    def body_norm(a_ref, w_ref, x_ref, dr_ref, g_ref, dx_ref, dg_ref):
        dhv = product(a_ref, w_ref)
        xv = x_ref[...]
        r = lax.rsqrt(jnp.mean(xv * xv, axis=-1, keepdims=True) + EPS)
        xh = xv * r
        dxh = dhv * g_ref[...]
        dx_ref[...] = dr_ref[...] + r * (dxh - xh * jnp.mean(dxh * xh, axis=-1, keepdims=True))
        _accum(dg_ref, pl.program_id(0), jnp.sum(dhv * xh, axis=0, keepdims=True))

    a_spec = (pl.BlockSpec((tm, a.shape[1]), lambda i: (i, 0)) if a.ndim == 2
              else pl.BlockSpec((2, tm, a.shape[2]), lambda i: (0, i, 0)))
    row = pl.BlockSpec((tm, Kw), lambda i: (i, 0))
    if norm is None:
        return pl.pallas_call(
            body, name=name, grid=(M // tm,), in_specs=[a_spec, _whole(w)], out_specs=row,
            out_shape=_sds((M, Kw), F32), compiler_params=_cp("parallel"))(a, w)
    vec = pl.BlockSpec((1, Kw), lambda i: (0, 0))
    return pl.pallas_call(
        body_norm, name=name, grid=(M // tm,), in_specs=[a_spec, _whole(w), row, row, vec], out_specs=[row, vec],
        out_shape=[_sds((M, Kw), F32), _sds((1, Kw), F32)], compiler_params=_cp("arbitrary"))(a, w, *norm)


def _mm_tn(a, b, *, tk, tn, name, shards=None, tm=None):
    M, K = a.shape
    N = _a_cols(b)
    tm = min(M, tm or _ROW_TILE)

    def body(a_ref, b_ref, o_ref):
        m = pl.program_id(2)
        part = _dot_tn(a_ref[...].astype(BF16), b_ref[...].astype(BF16))

        @pl.when(m == 0)
        def _():
            o_ref[...] = part

        @pl.when(m > 0)
        def _():
            o_ref[...] += part

    if shards is None:
        out_spec = pl.BlockSpec((tk, tn), lambda k, j, m: (k, j))
        out_shape = _sds((K, N), F32)
    else:
        bps = (N // shards) // tn
        out_spec = pl.BlockSpec((None, tk, tn), lambda k, j, m: (j // bps, k, j % bps))
        out_shape = _sds((shards, K, N // shards), F32)
    return pl.pallas_call(
        body, name=name, grid=(K // tk, N // tn, M // tm),
        in_specs=[pl.BlockSpec((tm, tk), lambda k, j, m: (m, k)),
                  _a_spec(b, tm, tn, lambda k, j, m: m, lambda k, j, m: j)],
        out_specs=out_spec, out_shape=out_shape,
        compiler_params=_cp("parallel", "parallel", "arbitrary"))(a, b)


def _halo_specs(T, tm, cw, ic):
    nb = tm // HALO
    last = T // HALO - 1

    def mk(rows, row_of):
        return pl.BlockSpec((rows, cw), lambda *g: (row_of(ic(*g)[0]), ic(*g)[1]))

    return [mk(HALO, lambda i: jnp.maximum(i * nb - 1, 0)), mk(tm, lambda i: i),
            mk(HALO, lambda i: jnp.minimum((i + 1) * nb, last))]


def _ext(prev_ref, cur_ref, next_ref, i, n):
    p = jnp.where(i > 0, prev_ref[...].astype(F32), 0.0)
    nx = jnp.where(i < n - 1, next_ref[...].astype(F32), 0.0)
    return jnp.concatenate([p, cur_ref[...].astype(F32), nx], axis=0)


def _dn(e):
    return pltpu.roll(e, 1, 0)


def _up(e):
    return pltpu.roll(e, e.shape[0] - 1, 0)


def _mid(e):
    return e[HALO:e.shape[0] - HALO]


def _taps(e):
    return _dn(e), e, _up(e)


def _conv3(taps, w):
    return taps[0] * w[0:1] + taps[1] * w[1:2] + taps[2] * w[2:3]


def _conv3_t(e, w):
    return _up(e) * w[0:1] + e * w[1:2] + _dn(e) * w[2:3]


def _conv3_wgrad(d, taps):
    return jnp.concatenate([jnp.sum(_mid(d * tap), axis=0, keepdims=True) for tap in taps], axis=0)


def _sigmoid(x):
    return 1.0 / (1.0 + jnp.exp(-x))


def _accum(ref, i, part):
    @pl.when(i == 0)
    def _():
        ref[...] = part

    @pl.when(i > 0)
    def _():
        ref[...] += part


def _ffn_act_fwd(up, cw):
    T = up.shape[0]
    tm = min(T, _FFN_ROW_TILE)
    cb = D_FF // 2
    nblk = D_FF // cb
    n = T // tm

    def body(gp, gc, gn, vp, vc, vn, wg_ref, wv_ref, o_ref):
        i = pl.program_id(1)
        gate = _conv3(_taps(_ext(gp, gc, gn, i, n)), wg_ref[...])
        val = _conv3(_taps(_ext(vp, vc, vn, i, n)), wv_ref[...])
        o_ref[...] = _mid(gate * _sigmoid(gate) * val).astype(BF16)

    return pl.pallas_call(
        body, name="ffn_act_fwd", grid=(nblk, n),
        in_specs=_halo_specs(T, tm, cb, lambda j, i: (i, j)) + _halo_specs(T, tm, cb, lambda j, i: (i, j + nblk))
        + [pl.BlockSpec((3, cb), lambda j, i: (0, j)), pl.BlockSpec((3, cb), lambda j, i: (0, j + nblk))],
        out_specs=pl.BlockSpec((tm, cb), lambda j, i: (i, j)),
        out_shape=_sds((T, D_FF), BF16), compiler_params=_cp("parallel", "parallel"))(
            up, up, up, up, up, up, cw, cw)


def _ffn_act_bwd(up, dact, cw):
    T = up.shape[0]
    tm = min(T, _FFN_ROW_TILE)
    cb = D_FF // 2
    nblk = D_FF // cb
    n = T // tm

    def body(gp, gc, gn, vp, vc, vn, dp_, dc, dn_, wg_ref, wv_ref, dup_ref, dcw_ref):
        i = pl.program_id(1)
        wg, wv = wg_ref[...], wv_ref[...]
        eg = _taps(_ext(gp, gc, gn, i, n))
        ev = _taps(_ext(vp, vc, vn, i, n))
        ed = _ext(dp_, dc, dn_, i, n)
        gate = _conv3(eg, wg)
        val = _conv3(ev, wv)
        sg = _sigmoid(gate)
        d_gate = ed * val * (sg * (1.0 + gate * (1.0 - sg)))
        d_val = ed * (gate * sg)
        dup_ref[0] = _mid(_conv3_t(d_gate, wg)).astype(BF16)
        dup_ref[1] = _mid(_conv3_t(d_val, wv)).astype(BF16)
        part = jnp.stack([_conv3_wgrad(d_gate, eg), _conv3_wgrad(d_val, ev)], axis=0)
        _accum(dcw_ref, i, part)

    return pl.pallas_call(
        body, name="ffn_act_bwd", grid=(nblk, n),
        in_specs=_halo_specs(T, tm, cb, lambda j, i: (i, j)) + _halo_specs(T, tm, cb, lambda j, i: (i, j + nblk))
        + _halo_specs(T, tm, cb, lambda j, i: (i, j))
        + [pl.BlockSpec((3, cb), lambda j, i: (0, j)), pl.BlockSpec((3, cb), lambda j, i: (0, j + nblk))],
        out_specs=[pl.BlockSpec((2, tm, cb), lambda j, i: (0, i, j)),
                   pl.BlockSpec((2, 3, cb), lambda j, i: (0, 0, j))],
        out_shape=[_sds((2, T, D_FF), BF16), _sds((2, 3, D_FF), F32)],
        compiler_params=_cp("parallel", "arbitrary"))(up, up, up, up, up, up, dact, dact, dact, cw, cw)


_CB_BLK, _CC_BLK, _CX_BLK = 5, 6, 7


def _convmix_fwd(p, w):
    T = p.shape[0]
    tm = min(T, _ROW_TILE)
    n = T // tm

    def body(cb_ref, ccp, ccc, ccn, cxp, cxc, cxn, w_ref, o_ref):
        i = pl.program_id(0)
        z = _ext(ccp, ccc, ccn, i, n) * _ext(cxp, cxc, cxn, i, n)
        o_ref[...] = (cb_ref[...] * _mid(_conv3(_taps(z), w_ref[...]))).astype(BF16)

    return pl.pallas_call(
        body, name="convmix_fwd", grid=(n,),
        in_specs=[pl.BlockSpec((tm, CONV_W), lambda i: (i, _CB_BLK))]
        + _halo_specs(T, tm, CONV_W, lambda i: (i, _CC_BLK)) + _halo_specs(T, tm, CONV_W, lambda i: (i, _CX_BLK))
        + [pl.BlockSpec((3, CONV_W), lambda i: (0, 0))],
        out_specs=pl.BlockSpec((tm, CONV_W), lambda i: (i, 0)),
        out_shape=_sds((T, CONV_W), BF16), compiler_params=_cp("parallel"))(p, p, p, p, p, p, p, w)


def _convmix_bwd(p, dmix, w):
    T = p.shape[0]
    tm = min(T, _ROW_TILE)
    n = T // tm
    dblk = (ATTN_W + SG_W) // CONV_W

    def body(cbp, cbc, cbn, ccp, ccc, ccn, cxp, cxc, cxn, dp_, dc, dn_, w_ref, o_ref, dw_ref):
        i = pl.program_id(0)
        wv = w_ref[...]
        ecb = _ext(cbp, cbc, cbn, i, n)
        ecc = _ext(ccp, ccc, ccn, i, n)
        ecx = _ext(cxp, cxc, cxn, i, n)
        ed = _ext(dp_, dc, dn_, i, n)
        z = _taps(ecc * ecx)
        d_cz = ed * ecb
        d_z = _conv3_t(d_cz, wv)
        o_ref[...] = jnp.concatenate([_mid(ed * _conv3(z, wv)), _mid(d_z * ecx), _mid(d_z * ecc)],
                                     axis=1).astype(BF16)
        _accum(dw_ref, i, _conv3_wgrad(d_cz, z))

    return pl.pallas_call(
        body, name="convmix_bwd", grid=(n,),
        in_specs=_halo_specs(T, tm, CONV_W, lambda i: (i, _CB_BLK)) + _halo_specs(T, tm, CONV_W, lambda i: (i, _CC_BLK))
        + _halo_specs(T, tm, CONV_W, lambda i: (i, _CX_BLK)) + _halo_specs(T, tm, CONV_W, lambda i: (i, dblk))
        + [pl.BlockSpec((3, CONV_W), lambda i: (0, 0))],
        out_specs=[pl.BlockSpec((tm, 3 * CONV_W), lambda i: (i, 0)), pl.BlockSpec((3, CONV_W), lambda i: (0, 0))],
        out_shape=[_sds((T, 3 * CONV_W), BF16), _sds((3, CONV_W), F32)],
        compiler_params=_cp("arbitrary"))(p, p, p, p, p, p, p, p, p, dmix, dmix, dmix, w)


_SU_BLK, _SV_BLK = 3, 4


def _sg_mixed(vnb, w_ref, bias, ch, pr, lo):
    vp = vnb[ch * SG_CHUNK:(ch + 1) * SG_CHUNK, pr * LANES:(pr + 1) * LANES]
    zero = jnp.zeros_like(vp)
    return (_dot(w_ref[2 * pr], jnp.where(lo, vp, zero)) + _dot(w_ref[2 * pr + 1], jnp.where(lo, zero, vp))
            + bias[:, pr * LANES:(pr + 1) * LANES]), vp


def _sg_fwd(p, g, w, bias):
    T = p.shape[0]
    tm = min(T, _ROW_TILE)

    def body(su_ref, sv_ref, g_ref, w_ref, b_ref, o_ref):
        lo = lax.broadcasted_iota(jnp.int32, (SG_CHUNK, LANES), 1) < HEAD_DIM
        sv = sv_ref[...]
        r = lax.rsqrt(jnp.mean(sv * sv, axis=-1, keepdims=True) + EPS)
        vnb = ((sv * r) * g_ref[...]).astype(BF16)
        bias_v = b_ref[...]
        for ch in range(tm // SG_CHUNK):
            for pr in range(2):
                mixed, _ = _sg_mixed(vnb, w_ref, bias_v, ch, pr, lo)
                rows, cols = pl.ds(ch * SG_CHUNK, SG_CHUNK), pl.ds(pr * LANES, LANES)
                o_ref[rows, cols] = (su_ref[rows, cols] * mixed).astype(BF16)

    return pl.pallas_call(
        body, name="sg_fwd", grid=(T // tm,),
        in_specs=[pl.BlockSpec((tm, SG_W), lambda i: (i, _SU_BLK)), pl.BlockSpec((tm, SG_W), lambda i: (i, _SV_BLK)),
                  pl.BlockSpec((1, SG_W), lambda i: (0, 0)), pl.BlockSpec((4, SG_CHUNK, SG_CHUNK), lambda i: (0, 0, 0)),
                  pl.BlockSpec((SG_CHUNK, SG_W), lambda i: (0, 0))],
        out_specs=pl.BlockSpec((tm, SG_W), lambda i: (i, 0)),
        out_shape=_sds((T, SG_W), BF16), compiler_params=_cp("parallel"))(p, p, g, w, bias)


def _sg_bwd(p, dmix, g, w, wt, bias):
    T = p.shape[0]
    tm = min(T, _ROW_TILE)
    dblk = ATTN_W // SG_W

    def body(su_ref, sv_ref, d_ref, g_ref, w_ref, wt_ref, b_ref, o_ref, dw_ref, db_ref, dg_ref, dvn_ref):
        i = pl.program_id(0)
        lo = lax.broadcasted_iota(jnp.int32, (SG_CHUNK, LANES), 1) < HEAD_DIM
        sv = sv_ref[...]
        gv = g_ref[...]
        r = lax.rsqrt(jnp.mean(sv * sv, axis=-1, keepdims=True) + EPS)
        xh = sv * r
        vnb = (xh * gv).astype(BF16)
        bias_v = b_ref[...]
        dw = [jnp.zeros((SG_CHUNK, SG_CHUNK), F32) for _ in range(4)]
        db = jnp.zeros((SG_CHUNK, SG_W), F32)
        for ch in range(tm // SG_CHUNK):
            dbs = []
            for pr in range(2):
                mixed, vp = _sg_mixed(vnb, w_ref, bias_v, ch, pr, lo)
                rows, cols = pl.ds(ch * SG_CHUNK, SG_CHUNK), pl.ds(pr * LANES, LANES)
                dgo = d_ref[rows, cols]
                o_ref[rows, cols] = (dgo * mixed).astype(BF16)
                dm = dgo * su_ref[rows, cols]
                dmb = dm.astype(BF16)
                zero = jnp.zeros_like(dmb)
                dw[2 * pr] += _dot_nt(jnp.where(lo, dmb, zero), vp)
                dw[2 * pr + 1] += _dot_nt(jnp.where(lo, zero, dmb), vp)
                dvn_ref[rows, cols] = jnp.where(lo, _dot(wt_ref[2 * pr], dmb), _dot(wt_ref[2 * pr + 1], dmb))
                dbs.append(dm)
            db += jnp.concatenate(dbs, axis=1)
        dvn = dvn_ref[...]
        dxh = dvn * gv
        o_ref[:, pl.ds(SG_W, SG_W)] = (r * (dxh - xh * jnp.mean(dxh * xh, axis=-1, keepdims=True))).astype(BF16)
        _accum(dw_ref, i, jnp.stack(dw, axis=0))
        _accum(db_ref, i, db)
        _accum(dg_ref, i, jnp.sum(dvn * xh, axis=0, keepdims=True))

    wspec = pl.BlockSpec((4, SG_CHUNK, SG_CHUNK), lambda i: (0, 0, 0))
    return pl.pallas_call(
        body, name="sg_bwd", grid=(T // tm,),
        in_specs=[pl.BlockSpec((tm, SG_W), lambda i: (i, _SU_BLK)), pl.BlockSpec((tm, SG_W), lambda i: (i, _SV_BLK)),
                  pl.BlockSpec((tm, SG_W), lambda i: (i, dblk)), pl.BlockSpec((1, SG_W), lambda i: (0, 0)),
                  wspec, wspec, pl.BlockSpec((SG_CHUNK, SG_W), lambda i: (0, 0))],
        out_specs=[pl.BlockSpec((tm, 2 * SG_W), lambda i: (i, 0)), wspec,
                   pl.BlockSpec((SG_CHUNK, SG_W), lambda i: (0, 0)), pl.BlockSpec((1, SG_W), lambda i: (0, 0))],
        out_shape=[_sds((T, 2 * SG_W), BF16), _sds((4, SG_CHUNK, SG_CHUNK), F32),
                   _sds((SG_CHUNK, SG_W), F32), _sds((1, SG_W), F32)],
        scratch_shapes=[pltpu.VMEM((tm, SG_W), F32)],
        compiler_params=_cp("arbitrary"))(p, p, dmix, g, w, wt, bias)


_ATTN_TQ = 256
_ATTN_TK = 1024
_ATTN_FWD_TK = 512
_SOFTMAX_STRIP = 32
_ONES_ROWS = 16


def _head_mean(v, bmat):
    return jnp.dot(v, bmat, preferred_element_type=F32, precision=lax.Precision.HIGHEST)


def _swap16(y):
    lane = lax.broadcasted_iota(jnp.int32, y.shape, 1)
    return jnp.where(lane % 32 < 16, pltpu.roll(y, y.shape[1] - 16, 1), pltpu.roll(y, 16, 1))


def _rope(y, cos, sin):
    return y * cos + _swap16(y) * sin


def _rope_t(dy, cos, sin):
    return dy * cos + _swap16(dy * sin)


def _dup_rows(t, gidx):
    h = t[gidx * HEAD_DIM:(gidx + 1) * HEAD_DIM]
    return jnp.concatenate([h, h], axis=0)


def _qk_prep(p, gq, gk, cos, sin, bmat, tk, tkf):
    T = p.shape[0]
    nk = T // tk
    sub = tk // tkf
    scale = HEAD_DIM ** -0.5

    def body(q_ref, kv_ref, gq_ref, gk_ref, cos_ref, sin_ref, b_ref, qo_ref, kt_ref, kd_ref, vt_ref, v1_ref):
        cosv, sinv, bm = cos_ref[...], sin_ref[...], b_ref[...]
        for pr in range(ATTN_W // LANES):
            cols = pl.ds(pr * LANES, LANES)
            xq = q_ref[:, cols]
            r = lax.rsqrt(_head_mean(xq * xq, bm) + EPS)
            qo_ref[:, cols] = (_rope((xq * r) * gq_ref[:, cols], cosv, sinv) * scale).astype(BF16)
        xk = kv_ref[:, pl.ds(0, LANES)]
        r = lax.rsqrt(_head_mean(xk * xk, bm) + EPS)
        kt = _rope((xk * r) * gk_ref[...], cosv, sinv).T
        vt = kv_ref[:, pl.ds(LANES, LANES)].T
        for gidx in range(2):
            kdup = _dup_rows(kt, gidx)
            kt_ref[gidx] = kdup.astype(BF16)
            vt_ref[gidx] = _dup_rows(vt, gidx).astype(BF16)
            v1 = jnp.concatenate([vt[gidx * HEAD_DIM:(gidx + 1) * HEAD_DIM],
                                  jnp.ones((_ONES_ROWS, tk), F32)], axis=0).astype(BF16)
            for b in range(sub):
                v1_ref[gidx, b] = v1[:, b * tkf:(b + 1) * tkf]
            kd_ref[gidx] = kdup.T.astype(BF16)

    tspec = pl.BlockSpec((2, None, LANES, tk), lambda i: (0, i, 0, 0))
    dspec = pl.BlockSpec((2, tk, LANES), lambda i: (0, i, 0))
    tab = pl.BlockSpec((tk, LANES), lambda i: (i, 0))
    return pl.pallas_call(
        body, name="qk_prep", grid=(nk,),
        in_specs=[pl.BlockSpec((tk, ATTN_W), lambda i: (i, 0)), pl.BlockSpec((tk, 2 * KV_W), lambda i: (i, ATTN_W // (2 * KV_W))),
                  pl.BlockSpec((1, ATTN_W), lambda i: (0, 0)), pl.BlockSpec((1, KV_W), lambda i: (0, 0)),
                  tab, tab, pl.BlockSpec((LANES, LANES), lambda i: (0, 0))],
        out_specs=[pl.BlockSpec((tk, ATTN_W), lambda i: (i, 0)), tspec, dspec, tspec,
                   pl.BlockSpec((2, sub, HEAD_DIM + _ONES_ROWS, tkf), lambda i: (0, i, 0, 0))],
        out_shape=[_sds((T, ATTN_W), BF16), _sds((2, nk, LANES, tk), BF16), _sds((2, T, LANES), BF16),
                   _sds((2, nk, LANES, tk), BF16), _sds((2, nk * sub, HEAD_DIM + _ONES_ROWS, tkf), BF16)],
        compiler_params=_cp("parallel"))(p, p, gq, gk, cos, sin, bmat)


def _stack_heads(t):
    lo = lax.broadcasted_iota(jnp.int32, (t.shape[0], LANES), 1) < HEAD_DIM
    parts = []
    for pr in range(2):
        tp = t[:, pr * LANES:(pr + 1) * LANES]
        zero = jnp.zeros_like(tp)
        parts += [jnp.where(lo, tp, zero), jnp.where(lo, zero, tp)]
    return jnp.concatenate(parts, axis=0)


def _rows8_reduce(s, op):
    parts = [s[r:r + 8] for r in range(0, s.shape[0], 8)]
    while len(parts) > 1:
        parts = [op(parts[k], parts[k + 1]) for k in range(0, len(parts) - 1, 2)] + (
            [parts[-1]] if len(parts) % 2 else [])
    return parts[0]


def _attn_fwd(q, kd, v1, tq):
    T = q.shape[0]
    nk, tk = v1.shape[1], v1.shape[3]
    vrows = v1.shape[2]
    nq = T // tq
    sq = 4 * tq
    strip = _SOFTMAX_STRIP
    depth = 4
    assert nk % depth == 0

    def body(q_ref, kd_ref, v1_ref, o_ref, lse_ref, qst_ref, s0_ref, s1_ref, s2_ref, s3_ref, pa_ref, pb_ref,
             m_ref, acc_ref):
        s_refs = (s0_ref, s1_ref, s2_ref, s3_ref)
        p_refs = (pa_ref, pb_ref)
        qst_ref[...] = _stack_heads(q_ref[...]).astype(F32).T.astype(BF16)
        m_ref[...] = jnp.full((1, sq), -jnp.inf, F32)
        acc_ref[...] = jnp.zeros((vrows, sq), F32)

        def scores(j):
            return _dot(kd_ref[pl.ds(pl.multiple_of(j * tk, tk), tk), :], qst_ref[...])

        def block_max(s_ref):
            m8 = None
            for c in range(tk // strip):
                part = _rows8_reduce(s_ref[pl.ds(c * strip, strip), :], jnp.maximum)
                m8 = part if m8 is None else jnp.maximum(m8, part)
            return m8

        def exp_pass(s_ref, p_ref, m8):
            m_old = m_ref[...]
            m_new = jnp.maximum(m_old, jnp.max(m8, axis=0, keepdims=True))
            m_ref[...] = m_new
            for c in range(tk // strip):
                rows = pl.ds(c * strip, strip)
                p_ref[rows, :] = jnp.exp(s_ref[rows, :] - m_new).astype(BF16)
            return jnp.exp(m_old - m_new)

        def apply(p_ref, alpha, j):
            acc_ref[...] = alpha * acc_ref[...] + _dot(v1_ref[j], p_ref[...])

        s_refs[0][...] = scores(0)
        s_refs[1][...] = scores(1)

        def trip(t, max_cur):
            for u in range(depth):
                j = depth * t + u
                s_refs[(u + 2) % depth][...] = scores(jnp.minimum(j + 2, nk - 1))
                alpha = exp_pass(s_refs[u], p_refs[u % 2], max_cur)
                max_cur = block_max(s_refs[(u + 1) % depth])
                apply(p_refs[u % 2], alpha, j)
            return max_cur

        lax.fori_loop(0, nk // depth, trip, block_max(s_refs[0]))
        l = acc_ref[pl.ds(HEAD_DIM, 1), :]
        on = acc_ref[pl.ds(0, HEAD_DIM), :] / l
        pairs = []
        for pr in range(2):
            two = jnp.concatenate([on[:, (2 * pr) * tq:(2 * pr + 1) * tq], on[:, (2 * pr + 1) * tq:(2 * pr + 2) * tq]],
                                  axis=0)
            pairs.append(two.T)
        o_ref[...] = jnp.concatenate(pairs, axis=1).astype(BF16)
        lse_ref[...] = jnp.broadcast_to(m_ref[...] + jnp.log(l), (LANES, sq)).T

    row = pltpu.VMEM((1, sq), F32)
    return pl.pallas_call(
        body, name="attn_fwd", grid=(2, nq),
        in_specs=[pl.BlockSpec((tq, 2 * LANES), lambda g, i: (i, g)),
                  pl.BlockSpec((None, T, LANES), lambda g, i: (g, 0, 0)),
                  pl.BlockSpec((None, nk, vrows, tk), lambda g, i: (g, 0, 0, 0))],
        out_specs=[pl.BlockSpec((tq, 2 * LANES), lambda g, i: (i, g)),
                   pl.BlockSpec((None, None, sq, LANES), lambda g, i: (g, i, 0, 0))],
        out_shape=[_sds((T, ATTN_W), BF16), _sds((2, nq, sq, LANES), F32)],
        scratch_shapes=[pltpu.VMEM((LANES, sq), BF16)] + [pltpu.VMEM((tk, sq), F32)] * depth
        + [pltpu.VMEM((tk, sq), BF16), pltpu.VMEM((tk, sq), BF16), row, pltpu.VMEM((vrows, sq), F32)],
        compiler_params=_cp("parallel", "parallel"))(q, kd, v1)


def _attn_bwd(q, o, dmix, lse, kt, kd, vt, tq):
    T = q.shape[0]
    nk, tk = kt.shape[1], kt.shape[3]
    nq = T // tq
    sq = 4 * tq
    rep = tk // LANES

    def body(q_ref, o_ref, do_ref, lse_ref, kt_ref, kd_ref, vt_ref, dq_ref, dkt_ref, dvt_ref):
        i = pl.program_id(1)
        qs = _stack_heads(q_ref[...])
        dof = _stack_heads(do_ref[...])
        dos = dof.astype(BF16)
        qst = qs.astype(F32).T.astype(BF16)
        dost = dof.T.astype(BF16)
        o_pair = o_ref[...].astype(F32)
        os_ = jnp.concatenate([o_pair[:, 0:LANES], o_pair[:, 0:LANES], o_pair[:, LANES:], o_pair[:, LANES:]], axis=0)
        delta = jnp.sum(dof * os_, axis=-1, keepdims=True)
        lse_t = jnp.concatenate([lse_ref[...]] * rep, axis=1)

        @pl.when(i == 0)
        def _():
            dkt_ref[...] = jnp.zeros_like(dkt_ref)
            dvt_ref[...] = jnp.zeros_like(dvt_ref)

        def step(j, dq):
            kdb = kd_ref[pl.ds(pl.multiple_of(j * tk, tk), tk), :]
            pexp = jnp.exp(_dot(qs, kt_ref[j]) - lse_t)
            ds = pexp * (_dot(dos, vt_ref[j]) - delta)
            pb = pexp.astype(BF16)
            dsb = ds.astype(BF16)
            dvt_ref[j] += _dot(dost, pb)
            dkt_ref[j] += _dot(qst, dsb)
            return dq + _dot(dsb, kdb)

        dq = lax.fori_loop(0, nk, step, jnp.zeros((sq, LANES), F32))
        lo = lax.broadcasted_iota(jnp.int32, (tq, LANES), 1) < HEAD_DIM
        dq_ref[...] = jnp.concatenate([jnp.where(lo, dq[0:tq], dq[tq:2 * tq]),
                                       jnp.where(lo, dq[2 * tq:3 * tq], dq[3 * tq:4 * tq])], axis=1)

    tspec = pl.BlockSpec((None, nk, LANES, tk), lambda g, i: (g, 0, 0, 0))
    qspec = pl.BlockSpec((tq, 2 * LANES), lambda g, i: (i, g))
    return pl.pallas_call(
        body, name="attn_bwd", grid=(2, nq),
        in_specs=[qspec, qspec, qspec, pl.BlockSpec((None, None, sq, LANES), lambda g, i: (g, i, 0, 0)),
                  tspec, pl.BlockSpec((None, T, LANES), lambda g, i: (g, 0, 0)), tspec],
        out_specs=[qspec, tspec, tspec],
        out_shape=[_sds((T, ATTN_W), F32), _sds((2, nk, LANES, tk), F32), _sds((2, nk, LANES, tk), F32)],
        compiler_params=_cp("parallel", "arbitrary"))(q, o, dmix, lse, kt, kd, vt)


def _fold_t(t_ref):
    rows = []
    for gidx in range(2):
        t = t_ref[gidx]
        rows.append(t[0:HEAD_DIM] + t[HEAD_DIM:2 * HEAD_DIM])
    return jnp.concatenate(rows, axis=0).T


def _qk_bwd(p, dq, dkt, dvt, gq, gk, cos, sin, bmat):
    T = p.shape[0]
    nk, tk = dkt.shape[1], dkt.shape[3]
    scale = HEAD_DIM ** -0.5

    def norm_bwd(x, dy, gain, bm):
        r = lax.rsqrt(_head_mean(x * x, bm) + EPS)
        xh = x * r
        dxh = dy * gain
        return r * (dxh - xh * _head_mean(dxh * xh, bm)), jnp.sum(dy * xh, axis=0, keepdims=True)

    def body(q_ref, kv_ref, dq_ref, dkt_ref, dvt_ref, gq_ref, gk_ref, cos_ref, sin_ref, b_ref, o_ref, dgq_ref, dgk_ref):
        i = pl.program_id(0)
        cosv, sinv, bm = cos_ref[...], sin_ref[...], b_ref[...]
        dgq = []
        for pr in range(ATTN_W // LANES):
            cols = pl.ds(pr * LANES, LANES)
            dy = _rope_t(dq_ref[:, cols] * scale, cosv, sinv)
            dx, dg = norm_bwd(q_ref[:, cols], dy, gq_ref[:, cols], bm)
            o_ref[:, cols] = dx.astype(BF16)
            dgq.append(dg)
        dy = _rope_t(_fold_t(dkt_ref), cosv, sinv)
        dx, dgk = norm_bwd(kv_ref[:, pl.ds(0, LANES)], dy, gk_ref[...], bm)
        o_ref[:, pl.ds(ATTN_W, LANES)] = dx.astype(BF16)
        o_ref[:, pl.ds(ATTN_W + LANES, LANES)] = _fold_t(dvt_ref).astype(BF16)
        _accum(dgq_ref, i, jnp.concatenate(dgq, axis=1))
        _accum(dgk_ref, i, dgk)

    tspec = pl.BlockSpec((2, None, LANES, tk), lambda i: (0, i, 0, 0))
    tab = pl.BlockSpec((tk, LANES), lambda i: (i, 0))
    return pl.pallas_call(
        body, name="qk_bwd", grid=(nk,),
        in_specs=[pl.BlockSpec((tk, ATTN_W), lambda i: (i, 0)), pl.BlockSpec((tk, 2 * KV_W), lambda i: (i, ATTN_W // (2 * KV_W))),
                  pl.BlockSpec((tk, ATTN_W), lambda i: (i, 0)), tspec, tspec,
                  pl.BlockSpec((1, ATTN_W), lambda i: (0, 0)), pl.BlockSpec((1, KV_W), lambda i: (0, 0)),
                  tab, tab, pl.BlockSpec((LANES, LANES), lambda i: (0, 0))],
        out_specs=[pl.BlockSpec((tk, ATTN_W + 2 * KV_W), lambda i: (i, 0)),
                   pl.BlockSpec((1, ATTN_W), lambda i: (0, 0)), pl.BlockSpec((1, KV_W), lambda i: (0, 0))],
        out_shape=[_sds((T, ATTN_W + 2 * KV_W), BF16), _sds((1, ATTN_W), F32), _sds((1, KV_W), F32)],
        compiler_params=_cp("arbitrary"))(p, p, dq, dkt, dvt, gq, gk, cos, sin, bmat)


def _loss_head(y, target):
    T, Dm = y.shape
    tm = min(T, _ROW_TILE)

    def body(y_ref, t_ref, dy_ref, l_ref):
        i = pl.program_id(0)
        err = y_ref[...] - t_ref[...]
        dy_ref[...] = err * (1.0 / Dm)
        part = jnp.sum(jnp.sum(err * err, axis=-1, keepdims=True), axis=0, keepdims=True) * (0.5 / Dm)
        _accum(l_ref, i, jnp.broadcast_to(part, (8, LANES)))

    row = pl.BlockSpec((tm, Dm), lambda i: (i, 0))
    return pl.pallas_call(
        body, name="loss_head", grid=(T // tm,), in_specs=[row, row],
        out_specs=[row, pl.BlockSpec((8, LANES), lambda i: (0, 0))],
        out_shape=[_sds((T, Dm), F32), _sds((8, LANES), F32)], compiler_params=_cp("arbitrary"))(y, target)


def _adamw(w, g, m, v, name):
    R, C = w.shape
    tr = R
    for cand in (512, 256, 128, 64, 32, 16, 8):
        if R % cand == 0:
            tr = cand
            break
    c1 = 1.0 - ADAM_B1 ** ADAM_STEP
    c2 = 1.0 - ADAM_B2 ** ADAM_STEP

    def body(w_ref, g_ref, m_ref, v_ref, d_ref, mo_ref, vo_ref):
        gv = g_ref[...]
        mn = ADAM_B1 * m_ref[...] + (1.0 - ADAM_B1) * gv
        vn = ADAM_B2 * v_ref[...] + (1.0 - ADAM_B2) * (gv * gv)
        d_ref[...] = -ADAM_LR * ((mn / c1) / (jnp.sqrt(vn / c2) + ADAM_EPS) + ADAM_WD * w_ref[...])
        mo_ref[...] = mn
        vo_ref[...] = vn

    blk = pl.BlockSpec((tr, C), lambda i: (i, 0))
    return pl.pallas_call(
        body, name=name, grid=(R // tr,), in_specs=[blk] * 4, out_specs=[blk] * 3,
        out_shape=[_sds((R, C), F32)] * 3, compiler_params=_cp("parallel"))(w, g, m, v)


def _cast_bf16(w, name):
    R, C = w.shape
    tr = 512 if R % 512 == 0 else 256

    def body(w_ref, o_ref):
        o_ref[...] = w_ref[...].astype(BF16)

    blk = pl.BlockSpec((tr, C), lambda i: (i, 0))
    return pl.pallas_call(body, name=name, grid=(R // tr,), in_specs=[blk], out_specs=blk,
                          out_shape=_sds((R, C), BF16), compiler_params=_cp("parallel"))(w)


def _position():
    x, y, c = lax.axis_index("x"), lax.axis_index("y"), lax.axis_index("c")
    return x, y, c


def _other_chips(x, y):
    return [(1 - x, y), (x, 1 - y), (1 - x, 1 - y)]


_HBM = pl.BlockSpec(memory_space=pltpu.HBM)
_SEM = pl.BlockSpec(memory_space=pltpu.SEMAPHORE)
_EFFECT = pltpu.SideEffectType.DATAFLOW_SIDE_EFFECTING


def _chip_copies(srcs, lands, send_sems, recv_sems, per_chip, arriving):
    x, y, c = _position()
    me = 2 * x + y
    copies = []
    for t, (src, land) in enumerate(zip(srcs, lands)):
        for k, (px, py) in enumerate(_other_chips(x, y)):
            peer = 2 * px + py
            copies.append(pltpu.make_async_remote_copy(
                src_ref=src.at[peer] if per_chip else src, dst_ref=land.at[peer if arriving else me],
                send_sem=send_sems[3 * t + k], recv_sem=recv_sems[3 * t + k],
                device_id=(px, py, c), device_id_type=MESH))
    return copies


def _chips_start(srcs, per_chip, name):
    n = len(srcs)
    slab = [s.shape[1:] if per_chip else s.shape for s in srcs]
    lands = [lax.empty((N_CHIPS,) + sh, s.dtype) for sh, s in zip(slab, srcs)]

    ns = 3 * n

    def body(*refs):
        ins = refs[:2 * n]
        send_sems, recv_sems = refs[2 * n:2 * n + ns], refs[2 * n + ns:2 * n + 2 * ns]
        token = refs[-1]
        for cp in _chip_copies(ins[:n], ins[n:], send_sems, recv_sems, per_chip, False):
            cp.start()
        token[...] = jnp.zeros_like(token)

    args = [pltpu.with_memory_space_constraint(a, pltpu.HBM) for a in list(srcs) + lands]
    outs = pl.pallas_call(
        body, name=name,
        out_shape=[pltpu.SemaphoreType.DMA(())] * (2 * ns)
        + [pltpu.HBM(a.shape, a.dtype) for a in args] + [_sds((8, LANES), F32)],
        in_specs=[_HBM] * (2 * n),
        out_specs=[_SEM] * (2 * ns) + [_HBM] * (2 * n) + [pl.BlockSpec(memory_space=pltpu.VMEM)],
        input_output_aliases={i: 2 * ns + i for i in range(2 * n)},
        compiler_params=pltpu.CompilerParams(has_side_effects=_EFFECT))(*args)
    sems, rest = outs[:2 * ns], outs[2 * ns:]
    return sems[:ns], sems[ns:], rest[:n], rest[n:2 * n], rest[-1]


def _chips_wait(handle, after, per_chip, name):
    send_sems, recv_sems, srcs, lands, _ = handle
    n = len(srcs)
    ns = 3 * n

    def body(*refs):
        ins = refs[:2 * n]
        s_sems, r_sems = refs[2 * n:2 * n + ns], refs[2 * n + ns:2 * n + 2 * ns]
        for cp in _chip_copies(ins[:n], ins[n:], s_sems, r_sems, per_chip, False):
            cp.wait_send()
        for cp in _chip_copies(ins[:n], ins[n:], s_sems, r_sems, per_chip, True):
            cp.wait_recv()

    outs = pl.pallas_call(
        body, name=name,
        out_shape=[pltpu.HBM(a.shape, a.dtype) for a in list(srcs) + list(lands)],
        in_specs=[_HBM] * (2 * n) + [_SEM] * (2 * ns) + [ANY],
        out_specs=[_HBM] * (2 * n),
        input_output_aliases={i: i for i in range(2 * n)},
        compiler_params=pltpu.CompilerParams(has_side_effects=_EFFECT))(*srcs, *lands, *send_sems, *recv_sems, after)
    return outs[:n], outs[n:]


def _sum_chips_own(land, own, name):
    S, R, C = land.shape
    tr = R
    for cand in (256, 128, 64, 32, 16, 8):
        if R % cand == 0:
            tr = cand
            break

    def body(l_ref, o_ref, out_ref):
        x, y, _ = _position()
        me = 2 * x + y
        mine = o_ref[me] if own.ndim == 3 else o_ref[...]
        acc = None
        for k in range(S):
            part = jnp.where(me == k, mine, l_ref[k])
            acc = part if acc is None else acc + part
        out_ref[...] = acc

    blk = pl.BlockSpec((S, tr, C), lambda i: (0, i, 0))
    row = pl.BlockSpec((tr, C), lambda i: (i, 0))
    return pl.pallas_call(
        body, name=name, grid=(R // tr,), in_specs=[blk, blk if own.ndim == 3 else row], out_specs=row,
        out_shape=_sds((R, C), F32), compiler_params=_cp("parallel"))(land, own)


def _add_pair(a, b, name):
    R, C = a.shape

    def body(a_ref, b_ref, o_ref):
        o_ref[...] = a_ref[...] + b_ref[...]

    blk = pl.BlockSpec((R, C), lambda: (0, 0))
    return pl.pallas_call(body, name=name, in_specs=[blk, blk], out_specs=blk, out_shape=_sds((R, C), F32))(a, b)


def _exchange_sibling(arrays):
    n = len(arrays)

    def body(*refs):
        ins, outs = refs[:n], refs[n:2 * n]
        send_sems, recv_sems = refs[2 * n:]
        x, y, c = _position()
        sends = []
        for t in range(n):
            cp = pltpu.make_async_remote_copy(src_ref=ins[t], dst_ref=outs[t], send_sem=send_sems.at[t],
                                              recv_sem=recv_sems.at[t], device_id=(x, y, 1 - c), device_id_type=MESH)
            cp.start()
            sends.append(cp)
        for cp in sends:
            cp.wait()

    return pl.pallas_call(
        body, name="grads_to_sibling",
        in_specs=[ANY] * n, out_specs=[ANY] * n, out_shape=[_sds(a.shape, a.dtype) for a in arrays],
        scratch_shapes=[pltpu.SemaphoreType.DMA((n,)), pltpu.SemaphoreType.DMA((n,))],
        compiler_params=pltpu.CompilerParams(has_side_effects=True))(*arrays)


def _adamw_sum(w, ga, gb, m, v, name):
    R, C = w.shape
    tr = next(t for t in (512, 256, 128, 64) if R % t == 0 and t * C * 4 <= (1 << 20))
    c1 = 1.0 - ADAM_B1 ** ADAM_STEP
    c2 = 1.0 - ADAM_B2 ** ADAM_STEP

    def body(w_ref, ga_ref, gb_ref, m_ref, v_ref, g_ref, d_ref, mo_ref, vo_ref):
        gv = ga_ref[...] + gb_ref[...]
        mn = ADAM_B1 * m_ref[...] + (1.0 - ADAM_B1) * gv
        vn = ADAM_B2 * v_ref[...] + (1.0 - ADAM_B2) * (gv * gv)
        g_ref[...] = gv
        d_ref[...] = -ADAM_LR * ((mn / c1) / (jnp.sqrt(vn / c2) + ADAM_EPS) + ADAM_WD * w_ref[...])
        mo_ref[...] = mn
        vo_ref[...] = vn

    blk = pl.BlockSpec((tr, C), lambda i: (i, 0))
    return pl.pallas_call(
        body, name=name, grid=(R // tr,), in_specs=[blk] * 5, out_specs=[blk] * 4,
        out_shape=[_sds((R, C), F32)] * 4, compiler_params=_cp("parallel"))(w, ga, gb, m, v)


def _rope_tables(T):
    pos = jnp.arange(T)
    row = (pos // GRID_W).astype(F32)
    col = (pos % GRID_W).astype(F32)
    inv = 1.0 / (ROPE_THETA ** (jnp.arange(AXIS_DIM // 2, dtype=F32) * 2.0 / AXIS_DIM))
    ar, ac = row[:, None] * inv[None, :], col[:, None] * inv[None, :]
    cos = jnp.concatenate([jnp.cos(ar), jnp.cos(ar), jnp.cos(ac), jnp.cos(ac)], axis=-1)
    sin = jnp.concatenate([-jnp.sin(ar), jnp.sin(ar), -jnp.sin(ac), jnp.sin(ac)], axis=-1)
    return jnp.tile(cos, (1, LANES // HEAD_DIM)), jnp.tile(sin, (1, LANES // HEAD_DIM))


def _head_mean_matrix():
    h = jnp.arange(LANES) // HEAD_DIM
    return jnp.where(h[:, None] == h[None, :], 1.0 / HEAD_DIM, 0.0).astype(F32)


def _pack(arrays):
    flat = jnp.concatenate([a.reshape(-1) for a in arrays])
    rows = -(-flat.shape[0] // LANES)
    rows = -(-rows // 256) * 256
    return jnp.pad(flat, (0, rows * LANES - flat.shape[0])).reshape(rows, LANES)


def _unpack(packed, like):
    flat = packed.reshape(-1)
    out, off = [], 0
    for a in like:
        out.append(flat[off:off + a.size].reshape(a.shape))
        off += a.size
    return out


def _layer_fwd(x, lw, consts, ffn_weights):
    cos, sin, bmat = consts
    T = x.shape[0]
    tk = min(T, _ATTN_TK)
    tq = min(T, _ATTN_TQ)
    h = _norm_fwd(x, lw["norm1_g"])
    p = _mm_nn(h, lw["w_in"], out_dtype=F32, name="mm_p")
    qn, kt, kd, vt, v1 = _qk_prep(p, lw["gq"], lw["gk"], cos, sin, bmat, tk, min(tk, _ATTN_FWD_TK))
    o, lse = _attn_fwd(qn, kd, v1, tq)
    go = _sg_fwd(p, lw["sg_norm_g"], lw["sg_w"], lw["sg_bias"])
    co = _convmix_fwd(p, lw["conv_w"])
    mix = jnp.concatenate([o, go, co], axis=1)
    x_mid = _mm_nn(mix, lw["w_out"], out_dtype=F32, name="mm_out", res=x)
    lw.update(ffn_weights(x_mid))
    h2 = _norm_fwd(x_mid, lw["norm2_g"])
    up = _mm_nn(h2, lw["w_up"], out_dtype=F32, name="mm_up", tm=_FFN_ROW_TILE)
    act = _ffn_act_fwd(up, lw["ffn_conv_w"])
    x_out = _mm_nn(act, lw["w_down"], out_dtype=F32, name="mm_down", res=x_mid)
    saved = dict(x=x, h=h, p=p, qn=qn, kt=kt, kd=kd, vt=vt, o=o, lse=lse, mix=mix, x_mid=x_mid, h2=h2, up=up, act=act)
    return x_out, saved


def _layer_bwd(dx, s, lw, consts, send):
    cos, sin, bmat = consts
    T = dx.shape[0]
    tq = min(T, _ATTN_TQ)
    g = {}
    d_act = _mm_nt(dx, lw["w_down"], name="mm_dact")
    g_down = _mm_tn(s["act"], dx, tk=D_FF // 2, tn=D_MODEL, name="mm_dwdown", tm=_WGRAD_ROWS // 2)
    tok = send("w_down", g_down.reshape(N_CHIPS, D_FF // N_CHIPS, D_MODEL))
    d_up, d_cw = _ffn_act_bwd(s["up"], d_act, lw["ffn_conv_w"] + tok)
    g["ffn_conv_w"] = d_cw.transpose(1, 0, 2).reshape(3, 2 * D_FF)
    tok = send("w_up", _mm_tn(s["h2"], d_up, tk=512, tn=D_FF // 2, name="mm_dwup", shards=N_CHIPS, tm=_WGRAD_ROWS))
    dx2, g["norm2_g"] = _mm_nt(d_up, lw["w_up"], name="mm_dh2", norm=(s["x_mid"], dx, lw["norm2_g"] + tok))
    d_mix = _mm_nt(dx2, lw["w_out"], name="mm_dmix")
    g_out = _mm_tn(s["mix"], dx2, tk=512, tn=D_MODEL, name="mm_dwout", tm=_WGRAD_ROWS // 2)
    tok = send("w_out", g_out.reshape(N_CHIPS, D_MODEL // N_CHIPS, D_MODEL))
    dp_c, g["conv_w"] = _convmix_bwd(s["p"], d_mix, lw["conv_w"] + tok)
    dp_b, g["sg_w"], d_bias, g["sg_norm_g"] = _sg_bwd(s["p"], d_mix, lw["sg_norm_g"], lw["sg_w"], lw["sg_wt"], lw["sg_bias"])
    g["sg_b"] = d_bias.reshape(SG_CHUNK, SG_W // HEAD_DIM, HEAD_DIM).sum(axis=-1).T
    dq, dkt, dvt = _attn_bwd(s["qn"], s["o"], d_mix, s["lse"], s["kt"], s["kd"], s["vt"], tq)
    dp_a, d_gq, d_gk = _qk_bwd(s["p"], dq, dkt, dvt, lw["gq"], lw["gk"], cos, sin, bmat)
    g["q_norm_g"] = d_gq.reshape(ATTN_W // HEAD_DIM, HEAD_DIM).sum(axis=0)
    g["k_norm_g"] = d_gk.reshape(KV_W // HEAD_DIM, HEAD_DIM).sum(axis=0)
    dp = jnp.concatenate([dp_a, dp_b, dp_c], axis=1)
    tok = send("w_in", _mm_tn(s["h"], dp, tk=D_MODEL, tn=512, name="mm_dwin", shards=N_CHIPS, tm=_WGRAD_ROWS))
    dx_in, g["norm1_g"] = _mm_nt(dp, lw["w_in"], name="mm_dh", norm=(s["x"], dx2, lw["norm1_g"] + tok))
    return dx_in, g


def _layer_weights(l, full, small):
    sg_w = small["sg_w"][l]
    sg_b = small["sg_b"][l]
    return dict(
        norm1_g=small["norm1_g"][l][None, :], norm2_g=small["norm2_g"][l][None, :],
        gq=jnp.tile(small["q_norm_g"][l], ATTN_W // HEAD_DIM)[None, :],
        gk=jnp.tile(small["k_norm_g"][l], KV_W // HEAD_DIM)[None, :],
        sg_norm_g=small["sg_norm_g"][l][None, :],
        sg_w=sg_w.astype(BF16), sg_wt=sg_w.transpose(0, 2, 1).astype(BF16),
        sg_bias=jnp.repeat(sg_b.T, HEAD_DIM, axis=1),
        **full)


_BIG = ("w_in", "w_out", "ffn_w_up", "ffn_w_down")
_SMALL_REPL = ("norm1_g", "q_norm_g", "k_norm_g", "sg_norm_g", "sg_w", "sg_b", "norm2_g")
_SMALL_SHARD = ("conv_w", "ffn_conv_w")
_ORDER = ("norm1_g", "w_in", "q_norm_g", "k_norm_g", "sg_norm_g", "sg_w", "sg_b", "conv_w", "w_out", "norm2_g",
          "ffn_w_up", "ffn_conv_w", "ffn_w_down")


def kernel(x, norm1_g, w_in, q_norm_g, k_norm_g, sg_norm_g, sg_w, sg_b, conv_w, w_out, norm2_g, ffn_w_up, ffn_conv_w, ffn_w_down, loss_target, m_norm1_g, m_w_in, m_q_norm_g, m_k_norm_g, m_sg_norm_g, m_sg_w, m_sg_b, m_conv_w, m_w_out, m_norm2_g, m_ffn_w_up, m_ffn_conv_w, m_ffn_w_down, v_norm1_g, v_w_in, v_q_norm_g, v_k_norm_g, v_sg_norm_g, v_sg_w, v_sg_b, v_conv_w, v_w_out, v_norm2_g, v_ffn_w_up, v_ffn_conv_w, v_ffn_w_down):
    w = dict(norm1_g=norm1_g, w_in=w_in, q_norm_g=q_norm_g, k_norm_g=k_norm_g, sg_norm_g=sg_norm_g, sg_w=sg_w,
             sg_b=sg_b, conv_w=conv_w, w_out=w_out, norm2_g=norm2_g, ffn_w_up=ffn_w_up, ffn_conv_w=ffn_conv_w,
             ffn_w_down=ffn_w_down)
    mom = dict(norm1_g=m_norm1_g, w_in=m_w_in, q_norm_g=m_q_norm_g, k_norm_g=m_k_norm_g, sg_norm_g=m_sg_norm_g,
               sg_w=m_sg_w, sg_b=m_sg_b, conv_w=m_conv_w, w_out=m_w_out, norm2_g=m_norm2_g, ffn_w_up=m_ffn_w_up,
               ffn_conv_w=m_ffn_conv_w, ffn_w_down=m_ffn_w_down)
    var = dict(norm1_g=v_norm1_g, w_in=v_w_in, q_norm_g=v_q_norm_g, k_norm_g=v_k_norm_g, sg_norm_g=v_sg_norm_g,
               sg_w=v_sg_w, sg_b=v_sg_b, conv_w=v_conv_w, w_out=v_w_out, norm2_g=v_norm2_g, ffn_w_up=v_ffn_w_up,
               ffn_conv_w=v_ffn_conv_w, ffn_w_down=v_ffn_w_down)
    L = DEPTH
    T = x.shape[1]
    xs = x.reshape(T, D_MODEL)
    target = loss_target.reshape(T, D_MODEL)

    chip = 2 * lax.axis_index("x") + lax.axis_index("y")

    shards = [_cast_bf16(w[n].reshape(-1, w[n].shape[-1]), "cast_" + n).reshape(w[n].shape) for n in _BIG]
    shards += [conv_w, ffn_conv_w]
    w_in_s, w_out_s, w_up_s, w_down_s, conv_s, fconv_s = shards
    gathers = []
    for l in range(L):
        gathers.append((_chips_start([w_in_s[l], w_out_s[l], conv_s[l]], False, "gather_start_%da" % l),
                        _chips_start([w_up_s[l], w_down_s[l], fconv_s[l]], False, "gather_start_%db" % l)))
    start_token = sum(h[4][0, 0] for pair in gathers for h in pair)
    consts = _rope_tables(T) + (_head_mean_matrix(),)

    def gathered(handle, after, name):
        own, lands = _chips_wait(handle, after, False, name)
        return [lax.dynamic_update_slice(ld, o[None], (chip,) + (jnp.int32(0),) * o.ndim) for ld, o in zip(lands, own)]

    saved, lws = [], []
    act_x = xs
    for l in range(L):
        g_in, g_out, g_conv = gathered(gathers[l][0], act_x if l else gathers[-1][1][4], "gather_wait_%da" % l)
        lw = _layer_weights(l, dict(w_in=g_in, w_out=g_out.reshape(D_MODEL, D_MODEL),
                                    conv_w=g_conv.transpose(1, 0, 2).reshape(3, CONV_W)), w)
        if l == 0:
            lw["norm1_g"] = lw["norm1_g"] + start_token

        def ffn_weights(after, l=l):
            g_up, g_down, g_fconv = gathered(gathers[l][1], after, "gather_wait_%db" % l)
            return dict(w_up=g_up, w_down=g_down.reshape(D_FF, D_MODEL),
                        ffn_conv_w=g_fconv.transpose(1, 0, 2).reshape(3, 2 * D_FF))

        act_x, s = _layer_fwd(act_x, lw, consts, ffn_weights)
        saved.append(s)
        lws.append(lw)
    dx, loss_blk = _loss_head(act_x, target)
    loss = lax.psum(loss_blk[0, 0], ("x", "y", "c"))

    grads = [None] * L
    partial = [None] * L

    def collect(pending, after, l):
        sums = {}
        for name, handle in pending:
            own, lands = _chips_wait(handle, after, True, "grad_wait_%d_%s" % (l, name))
            sums[name] = _sum_chips_own(lands[0], own[0], "sum_chips_" + name)
        return sums

    pending_prev = None
    for l in reversed(range(L)):
        pending = []

        def send(name, g4, l=l, pending=pending):
            handle = _chips_start([g4], True, "grad_start_%d_%s" % (l, name))
            pending.append((name, handle))
            return handle[4][0, 0]

        dx, g = _layer_bwd(dx, saved[l], lws[l], consts, send)
        grads[l] = g
        if pending_prev is not None:
            partial[l + 1] = collect(pending_prev, dx, l + 1)
        pending_prev = pending
    grad_x = dx.reshape(x.shape)

    small_names = _SMALL_REPL + _SMALL_SHARD
    small_local = [jnp.stack([grads[l][n].reshape(-1) for l in range(L)]) for n in small_names]
    small_handle = _chips_start([_pack(small_local)], False, "small_start")
    partial[0] = collect(pending_prev, small_handle[4], 0)

    short = dict(w_in="w_in", w_out="w_out", ffn_w_up="w_up", ffn_w_down="w_down")
    mine = [jnp.stack([partial[l][short[n]] for l in range(L)]) for n in _BIG]
    small_own, small_lands = _chips_wait(small_handle, mine[-1], False, "small_wait")
    mine.append(_sum_chips_own(small_lands[0], small_own[0], "sum_chips_small"))
    theirs = _exchange_sibling(mine)
    grad = dict(zip(small_names, _unpack(_add_pair(mine[-1], theirs[-1], "add_small"), small_local)))
    for n in _SMALL_REPL:
        grad[n] = grad[n].reshape(w[n].shape)
    for n in _SMALL_SHARD:
        full_w = grad[n].reshape(L, 3, -1)
        width = w[n].shape[-1]
        grad[n] = lax.dynamic_slice_in_dim(full_w, chip * width, width, axis=2)
    delta, new_m, new_v = {}, {}, {}
    for n, ga, gb in zip(_BIG, mine, theirs):
        shp = w[n].shape
        v2 = lambda a: a.reshape(-1, shp[-1])
        gsum, d, mn, vn = _adamw_sum(v2(w[n]), v2(ga), v2(gb), v2(mom[n]), v2(var[n]), "adamw_" + n)
        grad[n], delta[n], new_m[n], new_v[n] = gsum.reshape(shp), d.reshape(shp), mn.reshape(shp), vn.reshape(shp)
    for group, gname in ((_SMALL_REPL, "adamw_small"), (_SMALL_SHARD, "adamw_conv")):
        like = [w[n] for n in group]
        outs = _adamw(_pack([w[n] for n in group]), _pack([grad[n] for n in group]), _pack([mom[n] for n in group]),
                      _pack([var[n] for n in group]), gname)
        for res, dst in zip(outs, (delta, new_m, new_v)):
            for n, a in zip(group, _unpack(res, like)):
                dst[n] = a

    return (loss, grad_x, *[grad[n] for n in _ORDER], *[delta[n] for n in _ORDER],
            *[new_m[n] for n in _ORDER], *[new_v[n] for n in _ORDER])
```

```python
import jax
import jax.numpy as jnp
from jax import lax
from jax.experimental import pallas as pl
from jax.experimental.pallas import tpu as pltpu

F32 = jnp.float32
BF16 = jnp.bfloat16

DEPTH = 4
D_MODEL = 1024
HEAD_DIM = 64
ATTN_W = 512
KV_W = 128
SG_W = 256
CONV_W = 256
SG_CHUNK = 128
D_FF = 2816
PROJ_W = 2048
GRID_W = 64
ROPE_THETA = 10000.0
AXIS_DIM = HEAD_DIM // 2
EPS = 1e-6
N_CHIPS = 4

ADAM_LR = 0.001
ADAM_B1 = 0.9
ADAM_B2 = 0.999
ADAM_EPS = 1e-08
ADAM_WD = 0.01
ADAM_STEP = 10

_ROW_TILE = 512
_FFN_ROW_TILE = 256
_WGRAD_ROWS = 2048
LANES = 128
HALO = 8
VMEM_LIMIT_BYTES = 56 * 1024 * 1024
MESH = pl.DeviceIdType.MESH
ANY = pl.BlockSpec(memory_space=pl.ANY)


def _cp(*sem):
    return pltpu.CompilerParams(dimension_semantics=sem if sem else None,
                                vmem_limit_bytes=VMEM_LIMIT_BYTES)


def _sds(shape, dtype):
    return jax.ShapeDtypeStruct(shape, dtype)


def _dot(a, b):
    return jnp.dot(a, b, preferred_element_type=F32)


def _dot_nt(a, b):
    return lax.dot_general(a, b, (((1,), (1,)), ((), ())), preferred_element_type=F32)


def _dot_tn(a, b):
    return lax.dot_general(a, b, (((0,), (0,)), ((), ())), preferred_element_type=F32)


def _norm_fwd(x, g):
    T, Dm = x.shape
    tm = min(T, _ROW_TILE)

    def body(x_ref, g_ref, o_ref):
        xv = x_ref[...]
        r = lax.rsqrt(jnp.mean(xv * xv, axis=-1, keepdims=True) + EPS)
        o_ref[...] = ((xv * r) * g_ref[...]).astype(BF16)

    return pl.pallas_call(
        body, name="norm_fwd", grid=(T // tm,),
        in_specs=[pl.BlockSpec((tm, Dm), lambda i: (i, 0)), pl.BlockSpec((1, Dm), lambda i: (0, 0))],
        out_specs=pl.BlockSpec((tm, Dm), lambda i: (i, 0)),
        out_shape=_sds((T, Dm), BF16), compiler_params=_cp("parallel"))(x, g)


def _whole(w):
    return pl.BlockSpec(w.shape, lambda *g: (0,) * w.ndim)


def _mm_nn(a, w, *, out_dtype, name, res=None, tm=None, norm_g=None):
    M, K = a.shape
    N = w.shape[-1] if w.ndim == 2 else w.shape[0] * w.shape[2]
    tm = min(M, tm or _ROW_TILE)
    has_res = res is not None
    has_norm = norm_g is not None
    assert not has_norm or w.ndim == 2

    def body(*refs):
        a_ref, w_ref = refs[0], refs[1]
        o_ref = refs[-2] if has_norm else refs[-1]
        av = a_ref[...].astype(BF16)
        parts = [w_ref[...]] if w.ndim == 2 else [w_ref[s] for s in range(w.shape[0])]
        ns = N // len(parts)
        for s, wv in enumerate(parts):
            cols = pl.ds(s * ns, ns)
            acc = _dot(av, wv)
            if has_res:
                acc = acc + refs[2][:, cols]
            o_ref[:, cols] = acc.astype(out_dtype)
        if has_norm:
            r = lax.rsqrt(jnp.mean(acc * acc, axis=-1, keepdims=True) + EPS)
            refs[-1][...] = ((acc * r) * refs[2 + has_res][...]).astype(BF16)

    row = pl.BlockSpec((tm, N), lambda i: (i, 0))
    in_specs = [pl.BlockSpec((tm, K), lambda i: (i, 0)), _whole(w)]
    args = [a, w]
    if has_res:
        in_specs.append(row)
        args.append(res)
    if has_norm:
        in_specs.append(pl.BlockSpec((1, N), lambda i: (0, 0)))
        args.append(norm_g)
    return pl.pallas_call(
        body, name=name, grid=(M // tm,), in_specs=in_specs, out_specs=[row, row] if has_norm else row,
        out_shape=[_sds((M, N), out_dtype), _sds((M, N), BF16)] if has_norm else _sds((M, N), out_dtype),
        compiler_params=_cp("parallel"))(*args)


def _a_spec(a, tm, tn, row_of, col_of):
    if a.ndim == 2:
        return pl.BlockSpec((tm, tn), lambda *g: (row_of(*g), col_of(*g)))
    bph = a.shape[2] // tn
    return pl.BlockSpec((None, tm, tn), lambda *g: (col_of(*g) // bph, row_of(*g), col_of(*g) % bph))


def _a_cols(a):
    return a.shape[1] if a.ndim == 2 else a.shape[0] * a.shape[2]


def _mm_nt(a, w, *, name, tm=None, norm=None):
    M = a.shape[-2]
    Kw = w.shape[-2]
    tm = min(M, tm or _ROW_TILE)

    def product(a_ref, w_ref):
        if w.ndim == 2:
            return _dot_nt(a_ref[...].astype(BF16), w_ref[...])
        S, ns = w.shape[0], w.shape[2]
        acc = None
        for s in range(S):
            if a.ndim == 2:
                piece = a_ref[:, pl.ds(s * ns, ns)]
            else:
                per_half = S // 2
                piece = a_ref[s // per_half, :, pl.ds((s % per_half) * ns, ns)]
            part = _dot_nt(piece.astype(BF16), w_ref[s])
            acc = part if acc is None else acc + part
        return acc

    def body(a_ref, w_ref, o_ref):
        o_ref[...] = product(a_ref, w_ref)

    def body_norm(a_ref, w_ref, x_ref, dr_ref, g_ref, dx_ref, dg_ref):
        dhv = product(a_ref, w_ref)
        xv = x_ref[...]
        r = lax.rsqrt(jnp.mean(xv * xv, axis=-1, keepdims=True) + EPS)
        xh = xv * r
        dxh = dhv * g_ref[...]
        dx_ref[...] = dr_ref[...] + r * (dxh - xh * jnp.mean(dxh * xh, axis=-1, keepdims=True))
        _accum(dg_ref, pl.program_id(0), jnp.sum(dhv * xh, axis=0, keepdims=True))

    a_spec = (pl.BlockSpec((tm, a.shape[1]), lambda i: (i, 0)) if a.ndim == 2
              else pl.BlockSpec((2, tm, a.shape[2]), lambda i: (0, i, 0)))
    row = pl.BlockSpec((tm, Kw), lambda i: (i, 0))
    if norm is None:
        return pl.pallas_call(
            body, name=name, grid=(M // tm,), in_specs=[a_spec, _whole(w)], out_specs=row,
            out_shape=_sds((M, Kw), F32), compiler_params=_cp("parallel"))(a, w)
    vec = pl.BlockSpec((1, Kw), lambda i: (0, 0))
    return pl.pallas_call(
        body_norm, name=name, grid=(M // tm,), in_specs=[a_spec, _whole(w), row, row, vec], out_specs=[row, vec],
        out_shape=[_sds((M, Kw), F32), _sds((1, Kw), F32)], compiler_params=_cp("arbitrary"))(a, w, *norm)


def _mm_tn(a, b, *, tk, tn, name, shards=None, tm=None):
    M, K = a.shape
    N = _a_cols(b)
    tm = min(M, tm or _ROW_TILE)

    def body(a_ref, b_ref, o_ref):
        m = pl.program_id(2)
        part = _dot_tn(a_ref[...].astype(BF16), b_ref[...].astype(BF16))

        @pl.when(m == 0)
        def _():
            o_ref[...] = part

        @pl.when(m > 0)
        def _():
            o_ref[...] += part

    if shards is None:
        out_spec = pl.BlockSpec((tk, tn), lambda k, j, m: (k, j))
        out_shape = _sds((K, N), F32)
    else:
        bps = (N // shards) // tn
        out_spec = pl.BlockSpec((None, tk, tn), lambda k, j, m: (j // bps, k, j % bps))
        out_shape = _sds((shards, K, N // shards), F32)
    return pl.pallas_call(
        body, name=name, grid=(K // tk, N // tn, M // tm),
        in_specs=[pl.BlockSpec((tm, tk), lambda k, j, m: (m, k)),
                  _a_spec(b, tm, tn, lambda k, j, m: m, lambda k, j, m: j)],
        out_specs=out_spec, out_shape=out_shape,
        compiler_params=_cp("parallel", "parallel", "arbitrary"))(a, b)


def _halo_specs(T, tm, cw, ic):
    nb = tm // HALO
    last = T // HALO - 1

    def mk(rows, row_of):
        return pl.BlockSpec((rows, cw), lambda *g: (row_of(ic(*g)[0]), ic(*g)[1]))

    return [mk(HALO, lambda i: jnp.maximum(i * nb - 1, 0)), mk(tm, lambda i: i),
            mk(HALO, lambda i: jnp.minimum((i + 1) * nb, last))]


def _ext(prev_ref, cur_ref, next_ref, i, n):
    p = jnp.where(i > 0, prev_ref[...].astype(F32), 0.0)
    nx = jnp.where(i < n - 1, next_ref[...].astype(F32), 0.0)
    return jnp.concatenate([p, cur_ref[...].astype(F32), nx], axis=0)


def _dn(e):
    return pltpu.roll(e, 1, 0)


def _up(e):
    return pltpu.roll(e, e.shape[0] - 1, 0)


def _mid(e):
    return e[HALO:e.shape[0] - HALO]


def _taps(e):
    return _dn(e), e, _up(e)


def _conv3(taps, w):
    return taps[0] * w[0:1] + taps[1] * w[1:2] + taps[2] * w[2:3]


def _conv3_t(e, w):
    return _up(e) * w[0:1] + e * w[1:2] + _dn(e) * w[2:3]


def _conv3_wgrad(d, taps):
    return jnp.concatenate([jnp.sum(_mid(d * tap), axis=0, keepdims=True) for tap in taps], axis=0)


def _sigmoid(x):
    return 1.0 / (1.0 + jnp.exp(-x))


def _accum(ref, i, part):
    @pl.when(i == 0)
    def _():
        ref[...] = part

    @pl.when(i > 0)
    def _():
        ref[...] += part


def _ffn_act_fwd(up, cw):
    T = up.shape[0]
    tm = min(T, _FFN_ROW_TILE)
    cb = D_FF // 2
    nblk = D_FF // cb
    n = T // tm

    def body(gp, gc, gn, vp, vc, vn, wg_ref, wv_ref, o_ref):
        i = pl.program_id(1)
        gate = _conv3(_taps(_ext(gp, gc, gn, i, n)), wg_ref[...])
        val = _conv3(_taps(_ext(vp, vc, vn, i, n)), wv_ref[...])
        o_ref[...] = _mid(gate * _sigmoid(gate) * val).astype(BF16)

    return pl.pallas_call(
        body, name="ffn_act_fwd", grid=(nblk, n),
        in_specs=_halo_specs(T, tm, cb, lambda j, i: (i, j)) + _halo_specs(T, tm, cb, lambda j, i: (i, j + nblk))
        + [pl.BlockSpec((3, cb), lambda j, i: (0, j)), pl.BlockSpec((3, cb), lambda j, i: (0, j + nblk))],
        out_specs=pl.BlockSpec((tm, cb), lambda j, i: (i, j)),
        out_shape=_sds((T, D_FF), BF16), compiler_params=_cp("parallel", "parallel"))(
            up, up, up, up, up, up, cw, cw)


def _ffn_act_bwd(up, dact, cw):
    T = up.shape[0]
    tm = min(T, _FFN_ROW_TILE)
    cb = D_FF // 2
    nblk = D_FF // cb
    n = T // tm

    def body(gp, gc, gn, vp, vc, vn, dp_, dc, dn_, wg_ref, wv_ref, dup_ref, dcw_ref):
        i = pl.program_id(1)
        wg, wv = wg_ref[...], wv_ref[...]
        eg = _taps(_ext(gp, gc, gn, i, n))
        ev = _taps(_ext(vp, vc, vn, i, n))
        ed = _ext(dp_, dc, dn_, i, n)
        gate = _conv3(eg, wg)
        val = _conv3(ev, wv)
        sg = _sigmoid(gate)
        d_gate = ed * val * (sg * (1.0 + gate * (1.0 - sg)))
        d_val = ed * (gate * sg)
        dup_ref[0] = _mid(_conv3_t(d_gate, wg)).astype(BF16)
        dup_ref[1] = _mid(_conv3_t(d_val, wv)).astype(BF16)
        part = jnp.stack([_conv3_wgrad(d_gate, eg), _conv3_wgrad(d_val, ev)], axis=0)
        _accum(dcw_ref, i, part)

    return pl.pallas_call(
        body, name="ffn_act_bwd", grid=(nblk, n),
        in_specs=_halo_specs(T, tm, cb, lambda j, i: (i, j)) + _halo_specs(T, tm, cb, lambda j, i: (i, j + nblk))
        + _halo_specs(T, tm, cb, lambda j, i: (i, j))
        + [pl.BlockSpec((3, cb), lambda j, i: (0, j)), pl.BlockSpec((3, cb), lambda j, i: (0, j + nblk))],
        out_specs=[pl.BlockSpec((2, tm, cb), lambda j, i: (0, i, j)),
                   pl.BlockSpec((2, 3, cb), lambda j, i: (0, 0, j))],
        out_shape=[_sds((2, T, D_FF), BF16), _sds((2, 3, D_FF), F32)],
        compiler_params=_cp("parallel", "arbitrary"))(up, up, up, up, up, up, dact, dact, dact, cw, cw)


_CB_BLK, _CC_BLK, _CX_BLK = 5, 6, 7


def _convmix_fwd(p, w):
    T = p.shape[0]
    tm = min(T, _ROW_TILE)
    n = T // tm

    def body(cb_ref, ccp, ccc, ccn, cxp, cxc, cxn, w_ref, o_ref):
        i = pl.program_id(0)
        z = _ext(ccp, ccc, ccn, i, n) * _ext(cxp, cxc, cxn, i, n)
        o_ref[...] = (cb_ref[...] * _mid(_conv3(_taps(z), w_ref[...]))).astype(BF16)

    return pl.pallas_call(
        body, name="convmix_fwd", grid=(n,),
        in_specs=[pl.BlockSpec((tm, CONV_W), lambda i: (i, _CB_BLK))]
        + _halo_specs(T, tm, CONV_W, lambda i: (i, _CC_BLK)) + _halo_specs(T, tm, CONV_W, lambda i: (i, _CX_BLK))
        + [pl.BlockSpec((3, CONV_W), lambda i: (0, 0))],
        out_specs=pl.BlockSpec((tm, CONV_W), lambda i: (i, 0)),
        out_shape=_sds((T, CONV_W), BF16), compiler_params=_cp("parallel"))(p, p, p, p, p, p, p, w)


def _convmix_bwd(p, dmix, w):
    T = p.shape[0]
    tm = min(T, _ROW_TILE)
    n = T // tm
    dblk = (ATTN_W + SG_W) // CONV_W

    def body(cbp, cbc, cbn, ccp, ccc, ccn, cxp, cxc, cxn, dp_, dc, dn_, w_ref, o_ref, dw_ref):
        i = pl.program_id(0)
        wv = w_ref[...]
        ecb = _ext(cbp, cbc, cbn, i, n)
        ecc = _ext(ccp, ccc, ccn, i, n)
        ecx = _ext(cxp, cxc, cxn, i, n)
        ed = _ext(dp_, dc, dn_, i, n)
        z = _taps(ecc * ecx)
        d_cz = ed * ecb
        d_z = _conv3_t(d_cz, wv)
        o_ref[...] = jnp.concatenate([_mid(ed * _conv3(z, wv)), _mid(d_z * ecx), _mid(d_z * ecc)],
                                     axis=1).astype(BF16)
        _accum(dw_ref, i, _conv3_wgrad(d_cz, z))

    return pl.pallas_call(
        body, name="convmix_bwd", grid=(n,),
        in_specs=_halo_specs(T, tm, CONV_W, lambda i: (i, _CB_BLK)) + _halo_specs(T, tm, CONV_W, lambda i: (i, _CC_BLK))
        + _halo_specs(T, tm, CONV_W, lambda i: (i, _CX_BLK)) + _halo_specs(T, tm, CONV_W, lambda i: (i, dblk))
        + [pl.BlockSpec((3, CONV_W), lambda i: (0, 0))],
        out_specs=[pl.BlockSpec((tm, 3 * CONV_W), lambda i: (i, 0)), pl.BlockSpec((3, CONV_W), lambda i: (0, 0))],
        out_shape=[_sds((T, 3 * CONV_W), BF16), _sds((3, CONV_W), F32)],
        compiler_params=_cp("arbitrary"))(p, p, p, p, p, p, p, p, p, dmix, dmix, dmix, w)


_SU_BLK, _SV_BLK = 3, 4


def _sg_mixed(vnb, w_ref, bias, ch, pr, lo):
    vp = vnb[ch * SG_CHUNK:(ch + 1) * SG_CHUNK, pr * LANES:(pr + 1) * LANES]
    zero = jnp.zeros_like(vp)
    return (_dot(w_ref[2 * pr], jnp.where(lo, vp, zero)) + _dot(w_ref[2 * pr + 1], jnp.where(lo, zero, vp))
            + bias[:, pr * LANES:(pr + 1) * LANES]), vp


def _sg_fwd(p, g, w, bias):
    T = p.shape[0]
    tm = min(T, _ROW_TILE)

    def body(su_ref, sv_ref, g_ref, w_ref, b_ref, o_ref):
        lo = lax.broadcasted_iota(jnp.int32, (SG_CHUNK, LANES), 1) < HEAD_DIM
        sv = sv_ref[...]
        r = lax.rsqrt(jnp.mean(sv * sv, axis=-1, keepdims=True) + EPS)
        vnb = ((sv * r) * g_ref[...]).astype(BF16)
        bias_v = b_ref[...]
        for ch in range(tm // SG_CHUNK):
            for pr in range(2):
                mixed, _ = _sg_mixed(vnb, w_ref, bias_v, ch, pr, lo)
                rows, cols = pl.ds(ch * SG_CHUNK, SG_CHUNK), pl.ds(pr * LANES, LANES)
                o_ref[rows, cols] = (su_ref[rows, cols] * mixed).astype(BF16)

    return pl.pallas_call(
        body, name="sg_fwd", grid=(T // tm,),
        in_specs=[pl.BlockSpec((tm, SG_W), lambda i: (i, _SU_BLK)), pl.BlockSpec((tm, SG_W), lambda i: (i, _SV_BLK)),
                  pl.BlockSpec((1, SG_W), lambda i: (0, 0)), pl.BlockSpec((4, SG_CHUNK, SG_CHUNK), lambda i: (0, 0, 0)),
                  pl.BlockSpec((SG_CHUNK, SG_W), lambda i: (0, 0))],
        out_specs=pl.BlockSpec((tm, SG_W), lambda i: (i, 0)),
        out_shape=_sds((T, SG_W), BF16), compiler_params=_cp("parallel"))(p, p, g, w, bias)


def _sg_bwd(p, dmix, g, w, wt, bias):
    T = p.shape[0]
    tm = min(T, _ROW_TILE)
    dblk = ATTN_W // SG_W

    def body(su_ref, sv_ref, d_ref, g_ref, w_ref, wt_ref, b_ref, o_ref, dw_ref, db_ref, dg_ref, dvn_ref):
        i = pl.program_id(0)
        lo = lax.broadcasted_iota(jnp.int32, (SG_CHUNK, LANES), 1) < HEAD_DIM
        sv = sv_ref[...]
        gv = g_ref[...]
        r = lax.rsqrt(jnp.mean(sv * sv, axis=-1, keepdims=True) + EPS)
        xh = sv * r
        vnb = (xh * gv).astype(BF16)
        bias_v = b_ref[...]
        dw = [jnp.zeros((SG_CHUNK, SG_CHUNK), F32) for _ in range(4)]
        db = jnp.zeros((SG_CHUNK, SG_W), F32)
        for ch in range(tm // SG_CHUNK):
            dbs = []
            for pr in range(2):
                mixed, vp = _sg_mixed(vnb, w_ref, bias_v, ch, pr, lo)
                rows, cols = pl.ds(ch * SG_CHUNK, SG_CHUNK), pl.ds(pr * LANES, LANES)
                dgo = d_ref[rows, cols]
                o_ref[rows, cols] = (dgo * mixed).astype(BF16)
                dm = dgo * su_ref[rows, cols]
                dmb = dm.astype(BF16)
                zero = jnp.zeros_like(dmb)
                dw[2 * pr] += _dot_nt(jnp.where(lo, dmb, zero), vp)
                dw[2 * pr + 1] += _dot_nt(jnp.where(lo, zero, dmb), vp)
                dvn_ref[rows, cols] = jnp.where(lo, _dot(wt_ref[2 * pr], dmb), _dot(wt_ref[2 * pr + 1], dmb))
                dbs.append(dm)
            db += jnp.concatenate(dbs, axis=1)
        dvn = dvn_ref[...]
        dxh = dvn * gv
        o_ref[:, pl.ds(SG_W, SG_W)] = (r * (dxh - xh * jnp.mean(dxh * xh, axis=-1, keepdims=True))).astype(BF16)
        _accum(dw_ref, i, jnp.stack(dw, axis=0))
        _accum(db_ref, i, db)
        _accum(dg_ref, i, jnp.sum(dvn * xh, axis=0, keepdims=True))

    wspec = pl.BlockSpec((4, SG_CHUNK, SG_CHUNK), lambda i: (0, 0, 0))
    return pl.pallas_call(
        body, name="sg_bwd", grid=(T // tm,),
        in_specs=[pl.BlockSpec((tm, SG_W), lambda i: (i, _SU_BLK)), pl.BlockSpec((tm, SG_W), lambda i: (i, _SV_BLK)),
                  pl.BlockSpec((tm, SG_W), lambda i: (i, dblk)), pl.BlockSpec((1, SG_W), lambda i: (0, 0)),
                  wspec, wspec, pl.BlockSpec((SG_CHUNK, SG_W), lambda i: (0, 0))],
        out_specs=[pl.BlockSpec((tm, 2 * SG_W), lambda i: (i, 0)), wspec,
                   pl.BlockSpec((SG_CHUNK, SG_W), lambda i: (0, 0)), pl.BlockSpec((1, SG_W), lambda i: (0, 0))],
        out_shape=[_sds((T, 2 * SG_W), BF16), _sds((4, SG_CHUNK, SG_CHUNK), F32),
                   _sds((SG_CHUNK, SG_W), F32), _sds((1, SG_W), F32)],
        scratch_shapes=[pltpu.VMEM((tm, SG_W), F32)],
        compiler_params=_cp("arbitrary"))(p, p, dmix, g, w, wt, bias)


_ATTN_TQ = 256
_ATTN_TK = 1024
_ATTN_FWD_TK = 512
_SOFTMAX_STRIP = 32
_ONES_ROWS = 16


def _head_mean(v, bmat):
    return jnp.dot(v, bmat, preferred_element_type=F32, precision=lax.Precision.HIGHEST)


def _swap16(y):
    lane = lax.broadcasted_iota(jnp.int32, y.shape, 1)
    return jnp.where(lane % 32 < 16, pltpu.roll(y, y.shape[1] - 16, 1), pltpu.roll(y, 16, 1))


def _rope(y, cos, sin):
    return y * cos + _swap16(y) * sin


def _rope_t(dy, cos, sin):
    return dy * cos + _swap16(dy * sin)


def _dup_rows(t, gidx):
    h = t[gidx * HEAD_DIM:(gidx + 1) * HEAD_DIM]
    return jnp.concatenate([h, h], axis=0)


def _qk_prep(p, gq, gk, cos, sin, bmat, tk, tkf):
    T = p.shape[0]
    nk = T // tk
    sub = tk // tkf
    scale = HEAD_DIM ** -0.5

    def body(q_ref, kv_ref, gq_ref, gk_ref, cos_ref, sin_ref, b_ref, qo_ref, kt_ref, kd_ref, vt_ref, v1_ref):
        cosv, sinv, bm = cos_ref[...], sin_ref[...], b_ref[...]
        for pr in range(ATTN_W // LANES):
            cols = pl.ds(pr * LANES, LANES)
            xq = q_ref[:, cols]
            r = lax.rsqrt(_head_mean(xq * xq, bm) + EPS)
            qo_ref[:, cols] = (_rope((xq * r) * gq_ref[:, cols], cosv, sinv) * scale).astype(BF16)
        xk = kv_ref[:, pl.ds(0, LANES)]
        r = lax.rsqrt(_head_mean(xk * xk, bm) + EPS)
        kt = _rope((xk * r) * gk_ref[...], cosv, sinv).T
        vt = kv_ref[:, pl.ds(LANES, LANES)].T
        for gidx in range(2):
            kdup = _dup_rows(kt, gidx)
            kt_ref[gidx] = kdup.astype(BF16)
            vt_ref[gidx] = _dup_rows(vt, gidx).astype(BF16)
            v1 = jnp.concatenate([vt[gidx * HEAD_DIM:(gidx + 1) * HEAD_DIM],
                                  jnp.ones((_ONES_ROWS, tk), F32)], axis=0).astype(BF16)
            for b in range(sub):
                v1_ref[gidx, b] = v1[:, b * tkf:(b + 1) * tkf]
            kd_ref[gidx] = kdup.T.astype(BF16)

    tspec = pl.BlockSpec((2, None, LANES, tk), lambda i: (0, i, 0, 0))
    dspec = pl.BlockSpec((2, tk, LANES), lambda i: (0, i, 0))
    tab = pl.BlockSpec((tk, LANES), lambda i: (i, 0))
    return pl.pallas_call(
        body, name="qk_prep", grid=(nk,),
        in_specs=[pl.BlockSpec((tk, ATTN_W), lambda i: (i, 0)), pl.BlockSpec((tk, 2 * KV_W), lambda i: (i, ATTN_W // (2 * KV_W))),
                  pl.BlockSpec((1, ATTN_W), lambda i: (0, 0)), pl.BlockSpec((1, KV_W), lambda i: (0, 0)),
                  tab, tab, pl.BlockSpec((LANES, LANES), lambda i: (0, 0))],
        out_specs=[pl.BlockSpec((tk, ATTN_W), lambda i: (i, 0)), tspec, dspec, tspec,
                   pl.BlockSpec((2, sub, HEAD_DIM + _ONES_ROWS, tkf), lambda i: (0, i, 0, 0))],
        out_shape=[_sds((T, ATTN_W), BF16), _sds((2, nk, LANES, tk), BF16), _sds((2, T, LANES), BF16),
                   _sds((2, nk, LANES, tk), BF16), _sds((2, nk * sub, HEAD_DIM + _ONES_ROWS, tkf), BF16)],
        compiler_params=_cp("parallel"))(p, p, gq, gk, cos, sin, bmat)


def _stack_heads(t):
    lo = lax.broadcasted_iota(jnp.int32, (t.shape[0], LANES), 1) < HEAD_DIM
    parts = []
    for pr in range(2):
        tp = t[:, pr * LANES:(pr + 1) * LANES]
        zero = jnp.zeros_like(tp)
        parts += [jnp.where(lo, tp, zero), jnp.where(lo, zero, tp)]
    return jnp.concatenate(parts, axis=0)


def _rows8_reduce(s, op):
    parts = [s[r:r + 8] for r in range(0, s.shape[0], 8)]
    while len(parts) > 1:
        parts = [op(parts[k], parts[k + 1]) for k in range(0, len(parts) - 1, 2)] + (
            [parts[-1]] if len(parts) % 2 else [])
    return parts[0]


def _attn_fwd(q, kd, v1, tq):
    T = q.shape[0]
    nk, tk = v1.shape[1], v1.shape[3]
    vrows = v1.shape[2]
    nq = T // tq
    sq = 4 * tq
    strip = _SOFTMAX_STRIP
    depth = 4
    assert nk % depth == 0

    def body(q_ref, kd_ref, v1_ref, o_ref, lse_ref, qst_ref, s0_ref, s1_ref, s2_ref, s3_ref, pa_ref, pb_ref,
             m_ref, acc_ref):
        s_refs = (s0_ref, s1_ref, s2_ref, s3_ref)
        p_refs = (pa_ref, pb_ref)
        qst_ref[...] = _stack_heads(q_ref[...]).astype(F32).T.astype(BF16)
        m_ref[...] = jnp.full((1, sq), -jnp.inf, F32)
        acc_ref[...] = jnp.zeros((vrows, sq), F32)

        def scores(j):
            return _dot(kd_ref[pl.ds(pl.multiple_of(j * tk, tk), tk), :], qst_ref[...])

        def block_max(s_ref):
            m8 = None
            for c in range(tk // strip):
                part = _rows8_reduce(s_ref[pl.ds(c * strip, strip), :], jnp.maximum)
                m8 = part if m8 is None else jnp.maximum(m8, part)
            return m8

        def exp_pass(s_ref, p_ref, m8):
            m_old = m_ref[...]
            m_new = jnp.maximum(m_old, jnp.max(m8, axis=0, keepdims=True))
            m_ref[...] = m_new
            for c in range(tk // strip):
                rows = pl.ds(c * strip, strip)
                p_ref[rows, :] = jnp.exp(s_ref[rows, :] - m_new).astype(BF16)
            return jnp.exp(m_old - m_new)

        def apply(p_ref, alpha, j):
            acc_ref[...] = alpha * acc_ref[...] + _dot(v1_ref[j], p_ref[...])

        s_refs[0][...] = scores(0)
        s_refs[1][...] = scores(1)

        def trip(t, max_cur):
            for u in range(depth):
                j = depth * t + u
                s_refs[(u + 2) % depth][...] = scores(jnp.minimum(j + 2, nk - 1))
                alpha = exp_pass(s_refs[u], p_refs[u % 2], max_cur)
                max_cur = block_max(s_refs[(u + 1) % depth])
                apply(p_refs[u % 2], alpha, j)
            return max_cur

        lax.fori_loop(0, nk // depth, trip, block_max(s_refs[0]))
        l = acc_ref[pl.ds(HEAD_DIM, 1), :]
        on = acc_ref[pl.ds(0, HEAD_DIM), :] / l
        pairs = []
        for pr in range(2):
            two = jnp.concatenate([on[:, (2 * pr) * tq:(2 * pr + 1) * tq], on[:, (2 * pr + 1) * tq:(2 * pr + 2) * tq]],
                                  axis=0)
            pairs.append(two.T)
        o_ref[...] = jnp.concatenate(pairs, axis=1).astype(BF16)
        lse_ref[...] = jnp.broadcast_to(m_ref[...] + jnp.log(l), (LANES, sq)).T

    row = pltpu.VMEM((1, sq), F32)
    return pl.pallas_call(
        body, name="attn_fwd", grid=(2, nq),
        in_specs=[pl.BlockSpec((tq, 2 * LANES), lambda g, i: (i, g)),
                  pl.BlockSpec((None, T, LANES), lambda g, i: (g, 0, 0)),
                  pl.BlockSpec((None, nk, vrows, tk), lambda g, i: (g, 0, 0, 0))],
        out_specs=[pl.BlockSpec((tq, 2 * LANES), lambda g, i: (i, g)),
                   pl.BlockSpec((None, None, sq, LANES), lambda g, i: (g, i, 0, 0))],
        out_shape=[_sds((T, ATTN_W), BF16), _sds((2, nq, sq, LANES), F32)],
        scratch_shapes=[pltpu.VMEM((LANES, sq), BF16)] + [pltpu.VMEM((tk, sq), F32)] * depth
        + [pltpu.VMEM((tk, sq), BF16), pltpu.VMEM((tk, sq), BF16), row, pltpu.VMEM((vrows, sq), F32)],
        compiler_params=_cp("parallel", "parallel"))(q, kd, v1)


def _attn_bwd(q, o, dmix, lse, kt, kd, vt, tq):
    T = q.shape[0]
    nk, tk = kt.shape[1], kt.shape[3]
    nq = T // tq
    sq = 4 * tq
    rep = tk // LANES

    def body(q_ref, o_ref, do_ref, lse_ref, kt_ref, kd_ref, vt_ref, dq_ref, dkt_ref, dvt_ref):
        i = pl.program_id(1)
        qs = _stack_heads(q_ref[...])
        dof = _stack_heads(do_ref[...])
        dos = dof.astype(BF16)
        qst = qs.astype(F32).T.astype(BF16)
        dost = dof.T.astype(BF16)
        o_pair = o_ref[...].astype(F32)
        os_ = jnp.concatenate([o_pair[:, 0:LANES], o_pair[:, 0:LANES], o_pair[:, LANES:], o_pair[:, LANES:]], axis=0)
        delta = jnp.sum(dof * os_, axis=-1, keepdims=True)
        lse_t = jnp.concatenate([lse_ref[...]] * rep, axis=1)

        @pl.when(i == 0)
        def _():
            dkt_ref[...] = jnp.zeros_like(dkt_ref)
            dvt_ref[...] = jnp.zeros_like(dvt_ref)

        def step(j, dq):
            kdb = kd_ref[pl.ds(pl.multiple_of(j * tk, tk), tk), :]
            pexp = jnp.exp(_dot(qs, kt_ref[j]) - lse_t)
            ds = pexp * (_dot(dos, vt_ref[j]) - delta)
            pb = pexp.astype(BF16)
            dsb = ds.astype(BF16)
            dvt_ref[j] += _dot(dost, pb)
            dkt_ref[j] += _dot(qst, dsb)
            return dq + _dot(dsb, kdb)

        dq = lax.fori_loop(0, nk, step, jnp.zeros((sq, LANES), F32))
        lo = lax.broadcasted_iota(jnp.int32, (tq, LANES), 1) < HEAD_DIM
        dq_ref[...] = jnp.concatenate([jnp.where(lo, dq[0:tq], dq[tq:2 * tq]),
                                       jnp.where(lo, dq[2 * tq:3 * tq], dq[3 * tq:4 * tq])], axis=1)

    tspec = pl.BlockSpec((None, nk, LANES, tk), lambda g, i: (g, 0, 0, 0))
    qspec = pl.BlockSpec((tq, 2 * LANES), lambda g, i: (i, g))
    return pl.pallas_call(
        body, name="attn_bwd", grid=(2, nq),
        in_specs=[qspec, qspec, qspec, pl.BlockSpec((None, None, sq, LANES), lambda g, i: (g, i, 0, 0)),
                  tspec, pl.BlockSpec((None, T, LANES), lambda g, i: (g, 0, 0)), tspec],
        out_specs=[qspec, tspec, tspec],
        out_shape=[_sds((T, ATTN_W), F32), _sds((2, nk, LANES, tk), F32), _sds((2, nk, LANES, tk), F32)],
        compiler_params=_cp("parallel", "arbitrary"))(q, o, dmix, lse, kt, kd, vt)


def _fold_t(t_ref):
    rows = []
    for gidx in range(2):
        t = t_ref[gidx]
        rows.append(t[0:HEAD_DIM] + t[HEAD_DIM:2 * HEAD_DIM])
    return jnp.concatenate(rows, axis=0).T


def _qk_bwd(p, dq, dkt, dvt, gq, gk, cos, sin, bmat):
    T = p.shape[0]
    nk, tk = dkt.shape[1], dkt.shape[3]
    scale = HEAD_DIM ** -0.5

    def norm_bwd(x, dy, gain, bm):
        r = lax.rsqrt(_head_mean(x * x, bm) + EPS)
        xh = x * r
        dxh = dy * gain
        return r * (dxh - xh * _head_mean(dxh * xh, bm)), jnp.sum(dy * xh, axis=0, keepdims=True)

    def body(q_ref, kv_ref, dq_ref, dkt_ref, dvt_ref, gq_ref, gk_ref, cos_ref, sin_ref, b_ref, o_ref, dgq_ref, dgk_ref):
        i = pl.program_id(0)
        cosv, sinv, bm = cos_ref[...], sin_ref[...], b_ref[...]
        dgq = []
        for pr in range(ATTN_W // LANES):
            cols = pl.ds(pr * LANES, LANES)
            dy = _rope_t(dq_ref[:, cols] * scale, cosv, sinv)
            dx, dg = norm_bwd(q_ref[:, cols], dy, gq_ref[:, cols], bm)
            o_ref[:, cols] = dx.astype(BF16)
            dgq.append(dg)
        dy = _rope_t(_fold_t(dkt_ref), cosv, sinv)
        dx, dgk = norm_bwd(kv_ref[:, pl.ds(0, LANES)], dy, gk_ref[...], bm)
        o_ref[:, pl.ds(ATTN_W, LANES)] = dx.astype(BF16)
        o_ref[:, pl.ds(ATTN_W + LANES, LANES)] = _fold_t(dvt_ref).astype(BF16)
        _accum(dgq_ref, i, jnp.concatenate(dgq, axis=1))
        _accum(dgk_ref, i, dgk)

    tspec = pl.BlockSpec((2, None, LANES, tk), lambda i: (0, i, 0, 0))
    tab = pl.BlockSpec((tk, LANES), lambda i: (i, 0))
    return pl.pallas_call(
        body, name="qk_bwd", grid=(nk,),
        in_specs=[pl.BlockSpec((tk, ATTN_W), lambda i: (i, 0)), pl.BlockSpec((tk, 2 * KV_W), lambda i: (i, ATTN_W // (2 * KV_W))),
                  pl.BlockSpec((tk, ATTN_W), lambda i: (i, 0)), tspec, tspec,
                  pl.BlockSpec((1, ATTN_W), lambda i: (0, 0)), pl.BlockSpec((1, KV_W), lambda i: (0, 0)),
                  tab, tab, pl.BlockSpec((LANES, LANES), lambda i: (0, 0))],
        out_specs=[pl.BlockSpec((tk, ATTN_W + 2 * KV_W), lambda i: (i, 0)),
                   pl.BlockSpec((1, ATTN_W), lambda i: (0, 0)), pl.BlockSpec((1, KV_W), lambda i: (0, 0))],
        out_shape=[_sds((T, ATTN_W + 2 * KV_W), BF16), _sds((1, ATTN_W), F32), _sds((1, KV_W), F32)],
        compiler_params=_cp("arbitrary"))(p, p, dq, dkt, dvt, gq, gk, cos, sin, bmat)


def _loss_head(y, target):
    T, Dm = y.shape
    tm = min(T, _ROW_TILE)

    def body(y_ref, t_ref, dy_ref, l_ref):
        i = pl.program_id(0)
        err = y_ref[...] - t_ref[...]
        dy_ref[...] = err * (1.0 / Dm)
        part = jnp.sum(jnp.sum(err * err, axis=-1, keepdims=True), axis=0, keepdims=True) * (0.5 / Dm)
        _accum(l_ref, i, jnp.broadcast_to(part, (8, LANES)))

    row = pl.BlockSpec((tm, Dm), lambda i: (i, 0))
    return pl.pallas_call(
        body, name="loss_head", grid=(T // tm,), in_specs=[row, row],
        out_specs=[row, pl.BlockSpec((8, LANES), lambda i: (0, 0))],
        out_shape=[_sds((T, Dm), F32), _sds((8, LANES), F32)], compiler_params=_cp("arbitrary"))(y, target)


def _adamw(w, g, m, v, name):
    R, C = w.shape
    tr = R
    for cand in (512, 256, 128, 64, 32, 16, 8):
        if R % cand == 0:
            tr = cand
            break
    c1 = 1.0 - ADAM_B1 ** ADAM_STEP
    c2 = 1.0 - ADAM_B2 ** ADAM_STEP

    def body(w_ref, g_ref, m_ref, v_ref, d_ref, mo_ref, vo_ref):
        gv = g_ref[...]
        mn = ADAM_B1 * m_ref[...] + (1.0 - ADAM_B1) * gv
        vn = ADAM_B2 * v_ref[...] + (1.0 - ADAM_B2) * (gv * gv)
        d_ref[...] = -ADAM_LR * ((mn / c1) / (jnp.sqrt(vn / c2) + ADAM_EPS) + ADAM_WD * w_ref[...])
        mo_ref[...] = mn
        vo_ref[...] = vn

    blk = pl.BlockSpec((tr, C), lambda i: (i, 0))
    return pl.pallas_call(
        body, name=name, grid=(R // tr,), in_specs=[blk] * 4, out_specs=[blk] * 3,
        out_shape=[_sds((R, C), F32)] * 3, compiler_params=_cp("parallel"))(w, g, m, v)


def _cast_bf16(w, name):
    R, C = w.shape
    tr = 512 if R % 512 == 0 else 256

    def body(w_ref, o_ref):
        o_ref[...] = w_ref[...].astype(BF16)

    blk = pl.BlockSpec((tr, C), lambda i: (i, 0))
    return pl.pallas_call(body, name=name, grid=(R // tr,), in_specs=[blk], out_specs=blk,
                          out_shape=_sds((R, C), BF16), compiler_params=_cp("parallel"))(w)


def _position():
    x, y, c = lax.axis_index("x"), lax.axis_index("y"), lax.axis_index("c")
    return x, y, c


def _other_chips(x, y):
    return [(1 - x, y), (x, 1 - y), (1 - x, 1 - y)]


_HBM = pl.BlockSpec(memory_space=pltpu.HBM)
_SEM = pl.BlockSpec(memory_space=pltpu.SEMAPHORE)
_EFFECT = pltpu.SideEffectType.DATAFLOW_SIDE_EFFECTING


def _chip_copies(srcs, lands, send_sems, recv_sems, per_chip, arriving):
    x, y, c = _position()
    me = 2 * x + y
    copies = []
    for t, (src, land) in enumerate(zip(srcs, lands)):
        for k, (px, py) in enumerate(_other_chips(x, y)):
            peer = 2 * px + py
            copies.append(pltpu.make_async_remote_copy(
                src_ref=src.at[peer] if per_chip else src, dst_ref=land.at[peer if arriving else me],
                send_sem=send_sems[3 * t + k], recv_sem=recv_sems[3 * t + k],
                device_id=(px, py, c), device_id_type=MESH))
    return copies


def _chips_start(srcs, per_chip, name):
    n = len(srcs)
    slab = [s.shape[1:] if per_chip else s.shape for s in srcs]
    lands = [lax.empty((N_CHIPS,) + sh, s.dtype) for sh, s in zip(slab, srcs)]

    ns = 3 * n

    def body(*refs):
        ins = refs[:2 * n]
        send_sems, recv_sems = refs[2 * n:2 * n + ns], refs[2 * n + ns:2 * n + 2 * ns]
        token = refs[-1]
        for cp in _chip_copies(ins[:n], ins[n:], send_sems, recv_sems, per_chip, False):
            cp.start()
        token[...] = jnp.zeros_like(token)

    args = [pltpu.with_memory_space_constraint(a, pltpu.HBM) for a in list(srcs) + lands]
    outs = pl.pallas_call(
        body, name=name,
        out_shape=[pltpu.SemaphoreType.DMA(())] * (2 * ns)
        + [pltpu.HBM(a.shape, a.dtype) for a in args] + [_sds((8, LANES), F32)],
        in_specs=[_HBM] * (2 * n),
        out_specs=[_SEM] * (2 * ns) + [_HBM] * (2 * n) + [pl.BlockSpec(memory_space=pltpu.VMEM)],
        input_output_aliases={i: 2 * ns + i for i in range(2 * n)},
        compiler_params=pltpu.CompilerParams(has_side_effects=_EFFECT))(*args)
    sems, rest = outs[:2 * ns], outs[2 * ns:]
    return sems[:ns], sems[ns:], rest[:n], rest[n:2 * n], rest[-1]


def _chips_wait(handle, after, per_chip, name):
    send_sems, recv_sems, srcs, lands, _ = handle
    n = len(srcs)
    ns = 3 * n

    def body(*refs):
        ins = refs[:2 * n]
        s_sems, r_sems = refs[2 * n:2 * n + ns], refs[2 * n + ns:2 * n + 2 * ns]
        for cp in _chip_copies(ins[:n], ins[n:], s_sems, r_sems, per_chip, False):
            cp.wait_send()
        for cp in _chip_copies(ins[:n], ins[n:], s_sems, r_sems, per_chip, True):
            cp.wait_recv()

    outs = pl.pallas_call(
        body, name=name,
        out_shape=[pltpu.HBM(a.shape, a.dtype) for a in list(srcs) + list(lands)],
        in_specs=[_HBM] * (2 * n) + [_SEM] * (2 * ns) + [ANY],
        out_specs=[_HBM] * (2 * n),
        input_output_aliases={i: i for i in range(2 * n)},
        compiler_params=pltpu.CompilerParams(has_side_effects=_EFFECT))(*srcs, *lands, *send_sems, *recv_sems, after)
    return outs[:n], outs[n:]


def _sum_chips_own(land, own, name):
    S, R, C = land.shape
    tr = R
    for cand in (256, 128, 64, 32, 16, 8):
        if R % cand == 0:
            tr = cand
            break

    def body(l_ref, o_ref, out_ref):
        x, y, _ = _position()
        me = 2 * x + y
        mine = o_ref[me] if own.ndim == 3 else o_ref[...]
        acc = None
        for k in range(S):
            part = jnp.where(me == k, mine, l_ref[k])
            acc = part if acc is None else acc + part
        out_ref[...] = acc

    blk = pl.BlockSpec((S, tr, C), lambda i: (0, i, 0))
    row = pl.BlockSpec((tr, C), lambda i: (i, 0))
    return pl.pallas_call(
        body, name=name, grid=(R // tr,), in_specs=[blk, blk if own.ndim == 3 else row], out_specs=row,
        out_shape=_sds((R, C), F32), compiler_params=_cp("parallel"))(land, own)


def _add_pair(a, b, name):
    R, C = a.shape

    def body(a_ref, b_ref, o_ref):
        o_ref[...] = a_ref[...] + b_ref[...]

    blk = pl.BlockSpec((R, C), lambda: (0, 0))
    return pl.pallas_call(body, name=name, in_specs=[blk, blk], out_specs=blk, out_shape=_sds((R, C), F32))(a, b)


def _exchange_sibling(arrays):
    n = len(arrays)

    def body(*refs):
        ins, outs = refs[:n], refs[n:2 * n]
        send_sems, recv_sems = refs[2 * n:]
        x, y, c = _position()
        sends = []
        for t in range(n):
            cp = pltpu.make_async_remote_copy(src_ref=ins[t], dst_ref=outs[t], send_sem=send_sems.at[t],
                                              recv_sem=recv_sems.at[t], device_id=(x, y, 1 - c), device_id_type=MESH)
            cp.start()
            sends.append(cp)
        for cp in sends:
            cp.wait()

    return pl.pallas_call(
        body, name="grads_to_sibling",
        in_specs=[ANY] * n, out_specs=[ANY] * n, out_shape=[_sds(a.shape, a.dtype) for a in arrays],
        scratch_shapes=[pltpu.SemaphoreType.DMA((n,)), pltpu.SemaphoreType.DMA((n,))],
        compiler_params=pltpu.CompilerParams(has_side_effects=True))(*arrays)


def _adamw_sum(w, ga, gb, m, v, name):
    R, C = w.shape
    tr = next(t for t in (512, 256, 128, 64) if R % t == 0 and t * C * 4 <= (1 << 20))
    c1 = 1.0 - ADAM_B1 ** ADAM_STEP
    c2 = 1.0 - ADAM_B2 ** ADAM_STEP

    def body(w_ref, ga_ref, gb_ref, m_ref, v_ref, g_ref, d_ref, mo_ref, vo_ref):
        gv = ga_ref[...] + gb_ref[...]
        mn = ADAM_B1 * m_ref[...] + (1.0 - ADAM_B1) * gv
        vn = ADAM_B2 * v_ref[...] + (1.0 - ADAM_B2) * (gv * gv)
        g_ref[...] = gv
        d_ref[...] = -ADAM_LR * ((mn / c1) / (jnp.sqrt(vn / c2) + ADAM_EPS) + ADAM_WD * w_ref[...])
        mo_ref[...] = mn
        vo_ref[...] = vn

    blk = pl.BlockSpec((tr, C), lambda i: (i, 0))
    return pl.pallas_call(
        body, name=name, grid=(R // tr,), in_specs=[blk] * 5, out_specs=[blk] * 4,
        out_shape=[_sds((R, C), F32)] * 4, compiler_params=_cp("parallel"))(w, ga, gb, m, v)


def _rope_tables(T):
    pos = jnp.arange(T)
    row = (pos // GRID_W).astype(F32)
    col = (pos % GRID_W).astype(F32)
    inv = 1.0 / (ROPE_THETA ** (jnp.arange(AXIS_DIM // 2, dtype=F32) * 2.0 / AXIS_DIM))
    ar, ac = row[:, None] * inv[None, :], col[:, None] * inv[None, :]
    cos = jnp.concatenate([jnp.cos(ar), jnp.cos(ar), jnp.cos(ac), jnp.cos(ac)], axis=-1)
    sin = jnp.concatenate([-jnp.sin(ar), jnp.sin(ar), -jnp.sin(ac), jnp.sin(ac)], axis=-1)
    return jnp.tile(cos, (1, LANES // HEAD_DIM)), jnp.tile(sin, (1, LANES // HEAD_DIM))


def _head_mean_matrix():
    h = jnp.arange(LANES) // HEAD_DIM
    return jnp.where(h[:, None] == h[None, :], 1.0 / HEAD_DIM, 0.0).astype(F32)


def _pack(arrays):
    flat = jnp.concatenate([a.reshape(-1) for a in arrays])
    rows = -(-flat.shape[0] // LANES)
    rows = -(-rows // 256) * 256
    return jnp.pad(flat, (0, rows * LANES - flat.shape[0])).reshape(rows, LANES)


def _unpack(packed, like):
    flat = packed.reshape(-1)
    out, off = [], 0
    for a in like:
        out.append(flat[off:off + a.size].reshape(a.shape))
        off += a.size
    return out


def _layer_fwd(x, h, lw, consts, ffn_weights, next_g):
    cos, sin, bmat = consts
    T = x.shape[0]
    tk = min(T, _ATTN_TK)
    tq = min(T, _ATTN_TQ)
    if h is None:
        h = _norm_fwd(x, lw["norm1_g"])
    p = _mm_nn(h, lw["w_in"], out_dtype=F32, name="mm_p")
    qn, kt, kd, vt, v1 = _qk_prep(p, lw["gq"], lw["gk"], cos, sin, bmat, tk, min(tk, _ATTN_FWD_TK))
    o, lse = _attn_fwd(qn, kd, v1, tq)
    go = _sg_fwd(p, lw["sg_norm_g"], lw["sg_w"], lw["sg_bias"])
    co = _convmix_fwd(p, lw["conv_w"])
    mix = jnp.concatenate([o, go, co], axis=1)
    x_mid, h2 = _mm_nn(mix, lw["w_out"], out_dtype=F32, name="mm_out", res=x, norm_g=lw["norm2_g"])
    lw.update(ffn_weights(x_mid))
    up = _mm_nn(h2, lw["w_up"], out_dtype=F32, name="mm_up", tm=_FFN_ROW_TILE)
    act = _ffn_act_fwd(up, lw["ffn_conv_w"])
    if next_g is None:
        x_out, h_next = _mm_nn(act, lw["w_down"], out_dtype=F32, name="mm_down_last", res=x_mid), None
    else:
        x_out, h_next = _mm_nn(act, lw["w_down"], out_dtype=F32, name="mm_down", res=x_mid, norm_g=next_g)
    saved = dict(x=x, h=h, p=p, qn=qn, kt=kt, kd=kd, vt=vt, o=o, lse=lse, mix=mix, x_mid=x_mid, h2=h2, up=up, act=act)
    return x_out, h_next, saved


def _layer_bwd(dx, s, lw, consts, send):
    cos, sin, bmat = consts
    T = dx.shape[0]
    tq = min(T, _ATTN_TQ)
    g = {}
    d_act = _mm_nt(dx, lw["w_down"], name="mm_dact")
    g_down = _mm_tn(s["act"], dx, tk=D_FF // 2, tn=D_MODEL, name="mm_dwdown", tm=_WGRAD_ROWS // 2)
    tok = send("w_down", g_down.reshape(N_CHIPS, D_FF // N_CHIPS, D_MODEL))
    d_up, d_cw = _ffn_act_bwd(s["up"], d_act, lw["ffn_conv_w"] + tok)
    g["ffn_conv_w"] = d_cw.transpose(1, 0, 2).reshape(3, 2 * D_FF)
    tok = send("w_up", _mm_tn(s["h2"], d_up, tk=512, tn=D_FF // 2, name="mm_dwup", shards=N_CHIPS, tm=_WGRAD_ROWS))
    dx2, g["norm2_g"] = _mm_nt(d_up, lw["w_up"], name="mm_dh2", norm=(s["x_mid"], dx, lw["norm2_g"] + tok))
    d_mix = _mm_nt(dx2, lw["w_out"], name="mm_dmix")
    g_out = _mm_tn(s["mix"], dx2, tk=512, tn=D_MODEL, name="mm_dwout", tm=_WGRAD_ROWS // 2)
    tok = send("w_out", g_out.reshape(N_CHIPS, D_MODEL // N_CHIPS, D_MODEL))
    dp_c, g["conv_w"] = _convmix_bwd(s["p"], d_mix, lw["conv_w"] + tok)
    dp_b, g["sg_w"], d_bias, g["sg_norm_g"] = _sg_bwd(s["p"], d_mix, lw["sg_norm_g"], lw["sg_w"], lw["sg_wt"], lw["sg_bias"])
    g["sg_b"] = d_bias.reshape(SG_CHUNK, SG_W // HEAD_DIM, HEAD_DIM).sum(axis=-1).T
    dq, dkt, dvt = _attn_bwd(s["qn"], s["o"], d_mix, s["lse"], s["kt"], s["kd"], s["vt"], tq)
    dp_a, d_gq, d_gk = _qk_bwd(s["p"], dq, dkt, dvt, lw["gq"], lw["gk"], cos, sin, bmat)
    g["q_norm_g"] = d_gq.reshape(ATTN_W // HEAD_DIM, HEAD_DIM).sum(axis=0)
    g["k_norm_g"] = d_gk.reshape(KV_W // HEAD_DIM, HEAD_DIM).sum(axis=0)
    dp = jnp.concatenate([dp_a, dp_b, dp_c], axis=1)
    tok = send("w_in", _mm_tn(s["h"], dp, tk=D_MODEL, tn=512, name="mm_dwin", shards=N_CHIPS, tm=_WGRAD_ROWS))
    dx_in, g["norm1_g"] = _mm_nt(dp, lw["w_in"], name="mm_dh", norm=(s["x"], dx2, lw["norm1_g"] + tok))
    return dx_in, g


def _layer_weights(l, full, small):
    sg_w = small["sg_w"][l]
    sg_b = small["sg_b"][l]
    return dict(
        norm1_g=small["norm1_g"][l][None, :], norm2_g=small["norm2_g"][l][None, :],
        gq=jnp.tile(small["q_norm_g"][l], ATTN_W // HEAD_DIM)[None, :],
        gk=jnp.tile(small["k_norm_g"][l], KV_W // HEAD_DIM)[None, :],
        sg_norm_g=small["sg_norm_g"][l][None, :],
        sg_w=sg_w.astype(BF16), sg_wt=sg_w.transpose(0, 2, 1).astype(BF16),
        sg_bias=jnp.repeat(sg_b.T, HEAD_DIM, axis=1),
        **full)


_BIG = ("w_in", "w_out", "ffn_w_up", "ffn_w_down")
_SMALL_REPL = ("norm1_g", "q_norm_g", "k_norm_g", "sg_norm_g", "sg_w", "sg_b", "norm2_g")
_SMALL_SHARD = ("conv_w", "ffn_conv_w")
_ORDER = ("norm1_g", "w_in", "q_norm_g", "k_norm_g", "sg_norm_g", "sg_w", "sg_b", "conv_w", "w_out", "norm2_g",
          "ffn_w_up", "ffn_conv_w", "ffn_w_down")


def kernel(x, norm1_g, w_in, q_norm_g, k_norm_g, sg_norm_g, sg_w, sg_b, conv_w, w_out, norm2_g, ffn_w_up, ffn_conv_w, ffn_w_down, loss_target, m_norm1_g, m_w_in, m_q_norm_g, m_k_norm_g, m_sg_norm_g, m_sg_w, m_sg_b, m_conv_w, m_w_out, m_norm2_g, m_ffn_w_up, m_ffn_conv_w, m_ffn_w_down, v_norm1_g, v_w_in, v_q_norm_g, v_k_norm_g, v_sg_norm_g, v_sg_w, v_sg_b, v_conv_w, v_w_out, v_norm2_g, v_ffn_w_up, v_ffn_conv_w, v_ffn_w_down):
    w = dict(norm1_g=norm1_g, w_in=w_in, q_norm_g=q_norm_g, k_norm_g=k_norm_g, sg_norm_g=sg_norm_g, sg_w=sg_w,
             sg_b=sg_b, conv_w=conv_w, w_out=w_out, norm2_g=norm2_g, ffn_w_up=ffn_w_up, ffn_conv_w=ffn_conv_w,
             ffn_w_down=ffn_w_down)
    mom = dict(norm1_g=m_norm1_g, w_in=m_w_in, q_norm_g=m_q_norm_g, k_norm_g=m_k_norm_g, sg_norm_g=m_sg_norm_g,
               sg_w=m_sg_w, sg_b=m_sg_b, conv_w=m_conv_w, w_out=m_w_out, norm2_g=m_norm2_g, ffn_w_up=m_ffn_w_up,
               ffn_conv_w=m_ffn_conv_w, ffn_w_down=m_ffn_w_down)
    var = dict(norm1_g=v_norm1_g, w_in=v_w_in, q_norm_g=v_q_norm_g, k_norm_g=v_k_norm_g, sg_norm_g=v_sg_norm_g,
               sg_w=v_sg_w, sg_b=v_sg_b, conv_w=v_conv_w, w_out=v_w_out, norm2_g=v_norm2_g, ffn_w_up=v_ffn_w_up,
               ffn_conv_w=v_ffn_conv_w, ffn_w_down=v_ffn_w_down)
    L = DEPTH
    T = x.shape[1]
    xs = x.reshape(T, D_MODEL)
    target = loss_target.reshape(T, D_MODEL)

    chip = 2 * lax.axis_index("x") + lax.axis_index("y")

    shards = [_cast_bf16(w[n].reshape(-1, w[n].shape[-1]), "cast_" + n).reshape(w[n].shape) for n in _BIG]
    shards += [conv_w, ffn_conv_w]
    w_in_s, w_out_s, w_up_s, w_down_s, conv_s, fconv_s = shards
    gathers = []
    for l in range(L):
        gathers.append((_chips_start([w_in_s[l], w_out_s[l], conv_s[l]], False, "gather_start_%da" % l),
                        _chips_start([w_up_s[l], w_down_s[l], fconv_s[l]], False, "gather_start_%db" % l)))
    start_token = sum(h[4][0, 0] for pair in gathers for h in pair)
    consts = _rope_tables(T) + (_head_mean_matrix(),)

    def gathered(handle, after, name):
        own, lands = _chips_wait(handle, after, False, name)
        return [lax.dynamic_update_slice(ld, o[None], (chip,) + (jnp.int32(0),) * o.ndim) for ld, o in zip(lands, own)]

    saved, lws = [], []
    act_x, act_h = xs, None
    for l in range(L):
        g_in, g_out, g_conv = gathered(gathers[l][0], act_x if l else gathers[-1][1][4], "gather_wait_%da" % l)
        lw = _layer_weights(l, dict(w_in=g_in, w_out=g_out.reshape(D_MODEL, D_MODEL),
                                    conv_w=g_conv.transpose(1, 0, 2).reshape(3, CONV_W)), w)
        if l == 0:
            lw["norm1_g"] = lw["norm1_g"] + start_token

        def ffn_weights(after, l=l):
            g_up, g_down, g_fconv = gathered(gathers[l][1], after, "gather_wait_%db" % l)
            return dict(w_up=g_up, w_down=g_down.reshape(D_FF, D_MODEL),
                        ffn_conv_w=g_fconv.transpose(1, 0, 2).reshape(3, 2 * D_FF))

        next_g = w["norm1_g"][l + 1][None, :] if l + 1 < L else None
        act_x, act_h, s = _layer_fwd(act_x, act_h, lw, consts, ffn_weights, next_g)
        saved.append(s)
        lws.append(lw)
    dx, loss_blk = _loss_head(act_x, target)
    loss = lax.psum(loss_blk[0, 0], ("x", "y", "c"))

    grads = [None] * L
    partial = [None] * L

    def collect(pending, after, l):
        sums = {}
        for name, handle in pending:
            own, lands = _chips_wait(handle, after, True, "grad_wait_%d_%s" % (l, name))
            sums[name] = _sum_chips_own(lands[0], own[0], "sum_chips_" + name)
        return sums

    pending_prev = None
    for l in reversed(range(L)):
        pending = []

        def send(name, g4, l=l, pending=pending):
            handle = _chips_start([g4], True, "grad_start_%d_%s" % (l, name))
            pending.append((name, handle))
            return handle[4][0, 0]

        dx, g = _layer_bwd(dx, saved[l], lws[l], consts, send)
        grads[l] = g
        if pending_prev is not None:
            partial[l + 1] = collect(pending_prev, dx, l + 1)
        pending_prev = pending
    grad_x = dx.reshape(x.shape)

    small_names = _SMALL_REPL + _SMALL_SHARD
    small_local = [jnp.stack([grads[l][n].reshape(-1) for l in range(L)]) for n in small_names]
    small_handle = _chips_start([_pack(small_local)], False, "small_start")
    partial[0] = collect(pending_prev, small_handle[4], 0)

    short = dict(w_in="w_in", w_out="w_out", ffn_w_up="w_up", ffn_w_down="w_down")
    mine = [jnp.stack([partial[l][short[n]] for l in range(L)]) for n in _BIG]
    small_own, small_lands = _chips_wait(small_handle, mine[-1], False, "small_wait")
    mine.append(_sum_chips_own(small_lands[0], small_own[0], "sum_chips_small"))
    theirs = _exchange_sibling(mine)
    grad = dict(zip(small_names, _unpack(_add_pair(mine[-1], theirs[-1], "add_small"), small_local)))
    for n in _SMALL_REPL:
        grad[n] = grad[n].reshape(w[n].shape)
    for n in _SMALL_SHARD:
        full_w = grad[n].reshape(L, 3, -1)
        width = w[n].shape[-1]
        grad[n] = lax.dynamic_slice_in_dim(full_w, chip * width, width, axis=2)
    delta, new_m, new_v = {}, {}, {}
    for n, ga, gb in zip(_BIG, mine, theirs):
        shp = w[n].shape
        v2 = lambda a: a.reshape(-1, shp[-1])
        gsum, d, mn, vn = _adamw_sum(v2(w[n]), v2(ga), v2(gb), v2(mom[n]), v2(var[n]), "adamw_" + n)
        grad[n], delta[n], new_m[n], new_v[n] = gsum.reshape(shp), d.reshape(shp), mn.reshape(shp), vn.reshape(shp)
    for group, gname in ((_SMALL_REPL, "adamw_small"), (_SMALL_SHARD, "adamw_conv")):
        like = [w[n] for n in group]
        outs = _adamw(_pack([w[n] for n in group]), _pack([grad[n] for n in group]), _pack([mom[n] for n in group]),
                      _pack([var[n] for n in group]), gname)
        for res, dst in zip(outs, (delta, new_m, new_v)):
            for n, a in zip(group, _unpack(res, like)):
                dst[n] = a

    return (loss, grad_x, *[grad[n] for n in _ORDER], *[delta[n] for n in _ORDER],
            *[new_m[n] for n in _ORDER], *[new_v[n] for n in _ORDER])
```

```python
import jax
import jax.numpy as jnp
from jax import lax
from jax.experimental import pallas as pl
from jax.experimental.pallas import tpu as pltpu

F32 = jnp.float32
BF16 = jnp.bfloat16

DEPTH = 4
D_MODEL = 1024
HEAD_DIM = 64
ATTN_W = 512
KV_W = 128
SG_W = 256
CONV_W = 256
SG_CHUNK = 128
D_FF = 2816
PROJ_W = 2048
GRID_W = 64
ROPE_THETA = 10000.0
AXIS_DIM = HEAD_DIM // 2
EPS = 1e-6
N_CHIPS = 4

ADAM_LR = 0.001
ADAM_B1 = 0.9
ADAM_B2 = 0.999
ADAM_EPS = 1e-08
ADAM_WD = 0.01
ADAM_STEP = 10

_ROW_TILE = 512
_FFN_ROW_TILE = 256
_WGRAD_ROWS = 2048
LANES = 128
HALO = 8
VMEM_LIMIT_BYTES = 56 * 1024 * 1024
MESH = pl.DeviceIdType.MESH
ANY = pl.BlockSpec(memory_space=pl.ANY)


def _cp(*sem):
    return pltpu.CompilerParams(dimension_semantics=sem if sem else None,
                                vmem_limit_bytes=VMEM_LIMIT_BYTES)


def _sds(shape, dtype):
    return jax.ShapeDtypeStruct(shape, dtype)


def _dot(a, b):
    return jnp.dot(a, b, preferred_element_type=F32)


def _dot_nt(a, b):
    return lax.dot_general(a, b, (((1,), (1,)), ((), ())), preferred_element_type=F32)


def _dot_tn(a, b):
    return lax.dot_general(a, b, (((0,), (0,)), ((), ())), preferred_element_type=F32)


def _norm_fwd(x, g):
    T, Dm = x.shape
    tm = min(T, _ROW_TILE)

    def body(x_ref, g_ref, o_ref):
        xv = x_ref[...]
        r = lax.rsqrt(jnp.mean(xv * xv, axis=-1, keepdims=True) + EPS)
        o_ref[...] = ((xv * r) * g_ref[...]).astype(BF16)

    return pl.pallas_call(
        body, name="norm_fwd", grid=(T // tm,),
        in_specs=[pl.BlockSpec((tm, Dm), lambda i: (i, 0)), pl.BlockSpec((1, Dm), lambda i: (0, 0))],
        out_specs=pl.BlockSpec((tm, Dm), lambda i: (i, 0)),
        out_shape=_sds((T, Dm), BF16), compiler_params=_cp("parallel"))(x, g)


def _whole(w):
    return pl.BlockSpec(w.shape, lambda *g: (0,) * w.ndim)


def _mm_nn(a, w, *, out_dtype, name, res=None, tm=None, norm_g=None):
    M, K = a.shape
    N = w.shape[-1] if w.ndim == 2 else w.shape[0] * w.shape[2]
    tm = min(M, tm or _ROW_TILE)
    has_res = res is not None
    has_norm = norm_g is not None
    assert not has_norm or w.ndim == 2

    def body(*refs):
        a_ref, w_ref = refs[0], refs[1]
        o_ref = refs[-2] if has_norm else refs[-1]
        av = a_ref[...].astype(BF16)
        parts = [w_ref[...]] if w.ndim == 2 else [w_ref[s] for s in range(w.shape[0])]
        ns = N // len(parts)
        for s, wv in enumerate(parts):
            cols = pl.ds(s * ns, ns)
            acc = _dot(av, wv)
            if has_res:
                acc = acc + refs[2][:, cols]
            o_ref[:, cols] = acc.astype(out_dtype)
        if has_norm:
            r = lax.rsqrt(jnp.mean(acc * acc, axis=-1, keepdims=True) + EPS)
            refs[-1][...] = ((acc * r) * refs[2 + has_res][...]).astype(BF16)

    row = pl.BlockSpec((tm, N), lambda i: (i, 0))
    in_specs = [pl.BlockSpec((tm, K), lambda i: (i, 0)), _whole(w)]
    args = [a, w]
    if has_res:
        in_specs.append(row)
        args.append(res)
    if has_norm:
        in_specs.append(pl.BlockSpec((1, N), lambda i: (0, 0)))
        args.append(norm_g)
    return pl.pallas_call(
        body, name=name, grid=(M // tm,), in_specs=in_specs, out_specs=[row, row] if has_norm else row,
        out_shape=[_sds((M, N), out_dtype), _sds((M, N), BF16)] if has_norm else _sds((M, N), out_dtype),
        compiler_params=_cp("parallel"))(*args)


def _a_spec(a, tm, tn, row_of, col_of):
    if a.ndim == 2:
        return pl.BlockSpec((tm, tn), lambda *g: (row_of(*g), col_of(*g)))
    bph = a.shape[2] // tn
    return pl.BlockSpec((None, tm, tn), lambda *g: (col_of(*g) // bph, row_of(*g), col_of(*g) % bph))


def _a_cols(a):
    return a.shape[1] if a.ndim == 2 else a.shape[0] * a.shape[2]


def _mm_nt(a, w, *, name, tm=None, norm=None):
    M = a.shape[-2]
    Kw = w.shape[-2]
    tm = min(M, tm or _ROW_TILE)

    def product(a_ref, w_ref):
        if w.ndim == 2:
            return _dot_nt(a_ref[...].astype(BF16), w_ref[...])
        S, ns = w.shape[0], w.shape[2]
        acc = None
        for s in range(S):
            if a.ndim == 2:
                piece = a_ref[:, pl.ds(s * ns, ns)]
            else:
                per_half = S // 2
                piece = a_ref[s // per_half, :, pl.ds((s % per_half) * ns, ns)]
            part = _dot_nt(piece.astype(BF16), w_ref[s])
            acc = part if acc is None else acc + part
        return acc

    def body(a_ref, w_ref, o_ref):
        o_ref[...] = product(a_ref, w_ref)

    def body_norm(a_ref, w_ref, x_ref, dr_ref, g_ref, dx_ref, dg_ref):
        dhv = product(a_ref, w_ref)
        xv = x_ref[...]
        r = lax.rsqrt(jnp.mean(xv * xv, axis=-1, keepdims=True) + EPS)
        xh = xv * r
        dxh = dhv * g_ref[...]
        dx_ref[...] = dr_ref[...] + r * (dxh - xh * jnp.mean(dxh * xh, axis=-1, keepdims=True))
        _accum(dg_ref, pl.program_id(0), jnp.sum(dhv * xh, axis=0, keepdims=True))

    a_spec = (pl.BlockSpec((tm, a.shape[1]), lambda i: (i, 0)) if a.ndim == 2
              else pl.BlockSpec((2, tm, a.shape[2]), lambda i: (0, i, 0)))
    row = pl.BlockSpec((tm, Kw), lambda i: (i, 0))
    if norm is None:
        return pl.pallas_call(
            body, name=name, grid=(M // tm,), in_specs=[a_spec, _whole(w)], out_specs=row,
            out_shape=_sds((M, Kw), F32), compiler_params=_cp("parallel"))(a, w)
    vec = pl.BlockSpec((1, Kw), lambda i: (0, 0))
    return pl.pallas_call(
        body_norm, name=name, grid=(M // tm,), in_specs=[a_spec, _whole(w), row, row, vec], out_specs=[row, vec],
        out_shape=[_sds((M, Kw), F32), _sds((1, Kw), F32)], compiler_params=_cp("arbitrary"))(a, w, *norm)


def _mm_tn(a, b, *, tk, tn, name, shards=None, tm=None):
    M, K = a.shape
    N = _a_cols(b)
    tm = min(M, tm or _ROW_TILE)

    def body(a_ref, b_ref, o_ref):
        m = pl.program_id(2)
        part = _dot_tn(a_ref[...].astype(BF16), b_ref[...].astype(BF16))

        @pl.when(m == 0)
        def _():
            o_ref[...] = part

        @pl.when(m > 0)
        def _():
            o_ref[...] += part

    if shards is None:
        out_spec = pl.BlockSpec((tk, tn), lambda k, j, m: (k, j))
        out_shape = _sds((K, N), F32)
    else:
        bps = (N // shards) // tn
        out_spec = pl.BlockSpec((None, tk, tn), lambda k, j, m: (j // bps, k, j % bps))
        out_shape = _sds((shards, K, N // shards), F32)
    return pl.pallas_call(
        body, name=name, grid=(K // tk, N // tn, M // tm),
        in_specs=[pl.BlockSpec((tm, tk), lambda k, j, m: (m, k)),
                  _a_spec(b, tm, tn, lambda k, j, m: m, lambda k, j, m: j)],
        out_specs=out_spec, out_shape=out_shape,
        compiler_params=_cp("parallel", "parallel", "arbitrary"))(a, b)


def _halo_specs(T, tm, cw, ic):
    nb = tm // HALO
    last = T // HALO - 1

    def mk(rows, row_of):
        return pl.BlockSpec((rows, cw), lambda *g: (row_of(ic(*g)[0]), ic(*g)[1]))

    return [mk(HALO, lambda i: jnp.maximum(i * nb - 1, 0)), mk(tm, lambda i: i),
            mk(HALO, lambda i: jnp.minimum((i + 1) * nb, last))]


def _ext(prev_ref, cur_ref, next_ref, i, n):
    p = jnp.where(i > 0, prev_ref[...].astype(F32), 0.0)
    nx = jnp.where(i < n - 1, next_ref[...].astype(F32), 0.0)
    return jnp.concatenate([p, cur_ref[...].astype(F32), nx], axis=0)


def _dn(e):
    return pltpu.roll(e, 1, 0)


def _up(e):
    return pltpu.roll(e, e.shape[0] - 1, 0)


def _mid(e):
    return e[HALO:e.shape[0] - HALO]


def _taps(e):
    return _dn(e), e, _up(e)


def _conv3(taps, w):
    return taps[0] * w[0:1] + taps[1] * w[1:2] + taps[2] * w[2:3]


def _conv3_t(e, w):
    return _up(e) * w[0:1] + e * w[1:2] + _dn(e) * w[2:3]


def _conv3_wgrad(d, taps):
    return jnp.concatenate([jnp.sum(_mid(d * tap), axis=0, keepdims=True) for tap in taps], axis=0)


def _sigmoid(x):
    return 1.0 / (1.0 + jnp.exp(-x))


def _accum(ref, i, part):
    @pl.when(i == 0)
    def _():
        ref[...] = part

    @pl.when(i > 0)
    def _():
        ref[...] += part


def _ffn_act_fwd(up, cw):
    T = up.shape[0]
    tm = min(T, _FFN_ROW_TILE)
    cb = D_FF // 2
    nblk = D_FF // cb
    n = T // tm

    def body(gp, gc, gn, vp, vc, vn, wg_ref, wv_ref, o_ref):
        i = pl.program_id(1)
        gate = _conv3(_taps(_ext(gp, gc, gn, i, n)), wg_ref[...])
        val = _conv3(_taps(_ext(vp, vc, vn, i, n)), wv_ref[...])
        o_ref[...] = _mid(gate * _sigmoid(gate) * val).astype(BF16)

    return pl.pallas_call(
        body, name="ffn_act_fwd", grid=(nblk, n),
        in_specs=_halo_specs(T, tm, cb, lambda j, i: (i, j)) + _halo_specs(T, tm, cb, lambda j, i: (i, j + nblk))
        + [pl.BlockSpec((3, cb), lambda j, i: (0, j)), pl.BlockSpec((3, cb), lambda j, i: (0, j + nblk))],
        out_specs=pl.BlockSpec((tm, cb), lambda j, i: (i, j)),
        out_shape=_sds((T, D_FF), BF16), compiler_params=_cp("parallel", "parallel"))(
            up, up, up, up, up, up, cw, cw)


def _ffn_act_bwd(up, dact, cw):
    T = up.shape[0]
    tm = min(T, _FFN_ROW_TILE)
    cb = D_FF // 2
    nblk = D_FF // cb
    n = T // tm

    def body(gp, gc, gn, vp, vc, vn, dp_, dc, dn_, wg_ref, wv_ref, dup_ref, dcw_ref):
        i = pl.program_id(1)
        wg, wv = wg_ref[...], wv_ref[...]
        eg = _taps(_ext(gp, gc, gn, i, n))
        ev = _taps(_ext(vp, vc, vn, i, n))
        ed = _ext(dp_, dc, dn_, i, n)
        gate = _conv3(eg, wg)
        val = _conv3(ev, wv)
        sg = _sigmoid(gate)
        d_gate = ed * val * (sg * (1.0 + gate * (1.0 - sg)))
        d_val = ed * (gate * sg)
        dup_ref[0] = _mid(_conv3_t(d_gate, wg)).astype(BF16)
        dup_ref[1] = _mid(_conv3_t(d_val, wv)).astype(BF16)
        part = jnp.stack([_conv3_wgrad(d_gate, eg), _conv3_wgrad(d_val, ev)], axis=0)
        _accum(dcw_ref, i, part)

    return pl.pallas_call(
        body, name="ffn_act_bwd", grid=(nblk, n),
        in_specs=_halo_specs(T, tm, cb, lambda j, i: (i, j)) + _halo_specs(T, tm, cb, lambda j, i: (i, j + nblk))
        + _halo_specs(T, tm, cb, lambda j, i: (i, j))
        + [pl.BlockSpec((3, cb), lambda j, i: (0, j)), pl.BlockSpec((3, cb), lambda j, i: (0, j + nblk))],
        out_specs=[pl.BlockSpec((2, tm, cb), lambda j, i: (0, i, j)),
                   pl.BlockSpec((2, 3, cb), lambda j, i: (0, 0, j))],
        out_shape=[_sds((2, T, D_FF), BF16), _sds((2, 3, D_FF), F32)],
        compiler_params=_cp("parallel", "arbitrary"))(up, up, up, up, up, up, dact, dact, dact, cw, cw)


_CB_BLK, _CC_BLK, _CX_BLK = 5, 6, 7


def _convmix_fwd(p, w):
    T = p.shape[0]
    tm = min(T, _ROW_TILE)
    n = T // tm

    def body(cb_ref, ccp, ccc, ccn, cxp, cxc, cxn, w_ref, o_ref):
        i = pl.program_id(0)
        z = _ext(ccp, ccc, ccn, i, n) * _ext(cxp, cxc, cxn, i, n)
        o_ref[...] = (cb_ref[...] * _mid(_conv3(_taps(z), w_ref[...]))).astype(BF16)

    return pl.pallas_call(
        body, name="convmix_fwd", grid=(n,),
        in_specs=[pl.BlockSpec((tm, CONV_W), lambda i: (i, _CB_BLK))]
        + _halo_specs(T, tm, CONV_W, lambda i: (i, _CC_BLK)) + _halo_specs(T, tm, CONV_W, lambda i: (i, _CX_BLK))
        + [pl.BlockSpec((3, CONV_W), lambda i: (0, 0))],
        out_specs=pl.BlockSpec((tm, CONV_W), lambda i: (i, 0)),
        out_shape=_sds((T, CONV_W), BF16), compiler_params=_cp("parallel"))(p, p, p, p, p, p, p, w)


def _convmix_bwd(p, dmix, w):
    T = p.shape[0]
    tm = min(T, _ROW_TILE)
    n = T // tm
    dblk = (ATTN_W + SG_W) // CONV_W

    def body(cbp, cbc, cbn, ccp, ccc, ccn, cxp, cxc, cxn, dp_, dc, dn_, w_ref, o_ref, dw_ref):
        i = pl.program_id(0)
        wv = w_ref[...]
        ecb = _ext(cbp, cbc, cbn, i, n)
        ecc = _ext(ccp, ccc, ccn, i, n)
        ecx = _ext(cxp, cxc, cxn, i, n)
        ed = _ext(dp_, dc, dn_, i, n)
        z = _taps(ecc * ecx)
        d_cz = ed * ecb
        d_z = _conv3_t(d_cz, wv)
        o_ref[...] = jnp.concatenate([_mid(ed * _conv3(z, wv)), _mid(d_z * ecx), _mid(d_z * ecc)],
                                     axis=1).astype(BF16)
        _accum(dw_ref, i, _conv3_wgrad(d_cz, z))

    return pl.pallas_call(
        body, name="convmix_bwd", grid=(n,),
        in_specs=_halo_specs(T, tm, CONV_W, lambda i: (i, _CB_BLK)) + _halo_specs(T, tm, CONV_W, lambda i: (i, _CC_BLK))
        + _halo_specs(T, tm, CONV_W, lambda i: (i, _CX_BLK)) + _halo_specs(T, tm, CONV_W, lambda i: (i, dblk))
        + [pl.BlockSpec((3, CONV_W), lambda i: (0, 0))],
        out_specs=[pl.BlockSpec((tm, 3 * CONV_W), lambda i: (i, 0)), pl.BlockSpec((3, CONV_W), lambda i: (0, 0))],
        out_shape=[_sds((T, 3 * CONV_W), BF16), _sds((3, CONV_W), F32)],
        compiler_params=_cp("arbitrary"))(p, p, p, p, p, p, p, p, p, dmix, dmix, dmix, w)


_SU_BLK, _SV_BLK = 3, 4


def _sg_mixed(vnb, w_ref, bias, ch, pr, lo):
    vp = vnb[ch * SG_CHUNK:(ch + 1) * SG_CHUNK, pr * LANES:(pr + 1) * LANES]
    zero = jnp.zeros_like(vp)
    return (_dot(w_ref[2 * pr], jnp.where(lo, vp, zero)) + _dot(w_ref[2 * pr + 1], jnp.where(lo, zero, vp))
            + bias[:, pr * LANES:(pr + 1) * LANES]), vp


def _sg_fwd(p, g, w, bias):
    T = p.shape[0]
    tm = min(T, _ROW_TILE)

    def body(su_ref, sv_ref, g_ref, w_ref, b_ref, o_ref):
        lo = lax.broadcasted_iota(jnp.int32, (SG_CHUNK, LANES), 1) < HEAD_DIM
        sv = sv_ref[...]
        r = lax.rsqrt(jnp.mean(sv * sv, axis=-1, keepdims=True) + EPS)
        vnb = ((sv * r) * g_ref[...]).astype(BF16)
        bias_v = b_ref[...]
        for ch in range(tm // SG_CHUNK):
            for pr in range(2):
                mixed, _ = _sg_mixed(vnb, w_ref, bias_v, ch, pr, lo)
                rows, cols = pl.ds(ch * SG_CHUNK, SG_CHUNK), pl.ds(pr * LANES, LANES)
                o_ref[rows, cols] = (su_ref[rows, cols] * mixed).astype(BF16)

    return pl.pallas_call(
        body, name="sg_fwd", grid=(T // tm,),
        in_specs=[pl.BlockSpec((tm, SG_W), lambda i: (i, _SU_BLK)), pl.BlockSpec((tm, SG_W), lambda i: (i, _SV_BLK)),
                  pl.BlockSpec((1, SG_W), lambda i: (0, 0)), pl.BlockSpec((4, SG_CHUNK, SG_CHUNK), lambda i: (0, 0, 0)),
                  pl.BlockSpec((SG_CHUNK, SG_W), lambda i: (0, 0))],
        out_specs=pl.BlockSpec((tm, SG_W), lambda i: (i, 0)),
        out_shape=_sds((T, SG_W), BF16), compiler_params=_cp("parallel"))(p, p, g, w, bias)


def _sg_bwd(p, dmix, g, w, wt, bias):
    T = p.shape[0]
    tm = min(T, _ROW_TILE)
    dblk = ATTN_W // SG_W

    def body(su_ref, sv_ref, d_ref, g_ref, w_ref, wt_ref, b_ref, o_ref, dw_ref, db_ref, dg_ref, dvn_ref):
        i = pl.program_id(0)
        lo = lax.broadcasted_iota(jnp.int32, (SG_CHUNK, LANES), 1) < HEAD_DIM
        sv = sv_ref[...]
        gv = g_ref[...]
        r = lax.rsqrt(jnp.mean(sv * sv, axis=-1, keepdims=True) + EPS)
        xh = sv * r
        vnb = (xh * gv).astype(BF16)
        bias_v = b_ref[...]
        dw = [jnp.zeros((SG_CHUNK, SG_CHUNK), F32) for _ in range(4)]
        db = jnp.zeros((SG_CHUNK, SG_W), F32)
        for ch in range(tm // SG_CHUNK):
            dbs = []
            for pr in range(2):
                mixed, vp = _sg_mixed(vnb, w_ref, bias_v, ch, pr, lo)
                rows, cols = pl.ds(ch * SG_CHUNK, SG_CHUNK), pl.ds(pr * LANES, LANES)
                dgo = d_ref[rows, cols]
                o_ref[rows, cols] = (dgo * mixed).astype(BF16)
                dm = dgo * su_ref[rows, cols]
                dmb = dm.astype(BF16)
                zero = jnp.zeros_like(dmb)
                dw[2 * pr] += _dot_nt(jnp.where(lo, dmb, zero), vp)
                dw[2 * pr + 1] += _dot_nt(jnp.where(lo, zero, dmb), vp)
                dvn_ref[rows, cols] = jnp.where(lo, _dot(wt_ref[2 * pr], dmb), _dot(wt_ref[2 * pr + 1], dmb))
                dbs.append(dm)
            db += jnp.concatenate(dbs, axis=1)
        dvn = dvn_ref[...]
        dxh = dvn * gv
        o_ref[:, pl.ds(SG_W, SG_W)] = (r * (dxh - xh * jnp.mean(dxh * xh, axis=-1, keepdims=True))).astype(BF16)
        _accum(dw_ref, i, jnp.stack(dw, axis=0))
        _accum(db_ref, i, db)
        _accum(dg_ref, i, jnp.sum(dvn * xh, axis=0, keepdims=True))

    wspec = pl.BlockSpec((4, SG_CHUNK, SG_CHUNK), lambda i: (0, 0, 0))
    return pl.pallas_call(
        body, name="sg_bwd", grid=(T // tm,),
        in_specs=[pl.BlockSpec((tm, SG_W), lambda i: (i, _SU_BLK)), pl.BlockSpec((tm, SG_W), lambda i: (i, _SV_BLK)),
                  pl.BlockSpec((tm, SG_W), lambda i: (i, dblk)), pl.BlockSpec((1, SG_W), lambda i: (0, 0)),
                  wspec, wspec, pl.BlockSpec((SG_CHUNK, SG_W), lambda i: (0, 0))],
        out_specs=[pl.BlockSpec((tm, 2 * SG_W), lambda i: (i, 0)), wspec,
                   pl.BlockSpec((SG_CHUNK, SG_W), lambda i: (0, 0)), pl.BlockSpec((1, SG_W), lambda i: (0, 0))],
        out_shape=[_sds((T, 2 * SG_W), BF16), _sds((4, SG_CHUNK, SG_CHUNK), F32),
                   _sds((SG_CHUNK, SG_W), F32), _sds((1, SG_W), F32)],
        scratch_shapes=[pltpu.VMEM((tm, SG_W), F32)],
        compiler_params=_cp("arbitrary"))(p, p, dmix, g, w, wt, bias)


_ATTN_TQ = 256
_ATTN_TK = 1024
_ATTN_FWD_TK = 512
_SOFTMAX_STRIP = 32
_ONES_ROWS = 16


def _head_mean(v, bmat):
    hi = v.astype(BF16)
    lo = (v - hi.astype(F32)).astype(BF16)
    return _dot(hi, bmat) + _dot(lo, bmat)


def _swap16(y):
    lane = lax.broadcasted_iota(jnp.int32, y.shape, 1)
    return jnp.where(lane % 32 < 16, pltpu.roll(y, y.shape[1] - 16, 1), pltpu.roll(y, 16, 1))


def _rope(y, cos, sin):
    return y * cos + _swap16(y) * sin


def _rope_t(dy, cos, sin):
    return dy * cos + _swap16(dy * sin)


def _dup_rows(t, gidx):
    h = t[gidx * HEAD_DIM:(gidx + 1) * HEAD_DIM]
    return jnp.concatenate([h, h], axis=0)


def _qk_prep(p, gq, gk, cos, sin, bmat, tk, tkf):
    T = p.shape[0]
    nk = T // tk
    sub = tk // tkf
    scale = HEAD_DIM ** -0.5

    def body(q_ref, kv_ref, gq_ref, gk_ref, cos_ref, sin_ref, b_ref, qo_ref, kt_ref, kd_ref, vt_ref, v1_ref):
        cosv, sinv, bm = cos_ref[...], sin_ref[...], b_ref[...]
        for pr in range(ATTN_W // LANES):
            cols = pl.ds(pr * LANES, LANES)
            xq = q_ref[:, cols]
            r = lax.rsqrt(_head_mean(xq * xq, bm) + EPS)
            qo_ref[:, cols] = (_rope((xq * r) * gq_ref[:, cols], cosv, sinv) * scale).astype(BF16)
        xk = kv_ref[:, pl.ds(0, LANES)]
        r = lax.rsqrt(_head_mean(xk * xk, bm) + EPS)
        kt = _rope((xk * r) * gk_ref[...], cosv, sinv).T
        vt = kv_ref[:, pl.ds(LANES, LANES)].T
        for gidx in range(2):
            kdup = _dup_rows(kt, gidx)
            kt_ref[gidx] = kdup.astype(BF16)
            vt_ref[gidx] = _dup_rows(vt, gidx).astype(BF16)
            v1 = jnp.concatenate([vt[gidx * HEAD_DIM:(gidx + 1) * HEAD_DIM],
                                  jnp.ones((_ONES_ROWS, tk), F32)], axis=0).astype(BF16)
            for b in range(sub):
                v1_ref[gidx, b] = v1[:, b * tkf:(b + 1) * tkf]
            kd_ref[gidx] = kdup.T.astype(BF16)

    tspec = pl.BlockSpec((2, None, LANES, tk), lambda i: (0, i, 0, 0))
    dspec = pl.BlockSpec((2, tk, LANES), lambda i: (0, i, 0))
    tab = pl.BlockSpec((tk, LANES), lambda i: (i, 0))
    return pl.pallas_call(
        body, name="qk_prep", grid=(nk,),
        in_specs=[pl.BlockSpec((tk, ATTN_W), lambda i: (i, 0)), pl.BlockSpec((tk, 2 * KV_W), lambda i: (i, ATTN_W // (2 * KV_W))),
                  pl.BlockSpec((1, ATTN_W), lambda i: (0, 0)), pl.BlockSpec((1, KV_W), lambda i: (0, 0)),
                  tab, tab, pl.BlockSpec((LANES, LANES), lambda i: (0, 0))],
        out_specs=[pl.BlockSpec((tk, ATTN_W), lambda i: (i, 0)), tspec, dspec, tspec,
                   pl.BlockSpec((2, sub, HEAD_DIM + _ONES_ROWS, tkf), lambda i: (0, i, 0, 0))],
        out_shape=[_sds((T, ATTN_W), BF16), _sds((2, nk, LANES, tk), BF16), _sds((2, T, LANES), BF16),
                   _sds((2, nk, LANES, tk), BF16), _sds((2, nk * sub, HEAD_DIM + _ONES_ROWS, tkf), BF16)],
        compiler_params=_cp("parallel"))(p, p, gq, gk, cos, sin, bmat)


def _stack_heads(t):
    lo = lax.broadcasted_iota(jnp.int32, (t.shape[0], LANES), 1) < HEAD_DIM
    parts = []
    for pr in range(2):
        tp = t[:, pr * LANES:(pr + 1) * LANES]
        zero = jnp.zeros_like(tp)
        parts += [jnp.where(lo, tp, zero), jnp.where(lo, zero, tp)]
    return jnp.concatenate(parts, axis=0)


def _rows8_reduce(s, op):
    parts = [s[r:r + 8] for r in range(0, s.shape[0], 8)]
    while len(parts) > 1:
        parts = [op(parts[k], parts[k + 1]) for k in range(0, len(parts) - 1, 2)] + (
            [parts[-1]] if len(parts) % 2 else [])
    return parts[0]


def _attn_fwd(q, kd, v1, tq):
    T = q.shape[0]
    nk, tk = v1.shape[1], v1.shape[3]
    vrows = v1.shape[2]
    nq = T // tq
    sq = 4 * tq
    strip = _SOFTMAX_STRIP
    depth = 4
    assert nk % depth == 0

    def body(q_ref, kd_ref, v1_ref, o_ref, lse_ref, qst_ref, s0_ref, s1_ref, s2_ref, s3_ref, pa_ref, pb_ref,
             m_ref, acc_ref):
        s_refs = (s0_ref, s1_ref, s2_ref, s3_ref)
        p_refs = (pa_ref, pb_ref)
        qst_ref[...] = _stack_heads(q_ref[...]).astype(F32).T.astype(BF16)
        m_ref[...] = jnp.full((1, sq), -jnp.inf, F32)
        acc_ref[...] = jnp.zeros((vrows, sq), F32)

        def scores(j):
            return _dot(kd_ref[pl.ds(pl.multiple_of(j * tk, tk), tk), :], qst_ref[...])

        def block_max(s_ref):
            m8 = None
            for c in range(tk // strip):
                part = _rows8_reduce(s_ref[pl.ds(c * strip, strip), :], jnp.maximum)
                m8 = part if m8 is None else jnp.maximum(m8, part)
            return m8

        def exp_pass(s_ref, p_ref, m8):
            m_old = m_ref[...]
            m_new = jnp.maximum(m_old, jnp.max(m8, axis=0, keepdims=True))
            m_ref[...] = m_new
            for c in range(tk // strip):
                rows = pl.ds(c * strip, strip)
                p_ref[rows, :] = jnp.exp(s_ref[rows, :] - m_new).astype(BF16)
            return jnp.exp(m_old - m_new)

        def apply(p_ref, alpha, j):
            acc_ref[...] = alpha * acc_ref[...] + _dot(v1_ref[j], p_ref[...])

        s_refs[0][...] = scores(0)
        s_refs[1][...] = scores(1)

        def trip(t, max_cur):
            for u in range(depth):
                j = depth * t + u
                s_refs[(u + 2) % depth][...] = scores(jnp.minimum(j + 2, nk - 1))
                alpha = exp_pass(s_refs[u], p_refs[u % 2], max_cur)
                max_cur = block_max(s_refs[(u + 1) % depth])
                apply(p_refs[u % 2], alpha, j)
            return max_cur

        lax.fori_loop(0, nk // depth, trip, block_max(s_refs[0]))
        l = acc_ref[pl.ds(HEAD_DIM, 1), :]
        on = acc_ref[pl.ds(0, HEAD_DIM), :] / l
        pairs = []
        for pr in range(2):
            two = jnp.concatenate([on[:, (2 * pr) * tq:(2 * pr + 1) * tq], on[:, (2 * pr + 1) * tq:(2 * pr + 2) * tq]],
                                  axis=0)
            pairs.append(two.T)
        o_ref[...] = jnp.concatenate(pairs, axis=1).astype(BF16)
        lse_ref[...] = jnp.broadcast_to(m_ref[...] + jnp.log(l), (LANES, sq)).T

    row = pltpu.VMEM((1, sq), F32)
    return pl.pallas_call(
        body, name="attn_fwd", grid=(2, nq),
        in_specs=[pl.BlockSpec((tq, 2 * LANES), lambda g, i: (i, g)),
                  pl.BlockSpec((None, T, LANES), lambda g, i: (g, 0, 0)),
                  pl.BlockSpec((None, nk, vrows, tk), lambda g, i: (g, 0, 0, 0))],
        out_specs=[pl.BlockSpec((tq, 2 * LANES), lambda g, i: (i, g)),
                   pl.BlockSpec((None, None, sq, LANES), lambda g, i: (g, i, 0, 0))],
        out_shape=[_sds((T, ATTN_W), BF16), _sds((2, nq, sq, LANES), F32)],
        scratch_shapes=[pltpu.VMEM((LANES, sq), BF16)] + [pltpu.VMEM((tk, sq), F32)] * depth
        + [pltpu.VMEM((tk, sq), BF16), pltpu.VMEM((tk, sq), BF16), row, pltpu.VMEM((vrows, sq), F32)],
        compiler_params=_cp("parallel", "parallel"))(q, kd, v1)


def _attn_bwd(q, o, dmix, lse, kt, kd, vt, tq):
    T = q.shape[0]
    nk, tk = kt.shape[1], kt.shape[3]
    nq = T // tq
    sq = 4 * tq
    rep = tk // LANES

    def body(q_ref, o_ref, do_ref, lse_ref, kt_ref, kd_ref, vt_ref, dq_ref, dkt_ref, dvt_ref):
        i = pl.program_id(1)
        qs = _stack_heads(q_ref[...])
        dof = _stack_heads(do_ref[...])
        dos = dof.astype(BF16)
        qst = qs.astype(F32).T.astype(BF16)
        dost = dof.T.astype(BF16)
        o_pair = o_ref[...].astype(F32)
        os_ = jnp.concatenate([o_pair[:, 0:LANES], o_pair[:, 0:LANES], o_pair[:, LANES:], o_pair[:, LANES:]], axis=0)
        delta = jnp.sum(dof * os_, axis=-1, keepdims=True)
        lse_t = jnp.concatenate([lse_ref[...]] * rep, axis=1)

        @pl.when(i == 0)
        def _():
            dkt_ref[...] = jnp.zeros_like(dkt_ref)
            dvt_ref[...] = jnp.zeros_like(dvt_ref)

        def step(j, dq):
            kdb = kd_ref[pl.ds(pl.multiple_of(j * tk, tk), tk), :]
            pexp = jnp.exp(_dot(qs, kt_ref[j]) - lse_t)
            ds = pexp * (_dot(dos, vt_ref[j]) - delta)
            pb = pexp.astype(BF16)
            dsb = ds.astype(BF16)
            dvt_ref[j] += _dot(dost, pb)
            dkt_ref[j] += _dot(qst, dsb)
            return dq + _dot(dsb, kdb)

        dq = lax.fori_loop(0, nk, step, jnp.zeros((sq, LANES), F32))
        lo = lax.broadcasted_iota(jnp.int32, (tq, LANES), 1) < HEAD_DIM
        dq_ref[...] = jnp.concatenate([jnp.where(lo, dq[0:tq], dq[tq:2 * tq]),
                                       jnp.where(lo, dq[2 * tq:3 * tq], dq[3 * tq:4 * tq])], axis=1)

    tspec = pl.BlockSpec((None, nk, LANES, tk), lambda g, i: (g, 0, 0, 0))
    qspec = pl.BlockSpec((tq, 2 * LANES), lambda g, i: (i, g))
    return pl.pallas_call(
        body, name="attn_bwd", grid=(2, nq),
        in_specs=[qspec, qspec, qspec, pl.BlockSpec((None, None, sq, LANES), lambda g, i: (g, i, 0, 0)),
                  tspec, pl.BlockSpec((None, T, LANES), lambda g, i: (g, 0, 0)), tspec],
        out_specs=[qspec, tspec, tspec],
        out_shape=[_sds((T, ATTN_W), F32), _sds((2, nk, LANES, tk), F32), _sds((2, nk, LANES, tk), F32)],
        compiler_params=_cp("parallel", "arbitrary"))(q, o, dmix, lse, kt, kd, vt)


def _fold_t(t_ref):
    rows = []
    for gidx in range(2):
        t = t_ref[gidx]
        rows.append(t[0:HEAD_DIM] + t[HEAD_DIM:2 * HEAD_DIM])
    return jnp.concatenate(rows, axis=0).T


def _qk_bwd(p, dq, dkt, dvt, gq, gk, cos, sin, bmat):
    T = p.shape[0]
    nk, tk = dkt.shape[1], dkt.shape[3]
    scale = HEAD_DIM ** -0.5

    def norm_bwd(x, dy, gain, bm):
        r = lax.rsqrt(_head_mean(x * x, bm) + EPS)
        xh = x * r
        dxh = dy * gain
        return r * (dxh - xh * _head_mean(dxh * xh, bm)), jnp.sum(dy * xh, axis=0, keepdims=True)

    def body(q_ref, kv_ref, dq_ref, dkt_ref, dvt_ref, gq_ref, gk_ref, cos_ref, sin_ref, b_ref, o_ref, dgq_ref, dgk_ref):
        i = pl.program_id(0)
        cosv, sinv, bm = cos_ref[...], sin_ref[...], b_ref[...]
        dgq = []
        for pr in range(ATTN_W // LANES):
            cols = pl.ds(pr * LANES, LANES)
            dy = _rope_t(dq_ref[:, cols] * scale, cosv, sinv)
            dx, dg = norm_bwd(q_ref[:, cols], dy, gq_ref[:, cols], bm)
            o_ref[:, cols] = dx.astype(BF16)
            dgq.append(dg)
        dy = _rope_t(_fold_t(dkt_ref), cosv, sinv)
        dx, dgk = norm_bwd(kv_ref[:, pl.ds(0, LANES)], dy, gk_ref[...], bm)
        o_ref[:, pl.ds(ATTN_W, LANES)] = dx.astype(BF16)
        o_ref[:, pl.ds(ATTN_W + LANES, LANES)] = _fold_t(dvt_ref).astype(BF16)
        _accum(dgq_ref, i, jnp.concatenate(dgq, axis=1))
        _accum(dgk_ref, i, dgk)

    tspec = pl.BlockSpec((2, None, LANES, tk), lambda i: (0, i, 0, 0))
    tab = pl.BlockSpec((tk, LANES), lambda i: (i, 0))
    return pl.pallas_call(
        body, name="qk_bwd", grid=(nk,),
        in_specs=[pl.BlockSpec((tk, ATTN_W), lambda i: (i, 0)), pl.BlockSpec((tk, 2 * KV_W), lambda i: (i, ATTN_W // (2 * KV_W))),
                  pl.BlockSpec((tk, ATTN_W), lambda i: (i, 0)), tspec, tspec,
                  pl.BlockSpec((1, ATTN_W), lambda i: (0, 0)), pl.BlockSpec((1, KV_W), lambda i: (0, 0)),
                  tab, tab, pl.BlockSpec((LANES, LANES), lambda i: (0, 0))],
        out_specs=[pl.BlockSpec((tk, ATTN_W + 2 * KV_W), lambda i: (i, 0)),
                   pl.BlockSpec((1, ATTN_W), lambda i: (0, 0)), pl.BlockSpec((1, KV_W), lambda i: (0, 0))],
        out_shape=[_sds((T, ATTN_W + 2 * KV_W), BF16), _sds((1, ATTN_W), F32), _sds((1, KV_W), F32)],
        compiler_params=_cp("arbitrary"))(p, p, dq, dkt, dvt, gq, gk, cos, sin, bmat)


def _loss_head(y, target):
    T, Dm = y.shape
    tm = min(T, _ROW_TILE)

    def body(y_ref, t_ref, dy_ref, l_ref):
        i = pl.program_id(0)
        err = y_ref[...] - t_ref[...]
        dy_ref[...] = err * (1.0 / Dm)
        part = jnp.sum(jnp.sum(err * err, axis=-1, keepdims=True), axis=0, keepdims=True) * (0.5 / Dm)
        _accum(l_ref, i, jnp.broadcast_to(part, (8, LANES)))

    row = pl.BlockSpec((tm, Dm), lambda i: (i, 0))
    return pl.pallas_call(
        body, name="loss_head", grid=(T // tm,), in_specs=[row, row],
        out_specs=[row, pl.BlockSpec((8, LANES), lambda i: (0, 0))],
        out_shape=[_sds((T, Dm), F32), _sds((8, LANES), F32)], compiler_params=_cp("arbitrary"))(y, target)


def _adamw(w, g, m, v, name):
    R, C = w.shape
    tr = R
    for cand in (512, 256, 128, 64, 32, 16, 8):
        if R % cand == 0:
            tr = cand
            break
    c1 = 1.0 - ADAM_B1 ** ADAM_STEP
    c2 = 1.0 - ADAM_B2 ** ADAM_STEP

    def body(w_ref, g_ref, m_ref, v_ref, d_ref, mo_ref, vo_ref):
        gv = g_ref[...]
        mn = ADAM_B1 * m_ref[...] + (1.0 - ADAM_B1) * gv
        vn = ADAM_B2 * v_ref[...] + (1.0 - ADAM_B2) * (gv * gv)
        d_ref[...] = -ADAM_LR * ((mn / c1) / (jnp.sqrt(vn / c2) + ADAM_EPS) + ADAM_WD * w_ref[...])
        mo_ref[...] = mn
        vo_ref[...] = vn

    blk = pl.BlockSpec((tr, C), lambda i: (i, 0))
    return pl.pallas_call(
        body, name=name, grid=(R // tr,), in_specs=[blk] * 4, out_specs=[blk] * 3,
        out_shape=[_sds((R, C), F32)] * 3, compiler_params=_cp("parallel"))(w, g, m, v)


def _cast_bf16(w, name):
    R, C = w.shape
    tr = 512 if R % 512 == 0 else 256

    def body(w_ref, o_ref):
        o_ref[...] = w_ref[...].astype(BF16)

    blk = pl.BlockSpec((tr, C), lambda i: (i, 0))
    return pl.pallas_call(body, name=name, grid=(R // tr,), in_specs=[blk], out_specs=blk,
                          out_shape=_sds((R, C), BF16), compiler_params=_cp("parallel"))(w)


def _position():
    x, y, c = lax.axis_index("x"), lax.axis_index("y"), lax.axis_index("c")
    return x, y, c


def _other_chips(x, y):
    return [(1 - x, y), (x, 1 - y), (1 - x, 1 - y)]


_HBM = pl.BlockSpec(memory_space=pltpu.HBM)
_SEM = pl.BlockSpec(memory_space=pltpu.SEMAPHORE)
_EFFECT = pltpu.SideEffectType.DATAFLOW_SIDE_EFFECTING


def _chip_copies(srcs, lands, send_sems, recv_sems, per_chip, arriving):
    x, y, c = _position()
    me = 2 * x + y
    copies = []
    for t, (src, land) in enumerate(zip(srcs, lands)):
        for k, (px, py) in enumerate(_other_chips(x, y)):
            peer = 2 * px + py
            copies.append(pltpu.make_async_remote_copy(
                src_ref=src.at[peer] if per_chip else src, dst_ref=land.at[peer if arriving else me],
                send_sem=send_sems[3 * t + k], recv_sem=recv_sems[3 * t + k],
                device_id=(px, py, c), device_id_type=MESH))
    return copies


def _chips_start(srcs, per_chip, name):
    n = len(srcs)
    slab = [s.shape[1:] if per_chip else s.shape for s in srcs]
    lands = [lax.empty((N_CHIPS,) + sh, s.dtype) for sh, s in zip(slab, srcs)]

    ns = 3 * n

    def body(*refs):
        ins = refs[:2 * n]
        send_sems, recv_sems = refs[2 * n:2 * n + ns], refs[2 * n + ns:2 * n + 2 * ns]
        token = refs[-1]
        for cp in _chip_copies(ins[:n], ins[n:], send_sems, recv_sems, per_chip, False):
            cp.start()
        token[...] = jnp.zeros_like(token)

    args = [pltpu.with_memory_space_constraint(a, pltpu.HBM) for a in list(srcs) + lands]
    outs = pl.pallas_call(
        body, name=name,
        out_shape=[pltpu.SemaphoreType.DMA(())] * (2 * ns)
        + [pltpu.HBM(a.shape, a.dtype) for a in args] + [_sds((8, LANES), F32)],
        in_specs=[_HBM] * (2 * n),
        out_specs=[_SEM] * (2 * ns) + [_HBM] * (2 * n) + [pl.BlockSpec(memory_space=pltpu.VMEM)],
        input_output_aliases={i: 2 * ns + i for i in range(2 * n)},
        compiler_params=pltpu.CompilerParams(has_side_effects=_EFFECT))(*args)
    sems, rest = outs[:2 * ns], outs[2 * ns:]
    return sems[:ns], sems[ns:], rest[:n], rest[n:2 * n], rest[-1]


def _chips_wait(handle, after, per_chip, name):
    send_sems, recv_sems, srcs, lands, _ = handle
    n = len(srcs)
    ns = 3 * n

    def body(*refs):
        ins = refs[:2 * n]
        s_sems, r_sems = refs[2 * n:2 * n + ns], refs[2 * n + ns:2 * n + 2 * ns]
        for cp in _chip_copies(ins[:n], ins[n:], s_sems, r_sems, per_chip, False):
            cp.wait_send()
        for cp in _chip_copies(ins[:n], ins[n:], s_sems, r_sems, per_chip, True):
            cp.wait_recv()

    outs = pl.pallas_call(
        body, name=name,
        out_shape=[pltpu.HBM(a.shape, a.dtype) for a in list(srcs) + list(lands)],
        in_specs=[_HBM] * (2 * n) + [_SEM] * (2 * ns) + [ANY],
        out_specs=[_HBM] * (2 * n),
        input_output_aliases={i: i for i in range(2 * n)},
        compiler_params=pltpu.CompilerParams(has_side_effects=_EFFECT))(*srcs, *lands, *send_sems, *recv_sems, after)
    return outs[:n], outs[n:]


def _sum_chips_own(land, own, name):
    S, R, C = land.shape
    tr = R
    for cand in (256, 128, 64, 32, 16, 8):
        if R % cand == 0:
            tr = cand
            break

    def body(l_ref, o_ref, out_ref):
        x, y, _ = _position()
        me = 2 * x + y
        mine = o_ref[me] if own.ndim == 3 else o_ref[...]
        acc = None
        for k in range(S):
            part = jnp.where(me == k, mine, l_ref[k])
            acc = part if acc is None else acc + part
        out_ref[...] = acc

    blk = pl.BlockSpec((S, tr, C), lambda i: (0, i, 0))
    row = pl.BlockSpec((tr, C), lambda i: (i, 0))
    return pl.pallas_call(
        body, name=name, grid=(R // tr,), in_specs=[blk, blk if own.ndim == 3 else row], out_specs=row,
        out_shape=_sds((R, C), F32), compiler_params=_cp("parallel"))(land, own)


def _add_pair(a, b, name):
    R, C = a.shape

    def body(a_ref, b_ref, o_ref):
        o_ref[...] = a_ref[...] + b_ref[...]

    blk = pl.BlockSpec((R, C), lambda: (0, 0))
    return pl.pallas_call(body, name=name, in_specs=[blk, blk], out_specs=blk, out_shape=_sds((R, C), F32))(a, b)


def _exchange_sibling(arrays):
    n = len(arrays)

    def body(*refs):
        ins, outs = refs[:n], refs[n:2 * n]
        send_sems, recv_sems = refs[2 * n:]
        x, y, c = _position()
        sends = []
        for t in range(n):
            cp = pltpu.make_async_remote_copy(src_ref=ins[t], dst_ref=outs[t], send_sem=send_sems.at[t],
                                              recv_sem=recv_sems.at[t], device_id=(x, y, 1 - c), device_id_type=MESH)
            cp.start()
            sends.append(cp)
        for cp in sends:
            cp.wait()

    return pl.pallas_call(
        body, name="grads_to_sibling",
        in_specs=[ANY] * n, out_specs=[ANY] * n, out_shape=[_sds(a.shape, a.dtype) for a in arrays],
        scratch_shapes=[pltpu.SemaphoreType.DMA((n,)), pltpu.SemaphoreType.DMA((n,))],
        compiler_params=pltpu.CompilerParams(has_side_effects=True))(*arrays)


def _adamw_sum(w, ga, gb, m, v, name):
    R, C = w.shape
    tr = next(t for t in (512, 256, 128, 64) if R % t == 0 and t * C * 4 <= (1 << 20))
    c1 = 1.0 - ADAM_B1 ** ADAM_STEP
    c2 = 1.0 - ADAM_B2 ** ADAM_STEP

    def body(w_ref, ga_ref, gb_ref, m_ref, v_ref, g_ref, d_ref, mo_ref, vo_ref):
        gv = ga_ref[...] + gb_ref[...]
        mn = ADAM_B1 * m_ref[...] + (1.0 - ADAM_B1) * gv
        vn = ADAM_B2 * v_ref[...] + (1.0 - ADAM_B2) * (gv * gv)
        g_ref[...] = gv
        d_ref[...] = -ADAM_LR * ((mn / c1) / (jnp.sqrt(vn / c2) + ADAM_EPS) + ADAM_WD * w_ref[...])
        mo_ref[...] = mn
        vo_ref[...] = vn

    blk = pl.BlockSpec((tr, C), lambda i: (i, 0))
    return pl.pallas_call(
        body, name=name, grid=(R // tr,), in_specs=[blk] * 5, out_specs=[blk] * 4,
        out_shape=[_sds((R, C), F32)] * 4, compiler_params=_cp("parallel"))(w, ga, gb, m, v)


def _rope_tables(T):
    pos = jnp.arange(T)
    row = (pos // GRID_W).astype(F32)
    col = (pos % GRID_W).astype(F32)
    inv = 1.0 / (ROPE_THETA ** (jnp.arange(AXIS_DIM // 2, dtype=F32) * 2.0 / AXIS_DIM))
    ar, ac = row[:, None] * inv[None, :], col[:, None] * inv[None, :]
    cos = jnp.concatenate([jnp.cos(ar), jnp.cos(ar), jnp.cos(ac), jnp.cos(ac)], axis=-1)
    sin = jnp.concatenate([-jnp.sin(ar), jnp.sin(ar), -jnp.sin(ac), jnp.sin(ac)], axis=-1)
    return jnp.tile(cos, (1, LANES // HEAD_DIM)), jnp.tile(sin, (1, LANES // HEAD_DIM))


def _head_mean_matrix():
    h = jnp.arange(LANES) // HEAD_DIM
    return jnp.where(h[:, None] == h[None, :], 1.0 / HEAD_DIM, 0.0).astype(BF16)


def _pack(arrays):
    flat = jnp.concatenate([a.reshape(-1) for a in arrays])
    rows = -(-flat.shape[0] // LANES)
    rows = -(-rows // 256) * 256
    return jnp.pad(flat, (0, rows * LANES - flat.shape[0])).reshape(rows, LANES)


def _unpack(packed, like):
    flat = packed.reshape(-1)
    out, off = [], 0
    for a in like:
        out.append(flat[off:off + a.size].reshape(a.shape))
        off += a.size
    return out


def _layer_fwd(x, h, lw, consts, ffn_weights, next_g):
    cos, sin, bmat = consts
    T = x.shape[0]
    tk = min(T, _ATTN_TK)
    tq = min(T, _ATTN_TQ)
    if h is None:
        h = _norm_fwd(x, lw["norm1_g"])
    p = _mm_nn(h, lw["w_in"], out_dtype=F32, name="mm_p")
    qn, kt, kd, vt, v1 = _qk_prep(p, lw["gq"], lw["gk"], cos, sin, bmat, tk, min(tk, _ATTN_FWD_TK))
    o, lse = _attn_fwd(qn, kd, v1, tq)
    go = _sg_fwd(p, lw["sg_norm_g"], lw["sg_w"], lw["sg_bias"])
    co = _convmix_fwd(p, lw["conv_w"])
    mix = jnp.concatenate([o, go, co], axis=1)
    x_mid, h2 = _mm_nn(mix, lw["w_out"], out_dtype=F32, name="mm_out", res=x, norm_g=lw["norm2_g"])
    lw.update(ffn_weights(x_mid))
    up = _mm_nn(h2, lw["w_up"], out_dtype=F32, name="mm_up", tm=_FFN_ROW_TILE)
    act = _ffn_act_fwd(up, lw["ffn_conv_w"])
    if next_g is None:
        x_out, h_next = _mm_nn(act, lw["w_down"], out_dtype=F32, name="mm_down_last", res=x_mid), None
    else:
        x_out, h_next = _mm_nn(act, lw["w_down"], out_dtype=F32, name="mm_down", res=x_mid, norm_g=next_g)
    saved = dict(x=x, h=h, p=p, qn=qn, kt=kt, kd=kd, vt=vt, o=o, lse=lse, mix=mix, x_mid=x_mid, h2=h2, up=up, act=act)
    return x_out, h_next, saved


def _layer_bwd(dx, s, lw, consts, send):
    cos, sin, bmat = consts
    T = dx.shape[0]
    tq = min(T, _ATTN_TQ)
    g = {}
    d_act = _mm_nt(dx, lw["w_down"], name="mm_dact")
    g_down = _mm_tn(s["act"], dx, tk=D_FF // 2, tn=D_MODEL, name="mm_dwdown", tm=_WGRAD_ROWS // 2)
    tok = send("w_down", g_down.reshape(N_CHIPS, D_FF // N_CHIPS, D_MODEL))
    d_up, d_cw = _ffn_act_bwd(s["up"], d_act, lw["ffn_conv_w"] + tok)
    g["ffn_conv_w"] = d_cw.transpose(1, 0, 2).reshape(3, 2 * D_FF)
    tok = send("w_up", _mm_tn(s["h2"], d_up, tk=512, tn=D_FF // 2, name="mm_dwup", shards=N_CHIPS, tm=_WGRAD_ROWS))
    dx2, g["norm2_g"] = _mm_nt(d_up, lw["w_up"], name="mm_dh2", norm=(s["x_mid"], dx, lw["norm2_g"] + tok))
    d_mix = _mm_nt(dx2, lw["w_out"], name="mm_dmix")
    g_out = _mm_tn(s["mix"], dx2, tk=512, tn=D_MODEL, name="mm_dwout", tm=_WGRAD_ROWS // 2)
    tok = send("w_out", g_out.reshape(N_CHIPS, D_MODEL // N_CHIPS, D_MODEL))
    dp_c, g["conv_w"] = _convmix_bwd(s["p"], d_mix, lw["conv_w"] + tok)
    dp_b, g["sg_w"], d_bias, g["sg_norm_g"] = _sg_bwd(s["p"], d_mix, lw["sg_norm_g"], lw["sg_w"], lw["sg_wt"], lw["sg_bias"])
    g["sg_b"] = d_bias.reshape(SG_CHUNK, SG_W // HEAD_DIM, HEAD_DIM).sum(axis=-1).T
    dq, dkt, dvt = _attn_bwd(s["qn"], s["o"], d_mix, s["lse"], s["kt"], s["kd"], s["vt"], tq)
    dp_a, d_gq, d_gk = _qk_bwd(s["p"], dq, dkt, dvt, lw["gq"], lw["gk"], cos, sin, bmat)
    g["q_norm_g"] = d_gq.reshape(ATTN_W // HEAD_DIM, HEAD_DIM).sum(axis=0)
    g["k_norm_g"] = d_gk.reshape(KV_W // HEAD_DIM, HEAD_DIM).sum(axis=0)
    dp = jnp.concatenate([dp_a, dp_b, dp_c], axis=1)
    tok = send("w_in", _mm_tn(s["h"], dp, tk=D_MODEL, tn=512, name="mm_dwin", shards=N_CHIPS, tm=_WGRAD_ROWS))
    dx_in, g["norm1_g"] = _mm_nt(dp, lw["w_in"], name="mm_dh", norm=(s["x"], dx2, lw["norm1_g"] + tok))
    return dx_in, g


def _layer_weights(l, full, small):
    sg_w = small["sg_w"][l]
    sg_b = small["sg_b"][l]
    return dict(
        norm1_g=small["norm1_g"][l][None, :], norm2_g=small["norm2_g"][l][None, :],
        gq=jnp.tile(small["q_norm_g"][l], ATTN_W // HEAD_DIM)[None, :],
        gk=jnp.tile(small["k_norm_g"][l], KV_W // HEAD_DIM)[None, :],
        sg_norm_g=small["sg_norm_g"][l][None, :],
        sg_w=sg_w.astype(BF16), sg_wt=sg_w.transpose(0, 2, 1).astype(BF16),
        sg_bias=jnp.repeat(sg_b.T, HEAD_DIM, axis=1),
        **full)


_BIG = ("w_in", "w_out", "ffn_w_up", "ffn_w_down")
_SMALL_REPL = ("norm1_g", "q_norm_g", "k_norm_g", "sg_norm_g", "sg_w", "sg_b", "norm2_g")
_SMALL_SHARD = ("conv_w", "ffn_conv_w")
_ORDER = ("norm1_g", "w_in", "q_norm_g", "k_norm_g", "sg_norm_g", "sg_w", "sg_b", "conv_w", "w_out", "norm2_g",
          "ffn_w_up", "ffn_conv_w", "ffn_w_down")


def kernel(x, norm1_g, w_in, q_norm_g, k_norm_g, sg_norm_g, sg_w, sg_b, conv_w, w_out, norm2_g, ffn_w_up, ffn_conv_w, ffn_w_down, loss_target, m_norm1_g, m_w_in, m_q_norm_g, m_k_norm_g, m_sg_norm_g, m_sg_w, m_sg_b, m_conv_w, m_w_out, m_norm2_g, m_ffn_w_up, m_ffn_conv_w, m_ffn_w_down, v_norm1_g, v_w_in, v_q_norm_g, v_k_norm_g, v_sg_norm_g, v_sg_w, v_sg_b, v_conv_w, v_w_out, v_norm2_g, v_ffn_w_up, v_ffn_conv_w, v_ffn_w_down):
    w = dict(norm1_g=norm1_g, w_in=w_in, q_norm_g=q_norm_g, k_norm_g=k_norm_g, sg_norm_g=sg_norm_g, sg_w=sg_w,
             sg_b=sg_b, conv_w=conv_w, w_out=w_out, norm2_g=norm2_g, ffn_w_up=ffn_w_up, ffn_conv_w=ffn_conv_w,
             ffn_w_down=ffn_w_down)
    mom = dict(norm1_g=m_norm1_g, w_in=m_w_in, q_norm_g=m_q_norm_g, k_norm_g=m_k_norm_g, sg_norm_g=m_sg_norm_g,
               sg_w=m_sg_w, sg_b=m_sg_b, conv_w=m_conv_w, w_out=m_w_out, norm2_g=m_norm2_g, ffn_w_up=m_ffn_w_up,
               ffn_conv_w=m_ffn_conv_w, ffn_w_down=m_ffn_w_down)
    var = dict(norm1_g=v_norm1_g, w_in=v_w_in, q_norm_g=v_q_norm_g, k_norm_g=v_k_norm_g, sg_norm_g=v_sg_norm_g,
               sg_w=v_sg_w, sg_b=v_sg_b, conv_w=v_conv_w, w_out=v_w_out, norm2_g=v_norm2_g, ffn_w_up=v_ffn_w_up,
               ffn_conv_w=v_ffn_conv_w, ffn_w_down=v_ffn_w_down)
    L = DEPTH
    T = x.shape[1]
    xs = x.reshape(T, D_MODEL)
    target = loss_target.reshape(T, D_MODEL)

    chip = 2 * lax.axis_index("x") + lax.axis_index("y")

    shards = [_cast_bf16(w[n].reshape(-1, w[n].shape[-1]), "cast_" + n).reshape(w[n].shape) for n in _BIG]
    shards += [conv_w, ffn_conv_w]
    w_in_s, w_out_s, w_up_s, w_down_s, conv_s, fconv_s = shards
    gathers = []
    for l in range(L):
        gathers.append((_chips_start([w_in_s[l], w_out_s[l], conv_s[l]], False, "gather_start_%da" % l),
                        _chips_start([w_up_s[l], w_down_s[l], fconv_s[l]], False, "gather_start_%db" % l)))
    start_token = sum(h[4][0, 0] for pair in gathers for h in pair)
    consts = _rope_tables(T) + (_head_mean_matrix(),)

    def gathered(handle, after, name):
        own, lands = _chips_wait(handle, after, False, name)
        return [lax.dynamic_update_slice(ld, o[None], (chip,) + (jnp.int32(0),) * o.ndim) for ld, o in zip(lands, own)]

    saved, lws = [], []
    act_x, act_h = xs, None
    for l in range(L):
        g_in, g_out, g_conv = gathered(gathers[l][0], act_x if l else gathers[-1][1][4], "gather_wait_%da" % l)
        lw = _layer_weights(l, dict(w_in=g_in, w_out=g_out.reshape(D_MODEL, D_MODEL),
                                    conv_w=g_conv.transpose(1, 0, 2).reshape(3, CONV_W)), w)
        if l == 0:
            lw["norm1_g"] = lw["norm1_g"] + start_token

        def ffn_weights(after, l=l):
            g_up, g_down, g_fconv = gathered(gathers[l][1], after, "gather_wait_%db" % l)
            return dict(w_up=g_up, w_down=g_down.reshape(D_FF, D_MODEL),
                        ffn_conv_w=g_fconv.transpose(1, 0, 2).reshape(3, 2 * D_FF))

        next_g = w["norm1_g"][l + 1][None, :] if l + 1 < L else None
        act_x, act_h, s = _layer_fwd(act_x, act_h, lw, consts, ffn_weights, next_g)
        saved.append(s)
        lws.append(lw)
    dx, loss_blk = _loss_head(act_x, target)
    loss = lax.psum(loss_blk[0, 0], ("x", "y", "c"))

    grads = [None] * L
    partial = [None] * L

    def collect(pending, after, l):
        sums = {}
        for name, handle in pending:
            own, lands = _chips_wait(handle, after, True, "grad_wait_%d_%s" % (l, name))
            sums[name] = _sum_chips_own(lands[0], own[0], "sum_chips_" + name)
        return sums

    pending_prev = None
    for l in reversed(range(L)):
        pending = []

        def send(name, g4, l=l, pending=pending):
            handle = _chips_start([g4], True, "grad_start_%d_%s" % (l, name))
            pending.append((name, handle))
            return handle[4][0, 0]

        dx, g = _layer_bwd(dx, saved[l], lws[l], consts, send)
        grads[l] = g
        if pending_prev is not None:
            partial[l + 1] = collect(pending_prev, dx, l + 1)
        pending_prev = pending
    grad_x = dx.reshape(x.shape)

    small_names = _SMALL_REPL + _SMALL_SHARD
    small_local = [jnp.stack([grads[l][n].reshape(-1) for l in range(L)]) for n in small_names]
    small_handle = _chips_start([_pack(small_local)], False, "small_start")
    partial[0] = collect(pending_prev, small_handle[4], 0)

    short = dict(w_in="w_in", w_out="w_out", ffn_w_up="w_up", ffn_w_down="w_down")
    mine = [jnp.stack([partial[l][short[n]] for l in range(L)]) for n in _BIG]
    small_own, small_lands = _chips_wait(small_handle, mine[-1], False, "small_wait")
    mine.append(_sum_chips_own(small_lands[0], small_own[0], "sum_chips_small"))
    theirs = _exchange_sibling(mine)
    grad = dict(zip(small_names, _unpack(_add_pair(mine[-1], theirs[-1], "add_small"), small_local)))
    for n in _SMALL_REPL:
        grad[n] = grad[n].reshape(w[n].shape)
    for n in _SMALL_SHARD:
        full_w = grad[n].reshape(L, 3, -1)
        width = w[n].shape[-1]
        grad[n] = lax.dynamic_slice_in_dim(full_w, chip * width, width, axis=2)
    delta, new_m, new_v = {}, {}, {}
    for n, ga, gb in zip(_BIG, mine, theirs):
        shp = w[n].shape
        v2 = lambda a: a.reshape(-1, shp[-1])
        gsum, d, mn, vn = _adamw_sum(v2(w[n]), v2(ga), v2(gb), v2(mom[n]), v2(var[n]), "adamw_" + n)
        grad[n], delta[n], new_m[n], new_v[n] = gsum.reshape(shp), d.reshape(shp), mn.reshape(shp), vn.reshape(shp)
    for group, gname in ((_SMALL_REPL, "adamw_small"), (_SMALL_SHARD, "adamw_conv")):
        like = [w[n] for n in group]
        outs = _adamw(_pack([w[n] for n in group]), _pack([grad[n] for n in group]), _pack([mom[n] for n in group]),
                      _pack([var[n] for n in group]), gname)
        for res, dst in zip(outs, (delta, new_m, new_v)):
            for n, a in zip(group, _unpack(res, like)):
                dst[n] = a

    return (loss, grad_x, *[grad[n] for n in _ORDER], *[delta[n] for n in _ORDER],
            *[new_m[n] for n in _ORDER], *[new_v[n] for n in _ORDER])
```

```python
import jax
import jax.numpy as jnp
from jax import lax
from jax.experimental import pallas as pl
from jax.experimental.pallas import tpu as pltpu

F32 = jnp.float32
BF16 = jnp.bfloat16

DEPTH = 4
D_MODEL = 1024
HEAD_DIM = 64
ATTN_W = 512
KV_W = 128
SG_W = 256
CONV_W = 256
SG_CHUNK = 128
D_FF = 2816
PROJ_W = 2048
GRID_W = 64
ROPE_THETA = 10000.0
AXIS_DIM = HEAD_DIM // 2
EPS = 1e-6
N_CHIPS = 4

ADAM_LR = 0.001
ADAM_B1 = 0.9
ADAM_B2 = 0.999
ADAM_EPS = 1e-08
ADAM_WD = 0.01
ADAM_STEP = 10

_ROW_TILE = 512
_FFN_ROW_TILE = 256
_WGRAD_ROWS = 4096
LANES = 128
HALO = 8
VMEM_LIMIT_BYTES = 56 * 1024 * 1024
MESH = pl.DeviceIdType.MESH
ANY = pl.BlockSpec(memory_space=pl.ANY)


def _cp(*sem):
    return pltpu.CompilerParams(dimension_semantics=sem if sem else None,
                                vmem_limit_bytes=VMEM_LIMIT_BYTES)


def _sds(shape, dtype):
    return jax.ShapeDtypeStruct(shape, dtype)


def _dot(a, b):
    return jnp.dot(a, b, preferred_element_type=F32)


def _dot_nt(a, b):
    return lax.dot_general(a, b, (((1,), (1,)), ((), ())), preferred_element_type=F32)


def _dot_tn(a, b):
    return lax.dot_general(a, b, (((0,), (0,)), ((), ())), preferred_element_type=F32)


def _norm_fwd(x, g):
    T, Dm = x.shape
    tm = min(T, _ROW_TILE)

    def body(x_ref, g_ref, o_ref):
        xv = x_ref[...]
        r = lax.rsqrt(jnp.mean(xv * xv, axis=-1, keepdims=True) + EPS)
        o_ref[...] = ((xv * r) * g_ref[...]).astype(BF16)

    return pl.pallas_call(
        body, name="norm_fwd", grid=(T // tm,),
        in_specs=[pl.BlockSpec((tm, Dm), lambda i: (i, 0)), pl.BlockSpec((1, Dm), lambda i: (0, 0))],
        out_specs=pl.BlockSpec((tm, Dm), lambda i: (i, 0)),
        out_shape=_sds((T, Dm), BF16), compiler_params=_cp("parallel"))(x, g)


def _whole(w):
    return pl.BlockSpec(w.shape, lambda *g: (0,) * w.ndim)


def _mm_nn(a, w, *, out_dtype, name, res=None, tm=None, norm_g=None):
    M, K = a.shape
    N = w.shape[-1] if w.ndim == 2 else w.shape[0] * w.shape[2]
    tm = min(M, tm or _ROW_TILE)
    has_res = res is not None
    has_norm = norm_g is not None
    assert not has_norm or w.ndim == 2

    def body(*refs):
        a_ref, w_ref = refs[0], refs[1]
        o_ref = refs[-2] if has_norm else refs[-1]
        av = a_ref[...].astype(BF16)
        parts = [w_ref[...]] if w.ndim == 2 else [w_ref[s] for s in range(w.shape[0])]
        ns = N // len(parts)
        for s, wv in enumerate(parts):
            cols = pl.ds(s * ns, ns)
            acc = _dot(av, wv)
            if has_res:
                acc = acc + refs[2][:, cols]
            o_ref[:, cols] = acc.astype(out_dtype)
        if has_norm:
            r = lax.rsqrt(jnp.mean(acc * acc, axis=-1, keepdims=True) + EPS)
            refs[-1][...] = ((acc * r) * refs[2 + has_res][...]).astype(BF16)

    row = pl.BlockSpec((tm, N), lambda i: (i, 0))
    in_specs = [pl.BlockSpec((tm, K), lambda i: (i, 0)), _whole(w)]
    args = [a, w]
    if has_res:
        in_specs.append(row)
        args.append(res)
    if has_norm:
        in_specs.append(pl.BlockSpec((1, N), lambda i: (0, 0)))
        args.append(norm_g)
    return pl.pallas_call(
        body, name=name, grid=(M // tm,), in_specs=in_specs, out_specs=[row, row] if has_norm else row,
        out_shape=[_sds((M, N), out_dtype), _sds((M, N), BF16)] if has_norm else _sds((M, N), out_dtype),
        compiler_params=_cp("parallel"))(*args)


def _a_spec(a, tm, tn, row_of, col_of):
    if a.ndim == 2:
        return pl.BlockSpec((tm, tn), lambda *g: (row_of(*g), col_of(*g)))
    bph = a.shape[2] // tn
    return pl.BlockSpec((None, tm, tn), lambda *g: (col_of(*g) // bph, row_of(*g), col_of(*g) % bph))


def _a_cols(a):
    return a.shape[1] if a.ndim == 2 else a.shape[0] * a.shape[2]


def _mm_nt(a, w, *, name, tm=None, norm=None):
    M = a.shape[-2]
    Kw = w.shape[-2]
    tm = min(M, tm or _ROW_TILE)

    def product(a_ref, w_ref):
        if w.ndim == 2:
            return _dot_nt(a_ref[...].astype(BF16), w_ref[...])
        S, ns = w.shape[0], w.shape[2]
        acc = None
        for s in range(S):
            if a.ndim == 2:
                piece = a_ref[:, pl.ds(s * ns, ns)]
            else:
                per_half = S // 2
                piece = a_ref[s // per_half, :, pl.ds((s % per_half) * ns, ns)]
            part = _dot_nt(piece.astype(BF16), w_ref[s])
            acc = part if acc is None else acc + part
        return acc

    def body(a_ref, w_ref, o_ref):
        o_ref[...] = product(a_ref, w_ref)

    def body_norm(a_ref, w_ref, x_ref, dr_ref, g_ref, dx_ref, dg_ref):
        dhv = product(a_ref, w_ref)
        xv = x_ref[...]
        r = lax.rsqrt(jnp.mean(xv * xv, axis=-1, keepdims=True) + EPS)
        xh = xv * r
        dxh = dhv * g_ref[...]
        dx_ref[...] = dr_ref[...] + r * (dxh - xh * jnp.mean(dxh * xh, axis=-1, keepdims=True))
        _accum(dg_ref, pl.program_id(0), jnp.sum(dhv * xh, axis=0, keepdims=True))

    a_spec = (pl.BlockSpec((tm, a.shape[1]), lambda i: (i, 0)) if a.ndim == 2
              else pl.BlockSpec((2, tm, a.shape[2]), lambda i: (0, i, 0)))
    row = pl.BlockSpec((tm, Kw), lambda i: (i, 0))
    if norm is None:
        return pl.pallas_call(
            body, name=name, grid=(M // tm,), in_specs=[a_spec, _whole(w)], out_specs=row,
            out_shape=_sds((M, Kw), F32), compiler_params=_cp("parallel"))(a, w)
    vec = pl.BlockSpec((1, Kw), lambda i: (0, 0))
    return pl.pallas_call(
        body_norm, name=name, grid=(M // tm,), in_specs=[a_spec, _whole(w), row, row, vec], out_specs=[row, vec],
        out_shape=[_sds((M, Kw), F32), _sds((1, Kw), F32)], compiler_params=_cp("arbitrary"))(a, w, *norm)


def _mm_tn(a, b, *, tk, tn, name, shards=None, tm=None):
    M, K = a.shape
    N = _a_cols(b)
    tm = min(M, tm or _ROW_TILE)

    def body(a_ref, b_ref, o_ref):
        m = pl.program_id(2)
        part = _dot_tn(a_ref[...].astype(BF16), b_ref[...].astype(BF16))

        @pl.when(m == 0)
        def _():
            o_ref[...] = part

        @pl.when(m > 0)
        def _():
            o_ref[...] += part

    if shards is None:
        out_spec = pl.BlockSpec((tk, tn), lambda k, j, m: (k, j))
        out_shape = _sds((K, N), F32)
    else:
        bps = (N // shards) // tn
        out_spec = pl.BlockSpec((None, tk, tn), lambda k, j, m: (j // bps, k, j % bps))
        out_shape = _sds((shards, K, N // shards), F32)
    return pl.pallas_call(
        body, name=name, grid=(K // tk, N // tn, M // tm),
        in_specs=[pl.BlockSpec((tm, tk), lambda k, j, m: (m, k)),
                  _a_spec(b, tm, tn, lambda k, j, m: m, lambda k, j, m: j)],
        out_specs=out_spec, out_shape=out_shape,
        compiler_params=_cp("parallel", "parallel", "arbitrary"))(a, b)


def _halo_specs(T, tm, cw, ic):
    nb = tm // HALO
    last = T // HALO - 1

    def mk(rows, row_of):
        return pl.BlockSpec((rows, cw), lambda *g: (row_of(ic(*g)[0]), ic(*g)[1]))

    return [mk(HALO, lambda i: jnp.maximum(i * nb - 1, 0)), mk(tm, lambda i: i),
            mk(HALO, lambda i: jnp.minimum((i + 1) * nb, last))]


def _ext(prev_ref, cur_ref, next_ref, i, n):
    p = jnp.where(i > 0, prev_ref[...].astype(F32), 0.0)
    nx = jnp.where(i < n - 1, next_ref[...].astype(F32), 0.0)
    return jnp.concatenate([p, cur_ref[...].astype(F32), nx], axis=0)


def _dn(e):
    return pltpu.roll(e, 1, 0)


def _up(e):
    return pltpu.roll(e, e.shape[0] - 1, 0)


def _mid(e):
    return e[HALO:e.shape[0] - HALO]


def _taps(e):
    return _dn(e), e, _up(e)


def _conv3(taps, w):
    return taps[0] * w[0:1] + taps[1] * w[1:2] + taps[2] * w[2:3]


def _conv3_t(e, w):
    return _up(e) * w[0:1] + e * w[1:2] + _dn(e) * w[2:3]


def _conv3_wgrad(d, taps):
    return jnp.concatenate([jnp.sum(_mid(d * tap), axis=0, keepdims=True) for tap in taps], axis=0)


def _sigmoid(x):
    return 1.0 / (1.0 + jnp.exp(-x))


def _accum(ref, i, part):
    @pl.when(i == 0)
    def _():
        ref[...] = part

    @pl.when(i > 0)
    def _():
        ref[...] += part


def _ffn_act_fwd(up, cw):
    T = up.shape[0]
    tm = min(T, _FFN_ROW_TILE)
    cb = D_FF // 2
    nblk = D_FF // cb
    n = T // tm

    def body(gp, gc, gn, vp, vc, vn, wg_ref, wv_ref, o_ref):
        i = pl.program_id(1)
        gate = _conv3(_taps(_ext(gp, gc, gn, i, n)), wg_ref[...])
        val = _conv3(_taps(_ext(vp, vc, vn, i, n)), wv_ref[...])
        o_ref[...] = _mid(gate * _sigmoid(gate) * val).astype(BF16)

    return pl.pallas_call(
        body, name="ffn_act_fwd", grid=(nblk, n),
        in_specs=_halo_specs(T, tm, cb, lambda j, i: (i, j)) + _halo_specs(T, tm, cb, lambda j, i: (i, j + nblk))
        + [pl.BlockSpec((3, cb), lambda j, i: (0, j)), pl.BlockSpec((3, cb), lambda j, i: (0, j + nblk))],
        out_specs=pl.BlockSpec((tm, cb), lambda j, i: (i, j)),
        out_shape=_sds((T, D_FF), BF16), compiler_params=_cp("parallel", "parallel"))(
            up, up, up, up, up, up, cw, cw)


def _ffn_act_bwd(up, dact, cw):
    T = up.shape[0]
    tm = min(T, _FFN_ROW_TILE)
    cb = D_FF // 2
    nblk = D_FF // cb
    n = T // tm

    def body(gp, gc, gn, vp, vc, vn, dp_, dc, dn_, wg_ref, wv_ref, dup_ref, dcw_ref):
        i = pl.program_id(1)
        wg, wv = wg_ref[...], wv_ref[...]
        eg = _taps(_ext(gp, gc, gn, i, n))
        ev = _taps(_ext(vp, vc, vn, i, n))
        ed = _ext(dp_, dc, dn_, i, n)
        gate = _conv3(eg, wg)
        val = _conv3(ev, wv)
        sg = _sigmoid(gate)
        d_gate = ed * val * (sg * (1.0 + gate * (1.0 - sg)))
        d_val = ed * (gate * sg)
        dup_ref[0] = _mid(_conv3_t(d_gate, wg)).astype(BF16)
        dup_ref[1] = _mid(_conv3_t(d_val, wv)).astype(BF16)
        part = jnp.stack([_conv3_wgrad(d_gate, eg), _conv3_wgrad(d_val, ev)], axis=0)
        _accum(dcw_ref, i, part)

    return pl.pallas_call(
        body, name="ffn_act_bwd", grid=(nblk, n),
        in_specs=_halo_specs(T, tm, cb, lambda j, i: (i, j)) + _halo_specs(T, tm, cb, lambda j, i: (i, j + nblk))
        + _halo_specs(T, tm, cb, lambda j, i: (i, j))
        + [pl.BlockSpec((3, cb), lambda j, i: (0, j)), pl.BlockSpec((3, cb), lambda j, i: (0, j + nblk))],
        out_specs=[pl.BlockSpec((2, tm, cb), lambda j, i: (0, i, j)),
                   pl.BlockSpec((2, 3, cb), lambda j, i: (0, 0, j))],
        out_shape=[_sds((2, T, D_FF), BF16), _sds((2, 3, D_FF), F32)],
        compiler_params=_cp("parallel", "arbitrary"))(up, up, up, up, up, up, dact, dact, dact, cw, cw)


_CB_BLK, _CC_BLK, _CX_BLK = 5, 6, 7


def _convmix_fwd(p, w):
    T = p.shape[0]
    tm = min(T, _ROW_TILE)
    n = T // tm

    def body(cb_ref, ccp, ccc, ccn, cxp, cxc, cxn, w_ref, o_ref):
        i = pl.program_id(0)
        z = _ext(ccp, ccc, ccn, i, n) * _ext(cxp, cxc, cxn, i, n)
        o_ref[...] = (cb_ref[...] * _mid(_conv3(_taps(z), w_ref[...]))).astype(BF16)

    return pl.pallas_call(
        body, name="convmix_fwd", grid=(n,),
        in_specs=[pl.BlockSpec((tm, CONV_W), lambda i: (i, _CB_BLK))]
        + _halo_specs(T, tm, CONV_W, lambda i: (i, _CC_BLK)) + _halo_specs(T, tm, CONV_W, lambda i: (i, _CX_BLK))
        + [pl.BlockSpec((3, CONV_W), lambda i: (0, 0))],
        out_specs=pl.BlockSpec((tm, CONV_W), lambda i: (i, 0)),
        out_shape=_sds((T, CONV_W), BF16), compiler_params=_cp("parallel"))(p, p, p, p, p, p, p, w)


def _convmix_bwd(p, dmix, w):
    T = p.shape[0]
    tm = min(T, _ROW_TILE)
    n = T // tm
    dblk = (ATTN_W + SG_W) // CONV_W

    def body(cbp, cbc, cbn, ccp, ccc, ccn, cxp, cxc, cxn, dp_, dc, dn_, w_ref, o_ref, dw_ref):
        i = pl.program_id(0)
        wv = w_ref[...]
        ecb = _ext(cbp, cbc, cbn, i, n)
        ecc = _ext(ccp, ccc, ccn, i, n)
        ecx = _ext(cxp, cxc, cxn, i, n)
        ed = _ext(dp_, dc, dn_, i, n)
        z = _taps(ecc * ecx)
        d_cz = ed * ecb
        d_z = _conv3_t(d_cz, wv)
        o_ref[...] = jnp.concatenate([_mid(ed * _conv3(z, wv)), _mid(d_z * ecx), _mid(d_z * ecc)],
                                     axis=1).astype(BF16)
        _accum(dw_ref, i, _conv3_wgrad(d_cz, z))

    return pl.pallas_call(
        body, name="convmix_bwd", grid=(n,),
        in_specs=_halo_specs(T, tm, CONV_W, lambda i: (i, _CB_BLK)) + _halo_specs(T, tm, CONV_W, lambda i: (i, _CC_BLK))
        + _halo_specs(T, tm, CONV_W, lambda i: (i, _CX_BLK)) + _halo_specs(T, tm, CONV_W, lambda i: (i, dblk))
        + [pl.BlockSpec((3, CONV_W), lambda i: (0, 0))],
        out_specs=[pl.BlockSpec((tm, 3 * CONV_W), lambda i: (i, 0)), pl.BlockSpec((3, CONV_W), lambda i: (0, 0))],
        out_shape=[_sds((T, 3 * CONV_W), BF16), _sds((3, CONV_W), F32)],
        compiler_params=_cp("arbitrary"))(p, p, p, p, p, p, p, p, p, dmix, dmix, dmix, w)


_SU_BLK, _SV_BLK = 3, 4


def _sg_mixed(vnb, w_ref, bias, ch, pr, lo):
    vp = vnb[ch * SG_CHUNK:(ch + 1) * SG_CHUNK, pr * LANES:(pr + 1) * LANES]
    zero = jnp.zeros_like(vp)
    return (_dot(w_ref[2 * pr], jnp.where(lo, vp, zero)) + _dot(w_ref[2 * pr + 1], jnp.where(lo, zero, vp))
            + bias[:, pr * LANES:(pr + 1) * LANES]), vp


def _sg_fwd(p, g, w, bias):
    T = p.shape[0]
    tm = min(T, _ROW_TILE)

    def body(su_ref, sv_ref, g_ref, w_ref, b_ref, o_ref):
        lo = lax.broadcasted_iota(jnp.int32, (SG_CHUNK, LANES), 1) < HEAD_DIM
        sv = sv_ref[...]
        r = lax.rsqrt(jnp.mean(sv * sv, axis=-1, keepdims=True) + EPS)
        vnb = ((sv * r) * g_ref[...]).astype(BF16)
        bias_v = b_ref[...]
        for ch in range(tm // SG_CHUNK):
            for pr in range(2):
                mixed, _ = _sg_mixed(vnb, w_ref, bias_v, ch, pr, lo)
                rows, cols = pl.ds(ch * SG_CHUNK, SG_CHUNK), pl.ds(pr * LANES, LANES)
                o_ref[rows, cols] = (su_ref[rows, cols] * mixed).astype(BF16)

    return pl.pallas_call(
        body, name="sg_fwd", grid=(T // tm,),
        in_specs=[pl.BlockSpec((tm, SG_W), lambda i: (i, _SU_BLK)), pl.BlockSpec((tm, SG_W), lambda i: (i, _SV_BLK)),
                  pl.BlockSpec((1, SG_W), lambda i: (0, 0)), pl.BlockSpec((4, SG_CHUNK, SG_CHUNK), lambda i: (0, 0, 0)),
                  pl.BlockSpec((SG_CHUNK, SG_W), lambda i: (0, 0))],
        out_specs=pl.BlockSpec((tm, SG_W), lambda i: (i, 0)),
        out_shape=_sds((T, SG_W), BF16), compiler_params=_cp("parallel"))(p, p, g, w, bias)


def _sg_bwd(p, dmix, g, w, wt, bias):
    T = p.shape[0]
    tm = min(T, _ROW_TILE)
    dblk = ATTN_W // SG_W

    def body(su_ref, sv_ref, d_ref, g_ref, w_ref, wt_ref, b_ref, o_ref, dw_ref, db_ref, dg_ref, dvn_ref):
        i = pl.program_id(0)
        lo = lax.broadcasted_iota(jnp.int32, (SG_CHUNK, LANES), 1) < HEAD_DIM
        sv = sv_ref[...]
        gv = g_ref[...]
        r = lax.rsqrt(jnp.mean(sv * sv, axis=-1, keepdims=True) + EPS)
        xh = sv * r
        vnb = (xh * gv).astype(BF16)
        bias_v = b_ref[...]
        dw = [jnp.zeros((SG_CHUNK, SG_CHUNK), F32) for _ in range(4)]
        db = jnp.zeros((SG_CHUNK, SG_W), F32)
        for ch in range(tm // SG_CHUNK):
            dbs = []
            for pr in range(2):
                mixed, vp = _sg_mixed(vnb, w_ref, bias_v, ch, pr, lo)
                rows, cols = pl.ds(ch * SG_CHUNK, SG_CHUNK), pl.ds(pr * LANES, LANES)
                dgo = d_ref[rows, cols]
                o_ref[rows, cols] = (dgo * mixed).astype(BF16)
                dm = dgo * su_ref[rows, cols]
                dmb = dm.astype(BF16)
                zero = jnp.zeros_like(dmb)
                dw[2 * pr] += _dot_nt(jnp.where(lo, dmb, zero), vp)
                dw[2 * pr + 1] += _dot_nt(jnp.where(lo, zero, dmb), vp)
                dvn_ref[rows, cols] = jnp.where(lo, _dot(wt_ref[2 * pr], dmb), _dot(wt_ref[2 * pr + 1], dmb))
                dbs.append(dm)
            db += jnp.concatenate(dbs, axis=1)
        dvn = dvn_ref[...]
        dxh = dvn * gv
        o_ref[:, pl.ds(SG_W, SG_W)] = (r * (dxh - xh * jnp.mean(dxh * xh, axis=-1, keepdims=True))).astype(BF16)
        _accum(dw_ref, i, jnp.stack(dw, axis=0))
        _accum(db_ref, i, db)
        _accum(dg_ref, i, jnp.sum(dvn * xh, axis=0, keepdims=True))

    wspec = pl.BlockSpec((4, SG_CHUNK, SG_CHUNK), lambda i: (0, 0, 0))
    return pl.pallas_call(
        body, name="sg_bwd", grid=(T // tm,),
        in_specs=[pl.BlockSpec((tm, SG_W), lambda i: (i, _SU_BLK)), pl.BlockSpec((tm, SG_W), lambda i: (i, _SV_BLK)),
                  pl.BlockSpec((tm, SG_W), lambda i: (i, dblk)), pl.BlockSpec((1, SG_W), lambda i: (0, 0)),
                  wspec, wspec, pl.BlockSpec((SG_CHUNK, SG_W), lambda i: (0, 0))],
        out_specs=[pl.BlockSpec((tm, 2 * SG_W), lambda i: (i, 0)), wspec,
                   pl.BlockSpec((SG_CHUNK, SG_W), lambda i: (0, 0)), pl.BlockSpec((1, SG_W), lambda i: (0, 0))],
        out_shape=[_sds((T, 2 * SG_W), BF16), _sds((4, SG_CHUNK, SG_CHUNK), F32),
                   _sds((SG_CHUNK, SG_W), F32), _sds((1, SG_W), F32)],
        scratch_shapes=[pltpu.VMEM((tm, SG_W), F32)],
        compiler_params=_cp("arbitrary"))(p, p, dmix, g, w, wt, bias)


_ATTN_TQ = 256
_ATTN_TK = 1024
_ATTN_FWD_TK = 512
_SOFTMAX_STRIP = 32
_ONES_ROWS = 16


def _head_mean(v, bmat):
    hi = v.astype(BF16)
    lo = (v - hi.astype(F32)).astype(BF16)
    return _dot(hi, bmat) + _dot(lo, bmat)


def _swap16(y):
    lane = lax.broadcasted_iota(jnp.int32, y.shape, 1)
    return jnp.where(lane % 32 < 16, pltpu.roll(y, y.shape[1] - 16, 1), pltpu.roll(y, 16, 1))


def _rope(y, cos, sin):
    return y * cos + _swap16(y) * sin


def _rope_t(dy, cos, sin):
    return dy * cos + _swap16(dy * sin)


def _dup_rows(t, gidx):
    h = t[gidx * HEAD_DIM:(gidx + 1) * HEAD_DIM]
    return jnp.concatenate([h, h], axis=0)


def _qk_prep(p, gq, gk, cos, sin, bmat, tk, tkf):
    T = p.shape[0]
    nk = T // tk
    sub = tk // tkf
    scale = HEAD_DIM ** -0.5

    def body(q_ref, kv_ref, gq_ref, gk_ref, cos_ref, sin_ref, b_ref, qo_ref, kt_ref, kd_ref, vt_ref, v1_ref):
        cosv, sinv, bm = cos_ref[...], sin_ref[...], b_ref[...]
        for pr in range(ATTN_W // LANES):
            cols = pl.ds(pr * LANES, LANES)
            xq = q_ref[:, cols]
            r = lax.rsqrt(_head_mean(xq * xq, bm) + EPS)
            qo_ref[:, cols] = (_rope((xq * r) * gq_ref[:, cols], cosv, sinv) * scale).astype(BF16)
        xk = kv_ref[:, pl.ds(0, LANES)]
        r = lax.rsqrt(_head_mean(xk * xk, bm) + EPS)
        kt = _rope((xk * r) * gk_ref[...], cosv, sinv).T
        vt = kv_ref[:, pl.ds(LANES, LANES)].T
        for gidx in range(2):
            kdup = _dup_rows(kt, gidx)
            kt_ref[gidx] = kdup.astype(BF16)
            vt_ref[gidx] = _dup_rows(vt, gidx).astype(BF16)
            v1 = jnp.concatenate([vt[gidx * HEAD_DIM:(gidx + 1) * HEAD_DIM],
                                  jnp.ones((_ONES_ROWS, tk), F32)], axis=0).astype(BF16)
            for b in range(sub):
                v1_ref[gidx, b] = v1[:, b * tkf:(b + 1) * tkf]
            kd_ref[gidx] = kdup.T.astype(BF16)

    tspec = pl.BlockSpec((2, None, LANES, tk), lambda i: (0, i, 0, 0))
    dspec = pl.BlockSpec((2, tk, LANES), lambda i: (0, i, 0))
    tab = pl.BlockSpec((tk, LANES), lambda i: (i, 0))
    return pl.pallas_call(
        body, name="qk_prep", grid=(nk,),
        in_specs=[pl.BlockSpec((tk, ATTN_W), lambda i: (i, 0)), pl.BlockSpec((tk, 2 * KV_W), lambda i: (i, ATTN_W // (2 * KV_W))),
                  pl.BlockSpec((1, ATTN_W), lambda i: (0, 0)), pl.BlockSpec((1, KV_W), lambda i: (0, 0)),
                  tab, tab, pl.BlockSpec((LANES, LANES), lambda i: (0, 0))],
        out_specs=[pl.BlockSpec((tk, ATTN_W), lambda i: (i, 0)), tspec, dspec, tspec,
                   pl.BlockSpec((2, sub, HEAD_DIM + _ONES_ROWS, tkf), lambda i: (0, i, 0, 0))],
        out_shape=[_sds((T, ATTN_W), BF16), _sds((2, nk, LANES, tk), BF16), _sds((2, T, LANES), BF16),
                   _sds((2, nk, LANES, tk), BF16), _sds((2, nk * sub, HEAD_DIM + _ONES_ROWS, tkf), BF16)],
        compiler_params=_cp("parallel"))(p, p, gq, gk, cos, sin, bmat)


def _stack_heads(t):
    lo = lax.broadcasted_iota(jnp.int32, (t.shape[0], LANES), 1) < HEAD_DIM
    parts = []
    for pr in range(2):
        tp = t[:, pr * LANES:(pr + 1) * LANES]
        zero = jnp.zeros_like(tp)
        parts += [jnp.where(lo, tp, zero), jnp.where(lo, zero, tp)]
    return jnp.concatenate(parts, axis=0)


def _rows8_reduce(s, op):
    parts = [s[r:r + 8] for r in range(0, s.shape[0], 8)]
    while len(parts) > 1:
        parts = [op(parts[k], parts[k + 1]) for k in range(0, len(parts) - 1, 2)] + (
            [parts[-1]] if len(parts) % 2 else [])
    return parts[0]


def _attn_fwd(q, kd, v1, tq):
    T = q.shape[0]
    nk, tk = v1.shape[1], v1.shape[3]
    vrows = v1.shape[2]
    nq = T // tq
    sq = 4 * tq
    strip = _SOFTMAX_STRIP
    depth = 4
    assert nk % depth == 0

    def body(q_ref, kd_ref, v1_ref, o_ref, lse_ref, qst_ref, s0_ref, s1_ref, s2_ref, s3_ref, pa_ref, pb_ref,
             m_ref, acc_ref):
        s_refs = (s0_ref, s1_ref, s2_ref, s3_ref)
        p_refs = (pa_ref, pb_ref)
        qst_ref[...] = _stack_heads(q_ref[...]).astype(F32).T.astype(BF16)
        m_ref[...] = jnp.full((1, sq), -jnp.inf, F32)
        acc_ref[...] = jnp.zeros((vrows, sq), F32)

        def scores(j):
            return _dot(kd_ref[pl.ds(pl.multiple_of(j * tk, tk), tk), :], qst_ref[...])

        def block_max(s_ref):
            m8 = None
            for c in range(tk // strip):
                part = _rows8_reduce(s_ref[pl.ds(c * strip, strip), :], jnp.maximum)
                m8 = part if m8 is None else jnp.maximum(m8, part)
            return m8

        def exp_pass(s_ref, p_ref, m8):
            m_old = m_ref[...]
            m_new = jnp.maximum(m_old, jnp.max(m8, axis=0, keepdims=True))
            m_ref[...] = m_new
            for c in range(tk // strip):
                rows = pl.ds(c * strip, strip)
                p_ref[rows, :] = jnp.exp(s_ref[rows, :] - m_new).astype(BF16)
            return jnp.exp(m_old - m_new)

        def apply(p_ref, alpha, j):
            acc_ref[...] = alpha * acc_ref[...] + _dot(v1_ref[j], p_ref[...])

        s_refs[0][...] = scores(0)
        s_refs[1][...] = scores(1)

        def trip(t, max_cur):
            for u in range(depth):
                j = depth * t + u
                s_refs[(u + 2) % depth][...] = scores(jnp.minimum(j + 2, nk - 1))
                alpha = exp_pass(s_refs[u], p_refs[u % 2], max_cur)
                max_cur = block_max(s_refs[(u + 1) % depth])
                apply(p_refs[u % 2], alpha, j)
            return max_cur

        lax.fori_loop(0, nk // depth, trip, block_max(s_refs[0]))
        l = acc_ref[pl.ds(HEAD_DIM, 1), :]
        on = acc_ref[pl.ds(0, HEAD_DIM), :] / l
        pairs = []
        for pr in range(2):
            two = jnp.concatenate([on[:, (2 * pr) * tq:(2 * pr + 1) * tq], on[:, (2 * pr + 1) * tq:(2 * pr + 2) * tq]],
                                  axis=0)
            pairs.append(two.T)
        o_ref[...] = jnp.concatenate(pairs, axis=1).astype(BF16)
        lse_ref[...] = jnp.broadcast_to(m_ref[...] + jnp.log(l), (LANES, sq)).T

    row = pltpu.VMEM((1, sq), F32)
    return pl.pallas_call(
        body, name="attn_fwd", grid=(2, nq),
        in_specs=[pl.BlockSpec((tq, 2 * LANES), lambda g, i: (i, g)),
                  pl.BlockSpec((None, T, LANES), lambda g, i: (g, 0, 0)),
                  pl.BlockSpec((None, nk, vrows, tk), lambda g, i: (g, 0, 0, 0))],
        out_specs=[pl.BlockSpec((tq, 2 * LANES), lambda g, i: (i, g)),
                   pl.BlockSpec((None, None, sq, LANES), lambda g, i: (g, i, 0, 0))],
        out_shape=[_sds((T, ATTN_W), BF16), _sds((2, nq, sq, LANES), F32)],
        scratch_shapes=[pltpu.VMEM((LANES, sq), BF16)] + [pltpu.VMEM((tk, sq), F32)] * depth
        + [pltpu.VMEM((tk, sq), BF16), pltpu.VMEM((tk, sq), BF16), row, pltpu.VMEM((vrows, sq), F32)],
        compiler_params=_cp("parallel", "parallel"))(q, kd, v1)


def _attn_bwd(q, o, dmix, lse, kt, kd, vt, tq):
    T = q.shape[0]
    nk, tk = kt.shape[1], kt.shape[3]
    nq = T // tq
    sq = 4 * tq
    rep = tk // LANES

    def body(q_ref, o_ref, do_ref, lse_ref, kt_ref, kd_ref, vt_ref, dq_ref, dkt_ref, dvt_ref):
        i = pl.program_id(1)
        qs = _stack_heads(q_ref[...])
        dof = _stack_heads(do_ref[...])
        dos = dof.astype(BF16)
        qst = qs.astype(F32).T.astype(BF16)
        dost = dof.T.astype(BF16)
        o_pair = o_ref[...].astype(F32)
        os_ = jnp.concatenate([o_pair[:, 0:LANES], o_pair[:, 0:LANES], o_pair[:, LANES:], o_pair[:, LANES:]], axis=0)
        delta = jnp.sum(dof * os_, axis=-1, keepdims=True)
        lse_t = jnp.concatenate([lse_ref[...]] * rep, axis=1)

        @pl.when(i == 0)
        def _():
            dkt_ref[...] = jnp.zeros_like(dkt_ref)
            dvt_ref[...] = jnp.zeros_like(dvt_ref)

        def step(j, dq):
            kdb = kd_ref[pl.ds(pl.multiple_of(j * tk, tk), tk), :]
            pexp = jnp.exp(_dot(qs, kt_ref[j]) - lse_t)
            ds = pexp * (_dot(dos, vt_ref[j]) - delta)
            pb = pexp.astype(BF16)
            dsb = ds.astype(BF16)
            dvt_ref[j] += _dot(dost, pb)
            dkt_ref[j] += _dot(qst, dsb)
            return dq + _dot(dsb, kdb)

        dq = lax.fori_loop(0, nk, step, jnp.zeros((sq, LANES), F32))
        lo = lax.broadcasted_iota(jnp.int32, (tq, LANES), 1) < HEAD_DIM
        dq_ref[...] = jnp.concatenate([jnp.where(lo, dq[0:tq], dq[tq:2 * tq]),
                                       jnp.where(lo, dq[2 * tq:3 * tq], dq[3 * tq:4 * tq])], axis=1)

    tspec = pl.BlockSpec((None, nk, LANES, tk), lambda g, i: (g, 0, 0, 0))
    qspec = pl.BlockSpec((tq, 2 * LANES), lambda g, i: (i, g))
    return pl.pallas_call(
        body, name="attn_bwd", grid=(2, nq),
        in_specs=[qspec, qspec, qspec, pl.BlockSpec((None, None, sq, LANES), lambda g, i: (g, i, 0, 0)),
                  tspec, pl.BlockSpec((None, T, LANES), lambda g, i: (g, 0, 0)), tspec],
        out_specs=[qspec, tspec, tspec],
        out_shape=[_sds((T, ATTN_W), F32), _sds((2, nk, LANES, tk), F32), _sds((2, nk, LANES, tk), F32)],
        compiler_params=_cp("parallel", "arbitrary"))(q, o, dmix, lse, kt, kd, vt)


def _fold_t(t_ref):
    rows = []
    for gidx in range(2):
        t = t_ref[gidx]
        rows.append(t[0:HEAD_DIM] + t[HEAD_DIM:2 * HEAD_DIM])
    return jnp.concatenate(rows, axis=0).T


def _qk_bwd(p, dq, dkt, dvt, gq, gk, cos, sin, bmat):
    T = p.shape[0]
    nk, tk = dkt.shape[1], dkt.shape[3]
    scale = HEAD_DIM ** -0.5

    def norm_bwd(x, dy, gain, bm):
        r = lax.rsqrt(_head_mean(x * x, bm) + EPS)
        xh = x * r
        dxh = dy * gain
        return r * (dxh - xh * _head_mean(dxh * xh, bm)), jnp.sum(dy * xh, axis=0, keepdims=True)

    def body(q_ref, kv_ref, dq_ref, dkt_ref, dvt_ref, gq_ref, gk_ref, cos_ref, sin_ref, b_ref, o_ref, dgq_ref, dgk_ref):
        i = pl.program_id(0)
        cosv, sinv, bm = cos_ref[...], sin_ref[...], b_ref[...]
        dgq = []
        for pr in range(ATTN_W // LANES):
            cols = pl.ds(pr * LANES, LANES)
            dy = _rope_t(dq_ref[:, cols] * scale, cosv, sinv)
            dx, dg = norm_bwd(q_ref[:, cols], dy, gq_ref[:, cols], bm)
            o_ref[:, cols] = dx.astype(BF16)
            dgq.append(dg)
        dy = _rope_t(_fold_t(dkt_ref), cosv, sinv)
        dx, dgk = norm_bwd(kv_ref[:, pl.ds(0, LANES)], dy, gk_ref[...], bm)
        o_ref[:, pl.ds(ATTN_W, LANES)] = dx.astype(BF16)
        o_ref[:, pl.ds(ATTN_W + LANES, LANES)] = _fold_t(dvt_ref).astype(BF16)
        _accum(dgq_ref, i, jnp.concatenate(dgq, axis=1))
        _accum(dgk_ref, i, dgk)

    tspec = pl.BlockSpec((2, None, LANES, tk), lambda i: (0, i, 0, 0))
    tab = pl.BlockSpec((tk, LANES), lambda i: (i, 0))
    return pl.pallas_call(
        body, name="qk_bwd", grid=(nk,),
        in_specs=[pl.BlockSpec((tk, ATTN_W), lambda i: (i, 0)), pl.BlockSpec((tk, 2 * KV_W), lambda i: (i, ATTN_W // (2 * KV_W))),
                  pl.BlockSpec((tk, ATTN_W), lambda i: (i, 0)), tspec, tspec,
                  pl.BlockSpec((1, ATTN_W), lambda i: (0, 0)), pl.BlockSpec((1, KV_W), lambda i: (0, 0)),
                  tab, tab, pl.BlockSpec((LANES, LANES), lambda i: (0, 0))],
        out_specs=[pl.BlockSpec((tk, ATTN_W + 2 * KV_W), lambda i: (i, 0)),
                   pl.BlockSpec((1, ATTN_W), lambda i: (0, 0)), pl.BlockSpec((1, KV_W), lambda i: (0, 0))],
        out_shape=[_sds((T, ATTN_W + 2 * KV_W), BF16), _sds((1, ATTN_W), F32), _sds((1, KV_W), F32)],
        compiler_params=_cp("arbitrary"))(p, p, dq, dkt, dvt, gq, gk, cos, sin, bmat)


def _loss_head(y, target):
    T, Dm = y.shape
    tm = min(T, _ROW_TILE)

    def body(y_ref, t_ref, dy_ref, l_ref):
        i = pl.program_id(0)
        err = y_ref[...] - t_ref[...]
        dy_ref[...] = err * (1.0 / Dm)
        part = jnp.sum(jnp.sum(err * err, axis=-1, keepdims=True), axis=0, keepdims=True) * (0.5 / Dm)
        _accum(l_ref, i, jnp.broadcast_to(part, (8, LANES)))

    row = pl.BlockSpec((tm, Dm), lambda i: (i, 0))
    return pl.pallas_call(
        body, name="loss_head", grid=(T // tm,), in_specs=[row, row],
        out_specs=[row, pl.BlockSpec((8, LANES), lambda i: (0, 0))],
        out_shape=[_sds((T, Dm), F32), _sds((8, LANES), F32)], compiler_params=_cp("arbitrary"))(y, target)


def _adamw(w, g, m, v, name):
    R, C = w.shape
    tr = R
    for cand in (512, 256, 128, 64, 32, 16, 8):
        if R % cand == 0:
            tr = cand
            break
    c1 = 1.0 - ADAM_B1 ** ADAM_STEP
    c2 = 1.0 - ADAM_B2 ** ADAM_STEP

    def body(w_ref, g_ref, m_ref, v_ref, d_ref, mo_ref, vo_ref):
        gv = g_ref[...]
        mn = ADAM_B1 * m_ref[...] + (1.0 - ADAM_B1) * gv
        vn = ADAM_B2 * v_ref[...] + (1.0 - ADAM_B2) * (gv * gv)
        d_ref[...] = -ADAM_LR * ((mn / c1) / (jnp.sqrt(vn / c2) + ADAM_EPS) + ADAM_WD * w_ref[...])
        mo_ref[...] = mn
        vo_ref[...] = vn

    blk = pl.BlockSpec((tr, C), lambda i: (i, 0))
    return pl.pallas_call(
        body, name=name, grid=(R // tr,), in_specs=[blk] * 4, out_specs=[blk] * 3,
        out_shape=[_sds((R, C), F32)] * 3, compiler_params=_cp("parallel"))(w, g, m, v)


def _cast_bf16(w, name):
    R, C = w.shape
    tr = 512 if R % 512 == 0 else 256

    def body(w_ref, o_ref):
        o_ref[...] = w_ref[...].astype(BF16)

    blk = pl.BlockSpec((tr, C), lambda i: (i, 0))
    return pl.pallas_call(body, name=name, grid=(R // tr,), in_specs=[blk], out_specs=blk,
                          out_shape=_sds((R, C), BF16), compiler_params=_cp("parallel"))(w)


def _position():
    x, y, c = lax.axis_index("x"), lax.axis_index("y"), lax.axis_index("c")
    return x, y, c


def _other_chips(x, y):
    return [(1 - x, y), (x, 1 - y), (1 - x, 1 - y)]


_HBM = pl.BlockSpec(memory_space=pltpu.HBM)
_SEM = pl.BlockSpec(memory_space=pltpu.SEMAPHORE)
_EFFECT = pltpu.SideEffectType.DATAFLOW_SIDE_EFFECTING


def _chip_copies(srcs, lands, send_sems, recv_sems, per_chip, arriving):
    x, y, c = _position()
    me = 2 * x + y
    copies = []
    for t, (src, land) in enumerate(zip(srcs, lands)):
        for k, (px, py) in enumerate(_other_chips(x, y)):
            peer = 2 * px + py
            copies.append(pltpu.make_async_remote_copy(
                src_ref=src.at[peer] if per_chip else src, dst_ref=land.at[peer if arriving else me],
                send_sem=send_sems[3 * t + k], recv_sem=recv_sems[3 * t + k],
                device_id=(px, py, c), device_id_type=MESH))
    return copies


def _chips_start(srcs, per_chip, name):
    n = len(srcs)
    slab = [s.shape[1:] if per_chip else s.shape for s in srcs]
    lands = [lax.empty((N_CHIPS,) + sh, s.dtype) for sh, s in zip(slab, srcs)]

    ns = 3 * n

    def body(*refs):
        ins = refs[:2 * n]
        send_sems, recv_sems = refs[2 * n:2 * n + ns], refs[2 * n + ns:2 * n + 2 * ns]
        token = refs[-1]
        for cp in _chip_copies(ins[:n], ins[n:], send_sems, recv_sems, per_chip, False):
            cp.start()
        token[...] = jnp.zeros_like(token)

    args = [pltpu.with_memory_space_constraint(a, pltpu.HBM) for a in list(srcs) + lands]
    outs = pl.pallas_call(
        body, name=name,
        out_shape=[pltpu.SemaphoreType.DMA(())] * (2 * ns)
        + [pltpu.HBM(a.shape, a.dtype) for a in args] + [_sds((8, LANES), F32)],
        in_specs=[_HBM] * (2 * n),
        out_specs=[_SEM] * (2 * ns) + [_HBM] * (2 * n) + [pl.BlockSpec(memory_space=pltpu.VMEM)],
        input_output_aliases={i: 2 * ns + i for i in range(2 * n)},
        compiler_params=pltpu.CompilerParams(has_side_effects=_EFFECT))(*args)
    sems, rest = outs[:2 * ns], outs[2 * ns:]
    return sems[:ns], sems[ns:], rest[:n], rest[n:2 * n], rest[-1]


def _chips_wait(handle, after, per_chip, name):
    send_sems, recv_sems, srcs, lands, _ = handle
    n = len(srcs)
    ns = 3 * n

    def body(*refs):
        ins = refs[:2 * n]
        s_sems, r_sems = refs[2 * n:2 * n + ns], refs[2 * n + ns:2 * n + 2 * ns]
        for cp in _chip_copies(ins[:n], ins[n:], s_sems, r_sems, per_chip, False):
            cp.wait_send()
        for cp in _chip_copies(ins[:n], ins[n:], s_sems, r_sems, per_chip, True):
            cp.wait_recv()

    outs = pl.pallas_call(
        body, name=name,
        out_shape=[pltpu.HBM(a.shape, a.dtype) for a in list(srcs) + list(lands)],
        in_specs=[_HBM] * (2 * n) + [_SEM] * (2 * ns) + [ANY],
        out_specs=[_HBM] * (2 * n),
        input_output_aliases={i: i for i in range(2 * n)},
        compiler_params=pltpu.CompilerParams(has_side_effects=_EFFECT))(*srcs, *lands, *send_sems, *recv_sems, after)
    return outs[:n], outs[n:]


def _sum_chips_own(land, own, name):
    S, R, C = land.shape
    tr = R
    for cand in (256, 128, 64, 32, 16, 8):
        if R % cand == 0:
            tr = cand
            break

    def body(l_ref, o_ref, out_ref):
        x, y, _ = _position()
        me = 2 * x + y
        mine = o_ref[me] if own.ndim == 3 else o_ref[...]
        acc = None
        for k in range(S):
            part = jnp.where(me == k, mine, l_ref[k])
            acc = part if acc is None else acc + part
        out_ref[...] = acc

    blk = pl.BlockSpec((S, tr, C), lambda i: (0, i, 0))
    row = pl.BlockSpec((tr, C), lambda i: (i, 0))
    return pl.pallas_call(
        body, name=name, grid=(R // tr,), in_specs=[blk, blk if own.ndim == 3 else row], out_specs=row,
        out_shape=_sds((R, C), F32), compiler_params=_cp("parallel"))(land, own)


def _add_pair(a, b, name):
    R, C = a.shape

    def body(a_ref, b_ref, o_ref):
        o_ref[...] = a_ref[...] + b_ref[...]

    blk = pl.BlockSpec((R, C), lambda: (0, 0))
    return pl.pallas_call(body, name=name, in_specs=[blk, blk], out_specs=blk, out_shape=_sds((R, C), F32))(a, b)


def _exchange_sibling(arrays):
    n = len(arrays)

    def body(*refs):
        ins, outs = refs[:n], refs[n:2 * n]
        send_sems, recv_sems = refs[2 * n:]
        x, y, c = _position()
        sends = []
        for t in range(n):
            cp = pltpu.make_async_remote_copy(src_ref=ins[t], dst_ref=outs[t], send_sem=send_sems.at[t],
                                              recv_sem=recv_sems.at[t], device_id=(x, y, 1 - c), device_id_type=MESH)
            cp.start()
            sends.append(cp)
        for cp in sends:
            cp.wait()

    return pl.pallas_call(
        body, name="grads_to_sibling",
        in_specs=[ANY] * n, out_specs=[ANY] * n, out_shape=[_sds(a.shape, a.dtype) for a in arrays],
        scratch_shapes=[pltpu.SemaphoreType.DMA((n,)), pltpu.SemaphoreType.DMA((n,))],
        compiler_params=pltpu.CompilerParams(has_side_effects=True))(*arrays)


def _adamw_sum(w, ga, gb, m, v, name):
    R, C = w.shape
    tr = next(t for t in (512, 256, 128, 64) if R % t == 0 and t * C * 4 <= (1 << 20))
    c1 = 1.0 - ADAM_B1 ** ADAM_STEP
    c2 = 1.0 - ADAM_B2 ** ADAM_STEP

    def body(w_ref, ga_ref, gb_ref, m_ref, v_ref, g_ref, d_ref, mo_ref, vo_ref):
        gv = ga_ref[...] + gb_ref[...]
        mn = ADAM_B1 * m_ref[...] + (1.0 - ADAM_B1) * gv
        vn = ADAM_B2 * v_ref[...] + (1.0 - ADAM_B2) * (gv * gv)
        g_ref[...] = gv
        d_ref[...] = -ADAM_LR * ((mn / c1) / (jnp.sqrt(vn / c2) + ADAM_EPS) + ADAM_WD * w_ref[...])
        mo_ref[...] = mn
        vo_ref[...] = vn

    blk = pl.BlockSpec((tr, C), lambda i: (i, 0))
    return pl.pallas_call(
        body, name=name, grid=(R // tr,), in_specs=[blk] * 5, out_specs=[blk] * 4,
        out_shape=[_sds((R, C), F32)] * 4, compiler_params=_cp("parallel"))(w, ga, gb, m, v)


def _rope_tables(T):
    pos = jnp.arange(T)
    row = (pos // GRID_W).astype(F32)
    col = (pos % GRID_W).astype(F32)
    inv = 1.0 / (ROPE_THETA ** (jnp.arange(AXIS_DIM // 2, dtype=F32) * 2.0 / AXIS_DIM))
    ar, ac = row[:, None] * inv[None, :], col[:, None] * inv[None, :]
    cos = jnp.concatenate([jnp.cos(ar), jnp.cos(ar), jnp.cos(ac), jnp.cos(ac)], axis=-1)
    sin = jnp.concatenate([-jnp.sin(ar), jnp.sin(ar), -jnp.sin(ac), jnp.sin(ac)], axis=-1)
    return jnp.tile(cos, (1, LANES // HEAD_DIM)), jnp.tile(sin, (1, LANES // HEAD_DIM))


def _head_mean_matrix():
    h = jnp.arange(LANES) // HEAD_DIM
    return jnp.where(h[:, None] == h[None, :], 1.0 / HEAD_DIM, 0.0).astype(BF16)


def _pack(arrays):
    flat = jnp.concatenate([a.reshape(-1) for a in arrays])
    rows = -(-flat.shape[0] // LANES)
    rows = -(-rows // 256) * 256
    return jnp.pad(flat, (0, rows * LANES - flat.shape[0])).reshape(rows, LANES)


def _unpack(packed, like):
    flat = packed.reshape(-1)
    out, off = [], 0
    for a in like:
        out.append(flat[off:off + a.size].reshape(a.shape))
        off += a.size
    return out


def _layer_fwd(x, h, lw, consts, ffn_weights, next_g):
    cos, sin, bmat = consts
    T = x.shape[0]
    tk = min(T, _ATTN_TK)
    tq = min(T, _ATTN_TQ)
    if h is None:
        h = _norm_fwd(x, lw["norm1_g"])
    p = _mm_nn(h, lw["w_in"], out_dtype=F32, name="mm_p")
    qn, kt, kd, vt, v1 = _qk_prep(p, lw["gq"], lw["gk"], cos, sin, bmat, tk, min(tk, _ATTN_FWD_TK))
    o, lse = _attn_fwd(qn, kd, v1, tq)
    go = _sg_fwd(p, lw["sg_norm_g"], lw["sg_w"], lw["sg_bias"])
    co = _convmix_fwd(p, lw["conv_w"])
    mix = jnp.concatenate([o, go, co], axis=1)
    x_mid, h2 = _mm_nn(mix, lw["w_out"], out_dtype=F32, name="mm_out", res=x, norm_g=lw["norm2_g"])
    lw.update(ffn_weights(x_mid))
    up = _mm_nn(h2, lw["w_up"], out_dtype=F32, name="mm_up", tm=_FFN_ROW_TILE)
    act = _ffn_act_fwd(up, lw["ffn_conv_w"])
    if next_g is None:
        x_out, h_next = _mm_nn(act, lw["w_down"], out_dtype=F32, name="mm_down_last", res=x_mid), None
    else:
        x_out, h_next = _mm_nn(act, lw["w_down"], out_dtype=F32, name="mm_down", res=x_mid, norm_g=next_g)
    saved = dict(x=x, h=h, p=p, qn=qn, kt=kt, kd=kd, vt=vt, o=o, lse=lse, mix=mix, x_mid=x_mid, h2=h2, up=up, act=act)
    return x_out, h_next, saved


def _layer_bwd(dx, s, lw, consts, send):
    cos, sin, bmat = consts
    T = dx.shape[0]
    tq = min(T, _ATTN_TQ)
    g = {}
    d_act = _mm_nt(dx, lw["w_down"], name="mm_dact")
    g_down = _mm_tn(s["act"], dx, tk=D_FF // 2, tn=D_MODEL, name="mm_dwdown", tm=_WGRAD_ROWS // 2)
    tok = send("w_down", g_down.reshape(N_CHIPS, D_FF // N_CHIPS, D_MODEL))
    d_up, d_cw = _ffn_act_bwd(s["up"], d_act, lw["ffn_conv_w"] + tok)
    g["ffn_conv_w"] = d_cw.transpose(1, 0, 2).reshape(3, 2 * D_FF)
    tok = send("w_up", _mm_tn(s["h2"], d_up, tk=512, tn=D_FF // 2, name="mm_dwup", shards=N_CHIPS, tm=_WGRAD_ROWS))
    dx2, g["norm2_g"] = _mm_nt(d_up, lw["w_up"], name="mm_dh2", norm=(s["x_mid"], dx, lw["norm2_g"] + tok))
    d_mix = _mm_nt(dx2, lw["w_out"], name="mm_dmix")
    g_out = _mm_tn(s["mix"], dx2, tk=512, tn=D_MODEL, name="mm_dwout", tm=_WGRAD_ROWS // 2)
    tok = send("w_out", g_out.reshape(N_CHIPS, D_MODEL // N_CHIPS, D_MODEL))
    dp_c, g["conv_w"] = _convmix_bwd(s["p"], d_mix, lw["conv_w"] + tok)
    dp_b, g["sg_w"], d_bias, g["sg_norm_g"] = _sg_bwd(s["p"], d_mix, lw["sg_norm_g"], lw["sg_w"], lw["sg_wt"], lw["sg_bias"])
    g["sg_b"] = d_bias.reshape(SG_CHUNK, SG_W // HEAD_DIM, HEAD_DIM).sum(axis=-1).T
    dq, dkt, dvt = _attn_bwd(s["qn"], s["o"], d_mix, s["lse"], s["kt"], s["kd"], s["vt"], tq)
    dp_a, d_gq, d_gk = _qk_bwd(s["p"], dq, dkt, dvt, lw["gq"], lw["gk"], cos, sin, bmat)
    g["q_norm_g"] = d_gq.reshape(ATTN_W // HEAD_DIM, HEAD_DIM).sum(axis=0)
    g["k_norm_g"] = d_gk.reshape(KV_W // HEAD_DIM, HEAD_DIM).sum(axis=0)
    dp = jnp.concatenate([dp_a, dp_b, dp_c], axis=1)
    tok = send("w_in", _mm_tn(s["h"], dp, tk=D_MODEL, tn=512, name="mm_dwin", shards=N_CHIPS, tm=_WGRAD_ROWS))
    dx_in, g["norm1_g"] = _mm_nt(dp, lw["w_in"], name="mm_dh", norm=(s["x"], dx2, lw["norm1_g"] + tok))
    return dx_in, g


def _layer_weights(l, full, small):
    sg_w = small["sg_w"][l]
    sg_b = small["sg_b"][l]
    return dict(
        norm1_g=small["norm1_g"][l][None, :], norm2_g=small["norm2_g"][l][None, :],
        gq=jnp.tile(small["q_norm_g"][l], ATTN_W // HEAD_DIM)[None, :],
        gk=jnp.tile(small["k_norm_g"][l], KV_W // HEAD_DIM)[None, :],
        sg_norm_g=small["sg_norm_g"][l][None, :],
        sg_w=sg_w.astype(BF16), sg_wt=sg_w.transpose(0, 2, 1).astype(BF16),
        sg_bias=jnp.repeat(sg_b.T, HEAD_DIM, axis=1),
        **full)


_BIG = ("w_in", "w_out", "ffn_w_up", "ffn_w_down")
_SMALL_REPL = ("norm1_g", "q_norm_g", "k_norm_g", "sg_norm_g", "sg_w", "sg_b", "norm2_g")
_SMALL_SHARD = ("conv_w", "ffn_conv_w")
_ORDER = ("norm1_g", "w_in", "q_norm_g", "k_norm_g", "sg_norm_g", "sg_w", "sg_b", "conv_w", "w_out", "norm2_g",
          "ffn_w_up", "ffn_conv_w", "ffn_w_down")


def kernel(x, norm1_g, w_in, q_norm_g, k_norm_g, sg_norm_g, sg_w, sg_b, conv_w, w_out, norm2_g, ffn_w_up, ffn_conv_w, ffn_w_down, loss_target, m_norm1_g, m_w_in, m_q_norm_g, m_k_norm_g, m_sg_norm_g, m_sg_w, m_sg_b, m_conv_w, m_w_out, m_norm2_g, m_ffn_w_up, m_ffn_conv_w, m_ffn_w_down, v_norm1_g, v_w_in, v_q_norm_g, v_k_norm_g, v_sg_norm_g, v_sg_w, v_sg_b, v_conv_w, v_w_out, v_norm2_g, v_ffn_w_up, v_ffn_conv_w, v_ffn_w_down):
    w = dict(norm1_g=norm1_g, w_in=w_in, q_norm_g=q_norm_g, k_norm_g=k_norm_g, sg_norm_g=sg_norm_g, sg_w=sg_w,
             sg_b=sg_b, conv_w=conv_w, w_out=w_out, norm2_g=norm2_g, ffn_w_up=ffn_w_up, ffn_conv_w=ffn_conv_w,
             ffn_w_down=ffn_w_down)
    mom = dict(norm1_g=m_norm1_g, w_in=m_w_in, q_norm_g=m_q_norm_g, k_norm_g=m_k_norm_g, sg_norm_g=m_sg_norm_g,
               sg_w=m_sg_w, sg_b=m_sg_b, conv_w=m_conv_w, w_out=m_w_out, norm2_g=m_norm2_g, ffn_w_up=m_ffn_w_up,
               ffn_conv_w=m_ffn_conv_w, ffn_w_down=m_ffn_w_down)
    var = dict(norm1_g=v_norm1_g, w_in=v_w_in, q_norm_g=v_q_norm_g, k_norm_g=v_k_norm_g, sg_norm_g=v_sg_norm_g,
               sg_w=v_sg_w, sg_b=v_sg_b, conv_w=v_conv_w, w_out=v_w_out, norm2_g=v_norm2_g, ffn_w_up=v_ffn_w_up,
               ffn_conv_w=v_ffn_conv_w, ffn_w_down=v_ffn_w_down)
    L = DEPTH
    T = x.shape[1]
    xs = x.reshape(T, D_MODEL)
    target = loss_target.reshape(T, D_MODEL)

    chip = 2 * lax.axis_index("x") + lax.axis_index("y")

    shards = [_cast_bf16(w[n].reshape(-1, w[n].shape[-1]), "cast_" + n).reshape(w[n].shape) for n in _BIG]
    shards += [conv_w, ffn_conv_w]
    w_in_s, w_out_s, w_up_s, w_down_s, conv_s, fconv_s = shards
    gathers = []
    for l in range(L):
        gathers.append((_chips_start([w_in_s[l], w_out_s[l], conv_s[l]], False, "gather_start_%da" % l),
                        _chips_start([w_up_s[l], w_down_s[l], fconv_s[l]], False, "gather_start_%db" % l)))
    start_token = sum(h[4][0, 0] for pair in gathers for h in pair)
    consts = _rope_tables(T) + (_head_mean_matrix(),)

    def gathered(handle, after, name):
        own, lands = _chips_wait(handle, after, False, name)
        return [lax.dynamic_update_slice(ld, o[None], (chip,) + (jnp.int32(0),) * o.ndim) for ld, o in zip(lands, own)]

    saved, lws = [], []
    act_x, act_h = xs, None
    for l in range(L):
        g_in, g_out, g_conv = gathered(gathers[l][0], act_x if l else gathers[-1][1][4], "gather_wait_%da" % l)
        lw = _layer_weights(l, dict(w_in=g_in, w_out=g_out.reshape(D_MODEL, D_MODEL),
                                    conv_w=g_conv.transpose(1, 0, 2).reshape(3, CONV_W)), w)
        if l == 0:
            lw["norm1_g"] = lw["norm1_g"] + start_token

        def ffn_weights(after, l=l):
            g_up, g_down, g_fconv = gathered(gathers[l][1], after, "gather_wait_%db" % l)
            return dict(w_up=g_up, w_down=g_down.reshape(D_FF, D_MODEL),
                        ffn_conv_w=g_fconv.transpose(1, 0, 2).reshape(3, 2 * D_FF))

        next_g = w["norm1_g"][l + 1][None, :] if l + 1 < L else None
        act_x, act_h, s = _layer_fwd(act_x, act_h, lw, consts, ffn_weights, next_g)
        saved.append(s)
        lws.append(lw)
    dx, loss_blk = _loss_head(act_x, target)
    loss = lax.psum(loss_blk[0, 0], ("x", "y", "c"))

    grads = [None] * L
    partial = [None] * L

    def collect(pending, after, l):
        sums = {}
        for name, handle in pending:
            own, lands = _chips_wait(handle, after, True, "grad_wait_%d_%s" % (l, name))
            sums[name] = _sum_chips_own(lands[0], own[0], "sum_chips_" + name)
        return sums

    pending_prev = None
    for l in reversed(range(L)):
        pending = []

        def send(name, g4, l=l, pending=pending):
            handle = _chips_start([g4], True, "grad_start_%d_%s" % (l, name))
            pending.append((name, handle))
            return handle[4][0, 0]

        dx, g = _layer_bwd(dx, saved[l], lws[l], consts, send)
        grads[l] = g
        if pending_prev is not None:
            partial[l + 1] = collect(pending_prev, dx, l + 1)
        pending_prev = pending
    grad_x = dx.reshape(x.shape)

    small_names = _SMALL_REPL + _SMALL_SHARD
    small_local = [jnp.stack([grads[l][n].reshape(-1) for l in range(L)]) for n in small_names]
    small_handle = _chips_start([_pack(small_local)], False, "small_start")
    partial[0] = collect(pending_prev, small_handle[4], 0)

    short = dict(w_in="w_in", w_out="w_out", ffn_w_up="w_up", ffn_w_down="w_down")
    mine = [jnp.stack([partial[l][short[n]] for l in range(L)]) for n in _BIG]
    small_own, small_lands = _chips_wait(small_handle, mine[-1], False, "small_wait")
    mine.append(_sum_chips_own(small_lands[0], small_own[0], "sum_chips_small"))
    theirs = _exchange_sibling(mine)
    grad = dict(zip(small_names, _unpack(_add_pair(mine[-1], theirs[-1], "add_small"), small_local)))
    for n in _SMALL_REPL:
        grad[n] = grad[n].reshape(w[n].shape)
    for n in _SMALL_SHARD:
        full_w = grad[n].reshape(L, 3, -1)
        width = w[n].shape[-1]
        grad[n] = lax.dynamic_slice_in_dim(full_w, chip * width, width, axis=2)
    delta, new_m, new_v = {}, {}, {}
    for n, ga, gb in zip(_BIG, mine, theirs):
        shp = w[n].shape
        v2 = lambda a: a.reshape(-1, shp[-1])
        gsum, d, mn, vn = _adamw_sum(v2(w[n]), v2(ga), v2(gb), v2(mom[n]), v2(var[n]), "adamw_" + n)
        grad[n], delta[n], new_m[n], new_v[n] = gsum.reshape(shp), d.reshape(shp), mn.reshape(shp), vn.reshape(shp)
    for group, gname in ((_SMALL_REPL, "adamw_small"), (_SMALL_SHARD, "adamw_conv")):
        like = [w[n] for n in group]
        outs = _adamw(_pack([w[n] for n in group]), _pack([grad[n] for n in group]), _pack([mom[n] for n in group]),
                      _pack([var[n] for n in group]), gname)
        for res, dst in zip(outs, (delta, new_m, new_v)):
            for n, a in zip(group, _unpack(res, like)):
                dst[n] = a

    return (loss, grad_x, *[grad[n] for n in _ORDER], *[delta[n] for n in _ORDER],
            *[new_m[n] for n in _ORDER], *[new_v[n] for n in _ORDER])
```

```python
import jax
import jax.numpy as jnp
from jax import lax
from jax.experimental import pallas as pl
from jax.experimental.pallas import tpu as pltpu

F32 = jnp.float32
BF16 = jnp.bfloat16

DEPTH = 4
D_MODEL = 1024
HEAD_DIM = 64
ATTN_W = 512
KV_W = 128
SG_W = 256
CONV_W = 256
SG_CHUNK = 128
D_FF = 2816
PROJ_W = 2048
GRID_W = 64
ROPE_THETA = 10000.0
AXIS_DIM = HEAD_DIM // 2
EPS = 1e-6
N_CHIPS = 4

ADAM_LR = 0.001
ADAM_B1 = 0.9
ADAM_B2 = 0.999
ADAM_EPS = 1e-08
ADAM_WD = 0.01
ADAM_STEP = 10

_ROW_TILE = 512
_FFN_ROW_TILE = 256
_WIDE_ROW_TILE = 1024
_WGRAD_ROWS = 4096
LANES = 128
HALO = 8
VMEM_LIMIT_BYTES = 56 * 1024 * 1024
MESH = pl.DeviceIdType.MESH
ANY = pl.BlockSpec(memory_space=pl.ANY)


def _cp(*sem):
    return pltpu.CompilerParams(dimension_semantics=sem if sem else None,
                                vmem_limit_bytes=VMEM_LIMIT_BYTES)


def _sds(shape, dtype):
    return jax.ShapeDtypeStruct(shape, dtype)


def _dot(a, b):
    return jnp.dot(a, b, preferred_element_type=F32)


def _dot_nt(a, b):
    return lax.dot_general(a, b, (((1,), (1,)), ((), ())), preferred_element_type=F32)


def _dot_tn(a, b):
    return lax.dot_general(a, b, (((0,), (0,)), ((), ())), preferred_element_type=F32)


def _norm_fwd(x, g):
    T, Dm = x.shape
    tm = min(T, _ROW_TILE)

    def body(x_ref, g_ref, o_ref):
        xv = x_ref[...]
        r = lax.rsqrt(jnp.mean(xv * xv, axis=-1, keepdims=True) + EPS)
        o_ref[...] = ((xv * r) * g_ref[...]).astype(BF16)

    return pl.pallas_call(
        body, name="norm_fwd", grid=(T // tm,),
        in_specs=[pl.BlockSpec((tm, Dm), lambda i: (i, 0)), pl.BlockSpec((1, Dm), lambda i: (0, 0))],
        out_specs=pl.BlockSpec((tm, Dm), lambda i: (i, 0)),
        out_shape=_sds((T, Dm), BF16), compiler_params=_cp("parallel"))(x, g)


def _whole(w):
    return pl.BlockSpec(w.shape, lambda *g: (0,) * w.ndim)


def _mm_nn(a, w, *, out_dtype, name, res=None, tm=None, norm_g=None):
    M, K = a.shape
    N = w.shape[-1] if w.ndim == 2 else w.shape[0] * w.shape[2]
    tm = min(M, tm or _ROW_TILE)
    has_res = res is not None
    has_norm = norm_g is not None
    assert not has_norm or w.ndim == 2

    def body(*refs):
        a_ref, w_ref = refs[0], refs[1]
        o_ref = refs[-2] if has_norm else refs[-1]
        av = a_ref[...].astype(BF16)
        parts = [w_ref[...]] if w.ndim == 2 else [w_ref[s] for s in range(w.shape[0])]
        ns = N // len(parts)
        for s, wv in enumerate(parts):
            cols = pl.ds(s * ns, ns)
            acc = _dot(av, wv)
            if has_res:
                acc = acc + refs[2][:, cols]
            o_ref[:, cols] = acc.astype(out_dtype)
        if has_norm:
            r = lax.rsqrt(jnp.mean(acc * acc, axis=-1, keepdims=True) + EPS)
            refs[-1][...] = ((acc * r) * refs[2 + has_res][...]).astype(BF16)

    row = pl.BlockSpec((tm, N), lambda i: (i, 0))
    in_specs = [pl.BlockSpec((tm, K), lambda i: (i, 0)), _whole(w)]
    args = [a, w]
    if has_res:
        in_specs.append(row)
        args.append(res)
    if has_norm:
        in_specs.append(pl.BlockSpec((1, N), lambda i: (0, 0)))
        args.append(norm_g)
    return pl.pallas_call(
        body, name=name, grid=(M // tm,), in_specs=in_specs, out_specs=[row, row] if has_norm else row,
        out_shape=[_sds((M, N), out_dtype), _sds((M, N), BF16)] if has_norm else _sds((M, N), out_dtype),
        compiler_params=_cp("parallel"))(*args)


def _a_spec(a, tm, tn, row_of, col_of):
    if a.ndim == 2:
        return pl.BlockSpec((tm, tn), lambda *g: (row_of(*g), col_of(*g)))
    bph = a.shape[2] // tn
    return pl.BlockSpec((None, tm, tn), lambda *g: (col_of(*g) // bph, row_of(*g), col_of(*g) % bph))


def _a_cols(a):
    return a.shape[1] if a.ndim == 2 else a.shape[0] * a.shape[2]


def _mm_nt(a, w, *, name, tm=None, norm=None):
    M = a.shape[-2]
    Kw = w.shape[-2]
    tm = min(M, tm or _ROW_TILE)

    def product(a_ref, w_ref):
        if w.ndim == 2:
            return _dot_nt(a_ref[...].astype(BF16), w_ref[...])
        S, ns = w.shape[0], w.shape[2]
        acc = None
        for s in range(S):
            if a.ndim == 2:
                piece = a_ref[:, pl.ds(s * ns, ns)]
            else:
                per_half = S // 2
                piece = a_ref[s // per_half, :, pl.ds((s % per_half) * ns, ns)]
            part = _dot_nt(piece.astype(BF16), w_ref[s])
            acc = part if acc is None else acc + part
        return acc

    def body(a_ref, w_ref, o_ref):
        o_ref[...] = product(a_ref, w_ref)

    def body_norm(a_ref, w_ref, x_ref, dr_ref, g_ref, dx_ref, dg_ref):
        dhv = product(a_ref, w_ref)
        xv = x_ref[...]
        r = lax.rsqrt(jnp.mean(xv * xv, axis=-1, keepdims=True) + EPS)
        xh = xv * r
        dxh = dhv * g_ref[...]
        dx_ref[...] = dr_ref[...] + r * (dxh - xh * jnp.mean(dxh * xh, axis=-1, keepdims=True))
        _accum(dg_ref, pl.program_id(0), jnp.sum(dhv * xh, axis=0, keepdims=True))

    a_spec = (pl.BlockSpec((tm, a.shape[1]), lambda i: (i, 0)) if a.ndim == 2
              else pl.BlockSpec((2, tm, a.shape[2]), lambda i: (0, i, 0)))
    row = pl.BlockSpec((tm, Kw), lambda i: (i, 0))
    if norm is None:
        return pl.pallas_call(
            body, name=name, grid=(M // tm,), in_specs=[a_spec, _whole(w)], out_specs=row,
            out_shape=_sds((M, Kw), F32), compiler_params=_cp("parallel"))(a, w)
    vec = pl.BlockSpec((1, Kw), lambda i: (0, 0))
    return pl.pallas_call(
        body_norm, name=name, grid=(M // tm,), in_specs=[a_spec, _whole(w), row, row, vec], out_specs=[row, vec],
        out_shape=[_sds((M, Kw), F32), _sds((1, Kw), F32)], compiler_params=_cp("arbitrary"))(a, w, *norm)


def _mm_tn(a, b, *, tk, tn, name, shards=None, tm=None):
    M, K = a.shape
    N = _a_cols(b)
    tm = min(M, tm or _ROW_TILE)

    def body(a_ref, b_ref, o_ref):
        m = pl.program_id(2)
        part = _dot_tn(a_ref[...].astype(BF16), b_ref[...].astype(BF16))

        @pl.when(m == 0)
        def _():
            o_ref[...] = part

        @pl.when(m > 0)
        def _():
            o_ref[...] += part

    if shards is None:
        out_spec = pl.BlockSpec((tk, tn), lambda k, j, m: (k, j))
        out_shape = _sds((K, N), F32)
    else:
        bps = (N // shards) // tn
        out_spec = pl.BlockSpec((None, tk, tn), lambda k, j, m: (j // bps, k, j % bps))
        out_shape = _sds((shards, K, N // shards), F32)
    return pl.pallas_call(
        body, name=name, grid=(K // tk, N // tn, M // tm),
        in_specs=[pl.BlockSpec((tm, tk), lambda k, j, m: (m, k)),
                  _a_spec(b, tm, tn, lambda k, j, m: m, lambda k, j, m: j)],
        out_specs=out_spec, out_shape=out_shape,
        compiler_params=_cp("parallel", "parallel", "arbitrary"))(a, b)


def _halo_specs(T, tm, cw, ic):
    nb = tm // HALO
    last = T // HALO - 1

    def mk(rows, row_of):
        return pl.BlockSpec((rows, cw), lambda *g: (row_of(ic(*g)[0]), ic(*g)[1]))

    return [mk(HALO, lambda i: jnp.maximum(i * nb - 1, 0)), mk(tm, lambda i: i),
            mk(HALO, lambda i: jnp.minimum((i + 1) * nb, last))]


def _ext(prev_ref, cur_ref, next_ref, i, n):
    p = jnp.where(i > 0, prev_ref[...].astype(F32), 0.0)
    nx = jnp.where(i < n - 1, next_ref[...].astype(F32), 0.0)
    return jnp.concatenate([p, cur_ref[...].astype(F32), nx], axis=0)


def _dn(e):
    return pltpu.roll(e, 1, 0)


def _up(e):
    return pltpu.roll(e, e.shape[0] - 1, 0)


def _mid(e):
    return e[HALO:e.shape[0] - HALO]


def _taps(e):
    return _dn(e), e, _up(e)


def _conv3(taps, w):
    return taps[0] * w[0:1] + taps[1] * w[1:2] + taps[2] * w[2:3]


def _conv3_t(e, w):
    return _up(e) * w[0:1] + e * w[1:2] + _dn(e) * w[2:3]


def _conv3_wgrad(d, taps):
    return jnp.concatenate([jnp.sum(_mid(d * tap), axis=0, keepdims=True) for tap in taps], axis=0)


def _sigmoid(x):
    return 1.0 / (1.0 + jnp.exp(-x))


def _accum(ref, i, part):
    @pl.when(i == 0)
    def _():
        ref[...] = part

    @pl.when(i > 0)
    def _():
        ref[...] += part


def _ffn_act_fwd(up, cw):
    T = up.shape[0]
    tm = min(T, _FFN_ROW_TILE)
    cb = D_FF // 2
    nblk = D_FF // cb
    n = T // tm

    def body(gp, gc, gn, vp, vc, vn, wg_ref, wv_ref, o_ref):
        i = pl.program_id(1)
        gate = _conv3(_taps(_ext(gp, gc, gn, i, n)), wg_ref[...])
        val = _conv3(_taps(_ext(vp, vc, vn, i, n)), wv_ref[...])
        o_ref[...] = _mid(gate * _sigmoid(gate) * val).astype(BF16)

    return pl.pallas_call(
        body, name="ffn_act_fwd", grid=(nblk, n),
        in_specs=_halo_specs(T, tm, cb, lambda j, i: (i, j)) + _halo_specs(T, tm, cb, lambda j, i: (i, j + nblk))
        + [pl.BlockSpec((3, cb), lambda j, i: (0, j)), pl.BlockSpec((3, cb), lambda j, i: (0, j + nblk))],
        out_specs=pl.BlockSpec((tm, cb), lambda j, i: (i, j)),
        out_shape=_sds((T, D_FF), BF16), compiler_params=_cp("parallel", "parallel"))(
            up, up, up, up, up, up, cw, cw)


def _ffn_act_bwd(up, dact, cw):
    T = up.shape[0]
    tm = min(T, _FFN_ROW_TILE)
    cb = D_FF // 2
    nblk = D_FF // cb
    n = T // tm

    def body(gp, gc, gn, vp, vc, vn, dp_, dc, dn_, wg_ref, wv_ref, dup_ref, dcw_ref):
        i = pl.program_id(1)
        wg, wv = wg_ref[...], wv_ref[...]
        eg = _taps(_ext(gp, gc, gn, i, n))
        ev = _taps(_ext(vp, vc, vn, i, n))
        ed = _ext(dp_, dc, dn_, i, n)
        gate = _conv3(eg, wg)
        val = _conv3(ev, wv)
        sg = _sigmoid(gate)
        d_gate = ed * val * (sg * (1.0 + gate * (1.0 - sg)))
        d_val = ed * (gate * sg)
        dup_ref[0] = _mid(_conv3_t(d_gate, wg)).astype(BF16)
        dup_ref[1] = _mid(_conv3_t(d_val, wv)).astype(BF16)
        part = jnp.stack([_conv3_wgrad(d_gate, eg), _conv3_wgrad(d_val, ev)], axis=0)
        _accum(dcw_ref, i, part)

    return pl.pallas_call(
        body, name="ffn_act_bwd", grid=(nblk, n),
        in_specs=_halo_specs(T, tm, cb, lambda j, i: (i, j)) + _halo_specs(T, tm, cb, lambda j, i: (i, j + nblk))
        + _halo_specs(T, tm, cb, lambda j, i: (i, j))
        + [pl.BlockSpec((3, cb), lambda j, i: (0, j)), pl.BlockSpec((3, cb), lambda j, i: (0, j + nblk))],
        out_specs=[pl.BlockSpec((2, tm, cb), lambda j, i: (0, i, j)),
                   pl.BlockSpec((2, 3, cb), lambda j, i: (0, 0, j))],
        out_shape=[_sds((2, T, D_FF), BF16), _sds((2, 3, D_FF), F32)],
        compiler_params=_cp("parallel", "arbitrary"))(up, up, up, up, up, up, dact, dact, dact, cw, cw)


_CB_BLK, _CC_BLK, _CX_BLK = 5, 6, 7


def _convmix_fwd(p, w):
    T = p.shape[0]
    tm = min(T, _ROW_TILE)
    n = T // tm

    def body(cb_ref, ccp, ccc, ccn, cxp, cxc, cxn, w_ref, o_ref):
        i = pl.program_id(0)
        z = _ext(ccp, ccc, ccn, i, n) * _ext(cxp, cxc, cxn, i, n)
        o_ref[...] = (cb_ref[...] * _mid(_conv3(_taps(z), w_ref[...]))).astype(BF16)

    return pl.pallas_call(
        body, name="convmix_fwd", grid=(n,),
        in_specs=[pl.BlockSpec((tm, CONV_W), lambda i: (i, _CB_BLK))]
        + _halo_specs(T, tm, CONV_W, lambda i: (i, _CC_BLK)) + _halo_specs(T, tm, CONV_W, lambda i: (i, _CX_BLK))
        + [pl.BlockSpec((3, CONV_W), lambda i: (0, 0))],
        out_specs=pl.BlockSpec((tm, CONV_W), lambda i: (i, 0)),
        out_shape=_sds((T, CONV_W), BF16), compiler_params=_cp("parallel"))(p, p, p, p, p, p, p, w)


def _convmix_bwd(p, dmix, w):
    T = p.shape[0]
    tm = min(T, _ROW_TILE)
    n = T // tm
    dblk = (ATTN_W + SG_W) // CONV_W

    def body(cbp, cbc, cbn, ccp, ccc, ccn, cxp, cxc, cxn, dp_, dc, dn_, w_ref, o_ref, dw_ref):
        i = pl.program_id(0)
        wv = w_ref[...]
        ecb = _ext(cbp, cbc, cbn, i, n)
        ecc = _ext(ccp, ccc, ccn, i, n)
        ecx = _ext(cxp, cxc, cxn, i, n)
        ed = _ext(dp_, dc, dn_, i, n)
        z = _taps(ecc * ecx)
        d_cz = ed * ecb
        d_z = _conv3_t(d_cz, wv)
        o_ref[...] = jnp.concatenate([_mid(ed * _conv3(z, wv)), _mid(d_z * ecx), _mid(d_z * ecc)],
                                     axis=1).astype(BF16)
        _accum(dw_ref, i, _conv3_wgrad(d_cz, z))

    return pl.pallas_call(
        body, name="convmix_bwd", grid=(n,),
        in_specs=_halo_specs(T, tm, CONV_W, lambda i: (i, _CB_BLK)) + _halo_specs(T, tm, CONV_W, lambda i: (i, _CC_BLK))
        + _halo_specs(T, tm, CONV_W, lambda i: (i, _CX_BLK)) + _halo_specs(T, tm, CONV_W, lambda i: (i, dblk))
        + [pl.BlockSpec((3, CONV_W), lambda i: (0, 0))],
        out_specs=[pl.BlockSpec((tm, 3 * CONV_W), lambda i: (i, 0)), pl.BlockSpec((3, CONV_W), lambda i: (0, 0))],
        out_shape=[_sds((T, 3 * CONV_W), BF16), _sds((3, CONV_W), F32)],
        compiler_params=_cp("arbitrary"))(p, p, p, p, p, p, p, p, p, dmix, dmix, dmix, w)


_SU_BLK, _SV_BLK = 3, 4


def _sg_mixed(vnb, w_ref, bias, ch, pr, lo):
    vp = vnb[ch * SG_CHUNK:(ch + 1) * SG_CHUNK, pr * LANES:(pr + 1) * LANES]
    zero = jnp.zeros_like(vp)
    return (_dot(w_ref[2 * pr], jnp.where(lo, vp, zero)) + _dot(w_ref[2 * pr + 1], jnp.where(lo, zero, vp))
            + bias[:, pr * LANES:(pr + 1) * LANES]), vp


def _sg_fwd(p, g, w, bias):
    T = p.shape[0]
    tm = min(T, _ROW_TILE)

    def body(su_ref, sv_ref, g_ref, w_ref, b_ref, o_ref):
        lo = lax.broadcasted_iota(jnp.int32, (SG_CHUNK, LANES), 1) < HEAD_DIM
        sv = sv_ref[...]
        r = lax.rsqrt(jnp.mean(sv * sv, axis=-1, keepdims=True) + EPS)
        vnb = ((sv * r) * g_ref[...]).astype(BF16)
        bias_v = b_ref[...]
        for ch in range(tm // SG_CHUNK):
            for pr in range(2):
                mixed, _ = _sg_mixed(vnb, w_ref, bias_v, ch, pr, lo)
                rows, cols = pl.ds(ch * SG_CHUNK, SG_CHUNK), pl.ds(pr * LANES, LANES)
                o_ref[rows, cols] = (su_ref[rows, cols] * mixed).astype(BF16)

    return pl.pallas_call(
        body, name="sg_fwd", grid=(T // tm,),
        in_specs=[pl.BlockSpec((tm, SG_W), lambda i: (i, _SU_BLK)), pl.BlockSpec((tm, SG_W), lambda i: (i, _SV_BLK)),
                  pl.BlockSpec((1, SG_W), lambda i: (0, 0)), pl.BlockSpec((4, SG_CHUNK, SG_CHUNK), lambda i: (0, 0, 0)),
                  pl.BlockSpec((SG_CHUNK, SG_W), lambda i: (0, 0))],
        out_specs=pl.BlockSpec((tm, SG_W), lambda i: (i, 0)),
        out_shape=_sds((T, SG_W), BF16), compiler_params=_cp("parallel"))(p, p, g, w, bias)


def _sg_bwd(p, dmix, g, w, wt, bias):
    T = p.shape[0]
    tm = min(T, _ROW_TILE)
    dblk = ATTN_W // SG_W

    def body(su_ref, sv_ref, d_ref, g_ref, w_ref, wt_ref, b_ref, o_ref, dw_ref, db_ref, dg_ref, dvn_ref):
        i = pl.program_id(0)
        lo = lax.broadcasted_iota(jnp.int32, (SG_CHUNK, LANES), 1) < HEAD_DIM
        sv = sv_ref[...]
        gv = g_ref[...]
        r = lax.rsqrt(jnp.mean(sv * sv, axis=-1, keepdims=True) + EPS)
        xh = sv * r
        vnb = (xh * gv).astype(BF16)
        bias_v = b_ref[...]
        dw = [jnp.zeros((SG_CHUNK, SG_CHUNK), F32) for _ in range(4)]
        db = jnp.zeros((SG_CHUNK, SG_W), F32)
        for ch in range(tm // SG_CHUNK):
            dbs = []
            for pr in range(2):
                mixed, vp = _sg_mixed(vnb, w_ref, bias_v, ch, pr, lo)
                rows, cols = pl.ds(ch * SG_CHUNK, SG_CHUNK), pl.ds(pr * LANES, LANES)
                dgo = d_ref[rows, cols]
                o_ref[rows, cols] = (dgo * mixed).astype(BF16)
                dm = dgo * su_ref[rows, cols]
                dmb = dm.astype(BF16)
                zero = jnp.zeros_like(dmb)
                dw[2 * pr] += _dot_nt(jnp.where(lo, dmb, zero), vp)
                dw[2 * pr + 1] += _dot_nt(jnp.where(lo, zero, dmb), vp)
                dvn_ref[rows, cols] = jnp.where(lo, _dot(wt_ref[2 * pr], dmb), _dot(wt_ref[2 * pr + 1], dmb))
                dbs.append(dm)
            db += jnp.concatenate(dbs, axis=1)
        dvn = dvn_ref[...]
        dxh = dvn * gv
        o_ref[:, pl.ds(SG_W, SG_W)] = (r * (dxh - xh * jnp.mean(dxh * xh, axis=-1, keepdims=True))).astype(BF16)
        _accum(dw_ref, i, jnp.stack(dw, axis=0))
        _accum(db_ref, i, db)
        _accum(dg_ref, i, jnp.sum(dvn * xh, axis=0, keepdims=True))

    wspec = pl.BlockSpec((4, SG_CHUNK, SG_CHUNK), lambda i: (0, 0, 0))
    return pl.pallas_call(
        body, name="sg_bwd", grid=(T // tm,),
        in_specs=[pl.BlockSpec((tm, SG_W), lambda i: (i, _SU_BLK)), pl.BlockSpec((tm, SG_W), lambda i: (i, _SV_BLK)),
                  pl.BlockSpec((tm, SG_W), lambda i: (i, dblk)), pl.BlockSpec((1, SG_W), lambda i: (0, 0)),
                  wspec, wspec, pl.BlockSpec((SG_CHUNK, SG_W), lambda i: (0, 0))],
        out_specs=[pl.BlockSpec((tm, 2 * SG_W), lambda i: (i, 0)), wspec,
                   pl.BlockSpec((SG_CHUNK, SG_W), lambda i: (0, 0)), pl.BlockSpec((1, SG_W), lambda i: (0, 0))],
        out_shape=[_sds((T, 2 * SG_W), BF16), _sds((4, SG_CHUNK, SG_CHUNK), F32),
                   _sds((SG_CHUNK, SG_W), F32), _sds((1, SG_W), F32)],
        scratch_shapes=[pltpu.VMEM((tm, SG_W), F32)],
        compiler_params=_cp("arbitrary"))(p, p, dmix, g, w, wt, bias)


_ATTN_TQ = 256
_ATTN_TK = 1024
_ATTN_FWD_TK = 512
_SOFTMAX_STRIP = 32
_ONES_ROWS = 16


def _head_mean(v, bmat):
    hi = v.astype(BF16)
    lo = (v - hi.astype(F32)).astype(BF16)
    return _dot(hi, bmat) + _dot(lo, bmat)


def _swap16(y):
    lane = lax.broadcasted_iota(jnp.int32, y.shape, 1)
    return jnp.where(lane % 32 < 16, pltpu.roll(y, y.shape[1] - 16, 1), pltpu.roll(y, 16, 1))


def _rope(y, cos, sin):
    return y * cos + _swap16(y) * sin


def _rope_t(dy, cos, sin):
    return dy * cos + _swap16(dy * sin)


def _dup_rows(t, gidx):
    h = t[gidx * HEAD_DIM:(gidx + 1) * HEAD_DIM]
    return jnp.concatenate([h, h], axis=0)


def _qk_prep(p, gq, gk, cos, sin, bmat, tk, tkf):
    T = p.shape[0]
    nk = T // tk
    sub = tk // tkf
    scale = HEAD_DIM ** -0.5

    def body(q_ref, kv_ref, gq_ref, gk_ref, cos_ref, sin_ref, b_ref, qo_ref, kt_ref, kd_ref, vt_ref, v1_ref):
        cosv, sinv, bm = cos_ref[...], sin_ref[...], b_ref[...]
        for pr in range(ATTN_W // LANES):
            cols = pl.ds(pr * LANES, LANES)
            xq = q_ref[:, cols]
            r = lax.rsqrt(_head_mean(xq * xq, bm) + EPS)
            qo_ref[:, cols] = (_rope((xq * r) * gq_ref[:, cols], cosv, sinv) * scale).astype(BF16)
        xk = kv_ref[:, pl.ds(0, LANES)]
        r = lax.rsqrt(_head_mean(xk * xk, bm) + EPS)
        kt = _rope((xk * r) * gk_ref[...], cosv, sinv).T
        vt = kv_ref[:, pl.ds(LANES, LANES)].T
        for gidx in range(2):
            kdup = _dup_rows(kt, gidx)
            kt_ref[gidx] = kdup.astype(BF16)
            vt_ref[gidx] = _dup_rows(vt, gidx).astype(BF16)
            v1 = jnp.concatenate([vt[gidx * HEAD_DIM:(gidx + 1) * HEAD_DIM],
                                  jnp.ones((_ONES_ROWS, tk), F32)], axis=0).astype(BF16)
            for b in range(sub):
                v1_ref[gidx, b] = v1[:, b * tkf:(b + 1) * tkf]
            kd_ref[gidx] = kdup.T.astype(BF16)

    tspec = pl.BlockSpec((2, None, LANES, tk), lambda i: (0, i, 0, 0))
    dspec = pl.BlockSpec((2, tk, LANES), lambda i: (0, i, 0))
    tab = pl.BlockSpec((tk, LANES), lambda i: (i, 0))
    return pl.pallas_call(
        body, name="qk_prep", grid=(nk,),
        in_specs=[pl.BlockSpec((tk, ATTN_W), lambda i: (i, 0)), pl.BlockSpec((tk, 2 * KV_W), lambda i: (i, ATTN_W // (2 * KV_W))),
                  pl.BlockSpec((1, ATTN_W), lambda i: (0, 0)), pl.BlockSpec((1, KV_W), lambda i: (0, 0)),
                  tab, tab, pl.BlockSpec((LANES, LANES), lambda i: (0, 0))],
        out_specs=[pl.BlockSpec((tk, ATTN_W), lambda i: (i, 0)), tspec, dspec, tspec,
                   pl.BlockSpec((2, sub, HEAD_DIM + _ONES_ROWS, tkf), lambda i: (0, i, 0, 0))],
        out_shape=[_sds((T, ATTN_W), BF16), _sds((2, nk, LANES, tk), BF16), _sds((2, T, LANES), BF16),
                   _sds((2, nk, LANES, tk), BF16), _sds((2, nk * sub, HEAD_DIM + _ONES_ROWS, tkf), BF16)],
        compiler_params=_cp("parallel"))(p, p, gq, gk, cos, sin, bmat)


def _stack_heads(t):
    lo = lax.broadcasted_iota(jnp.int32, (t.shape[0], LANES), 1) < HEAD_DIM
    parts = []
    for pr in range(2):
        tp = t[:, pr * LANES:(pr + 1) * LANES]
        zero = jnp.zeros_like(tp)
        parts += [jnp.where(lo, tp, zero), jnp.where(lo, zero, tp)]
    return jnp.concatenate(parts, axis=0)


def _rows8_reduce(s, op):
    parts = [s[r:r + 8] for r in range(0, s.shape[0], 8)]
    while len(parts) > 1:
        parts = [op(parts[k], parts[k + 1]) for k in range(0, len(parts) - 1, 2)] + (
            [parts[-1]] if len(parts) % 2 else [])
    return parts[0]


def _attn_fwd(q, kd, v1, tq):
    T = q.shape[0]
    nk, tk = v1.shape[1], v1.shape[3]
    vrows = v1.shape[2]
    nq = T // tq
    sq = 4 * tq
    strip = _SOFTMAX_STRIP
    depth = 4
    assert nk % depth == 0

    def body(q_ref, kd_ref, v1_ref, o_ref, lse_ref, qst_ref, s0_ref, s1_ref, s2_ref, s3_ref, pa_ref, pb_ref,
             m_ref, acc_ref):
        s_refs = (s0_ref, s1_ref, s2_ref, s3_ref)
        p_refs = (pa_ref, pb_ref)
        qst_ref[...] = _stack_heads(q_ref[...]).astype(F32).T.astype(BF16)
        m_ref[...] = jnp.full((1, sq), -jnp.inf, F32)
        acc_ref[...] = jnp.zeros((vrows, sq), F32)

        def scores(j):
            return _dot(kd_ref[pl.ds(pl.multiple_of(j * tk, tk), tk), :], qst_ref[...])

        def block_max(s_ref):
            m8 = None
            for c in range(tk // strip):
                part = _rows8_reduce(s_ref[pl.ds(c * strip, strip), :], jnp.maximum)
                m8 = part if m8 is None else jnp.maximum(m8, part)
            return m8

        def exp_pass(s_ref, p_ref, m8):
            m_old = m_ref[...]
            m_new = jnp.maximum(m_old, jnp.max(m8, axis=0, keepdims=True))
            m_ref[...] = m_new
            for c in range(tk // strip):
                rows = pl.ds(c * strip, strip)
                p_ref[rows, :] = jnp.exp(s_ref[rows, :] - m_new).astype(BF16)
            return jnp.exp(m_old - m_new)

        def apply(p_ref, alpha, j):
            acc_ref[...] = alpha * acc_ref[...] + _dot(v1_ref[j], p_ref[...])

        s_refs[0][...] = scores(0)
        s_refs[1][...] = scores(1)

        def trip(t, max_cur):
            for u in range(depth):
                j = depth * t + u
                s_refs[(u + 2) % depth][...] = scores(jnp.minimum(j + 2, nk - 1))
                alpha = exp_pass(s_refs[u], p_refs[u % 2], max_cur)
                max_cur = block_max(s_refs[(u + 1) % depth])
                apply(p_refs[u % 2], alpha, j)
            return max_cur

        lax.fori_loop(0, nk // depth, trip, block_max(s_refs[0]))
        l = acc_ref[pl.ds(HEAD_DIM, 1), :]
        on = acc_ref[pl.ds(0, HEAD_DIM), :] / l
        pairs = []
        for pr in range(2):
            two = jnp.concatenate([on[:, (2 * pr) * tq:(2 * pr + 1) * tq], on[:, (2 * pr + 1) * tq:(2 * pr + 2) * tq]],
                                  axis=0)
            pairs.append(two.T)
        o_ref[...] = jnp.concatenate(pairs, axis=1).astype(BF16)
        lse_ref[...] = jnp.broadcast_to(m_ref[...] + jnp.log(l), (LANES, sq)).T

    row = pltpu.VMEM((1, sq), F32)
    return pl.pallas_call(
        body, name="attn_fwd", grid=(2, nq),
        in_specs=[pl.BlockSpec((tq, 2 * LANES), lambda g, i: (i, g)),
                  pl.BlockSpec((None, T, LANES), lambda g, i: (g, 0, 0)),
                  pl.BlockSpec((None, nk, vrows, tk), lambda g, i: (g, 0, 0, 0))],
        out_specs=[pl.BlockSpec((tq, 2 * LANES), lambda g, i: (i, g)),
                   pl.BlockSpec((None, None, sq, LANES), lambda g, i: (g, i, 0, 0))],
        out_shape=[_sds((T, ATTN_W), BF16), _sds((2, nq, sq, LANES), F32)],
        scratch_shapes=[pltpu.VMEM((LANES, sq), BF16)] + [pltpu.VMEM((tk, sq), F32)] * depth
        + [pltpu.VMEM((tk, sq), BF16), pltpu.VMEM((tk, sq), BF16), row, pltpu.VMEM((vrows, sq), F32)],
        compiler_params=_cp("parallel", "parallel"))(q, kd, v1)


def _attn_bwd(q, o, dmix, lse, kt, kd, vt, tq):
    T = q.shape[0]
    nk, tk = kt.shape[1], kt.shape[3]
    nq = T // tq
    sq = 4 * tq
    rep = tk // LANES

    def body(q_ref, o_ref, do_ref, lse_ref, kt_ref, kd_ref, vt_ref, dq_ref, dkt_ref, dvt_ref):
        i = pl.program_id(1)
        qs = _stack_heads(q_ref[...])
        dof = _stack_heads(do_ref[...])
        dos = dof.astype(BF16)
        qst = qs.astype(F32).T.astype(BF16)
        dost = dof.T.astype(BF16)
        o_pair = o_ref[...].astype(F32)
        os_ = jnp.concatenate([o_pair[:, 0:LANES], o_pair[:, 0:LANES], o_pair[:, LANES:], o_pair[:, LANES:]], axis=0)
        delta = jnp.sum(dof * os_, axis=-1, keepdims=True)
        lse_t = jnp.concatenate([lse_ref[...]] * rep, axis=1)

        @pl.when(i == 0)
        def _():
            dkt_ref[...] = jnp.zeros_like(dkt_ref)
            dvt_ref[...] = jnp.zeros_like(dvt_ref)

        def step(j, dq):
            kdb = kd_ref[pl.ds(pl.multiple_of(j * tk, tk), tk), :]
            pexp = jnp.exp(_dot(qs, kt_ref[j]) - lse_t)
            ds = pexp * (_dot(dos, vt_ref[j]) - delta)
            pb = pexp.astype(BF16)
            dsb = ds.astype(BF16)
            dvt_ref[j] += _dot(dost, pb)
            dkt_ref[j] += _dot(qst, dsb)
            return dq + _dot(dsb, kdb)

        dq = lax.fori_loop(0, nk, step, jnp.zeros((sq, LANES), F32))
        lo = lax.broadcasted_iota(jnp.int32, (tq, LANES), 1) < HEAD_DIM
        dq_ref[...] = jnp.concatenate([jnp.where(lo, dq[0:tq], dq[tq:2 * tq]),
                                       jnp.where(lo, dq[2 * tq:3 * tq], dq[3 * tq:4 * tq])], axis=1)

    tspec = pl.BlockSpec((None, nk, LANES, tk), lambda g, i: (g, 0, 0, 0))
    qspec = pl.BlockSpec((tq, 2 * LANES), lambda g, i: (i, g))
    return pl.pallas_call(
        body, name="attn_bwd", grid=(2, nq),
        in_specs=[qspec, qspec, qspec, pl.BlockSpec((None, None, sq, LANES), lambda g, i: (g, i, 0, 0)),
                  tspec, pl.BlockSpec((None, T, LANES), lambda g, i: (g, 0, 0)), tspec],
        out_specs=[qspec, tspec, tspec],
        out_shape=[_sds((T, ATTN_W), F32), _sds((2, nk, LANES, tk), F32), _sds((2, nk, LANES, tk), F32)],
        compiler_params=_cp("parallel", "arbitrary"))(q, o, dmix, lse, kt, kd, vt)


def _fold_t(t_ref):
    rows = []
    for gidx in range(2):
        t = t_ref[gidx]
        rows.append(t[0:HEAD_DIM] + t[HEAD_DIM:2 * HEAD_DIM])
    return jnp.concatenate(rows, axis=0).T


def _qk_bwd(p, dq, dkt, dvt, gq, gk, cos, sin, bmat):
    T = p.shape[0]
    nk, tk = dkt.shape[1], dkt.shape[3]
    scale = HEAD_DIM ** -0.5

    def norm_bwd(x, dy, gain, bm):
        r = lax.rsqrt(_head_mean(x * x, bm) + EPS)
        xh = x * r
        dxh = dy * gain
        return r * (dxh - xh * _head_mean(dxh * xh, bm)), jnp.sum(dy * xh, axis=0, keepdims=True)

    def body(q_ref, kv_ref, dq_ref, dkt_ref, dvt_ref, gq_ref, gk_ref, cos_ref, sin_ref, b_ref, o_ref, dgq_ref, dgk_ref):
        i = pl.program_id(0)
        cosv, sinv, bm = cos_ref[...], sin_ref[...], b_ref[...]
        dgq = []
        for pr in range(ATTN_W // LANES):
            cols = pl.ds(pr * LANES, LANES)
            dy = _rope_t(dq_ref[:, cols] * scale, cosv, sinv)
            dx, dg = norm_bwd(q_ref[:, cols], dy, gq_ref[:, cols], bm)
            o_ref[:, cols] = dx.astype(BF16)
            dgq.append(dg)
        dy = _rope_t(_fold_t(dkt_ref), cosv, sinv)
        dx, dgk = norm_bwd(kv_ref[:, pl.ds(0, LANES)], dy, gk_ref[...], bm)
        o_ref[:, pl.ds(ATTN_W, LANES)] = dx.astype(BF16)
        o_ref[:, pl.ds(ATTN_W + LANES, LANES)] = _fold_t(dvt_ref).astype(BF16)
        _accum(dgq_ref, i, jnp.concatenate(dgq, axis=1))
        _accum(dgk_ref, i, dgk)

    tspec = pl.BlockSpec((2, None, LANES, tk), lambda i: (0, i, 0, 0))
    tab = pl.BlockSpec((tk, LANES), lambda i: (i, 0))
    return pl.pallas_call(
        body, name="qk_bwd", grid=(nk,),
        in_specs=[pl.BlockSpec((tk, ATTN_W), lambda i: (i, 0)), pl.BlockSpec((tk, 2 * KV_W), lambda i: (i, ATTN_W // (2 * KV_W))),
                  pl.BlockSpec((tk, ATTN_W), lambda i: (i, 0)), tspec, tspec,
                  pl.BlockSpec((1, ATTN_W), lambda i: (0, 0)), pl.BlockSpec((1, KV_W), lambda i: (0, 0)),
                  tab, tab, pl.BlockSpec((LANES, LANES), lambda i: (0, 0))],
        out_specs=[pl.BlockSpec((tk, ATTN_W + 2 * KV_W), lambda i: (i, 0)),
                   pl.BlockSpec((1, ATTN_W), lambda i: (0, 0)), pl.BlockSpec((1, KV_W), lambda i: (0, 0))],
        out_shape=[_sds((T, ATTN_W + 2 * KV_W), BF16), _sds((1, ATTN_W), F32), _sds((1, KV_W), F32)],
        compiler_params=_cp("arbitrary"))(p, p, dq, dkt, dvt, gq, gk, cos, sin, bmat)


def _loss_head(y, target):
    T, Dm = y.shape
    tm = min(T, _ROW_TILE)

    def body(y_ref, t_ref, dy_ref, l_ref):
        i = pl.program_id(0)
        err = y_ref[...] - t_ref[...]
        dy_ref[...] = err * (1.0 / Dm)
        part = jnp.sum(jnp.sum(err * err, axis=-1, keepdims=True), axis=0, keepdims=True) * (0.5 / Dm)
        _accum(l_ref, i, jnp.broadcast_to(part, (8, LANES)))

    row = pl.BlockSpec((tm, Dm), lambda i: (i, 0))
    return pl.pallas_call(
        body, name="loss_head", grid=(T // tm,), in_specs=[row, row],
        out_specs=[row, pl.BlockSpec((8, LANES), lambda i: (0, 0))],
        out_shape=[_sds((T, Dm), F32), _sds((8, LANES), F32)], compiler_params=_cp("arbitrary"))(y, target)


def _adamw(w, g, m, v, name):
    R, C = w.shape
    tr = R
    for cand in (512, 256, 128, 64, 32, 16, 8):
        if R % cand == 0:
            tr = cand
            break
    c1 = 1.0 - ADAM_B1 ** ADAM_STEP
    c2 = 1.0 - ADAM_B2 ** ADAM_STEP

    def body(w_ref, g_ref, m_ref, v_ref, d_ref, mo_ref, vo_ref):
        gv = g_ref[...]
        mn = ADAM_B1 * m_ref[...] + (1.0 - ADAM_B1) * gv
        vn = ADAM_B2 * v_ref[...] + (1.0 - ADAM_B2) * (gv * gv)
        d_ref[...] = -ADAM_LR * ((mn / c1) / (jnp.sqrt(vn / c2) + ADAM_EPS) + ADAM_WD * w_ref[...])
        mo_ref[...] = mn
        vo_ref[...] = vn

    blk = pl.BlockSpec((tr, C), lambda i: (i, 0))
    return pl.pallas_call(
        body, name=name, grid=(R // tr,), in_specs=[blk] * 4, out_specs=[blk] * 3,
        out_shape=[_sds((R, C), F32)] * 3, compiler_params=_cp("parallel"))(w, g, m, v)


def _cast_bf16(w, name):
    R, C = w.shape
    tr = 512 if R % 512 == 0 else 256

    def body(w_ref, o_ref):
        o_ref[...] = w_ref[...].astype(BF16)

    blk = pl.BlockSpec((tr, C), lambda i: (i, 0))
    return pl.pallas_call(body, name=name, grid=(R // tr,), in_specs=[blk], out_specs=blk,
                          out_shape=_sds((R, C), BF16), compiler_params=_cp("parallel"))(w)


def _position():
    x, y, c = lax.axis_index("x"), lax.axis_index("y"), lax.axis_index("c")
    return x, y, c


def _other_chips(x, y):
    return [(1 - x, y), (x, 1 - y), (1 - x, 1 - y)]


_HBM = pl.BlockSpec(memory_space=pltpu.HBM)
_SEM = pl.BlockSpec(memory_space=pltpu.SEMAPHORE)
_EFFECT = pltpu.SideEffectType.DATAFLOW_SIDE_EFFECTING


def _chip_copies(srcs, lands, send_sems, recv_sems, per_chip, arriving):
    x, y, c = _position()
    me = 2 * x + y
    copies = []
    for t, (src, land) in enumerate(zip(srcs, lands)):
        for k, (px, py) in enumerate(_other_chips(x, y)):
            peer = 2 * px + py
            copies.append(pltpu.make_async_remote_copy(
                src_ref=src.at[peer] if per_chip else src, dst_ref=land.at[peer if arriving else me],
                send_sem=send_sems[3 * t + k], recv_sem=recv_sems[3 * t + k],
                device_id=(px, py, c), device_id_type=MESH))
    return copies


def _chips_start(srcs, per_chip, name):
    n = len(srcs)
    slab = [s.shape[1:] if per_chip else s.shape for s in srcs]
    lands = [lax.empty((N_CHIPS,) + sh, s.dtype) for sh, s in zip(slab, srcs)]

    ns = 3 * n

    def body(*refs):
        ins = refs[:2 * n]
        send_sems, recv_sems = refs[2 * n:2 * n + ns], refs[2 * n + ns:2 * n + 2 * ns]
        token = refs[-1]
        for cp in _chip_copies(ins[:n], ins[n:], send_sems, recv_sems, per_chip, False):
            cp.start()
        token[...] = jnp.zeros_like(token)

    args = [pltpu.with_memory_space_constraint(a, pltpu.HBM) for a in list(srcs) + lands]
    outs = pl.pallas_call(
        body, name=name,
        out_shape=[pltpu.SemaphoreType.DMA(())] * (2 * ns)
        + [pltpu.HBM(a.shape, a.dtype) for a in args] + [_sds((8, LANES), F32)],
        in_specs=[_HBM] * (2 * n),
        out_specs=[_SEM] * (2 * ns) + [_HBM] * (2 * n) + [pl.BlockSpec(memory_space=pltpu.VMEM)],
        input_output_aliases={i: 2 * ns + i for i in range(2 * n)},
        compiler_params=pltpu.CompilerParams(has_side_effects=_EFFECT))(*args)
    sems, rest = outs[:2 * ns], outs[2 * ns:]
    return sems[:ns], sems[ns:], rest[:n], rest[n:2 * n], rest[-1]


def _chips_wait(handle, after, per_chip, name):
    send_sems, recv_sems, srcs, lands, _ = handle
    n = len(srcs)
    ns = 3 * n

    def body(*refs):
        ins = refs[:2 * n]
        s_sems, r_sems = refs[2 * n:2 * n + ns], refs[2 * n + ns:2 * n + 2 * ns]
        for cp in _chip_copies(ins[:n], ins[n:], s_sems, r_sems, per_chip, False):
            cp.wait_send()
        for cp in _chip_copies(ins[:n], ins[n:], s_sems, r_sems, per_chip, True):
            cp.wait_recv()

    outs = pl.pallas_call(
        body, name=name,
        out_shape=[pltpu.HBM(a.shape, a.dtype) for a in list(srcs) + list(lands)],
        in_specs=[_HBM] * (2 * n) + [_SEM] * (2 * ns) + [ANY],
        out_specs=[_HBM] * (2 * n),
        input_output_aliases={i: i for i in range(2 * n)},
        compiler_params=pltpu.CompilerParams(has_side_effects=_EFFECT))(*srcs, *lands, *send_sems, *recv_sems, after)
    return outs[:n], outs[n:]


def _sum_chips_own(land, own, name):
    S, R, C = land.shape
    tr = R
    for cand in (256, 128, 64, 32, 16, 8):
        if R % cand == 0:
            tr = cand
            break

    def body(l_ref, o_ref, out_ref):
        x, y, _ = _position()
        me = 2 * x + y
        mine = o_ref[me] if own.ndim == 3 else o_ref[...]
        acc = None
        for k in range(S):
            part = jnp.where(me == k, mine, l_ref[k])
            acc = part if acc is None else acc + part
        out_ref[...] = acc

    blk = pl.BlockSpec((S, tr, C), lambda i: (0, i, 0))
    row = pl.BlockSpec((tr, C), lambda i: (i, 0))
    return pl.pallas_call(
        body, name=name, grid=(R // tr,), in_specs=[blk, blk if own.ndim == 3 else row], out_specs=row,
        out_shape=_sds((R, C), F32), compiler_params=_cp("parallel"))(land, own)


def _add_pair(a, b, name):
    R, C = a.shape

    def body(a_ref, b_ref, o_ref):
        o_ref[...] = a_ref[...] + b_ref[...]

    blk = pl.BlockSpec((R, C), lambda: (0, 0))
    return pl.pallas_call(body, name=name, in_specs=[blk, blk], out_specs=blk, out_shape=_sds((R, C), F32))(a, b)


def _exchange_sibling(arrays):
    n = len(arrays)

    def body(*refs):
        ins, outs = refs[:n], refs[n:2 * n]
        send_sems, recv_sems = refs[2 * n:]
        x, y, c = _position()
        sends = []
        for t in range(n):
            cp = pltpu.make_async_remote_copy(src_ref=ins[t], dst_ref=outs[t], send_sem=send_sems.at[t],
                                              recv_sem=recv_sems.at[t], device_id=(x, y, 1 - c), device_id_type=MESH)
            cp.start()
            sends.append(cp)
        for cp in sends:
            cp.wait()

    return pl.pallas_call(
        body, name="grads_to_sibling",
        in_specs=[ANY] * n, out_specs=[ANY] * n, out_shape=[_sds(a.shape, a.dtype) for a in arrays],
        scratch_shapes=[pltpu.SemaphoreType.DMA((n,)), pltpu.SemaphoreType.DMA((n,))],
        compiler_params=pltpu.CompilerParams(has_side_effects=True))(*arrays)


def _adamw_sum(w, ga, gb, m, v, name):
    R, C = w.shape
    tr = next(t for t in (512, 256, 128, 64) if R % t == 0 and t * C * 4 <= (1 << 20))
    c1 = 1.0 - ADAM_B1 ** ADAM_STEP
    c2 = 1.0 - ADAM_B2 ** ADAM_STEP

    def body(w_ref, ga_ref, gb_ref, m_ref, v_ref, g_ref, d_ref, mo_ref, vo_ref):
        gv = ga_ref[...] + gb_ref[...]
        mn = ADAM_B1 * m_ref[...] + (1.0 - ADAM_B1) * gv
        vn = ADAM_B2 * v_ref[...] + (1.0 - ADAM_B2) * (gv * gv)
        g_ref[...] = gv
        d_ref[...] = -ADAM_LR * ((mn / c1) / (jnp.sqrt(vn / c2) + ADAM_EPS) + ADAM_WD * w_ref[...])
        mo_ref[...] = mn
        vo_ref[...] = vn

    blk = pl.BlockSpec((tr, C), lambda i: (i, 0))
    return pl.pallas_call(
        body, name=name, grid=(R // tr,), in_specs=[blk] * 5, out_specs=[blk] * 4,
        out_shape=[_sds((R, C), F32)] * 4, compiler_params=_cp("parallel"))(w, ga, gb, m, v)


def _rope_tables(T):
    pos = jnp.arange(T)
    row = (pos // GRID_W).astype(F32)
    col = (pos % GRID_W).astype(F32)
    inv = 1.0 / (ROPE_THETA ** (jnp.arange(AXIS_DIM // 2, dtype=F32) * 2.0 / AXIS_DIM))
    ar, ac = row[:, None] * inv[None, :], col[:, None] * inv[None, :]
    cos = jnp.concatenate([jnp.cos(ar), jnp.cos(ar), jnp.cos(ac), jnp.cos(ac)], axis=-1)
    sin = jnp.concatenate([-jnp.sin(ar), jnp.sin(ar), -jnp.sin(ac), jnp.sin(ac)], axis=-1)
    return jnp.tile(cos, (1, LANES // HEAD_DIM)), jnp.tile(sin, (1, LANES // HEAD_DIM))


def _head_mean_matrix():
    h = jnp.arange(LANES) // HEAD_DIM
    return jnp.where(h[:, None] == h[None, :], 1.0 / HEAD_DIM, 0.0).astype(BF16)


def _pack(arrays):
    flat = jnp.concatenate([a.reshape(-1) for a in arrays])
    rows = -(-flat.shape[0] // LANES)
    rows = -(-rows // 256) * 256
    return jnp.pad(flat, (0, rows * LANES - flat.shape[0])).reshape(rows, LANES)


def _unpack(packed, like):
    flat = packed.reshape(-1)
    out, off = [], 0
    for a in like:
        out.append(flat[off:off + a.size].reshape(a.shape))
        off += a.size
    return out


def _layer_fwd(x, h, lw, consts, ffn_weights, next_g):
    cos, sin, bmat = consts
    T = x.shape[0]
    tk = min(T, _ATTN_TK)
    tq = min(T, _ATTN_TQ)
    if h is None:
        h = _norm_fwd(x, lw["norm1_g"])
    p = _mm_nn(h, lw["w_in"], out_dtype=F32, name="mm_p", tm=_WIDE_ROW_TILE)
    qn, kt, kd, vt, v1 = _qk_prep(p, lw["gq"], lw["gk"], cos, sin, bmat, tk, min(tk, _ATTN_FWD_TK))
    o, lse = _attn_fwd(qn, kd, v1, tq)
    go = _sg_fwd(p, lw["sg_norm_g"], lw["sg_w"], lw["sg_bias"])
    co = _convmix_fwd(p, lw["conv_w"])
    mix = jnp.concatenate([o, go, co], axis=1)
    x_mid, h2 = _mm_nn(mix, lw["w_out"], out_dtype=F32, name="mm_out", res=x, norm_g=lw["norm2_g"], tm=_WIDE_ROW_TILE)
    lw.update(ffn_weights(x_mid))
    up = _mm_nn(h2, lw["w_up"], out_dtype=F32, name="mm_up", tm=_FFN_ROW_TILE)
    act = _ffn_act_fwd(up, lw["ffn_conv_w"])
    if next_g is None:
        x_out, h_next = _mm_nn(act, lw["w_down"], out_dtype=F32, name="mm_down_last", res=x_mid), None
    else:
        x_out, h_next = _mm_nn(act, lw["w_down"], out_dtype=F32, name="mm_down", res=x_mid, norm_g=next_g)
    saved = dict(x=x, h=h, p=p, qn=qn, kt=kt, kd=kd, vt=vt, o=o, lse=lse, mix=mix, x_mid=x_mid, h2=h2, up=up, act=act)
    return x_out, h_next, saved


def _layer_bwd(dx, s, lw, consts, send):
    cos, sin, bmat = consts
    T = dx.shape[0]
    tq = min(T, _ATTN_TQ)
    g = {}
    d_act = _mm_nt(dx, lw["w_down"], name="mm_dact")
    g_down = _mm_tn(s["act"], dx, tk=D_FF // 2, tn=D_MODEL, name="mm_dwdown", tm=_WGRAD_ROWS // 2)
    tok = send("w_down", g_down.reshape(N_CHIPS, D_FF // N_CHIPS, D_MODEL))
    d_up, d_cw = _ffn_act_bwd(s["up"], d_act, lw["ffn_conv_w"] + tok)
    g["ffn_conv_w"] = d_cw.transpose(1, 0, 2).reshape(3, 2 * D_FF)
    tok = send("w_up", _mm_tn(s["h2"], d_up, tk=512, tn=D_FF // 2, name="mm_dwup", shards=N_CHIPS, tm=_WGRAD_ROWS))
    dx2, g["norm2_g"] = _mm_nt(d_up, lw["w_up"], name="mm_dh2", norm=(s["x_mid"], dx, lw["norm2_g"] + tok))
    d_mix = _mm_nt(dx2, lw["w_out"], name="mm_dmix", tm=_WIDE_ROW_TILE)
    g_out = _mm_tn(s["mix"], dx2, tk=512, tn=D_MODEL, name="mm_dwout", tm=_WGRAD_ROWS // 2)
    tok = send("w_out", g_out.reshape(N_CHIPS, D_MODEL // N_CHIPS, D_MODEL))
    dp_c, g["conv_w"] = _convmix_bwd(s["p"], d_mix, lw["conv_w"] + tok)
    dp_b, g["sg_w"], d_bias, g["sg_norm_g"] = _sg_bwd(s["p"], d_mix, lw["sg_norm_g"], lw["sg_w"], lw["sg_wt"], lw["sg_bias"])
    g["sg_b"] = d_bias.reshape(SG_CHUNK, SG_W // HEAD_DIM, HEAD_DIM).sum(axis=-1).T
    dq, dkt, dvt = _attn_bwd(s["qn"], s["o"], d_mix, s["lse"], s["kt"], s["kd"], s["vt"], tq)
    dp_a, d_gq, d_gk = _qk_bwd(s["p"], dq, dkt, dvt, lw["gq"], lw["gk"], cos, sin, bmat)
    g["q_norm_g"] = d_gq.reshape(ATTN_W // HEAD_DIM, HEAD_DIM).sum(axis=0)
    g["k_norm_g"] = d_gk.reshape(KV_W // HEAD_DIM, HEAD_DIM).sum(axis=0)
    dp = jnp.concatenate([dp_a, dp_b, dp_c], axis=1)
    tok = send("w_in", _mm_tn(s["h"], dp, tk=D_MODEL, tn=512, name="mm_dwin", shards=N_CHIPS, tm=_WGRAD_ROWS))
    dx_in, g["norm1_g"] = _mm_nt(dp, lw["w_in"], name="mm_dh", norm=(s["x"], dx2, lw["norm1_g"] + tok), tm=_WIDE_ROW_TILE)
    return dx_in, g


def _layer_weights(l, full, small):
    sg_w = small["sg_w"][l]
    sg_b = small["sg_b"][l]
    return dict(
        norm1_g=small["norm1_g"][l][None, :], norm2_g=small["norm2_g"][l][None, :],
        gq=jnp.tile(small["q_norm_g"][l], ATTN_W // HEAD_DIM)[None, :],
        gk=jnp.tile(small["k_norm_g"][l], KV_W // HEAD_DIM)[None, :],
        sg_norm_g=small["sg_norm_g"][l][None, :],
        sg_w=sg_w.astype(BF16), sg_wt=sg_w.transpose(0, 2, 1).astype(BF16),
        sg_bias=jnp.repeat(sg_b.T, HEAD_DIM, axis=1),
        **full)


_BIG = ("w_in", "w_out", "ffn_w_up", "ffn_w_down")
_SMALL_REPL = ("norm1_g", "q_norm_g", "k_norm_g", "sg_norm_g", "sg_w", "sg_b", "norm2_g")
_SMALL_SHARD = ("conv_w", "ffn_conv_w")
_ORDER = ("norm1_g", "w_in", "q_norm_g", "k_norm_g", "sg_norm_g", "sg_w", "sg_b", "conv_w", "w_out", "norm2_g",
          "ffn_w_up", "ffn_conv_w", "ffn_w_down")


def kernel(x, norm1_g, w_in, q_norm_g, k_norm_g, sg_norm_g, sg_w, sg_b, conv_w, w_out, norm2_g, ffn_w_up, ffn_conv_w, ffn_w_down, loss_target, m_norm1_g, m_w_in, m_q_norm_g, m_k_norm_g, m_sg_norm_g, m_sg_w, m_sg_b, m_conv_w, m_w_out, m_norm2_g, m_ffn_w_up, m_ffn_conv_w, m_ffn_w_down, v_norm1_g, v_w_in, v_q_norm_g, v_k_norm_g, v_sg_norm_g, v_sg_w, v_sg_b, v_conv_w, v_w_out, v_norm2_g, v_ffn_w_up, v_ffn_conv_w, v_ffn_w_down):
    w = dict(norm1_g=norm1_g, w_in=w_in, q_norm_g=q_norm_g, k_norm_g=k_norm_g, sg_norm_g=sg_norm_g, sg_w=sg_w,
             sg_b=sg_b, conv_w=conv_w, w_out=w_out, norm2_g=norm2_g, ffn_w_up=ffn_w_up, ffn_conv_w=ffn_conv_w,
             ffn_w_down=ffn_w_down)
    mom = dict(norm1_g=m_norm1_g, w_in=m_w_in, q_norm_g=m_q_norm_g, k_norm_g=m_k_norm_g, sg_norm_g=m_sg_norm_g,
               sg_w=m_sg_w, sg_b=m_sg_b, conv_w=m_conv_w, w_out=m_w_out, norm2_g=m_norm2_g, ffn_w_up=m_ffn_w_up,
               ffn_conv_w=m_ffn_conv_w, ffn_w_down=m_ffn_w_down)
    var = dict(norm1_g=v_norm1_g, w_in=v_w_in, q_norm_g=v_q_norm_g, k_norm_g=v_k_norm_g, sg_norm_g=v_sg_norm_g,
               sg_w=v_sg_w, sg_b=v_sg_b, conv_w=v_conv_w, w_out=v_w_out, norm2_g=v_norm2_g, ffn_w_up=v_ffn_w_up,
               ffn_conv_w=v_ffn_conv_w, ffn_w_down=v_ffn_w_down)
    L = DEPTH
    T = x.shape[1]
    xs = x.reshape(T, D_MODEL)
    target = loss_target.reshape(T, D_MODEL)

    chip = 2 * lax.axis_index("x") + lax.axis_index("y")

    shards = [_cast_bf16(w[n].reshape(-1, w[n].shape[-1]), "cast_" + n).reshape(w[n].shape) for n in _BIG]
    shards += [conv_w, ffn_conv_w]
    w_in_s, w_out_s, w_up_s, w_down_s, conv_s, fconv_s = shards
    gathers = []
    for l in range(L):
        gathers.append((_chips_start([w_in_s[l], w_out_s[l], conv_s[l]], False, "gather_start_%da" % l),
                        _chips_start([w_up_s[l], w_down_s[l], fconv_s[l]], False, "gather_start_%db" % l)))
    start_token = sum(h[4][0, 0] for pair in gathers for h in pair)
    consts = _rope_tables(T) + (_head_mean_matrix(),)

    def gathered(handle, after, name):
        own, lands = _chips_wait(handle, after, False, name)
        return [lax.dynamic_update_slice(ld, o[None], (chip,) + (jnp.int32(0),) * o.ndim) for ld, o in zip(lands, own)]

    saved, lws = [], []
    act_x, act_h = xs, None
    for l in range(L):
        g_in, g_out, g_conv = gathered(gathers[l][0], act_x if l else gathers[-1][1][4], "gather_wait_%da" % l)
        lw = _layer_weights(l, dict(w_in=g_in, w_out=g_out.reshape(D_MODEL, D_MODEL),
                                    conv_w=g_conv.transpose(1, 0, 2).reshape(3, CONV_W)), w)
        if l == 0:
            lw["norm1_g"] = lw["norm1_g"] + start_token

        def ffn_weights(after, l=l):
            g_up, g_down, g_fconv = gathered(gathers[l][1], after, "gather_wait_%db" % l)
            return dict(w_up=g_up, w_down=g_down.reshape(D_FF, D_MODEL),
                        ffn_conv_w=g_fconv.transpose(1, 0, 2).reshape(3, 2 * D_FF))

        next_g = w["norm1_g"][l + 1][None, :] if l + 1 < L else None
        act_x, act_h, s = _layer_fwd(act_x, act_h, lw, consts, ffn_weights, next_g)
        saved.append(s)
        lws.append(lw)
    dx, loss_blk = _loss_head(act_x, target)
    loss = lax.psum(loss_blk[0, 0], ("x", "y", "c"))

    grads = [None] * L
    partial = [None] * L

    def collect(pending, after, l):
        sums = {}
        for name, handle in pending:
            own, lands = _chips_wait(handle, after, True, "grad_wait_%d_%s" % (l, name))
            sums[name] = _sum_chips_own(lands[0], own[0], "sum_chips_" + name)
        return sums

    pending_prev = None
    for l in reversed(range(L)):
        pending = []

        def send(name, g4, l=l, pending=pending):
            handle = _chips_start([g4], True, "grad_start_%d_%s" % (l, name))
            pending.append((name, handle))
            return handle[4][0, 0]

        dx, g = _layer_bwd(dx, saved[l], lws[l], consts, send)
        grads[l] = g
        if pending_prev is not None:
            partial[l + 1] = collect(pending_prev, dx, l + 1)
        pending_prev = pending
    grad_x = dx.reshape(x.shape)

    small_names = _SMALL_REPL + _SMALL_SHARD
    small_local = [jnp.stack([grads[l][n].reshape(-1) for l in range(L)]) for n in small_names]
    small_handle = _chips_start([_pack(small_local)], False, "small_start")
    partial[0] = collect(pending_prev, small_handle[4], 0)

    short = dict(w_in="w_in", w_out="w_out", ffn_w_up="w_up", ffn_w_down="w_down")
    mine = [jnp.stack([partial[l][short[n]] for l in range(L)]) for n in _BIG]
    small_own, small_lands = _chips_wait(small_handle, mine[-1], False, "small_wait")
    mine.append(_sum_chips_own(small_lands[0], small_own[0], "sum_chips_small"))
    theirs = _exchange_sibling(mine)
    grad = dict(zip(small_names, _unpack(_add_pair(mine[-1], theirs[-1], "add_small"), small_local)))
    for n in _SMALL_REPL:
        grad[n] = grad[n].reshape(w[n].shape)
    for n in _SMALL_SHARD:
        full_w = grad[n].reshape(L, 3, -1)
        width = w[n].shape[-1]
        grad[n] = lax.dynamic_slice_in_dim(full_w, chip * width, width, axis=2)
    delta, new_m, new_v = {}, {}, {}
    for n, ga, gb in zip(_BIG, mine, theirs):
        shp = w[n].shape
        v2 = lambda a: a.reshape(-1, shp[-1])
        gsum, d, mn, vn = _adamw_sum(v2(w[n]), v2(ga), v2(gb), v2(mom[n]), v2(var[n]), "adamw_" + n)
        grad[n], delta[n], new_m[n], new_v[n] = gsum.reshape(shp), d.reshape(shp), mn.reshape(shp), vn.reshape(shp)
    for group, gname in ((_SMALL_REPL, "adamw_small"), (_SMALL_SHARD, "adamw_conv")):
        like = [w[n] for n in group]
        outs = _adamw(_pack([w[n] for n in group]), _pack([grad[n] for n in group]), _pack([mom[n] for n in group]),
                      _pack([var[n] for n in group]), gname)
        for res, dst in zip(outs, (delta, new_m, new_v)):
            for n, a in zip(group, _unpack(res, like)):
                dst[n] = a

    return (loss, grad_x, *[grad[n] for n in _ORDER], *[delta[n] for n in _ORDER],
            *[new_m[n] for n in _ORDER], *[new_v[n] for n in _ORDER])
```

```python
import jax
import jax.numpy as jnp
from jax import lax
from jax.experimental import pallas as pl
from jax.experimental.pallas import tpu as pltpu

F32 = jnp.float32
BF16 = jnp.bfloat16

DEPTH = 4
D_MODEL = 1024
HEAD_DIM = 64
ATTN_W = 512
KV_W = 128
SG_W = 256
CONV_W = 256
SG_CHUNK = 128
D_FF = 2816
PROJ_W = 2048
GRID_W = 64
ROPE_THETA = 10000.0
AXIS_DIM = HEAD_DIM // 2
EPS = 1e-6
N_CHIPS = 4

ADAM_LR = 0.001
ADAM_B1 = 0.9
ADAM_B2 = 0.999
ADAM_EPS = 1e-08
ADAM_WD = 0.01
ADAM_STEP = 10

_ROW_TILE = 512
_FFN_ROW_TILE = 256
_WIDE_ROW_TILE = 1024
_WGRAD_ROWS = 4096
LANES = 128
HALO = 8
VMEM_LIMIT_BYTES = 56 * 1024 * 1024
MESH = pl.DeviceIdType.MESH
ANY = pl.BlockSpec(memory_space=pl.ANY)


def _cp(*sem):
    return pltpu.CompilerParams(dimension_semantics=sem if sem else None,
                                vmem_limit_bytes=VMEM_LIMIT_BYTES)


def _sds(shape, dtype):
    return jax.ShapeDtypeStruct(shape, dtype)


def _dot(a, b):
    return jnp.dot(a, b, preferred_element_type=F32)


def _dot_nt(a, b):
    return lax.dot_general(a, b, (((1,), (1,)), ((), ())), preferred_element_type=F32)


def _dot_tn(a, b):
    return lax.dot_general(a, b, (((0,), (0,)), ((), ())), preferred_element_type=F32)


def _norm_fwd(x, g):
    T, Dm = x.shape
    tm = min(T, _ROW_TILE)

    def body(x_ref, g_ref, o_ref):
        xv = x_ref[...]
        r = lax.rsqrt(jnp.mean(xv * xv, axis=-1, keepdims=True) + EPS)
        o_ref[...] = ((xv * r) * g_ref[...]).astype(BF16)

    return pl.pallas_call(
        body, name="norm_fwd", grid=(T // tm,),
        in_specs=[pl.BlockSpec((tm, Dm), lambda i: (i, 0)), pl.BlockSpec((1, Dm), lambda i: (0, 0))],
        out_specs=pl.BlockSpec((tm, Dm), lambda i: (i, 0)),
        out_shape=_sds((T, Dm), BF16), compiler_params=_cp("parallel"))(x, g)


def _whole(w):
    return pl.BlockSpec(w.shape, lambda *g: (0,) * w.ndim)


def _mm_nn(a, w, *, out_dtype, name, res=None, tm=None, norm_g=None):
    M, K = a.shape
    N = w.shape[-1] if w.ndim == 2 else w.shape[0] * w.shape[2]
    tm = min(M, tm or _ROW_TILE)
    has_res = res is not None
    has_norm = norm_g is not None
    assert not has_norm or w.ndim == 2

    def body(*refs):
        a_ref, w_ref = refs[0], refs[1]
        o_ref = refs[-2] if has_norm else refs[-1]
        av = a_ref[...].astype(BF16)
        parts = [w_ref[...]] if w.ndim == 2 else [w_ref[s] for s in range(w.shape[0])]
        ns = N // len(parts)
        for s, wv in enumerate(parts):
            cols = pl.ds(s * ns, ns)
            acc = _dot(av, wv)
            if has_res:
                acc = acc + refs[2][:, cols]
            o_ref[:, cols] = acc.astype(out_dtype)
        if has_norm:
            r = lax.rsqrt(jnp.mean(acc * acc, axis=-1, keepdims=True) + EPS)
            refs[-1][...] = ((acc * r) * refs[2 + has_res][...]).astype(BF16)

    row = pl.BlockSpec((tm, N), lambda i: (i, 0))
    in_specs = [pl.BlockSpec((tm, K), lambda i: (i, 0)), _whole(w)]
    args = [a, w]
    if has_res:
        in_specs.append(row)
        args.append(res)
    if has_norm:
        in_specs.append(pl.BlockSpec((1, N), lambda i: (0, 0)))
        args.append(norm_g)
    return pl.pallas_call(
        body, name=name, grid=(M // tm,), in_specs=in_specs, out_specs=[row, row] if has_norm else row,
        out_shape=[_sds((M, N), out_dtype), _sds((M, N), BF16)] if has_norm else _sds((M, N), out_dtype),
        compiler_params=_cp("parallel"))(*args)


def _a_spec(a, tm, tn, row_of, col_of):
    if a.ndim == 2:
        return pl.BlockSpec((tm, tn), lambda *g: (row_of(*g), col_of(*g)))
    bph = a.shape[2] // tn
    return pl.BlockSpec((None, tm, tn), lambda *g: (col_of(*g) // bph, row_of(*g), col_of(*g) % bph))


def _a_cols(a):
    return a.shape[1] if a.ndim == 2 else a.shape[0] * a.shape[2]


def _mm_nt(a, w, *, name, tm=None, norm=None):
    M = a.shape[-2]
    Kw = w.shape[-2]
    tm = min(M, tm or _ROW_TILE)

    def product(a_ref, w_ref):
        if w.ndim == 2:
            return _dot_nt(a_ref[...].astype(BF16), w_ref[...])
        S, ns = w.shape[0], w.shape[2]
        acc = None
        for s in range(S):
            if a.ndim == 2:
                piece = a_ref[:, pl.ds(s * ns, ns)]
            else:
                per_half = S // 2
                piece = a_ref[s // per_half, :, pl.ds((s % per_half) * ns, ns)]
            part = _dot_nt(piece.astype(BF16), w_ref[s])
            acc = part if acc is None else acc + part
        return acc

    def body(a_ref, w_ref, o_ref):
        o_ref[...] = product(a_ref, w_ref)

    def body_norm(a_ref, w_ref, x_ref, dr_ref, g_ref, dx_ref, dg_ref):
        dhv = product(a_ref, w_ref)
        xv = x_ref[...]
        r = lax.rsqrt(jnp.mean(xv * xv, axis=-1, keepdims=True) + EPS)
        xh = xv * r
        dxh = dhv * g_ref[...]
        dx_ref[...] = dr_ref[...] + r * (dxh - xh * jnp.mean(dxh * xh, axis=-1, keepdims=True))
        _accum(dg_ref, pl.program_id(0), jnp.sum(dhv * xh, axis=0, keepdims=True))

    a_spec = (pl.BlockSpec((tm, a.shape[1]), lambda i: (i, 0)) if a.ndim == 2
              else pl.BlockSpec((2, tm, a.shape[2]), lambda i: (0, i, 0)))
    row = pl.BlockSpec((tm, Kw), lambda i: (i, 0))
    if norm is None:
        return pl.pallas_call(
            body, name=name, grid=(M // tm,), in_specs=[a_spec, _whole(w)], out_specs=row,
            out_shape=_sds((M, Kw), F32), compiler_params=_cp("parallel"))(a, w)
    vec = pl.BlockSpec((1, Kw), lambda i: (0, 0))
    return pl.pallas_call(
        body_norm, name=name, grid=(M // tm,), in_specs=[a_spec, _whole(w), row, row, vec], out_specs=[row, vec],
        out_shape=[_sds((M, Kw), F32), _sds((1, Kw), F32)], compiler_params=_cp("arbitrary"))(a, w, *norm)


def _mm_tn(a, b, *, tk, tn, name, shards=None, tm=None):
    M, K = a.shape
    N = _a_cols(b)
    tm = min(M, tm or _ROW_TILE)

    def body(a_ref, b_ref, o_ref):
        m = pl.program_id(2)
        part = _dot_tn(a_ref[...].astype(BF16), b_ref[...].astype(BF16))

        @pl.when(m == 0)
        def _():
            o_ref[...] = part

        @pl.when(m > 0)
        def _():
            o_ref[...] += part

    if shards is None:
        out_spec = pl.BlockSpec((tk, tn), lambda k, j, m: (k, j))
        out_shape = _sds((K, N), F32)
    else:
        bps = (N // shards) // tn
        out_spec = pl.BlockSpec((None, tk, tn), lambda k, j, m: (j // bps, k, j % bps))
        out_shape = _sds((shards, K, N // shards), F32)
    return pl.pallas_call(
        body, name=name, grid=(K // tk, N // tn, M // tm),
        in_specs=[pl.BlockSpec((tm, tk), lambda k, j, m: (m, k)),
                  _a_spec(b, tm, tn, lambda k, j, m: m, lambda k, j, m: j)],
        out_specs=out_spec, out_shape=out_shape,
        compiler_params=_cp("parallel", "parallel", "arbitrary"))(a, b)


def _halo_specs(T, tm, cw, ic):
    nb = tm // HALO
    last = T // HALO - 1

    def mk(rows, row_of):
        return pl.BlockSpec((rows, cw), lambda *g: (row_of(ic(*g)[0]), ic(*g)[1]))

    return [mk(HALO, lambda i: jnp.maximum(i * nb - 1, 0)), mk(tm, lambda i: i),
            mk(HALO, lambda i: jnp.minimum((i + 1) * nb, last))]


def _ext(prev_ref, cur_ref, next_ref, i, n):
    p = jnp.where(i > 0, prev_ref[...].astype(F32), 0.0)
    nx = jnp.where(i < n - 1, next_ref[...].astype(F32), 0.0)
    return jnp.concatenate([p, cur_ref[...].astype(F32), nx], axis=0)


def _dn(e):
    return pltpu.roll(e, 1, 0)


def _up(e):
    return pltpu.roll(e, e.shape[0] - 1, 0)


def _mid(e):
    return e[HALO:e.shape[0] - HALO]


def _taps(e):
    return _dn(e), e, _up(e)


def _conv3(taps, w):
    return taps[0] * w[0:1] + taps[1] * w[1:2] + taps[2] * w[2:3]


def _conv3_t(e, w):
    return _up(e) * w[0:1] + e * w[1:2] + _dn(e) * w[2:3]


def _conv3_wgrad(d, taps):
    return jnp.concatenate([jnp.sum(_mid(d * tap), axis=0, keepdims=True) for tap in taps], axis=0)


def _sigmoid(x):
    return 1.0 / (1.0 + jnp.exp(-x))


def _accum(ref, i, part):
    @pl.when(i == 0)
    def _():
        ref[...] = part

    @pl.when(i > 0)
    def _():
        ref[...] += part


def _ffn_act_fwd(up, cw):
    T = up.shape[0]
    tm = min(T, _FFN_ROW_TILE)
    cb = D_FF // 2
    nblk = D_FF // cb
    n = T // tm

    def body(gp, gc, gn, vp, vc, vn, wg_ref, wv_ref, o_ref):
        i = pl.program_id(1)
        gate = _conv3(_taps(_ext(gp, gc, gn, i, n)), wg_ref[...])
        val = _conv3(_taps(_ext(vp, vc, vn, i, n)), wv_ref[...])
        o_ref[...] = _mid(gate * _sigmoid(gate) * val).astype(BF16)

    return pl.pallas_call(
        body, name="ffn_act_fwd", grid=(nblk, n),
        in_specs=_halo_specs(T, tm, cb, lambda j, i: (i, j)) + _halo_specs(T, tm, cb, lambda j, i: (i, j + nblk))
        + [pl.BlockSpec((3, cb), lambda j, i: (0, j)), pl.BlockSpec((3, cb), lambda j, i: (0, j + nblk))],
        out_specs=pl.BlockSpec((tm, cb), lambda j, i: (i, j)),
        out_shape=_sds((T, D_FF), BF16), compiler_params=_cp("parallel", "parallel"))(
            up, up, up, up, up, up, cw, cw)


def _ffn_act_bwd(up, dact, cw):
    T = up.shape[0]
    tm = min(T, _FFN_ROW_TILE)
    cb = D_FF // 2
    nblk = D_FF // cb
    n = T // tm

    def body(gp, gc, gn, vp, vc, vn, dp_, dc, dn_, wg_ref, wv_ref, dup_ref, dcw_ref):
        i = pl.program_id(1)
        wg, wv = wg_ref[...], wv_ref[...]
        eg = _taps(_ext(gp, gc, gn, i, n))
        ev = _taps(_ext(vp, vc, vn, i, n))
        ed = _ext(dp_, dc, dn_, i, n)
        gate = _conv3(eg, wg)
        val = _conv3(ev, wv)
        sg = _sigmoid(gate)
        d_gate = ed * val * (sg * (1.0 + gate * (1.0 - sg)))
        d_val = ed * (gate * sg)
        dup_ref[0] = _mid(_conv3_t(d_gate, wg)).astype(BF16)
        dup_ref[1] = _mid(_conv3_t(d_val, wv)).astype(BF16)
        part = jnp.stack([_conv3_wgrad(d_gate, eg), _conv3_wgrad(d_val, ev)], axis=0)
        _accum(dcw_ref, i, part)

    return pl.pallas_call(
        body, name="ffn_act_bwd", grid=(nblk, n),
        in_specs=_halo_specs(T, tm, cb, lambda j, i: (i, j)) + _halo_specs(T, tm, cb, lambda j, i: (i, j + nblk))
        + _halo_specs(T, tm, cb, lambda j, i: (i, j))
        + [pl.BlockSpec((3, cb), lambda j, i: (0, j)), pl.BlockSpec((3, cb), lambda j, i: (0, j + nblk))],
        out_specs=[pl.BlockSpec((2, tm, cb), lambda j, i: (0, i, j)),
                   pl.BlockSpec((2, 3, cb), lambda j, i: (0, 0, j))],
        out_shape=[_sds((2, T, D_FF), BF16), _sds((2, 3, D_FF), F32)],
        compiler_params=_cp("parallel", "arbitrary"))(up, up, up, up, up, up, dact, dact, dact, cw, cw)


_CB_BLK, _CC_BLK, _CX_BLK = 5, 6, 7


def _convmix_fwd(p, w):
    T = p.shape[0]
    tm = min(T, _ROW_TILE)
    n = T // tm

    def body(cb_ref, ccp, ccc, ccn, cxp, cxc, cxn, w_ref, o_ref):
        i = pl.program_id(0)
        z = _ext(ccp, ccc, ccn, i, n) * _ext(cxp, cxc, cxn, i, n)
        o_ref[...] = (cb_ref[...] * _mid(_conv3(_taps(z), w_ref[...]))).astype(BF16)

    return pl.pallas_call(
        body, name="convmix_fwd", grid=(n,),
        in_specs=[pl.BlockSpec((tm, CONV_W), lambda i: (i, _CB_BLK))]
        + _halo_specs(T, tm, CONV_W, lambda i: (i, _CC_BLK)) + _halo_specs(T, tm, CONV_W, lambda i: (i, _CX_BLK))
        + [pl.BlockSpec((3, CONV_W), lambda i: (0, 0))],
        out_specs=pl.BlockSpec((tm, CONV_W), lambda i: (i, 0)),
        out_shape=_sds((T, CONV_W), BF16), compiler_params=_cp("parallel"))(p, p, p, p, p, p, p, w)


def _convmix_bwd(p, dmix, w):
    T = p.shape[0]
    tm = min(T, _ROW_TILE)
    n = T // tm
    dblk = (ATTN_W + SG_W) // CONV_W

    def body(cbp, cbc, cbn, ccp, ccc, ccn, cxp, cxc, cxn, dp_, dc, dn_, w_ref, o_ref, dw_ref):
        i = pl.program_id(0)
        wv = w_ref[...]
        ecb = _ext(cbp, cbc, cbn, i, n)
        ecc = _ext(ccp, ccc, ccn, i, n)
        ecx = _ext(cxp, cxc, cxn, i, n)
        ed = _ext(dp_, dc, dn_, i, n)
        z = _taps(ecc * ecx)
        d_cz = ed * ecb
        d_z = _conv3_t(d_cz, wv)
        o_ref[...] = jnp.concatenate([_mid(ed * _conv3(z, wv)), _mid(d_z * ecx), _mid(d_z * ecc)],
                                     axis=1).astype(BF16)
        _accum(dw_ref, i, _conv3_wgrad(d_cz, z))

    return pl.pallas_call(
        body, name="convmix_bwd", grid=(n,),
        in_specs=_halo_specs(T, tm, CONV_W, lambda i: (i, _CB_BLK)) + _halo_specs(T, tm, CONV_W, lambda i: (i, _CC_BLK))
        + _halo_specs(T, tm, CONV_W, lambda i: (i, _CX_BLK)) + _halo_specs(T, tm, CONV_W, lambda i: (i, dblk))
        + [pl.BlockSpec((3, CONV_W), lambda i: (0, 0))],
        out_specs=[pl.BlockSpec((tm, 3 * CONV_W), lambda i: (i, 0)), pl.BlockSpec((3, CONV_W), lambda i: (0, 0))],
        out_shape=[_sds((T, 3 * CONV_W), BF16), _sds((3, CONV_W), F32)],
        compiler_params=_cp("arbitrary"))(p, p, p, p, p, p, p, p, p, dmix, dmix, dmix, w)


_SU_BLK, _SV_BLK = 3, 4


def _sg_mixed(vnb, w_ref, bias, ch, pr, lo):
    vp = vnb[ch * SG_CHUNK:(ch + 1) * SG_CHUNK, pr * LANES:(pr + 1) * LANES]
    zero = jnp.zeros_like(vp)
    return (_dot(w_ref[2 * pr], jnp.where(lo, vp, zero)) + _dot(w_ref[2 * pr + 1], jnp.where(lo, zero, vp))
            + bias[:, pr * LANES:(pr + 1) * LANES]), vp


def _sg_fwd(p, g, w, bias):
    T = p.shape[0]
    tm = min(T, _ROW_TILE)

    def body(su_ref, sv_ref, g_ref, w_ref, b_ref, o_ref):
        lo = lax.broadcasted_iota(jnp.int32, (SG_CHUNK, LANES), 1) < HEAD_DIM
        sv = sv_ref[...]
        r = lax.rsqrt(jnp.mean(sv * sv, axis=-1, keepdims=True) + EPS)
        vnb = ((sv * r) * g_ref[...]).astype(BF16)
        bias_v = b_ref[...]
        for ch in range(tm // SG_CHUNK):
            for pr in range(2):
                mixed, _ = _sg_mixed(vnb, w_ref, bias_v, ch, pr, lo)
                rows, cols = pl.ds(ch * SG_CHUNK, SG_CHUNK), pl.ds(pr * LANES, LANES)
                o_ref[rows, cols] = (su_ref[rows, cols] * mixed).astype(BF16)

    return pl.pallas_call(
        body, name="sg_fwd", grid=(T // tm,),
        in_specs=[pl.BlockSpec((tm, SG_W), lambda i: (i, _SU_BLK)), pl.BlockSpec((tm, SG_W), lambda i: (i, _SV_BLK)),
                  pl.BlockSpec((1, SG_W), lambda i: (0, 0)), pl.BlockSpec((4, SG_CHUNK, SG_CHUNK), lambda i: (0, 0, 0)),
                  pl.BlockSpec((SG_CHUNK, SG_W), lambda i: (0, 0))],
        out_specs=pl.BlockSpec((tm, SG_W), lambda i: (i, 0)),
        out_shape=_sds((T, SG_W), BF16), compiler_params=_cp("parallel"))(p, p, g, w, bias)


def _sg_bwd(p, dmix, g, w, wt, bias):
    T = p.shape[0]
    tm = min(T, _ROW_TILE)
    dblk = ATTN_W // SG_W

    def body(su_ref, sv_ref, d_ref, g_ref, w_ref, wt_ref, b_ref, o_ref, dw_ref, db_ref, dg_ref, dvn_ref):
        i = pl.program_id(0)
        lo = lax.broadcasted_iota(jnp.int32, (SG_CHUNK, LANES), 1) < HEAD_DIM
        sv = sv_ref[...]
        gv = g_ref[...]
        r = lax.rsqrt(jnp.mean(sv * sv, axis=-1, keepdims=True) + EPS)
        xh = sv * r
        vnb = (xh * gv).astype(BF16)
        bias_v = b_ref[...]
        dw = [jnp.zeros((SG_CHUNK, SG_CHUNK), F32) for _ in range(4)]
        db = jnp.zeros((SG_CHUNK, SG_W), F32)
        for ch in range(tm // SG_CHUNK):
            dbs = []
            for pr in range(2):
                mixed, vp = _sg_mixed(vnb, w_ref, bias_v, ch, pr, lo)
                rows, cols = pl.ds(ch * SG_CHUNK, SG_CHUNK), pl.ds(pr * LANES, LANES)
                dgo = d_ref[rows, cols]
                o_ref[rows, cols] = (dgo * mixed).astype(BF16)
                dm = dgo * su_ref[rows, cols]
                dmb = dm.astype(BF16)
                zero = jnp.zeros_like(dmb)
                dw[2 * pr] += _dot_nt(jnp.where(lo, dmb, zero), vp)
                dw[2 * pr + 1] += _dot_nt(jnp.where(lo, zero, dmb), vp)
                dvn_ref[rows, cols] = jnp.where(lo, _dot(wt_ref[2 * pr], dmb), _dot(wt_ref[2 * pr + 1], dmb))
                dbs.append(dm)
            db += jnp.concatenate(dbs, axis=1)
        dvn = dvn_ref[...]
        dxh = dvn * gv
        o_ref[:, pl.ds(SG_W, SG_W)] = (r * (dxh - xh * jnp.mean(dxh * xh, axis=-1, keepdims=True))).astype(BF16)
        _accum(dw_ref, i, jnp.stack(dw, axis=0))
        _accum(db_ref, i, db)
        _accum(dg_ref, i, jnp.sum(dvn * xh, axis=0, keepdims=True))

    wspec = pl.BlockSpec((4, SG_CHUNK, SG_CHUNK), lambda i: (0, 0, 0))
    return pl.pallas_call(
        body, name="sg_bwd", grid=(T // tm,),
        in_specs=[pl.BlockSpec((tm, SG_W), lambda i: (i, _SU_BLK)), pl.BlockSpec((tm, SG_W), lambda i: (i, _SV_BLK)),
                  pl.BlockSpec((tm, SG_W), lambda i: (i, dblk)), pl.BlockSpec((1, SG_W), lambda i: (0, 0)),
                  wspec, wspec, pl.BlockSpec((SG_CHUNK, SG_W), lambda i: (0, 0))],
        out_specs=[pl.BlockSpec((tm, 2 * SG_W), lambda i: (i, 0)), wspec,
                   pl.BlockSpec((SG_CHUNK, SG_W), lambda i: (0, 0)), pl.BlockSpec((1, SG_W), lambda i: (0, 0))],
        out_shape=[_sds((T, 2 * SG_W), BF16), _sds((4, SG_CHUNK, SG_CHUNK), F32),
                   _sds((SG_CHUNK, SG_W), F32), _sds((1, SG_W), F32)],
        scratch_shapes=[pltpu.VMEM((tm, SG_W), F32)],
        compiler_params=_cp("arbitrary"))(p, p, dmix, g, w, wt, bias)


_ATTN_TQ = 256
_ATTN_TK = 1024
_ATTN_FWD_TK = 512
_SOFTMAX_STRIP = 32
_ONES_ROWS = 16


def _head_mean(v, bmat):
    hi = v.astype(BF16)
    lo = (v - hi.astype(F32)).astype(BF16)
    return _dot(hi, bmat) + _dot(lo, bmat)


def _swap16(y):
    lane = lax.broadcasted_iota(jnp.int32, y.shape, 1)
    return jnp.where(lane % 32 < 16, pltpu.roll(y, y.shape[1] - 16, 1), pltpu.roll(y, 16, 1))


def _rope(y, cos, sin):
    return y * cos + _swap16(y) * sin


def _rope_t(dy, cos, sin):
    return dy * cos + _swap16(dy * sin)


def _dup_rows(t, gidx):
    h = t[gidx * HEAD_DIM:(gidx + 1) * HEAD_DIM]
    return jnp.concatenate([h, h], axis=0)


def _qk_prep(p, gq, gk, cos, sin, bmat, tk, tkf):
    T = p.shape[0]
    nk = T // tk
    sub = tk // tkf
    scale = HEAD_DIM ** -0.5

    def body(q_ref, kv_ref, gq_ref, gk_ref, cos_ref, sin_ref, b_ref, qo_ref, kt_ref, kd_ref, vt_ref, v1_ref):
        cosv, sinv, bm = cos_ref[...], sin_ref[...], b_ref[...]
        for pr in range(ATTN_W // LANES):
            cols = pl.ds(pr * LANES, LANES)
            xq = q_ref[:, cols]
            r = lax.rsqrt(_head_mean(xq * xq, bm) + EPS)
            qo_ref[:, cols] = (_rope((xq * r) * gq_ref[:, cols], cosv, sinv) * scale).astype(BF16)
        xk = kv_ref[:, pl.ds(0, LANES)]
        r = lax.rsqrt(_head_mean(xk * xk, bm) + EPS)
        kt = _rope((xk * r) * gk_ref[...], cosv, sinv).T
        vt = kv_ref[:, pl.ds(LANES, LANES)].T
        for gidx in range(2):
            kdup = _dup_rows(kt, gidx)
            kt_ref[gidx] = kdup.astype(BF16)
            vt_ref[gidx] = _dup_rows(vt, gidx).astype(BF16)
            v1 = jnp.concatenate([vt[gidx * HEAD_DIM:(gidx + 1) * HEAD_DIM],
                                  jnp.ones((_ONES_ROWS, tk), F32)], axis=0).astype(BF16)
            for b in range(sub):
                v1_ref[gidx, b] = v1[:, b * tkf:(b + 1) * tkf]
            kd_ref[gidx] = kdup.T.astype(BF16)

    tspec = pl.BlockSpec((2, None, LANES, tk), lambda i: (0, i, 0, 0))
    dspec = pl.BlockSpec((2, tk, LANES), lambda i: (0, i, 0))
    tab = pl.BlockSpec((tk, LANES), lambda i: (i, 0))
    return pl.pallas_call(
        body, name="qk_prep", grid=(nk,),
        in_specs=[pl.BlockSpec((tk, ATTN_W), lambda i: (i, 0)), pl.BlockSpec((tk, 2 * KV_W), lambda i: (i, ATTN_W // (2 * KV_W))),
                  pl.BlockSpec((1, ATTN_W), lambda i: (0, 0)), pl.BlockSpec((1, KV_W), lambda i: (0, 0)),
                  tab, tab, pl.BlockSpec((LANES, LANES), lambda i: (0, 0))],
        out_specs=[pl.BlockSpec((tk, ATTN_W), lambda i: (i, 0)), tspec, dspec, tspec,
                   pl.BlockSpec((2, sub, HEAD_DIM + _ONES_ROWS, tkf), lambda i: (0, i, 0, 0))],
        out_shape=[_sds((T, ATTN_W), BF16), _sds((2, nk, LANES, tk), BF16), _sds((2, T, LANES), BF16),
                   _sds((2, nk, LANES, tk), BF16), _sds((2, nk * sub, HEAD_DIM + _ONES_ROWS, tkf), BF16)],
        compiler_params=_cp("parallel"))(p, p, gq, gk, cos, sin, bmat)


def _stack_heads(t):
    lo = lax.broadcasted_iota(jnp.int32, (t.shape[0], LANES), 1) < HEAD_DIM
    parts = []
    for pr in range(2):
        tp = t[:, pr * LANES:(pr + 1) * LANES]
        zero = jnp.zeros_like(tp)
        parts += [jnp.where(lo, tp, zero), jnp.where(lo, zero, tp)]
    return jnp.concatenate(parts, axis=0)


def _rows8_reduce(s, op):
    parts = [s[r:r + 8] for r in range(0, s.shape[0], 8)]
    while len(parts) > 1:
        parts = [op(parts[k], parts[k + 1]) for k in range(0, len(parts) - 1, 2)] + (
            [parts[-1]] if len(parts) % 2 else [])
    return parts[0]


def _attn_fwd(q, kd, v1, tq):
    T = q.shape[0]
    nk, tk = v1.shape[1], v1.shape[3]
    vrows = v1.shape[2]
    nq = T // tq
    sq = 4 * tq
    strip = _SOFTMAX_STRIP
    depth = 4
    assert nk % depth == 0

    def body(q_ref, kd_ref, v1_ref, o_ref, lse_ref, qst_ref, s0_ref, s1_ref, s2_ref, s3_ref, pa_ref, pb_ref,
             m_ref, acc_ref):
        s_refs = (s0_ref, s1_ref, s2_ref, s3_ref)
        p_refs = (pa_ref, pb_ref)
        qst_ref[...] = _stack_heads(q_ref[...]).astype(F32).T.astype(BF16)
        m_ref[...] = jnp.full((1, sq), -jnp.inf, F32)
        acc_ref[...] = jnp.zeros((vrows, sq), F32)

        def scores(j):
            return _dot(kd_ref[pl.ds(pl.multiple_of(j * tk, tk), tk), :], qst_ref[...])

        def block_max(s_ref):
            m8 = None
            for c in range(tk // strip):
                part = _rows8_reduce(s_ref[pl.ds(c * strip, strip), :], jnp.maximum)
                m8 = part if m8 is None else jnp.maximum(m8, part)
            return m8

        def exp_pass(s_ref, p_ref, m8):
            m_old = m_ref[...]
            m_new = jnp.maximum(m_old, jnp.max(m8, axis=0, keepdims=True))
            m_ref[...] = m_new
            for c in range(tk // strip):
                rows = pl.ds(c * strip, strip)
                p_ref[rows, :] = jnp.exp(s_ref[rows, :] - m_new).astype(BF16)
            return jnp.exp(m_old - m_new)

        def apply(p_ref, alpha, j):
            acc_ref[...] = alpha * acc_ref[...] + _dot(v1_ref[j], p_ref[...])

        s_refs[0][...] = scores(0)
        s_refs[1][...] = scores(1)

        def trip(t, max_cur):
            for u in range(depth):
                j = depth * t + u
                s_refs[(u + 2) % depth][...] = scores(jnp.minimum(j + 2, nk - 1))
                alpha = exp_pass(s_refs[u], p_refs[u % 2], max_cur)
                max_cur = block_max(s_refs[(u + 1) % depth])
                apply(p_refs[u % 2], alpha, j)
            return max_cur

        lax.fori_loop(0, nk // depth, trip, block_max(s_refs[0]))
        l = acc_ref[pl.ds(HEAD_DIM, 1), :]
        on = acc_ref[pl.ds(0, HEAD_DIM), :] / l
        pairs = []
        for pr in range(2):
            two = jnp.concatenate([on[:, (2 * pr) * tq:(2 * pr + 1) * tq], on[:, (2 * pr + 1) * tq:(2 * pr + 2) * tq]],
                                  axis=0)
            pairs.append(two.T)
        o_ref[...] = jnp.concatenate(pairs, axis=1).astype(BF16)
        lse_ref[...] = jnp.broadcast_to(m_ref[...] + jnp.log(l), (LANES, sq)).T

    row = pltpu.VMEM((1, sq), F32)
    return pl.pallas_call(
        body, name="attn_fwd", grid=(2, nq),
        in_specs=[pl.BlockSpec((tq, 2 * LANES), lambda g, i: (i, g)),
                  pl.BlockSpec((None, T, LANES), lambda g, i: (g, 0, 0)),
                  pl.BlockSpec((None, nk, vrows, tk), lambda g, i: (g, 0, 0, 0))],
        out_specs=[pl.BlockSpec((tq, 2 * LANES), lambda g, i: (i, g)),
                   pl.BlockSpec((None, None, sq, LANES), lambda g, i: (g, i, 0, 0))],
        out_shape=[_sds((T, ATTN_W), BF16), _sds((2, nq, sq, LANES), F32)],
        scratch_shapes=[pltpu.VMEM((LANES, sq), BF16)] + [pltpu.VMEM((tk, sq), F32)] * depth
        + [pltpu.VMEM((tk, sq), BF16), pltpu.VMEM((tk, sq), BF16), row, pltpu.VMEM((vrows, sq), F32)],
        compiler_params=_cp("parallel", "parallel"))(q, kd, v1)


def _attn_bwd(q, o, dmix, lse, kt, kd, vt, tq):
    T = q.shape[0]
    nk, tk = kt.shape[1], kt.shape[3]
    nq = T // tq
    sq = 4 * tq
    rep = tk // LANES

    def body(q_ref, o_ref, do_ref, lse_ref, kt_ref, kd_ref, vt_ref, dq_ref, dkt_ref, dvt_ref):
        i = pl.program_id(1)
        qs = _stack_heads(q_ref[...])
        dof = _stack_heads(do_ref[...])
        dos = dof.astype(BF16)
        qst = qs.astype(F32).T.astype(BF16)
        dost = dof.T.astype(BF16)
        o_pair = o_ref[...].astype(F32)
        os_ = jnp.concatenate([o_pair[:, 0:LANES], o_pair[:, 0:LANES], o_pair[:, LANES:], o_pair[:, LANES:]], axis=0)
        delta = jnp.sum(dof * os_, axis=-1, keepdims=True)
        lse_t = jnp.concatenate([lse_ref[...]] * rep, axis=1)

        @pl.when(i == 0)
        def _():
            dkt_ref[...] = jnp.zeros_like(dkt_ref)
            dvt_ref[...] = jnp.zeros_like(dvt_ref)

        def step(j, dq):
            kdb = kd_ref[pl.ds(pl.multiple_of(j * tk, tk), tk), :]
            pexp = jnp.exp(_dot(qs, kt_ref[j]) - lse_t)
            ds = pexp * (_dot(dos, vt_ref[j]) - delta)
            pb = pexp.astype(BF16)
            dsb = ds.astype(BF16)
            dvt_ref[j] += _dot(dost, pb)
            dkt_ref[j] += _dot(qst, dsb)
            return dq + _dot(dsb, kdb)

        dq = lax.fori_loop(0, nk, step, jnp.zeros((sq, LANES), F32))
        lo = lax.broadcasted_iota(jnp.int32, (tq, LANES), 1) < HEAD_DIM
        dq_ref[...] = jnp.concatenate([jnp.where(lo, dq[0:tq], dq[tq:2 * tq]),
                                       jnp.where(lo, dq[2 * tq:3 * tq], dq[3 * tq:4 * tq])], axis=1)

    tspec = pl.BlockSpec((None, nk, LANES, tk), lambda g, i: (g, 0, 0, 0))
    qspec = pl.BlockSpec((tq, 2 * LANES), lambda g, i: (i, g))
    return pl.pallas_call(
        body, name="attn_bwd", grid=(2, nq),
        in_specs=[qspec, qspec, qspec, pl.BlockSpec((None, None, sq, LANES), lambda g, i: (g, i, 0, 0)),
                  tspec, pl.BlockSpec((None, T, LANES), lambda g, i: (g, 0, 0)), tspec],
        out_specs=[qspec, tspec, tspec],
        out_shape=[_sds((T, ATTN_W), F32), _sds((2, nk, LANES, tk), F32), _sds((2, nk, LANES, tk), F32)],
        compiler_params=_cp("parallel", "arbitrary"))(q, o, dmix, lse, kt, kd, vt)


def _fold_t(t_ref):
    rows = []
    for gidx in range(2):
        t = t_ref[gidx]
        rows.append(t[0:HEAD_DIM] + t[HEAD_DIM:2 * HEAD_DIM])
    return jnp.concatenate(rows, axis=0).T


def _qk_bwd(p, dq, dkt, dvt, gq, gk, cos, sin, bmat):
    T = p.shape[0]
    nk, tk = dkt.shape[1], dkt.shape[3]
    scale = HEAD_DIM ** -0.5

    def norm_bwd(x, dy, gain, bm):
        r = lax.rsqrt(_head_mean(x * x, bm) + EPS)
        xh = x * r
        dxh = dy * gain
        return r * (dxh - xh * _head_mean(dxh * xh, bm)), jnp.sum(dy * xh, axis=0, keepdims=True)

    def body(q_ref, kv_ref, dq_ref, dkt_ref, dvt_ref, gq_ref, gk_ref, cos_ref, sin_ref, b_ref, o_ref, dgq_ref, dgk_ref):
        i = pl.program_id(0)
        cosv, sinv, bm = cos_ref[...], sin_ref[...], b_ref[...]
        dgq = []
        for pr in range(ATTN_W // LANES):
            cols = pl.ds(pr * LANES, LANES)
            dy = _rope_t(dq_ref[:, cols] * scale, cosv, sinv)
            dx, dg = norm_bwd(q_ref[:, cols], dy, gq_ref[:, cols], bm)
            o_ref[:, cols] = dx.astype(BF16)
            dgq.append(dg)
        dy = _rope_t(_fold_t(dkt_ref), cosv, sinv)
        dx, dgk = norm_bwd(kv_ref[:, pl.ds(0, LANES)], dy, gk_ref[...], bm)
        o_ref[:, pl.ds(ATTN_W, LANES)] = dx.astype(BF16)
        o_ref[:, pl.ds(ATTN_W + LANES, LANES)] = _fold_t(dvt_ref).astype(BF16)
        _accum(dgq_ref, i, jnp.concatenate(dgq, axis=1))
        _accum(dgk_ref, i, dgk)

    tspec = pl.BlockSpec((2, None, LANES, tk), lambda i: (0, i, 0, 0))
    tab = pl.BlockSpec((tk, LANES), lambda i: (i, 0))
    return pl.pallas_call(
        body, name="qk_bwd", grid=(nk,),
        in_specs=[pl.BlockSpec((tk, ATTN_W), lambda i: (i, 0)), pl.BlockSpec((tk, 2 * KV_W), lambda i: (i, ATTN_W // (2 * KV_W))),
                  pl.BlockSpec((tk, ATTN_W), lambda i: (i, 0)), tspec, tspec,
                  pl.BlockSpec((1, ATTN_W), lambda i: (0, 0)), pl.BlockSpec((1, KV_W), lambda i: (0, 0)),
                  tab, tab, pl.BlockSpec((LANES, LANES), lambda i: (0, 0))],
        out_specs=[pl.BlockSpec((tk, ATTN_W + 2 * KV_W), lambda i: (i, 0)),
                   pl.BlockSpec((1, ATTN_W), lambda i: (0, 0)), pl.BlockSpec((1, KV_W), lambda i: (0, 0))],
        out_shape=[_sds((T, ATTN_W + 2 * KV_W), BF16), _sds((1, ATTN_W), F32), _sds((1, KV_W), F32)],
        compiler_params=_cp("arbitrary"))(p, p, dq, dkt, dvt, gq, gk, cos, sin, bmat)


def _loss_head(y, target):
    T, Dm = y.shape
    tm = min(T, _ROW_TILE)

    def body(y_ref, t_ref, dy_ref, l_ref):
        i = pl.program_id(0)
        err = y_ref[...] - t_ref[...]
        dy_ref[...] = err * (1.0 / Dm)
        part = jnp.sum(jnp.sum(err * err, axis=-1, keepdims=True), axis=0, keepdims=True) * (0.5 / Dm)
        _accum(l_ref, i, jnp.broadcast_to(part, (8, LANES)))

    row = pl.BlockSpec((tm, Dm), lambda i: (i, 0))
    return pl.pallas_call(
        body, name="loss_head", grid=(T // tm,), in_specs=[row, row],
        out_specs=[row, pl.BlockSpec((8, LANES), lambda i: (0, 0))],
        out_shape=[_sds((T, Dm), F32), _sds((8, LANES), F32)], compiler_params=_cp("arbitrary"))(y, target)


def _adamw(w, g, m, v, name):
    R, C = w.shape
    tr = R
    for cand in (512, 256, 128, 64, 32, 16, 8):
        if R % cand == 0:
            tr = cand
            break
    c1 = 1.0 - ADAM_B1 ** ADAM_STEP
    c2 = 1.0 - ADAM_B2 ** ADAM_STEP

    def body(w_ref, g_ref, m_ref, v_ref, d_ref, mo_ref, vo_ref):
        gv = g_ref[...]
        mn = ADAM_B1 * m_ref[...] + (1.0 - ADAM_B1) * gv
        vn = ADAM_B2 * v_ref[...] + (1.0 - ADAM_B2) * (gv * gv)
        d_ref[...] = -ADAM_LR * ((mn / c1) / (jnp.sqrt(vn / c2) + ADAM_EPS) + ADAM_WD * w_ref[...])
        mo_ref[...] = mn
        vo_ref[...] = vn

    blk = pl.BlockSpec((tr, C), lambda i: (i, 0))
    return pl.pallas_call(
        body, name=name, grid=(R // tr,), in_specs=[blk] * 4, out_specs=[blk] * 3,
        out_shape=[_sds((R, C), F32)] * 3, compiler_params=_cp("parallel"))(w, g, m, v)


def _cast_bf16(w, name):
    R, C = w.shape
    tr = 512 if R % 512 == 0 else 256

    def body(w_ref, o_ref):
        o_ref[...] = w_ref[...].astype(BF16)

    blk = pl.BlockSpec((tr, C), lambda i: (i, 0))
    return pl.pallas_call(body, name=name, grid=(R // tr,), in_specs=[blk], out_specs=blk,
                          out_shape=_sds((R, C), BF16), compiler_params=_cp("parallel"))(w)


def _position():
    x, y, c = lax.axis_index("x"), lax.axis_index("y"), lax.axis_index("c")
    return x, y, c


def _other_chips(x, y):
    return [(1 - x, y), (x, 1 - y), (1 - x, 1 - y)]


_HBM = pl.BlockSpec(memory_space=pltpu.HBM)
_SEM = pl.BlockSpec(memory_space=pltpu.SEMAPHORE)
_EFFECT = pltpu.SideEffectType.DATAFLOW_SIDE_EFFECTING


def _chip_copies(srcs, lands, send_sems, recv_sems, per_chip, arriving):
    x, y, c = _position()
    me = 2 * x + y
    copies = []
    for t, (src, land) in enumerate(zip(srcs, lands)):
        for k, (px, py) in enumerate(_other_chips(x, y)):
            peer = 2 * px + py
            copies.append(pltpu.make_async_remote_copy(
                src_ref=src.at[peer] if per_chip else src, dst_ref=land.at[peer if arriving else me],
                send_sem=send_sems[3 * t + k], recv_sem=recv_sems[3 * t + k],
                device_id=(px, py, c), device_id_type=MESH))
    return copies


def _chips_start(srcs, per_chip, name):
    n = len(srcs)
    slab = [s.shape[1:] if per_chip else s.shape for s in srcs]
    lands = [lax.empty((N_CHIPS,) + sh, s.dtype) for sh, s in zip(slab, srcs)]

    ns = 3 * n

    def body(*refs):
        ins = refs[:2 * n]
        send_sems, recv_sems = refs[2 * n:2 * n + ns], refs[2 * n + ns:2 * n + 2 * ns]
        token = refs[-1]
        for cp in _chip_copies(ins[:n], ins[n:], send_sems, recv_sems, per_chip, False):
            cp.start()
        token[...] = jnp.zeros_like(token)

    args = [pltpu.with_memory_space_constraint(a, pltpu.HBM) for a in list(srcs) + lands]
    outs = pl.pallas_call(
        body, name=name,
        out_shape=[pltpu.SemaphoreType.DMA(())] * (2 * ns)
        + [pltpu.HBM(a.shape, a.dtype) for a in args] + [_sds((8, LANES), F32)],
        in_specs=[_HBM] * (2 * n),
        out_specs=[_SEM] * (2 * ns) + [_HBM] * (2 * n) + [pl.BlockSpec(memory_space=pltpu.VMEM)],
        input_output_aliases={i: 2 * ns + i for i in range(2 * n)},
        compiler_params=pltpu.CompilerParams(has_side_effects=_EFFECT))(*args)
    sems, rest = outs[:2 * ns], outs[2 * ns:]
    return sems[:ns], sems[ns:], rest[:n], rest[n:2 * n], rest[-1]


def _chips_wait(handle, after, per_chip, name):
    send_sems, recv_sems, srcs, lands, _ = handle
    n = len(srcs)
    ns = 3 * n

    def body(*refs):
        ins = refs[:2 * n]
        s_sems, r_sems = refs[2 * n:2 * n + ns], refs[2 * n + ns:2 * n + 2 * ns]
        for cp in _chip_copies(ins[:n], ins[n:], s_sems, r_sems, per_chip, False):
            cp.wait_send()
        for cp in _chip_copies(ins[:n], ins[n:], s_sems, r_sems, per_chip, True):
            cp.wait_recv()

    outs = pl.pallas_call(
        body, name=name,
        out_shape=[pltpu.HBM(a.shape, a.dtype) for a in list(srcs) + list(lands)],
        in_specs=[_HBM] * (2 * n) + [_SEM] * (2 * ns) + [ANY],
        out_specs=[_HBM] * (2 * n),
        input_output_aliases={i: i for i in range(2 * n)},
        compiler_params=pltpu.CompilerParams(has_side_effects=_EFFECT))(*srcs, *lands, *send_sems, *recv_sems, after)
    return outs[:n], outs[n:]


def _sum_chips_own(land, own, name):
    S, R, C = land.shape
    tr = R
    for cand in (256, 128, 64, 32, 16, 8):
        if R % cand == 0:
            tr = cand
            break

    def body(l_ref, o_ref, out_ref):
        x, y, _ = _position()
        me = 2 * x + y
        mine = o_ref[me] if own.ndim == 3 else o_ref[...]
        acc = None
        for k in range(S):
            part = jnp.where(me == k, mine, l_ref[k])
            acc = part if acc is None else acc + part
        out_ref[...] = acc

    blk = pl.BlockSpec((S, tr, C), lambda i: (0, i, 0))
    row = pl.BlockSpec((tr, C), lambda i: (i, 0))
    return pl.pallas_call(
        body, name=name, grid=(R // tr,), in_specs=[blk, blk if own.ndim == 3 else row], out_specs=row,
        out_shape=_sds((R, C), F32), compiler_params=_cp("parallel"))(land, own)


def _add_pair(a, b, name):
    R, C = a.shape

    def body(a_ref, b_ref, o_ref):
        o_ref[...] = a_ref[...] + b_ref[...]

    blk = pl.BlockSpec((R, C), lambda: (0, 0))
    return pl.pallas_call(body, name=name, in_specs=[blk, blk], out_specs=blk, out_shape=_sds((R, C), F32))(a, b)


def _exchange_sibling(arrays):
    n = len(arrays)

    def body(*refs):
        ins, outs = refs[:n], refs[n:2 * n]
        send_sems, recv_sems = refs[2 * n:]
        x, y, c = _position()
        sends = []
        for t in range(n):
            cp = pltpu.make_async_remote_copy(src_ref=ins[t], dst_ref=outs[t], send_sem=send_sems.at[t],
                                              recv_sem=recv_sems.at[t], device_id=(x, y, 1 - c), device_id_type=MESH)
            cp.start()
            sends.append(cp)
        for cp in sends:
            cp.wait()

    return pl.pallas_call(
        body, name="grads_to_sibling",
        in_specs=[ANY] * n, out_specs=[ANY] * n, out_shape=[_sds(a.shape, a.dtype) for a in arrays],
        scratch_shapes=[pltpu.SemaphoreType.DMA((n,)), pltpu.SemaphoreType.DMA((n,))],
        compiler_params=pltpu.CompilerParams(has_side_effects=True))(*arrays)


def _sibling_start(src, name):
    land = lax.empty(src.shape, src.dtype)

    def body(s_ref, l_ref, send_sem, recv_sem, s_thru, l_thru, token):
        x, y, c = _position()
        pltpu.make_async_remote_copy(src_ref=s_ref, dst_ref=l_ref, send_sem=send_sem, recv_sem=recv_sem,
                                     device_id=(x, y, 1 - c), device_id_type=MESH).start()
        token[...] = jnp.zeros_like(token)

    args = [pltpu.with_memory_space_constraint(a, pltpu.HBM) for a in (src, land)]
    return pl.pallas_call(
        body, name=name,
        out_shape=[pltpu.SemaphoreType.DMA(()), pltpu.SemaphoreType.DMA(())]
        + [pltpu.HBM(a.shape, a.dtype) for a in args] + [_sds((8, LANES), F32)],
        in_specs=[_HBM, _HBM], out_specs=[_SEM, _SEM, _HBM, _HBM, pl.BlockSpec(memory_space=pltpu.VMEM)],
        input_output_aliases={0: 2, 1: 3},
        compiler_params=pltpu.CompilerParams(has_side_effects=_EFFECT))(*args)


def _sibling_wait(handle, after, name):
    send_sem, recv_sem, src, land, _ = handle

    def body(s_ref, l_ref, s_sem, r_sem, after_ref, s_dead, got_ref):
        x, y, c = _position()
        cp = pltpu.make_async_remote_copy(src_ref=s_ref, dst_ref=l_ref, send_sem=s_sem, recv_sem=r_sem,
                                          device_id=(x, y, 1 - c), device_id_type=MESH)
        cp.wait_send()
        cp.wait_recv()

    return pl.pallas_call(
        body, name=name, out_shape=[pltpu.HBM(src.shape, src.dtype), pltpu.HBM(land.shape, land.dtype)],
        in_specs=[_HBM, _HBM, _SEM, _SEM, ANY], out_specs=[_HBM, _HBM], input_output_aliases={0: 0, 1: 1},
        compiler_params=pltpu.CompilerParams(has_side_effects=_EFFECT))(src, land, send_sem, recv_sem, after)


def _adamw_sum(w, ga, gb, m, v, name):
    R, C = w.shape
    tr = next(t for t in (512, 256, 128, 64) if R % t == 0 and t * C * 4 <= (1 << 20))
    c1 = 1.0 - ADAM_B1 ** ADAM_STEP
    c2 = 1.0 - ADAM_B2 ** ADAM_STEP

    def body(w_ref, ga_ref, gb_ref, m_ref, v_ref, g_ref, d_ref, mo_ref, vo_ref):
        gv = ga_ref[...] + gb_ref[...]
        mn = ADAM_B1 * m_ref[...] + (1.0 - ADAM_B1) * gv
        vn = ADAM_B2 * v_ref[...] + (1.0 - ADAM_B2) * (gv * gv)
        g_ref[...] = gv
        d_ref[...] = -ADAM_LR * ((mn / c1) / (jnp.sqrt(vn / c2) + ADAM_EPS) + ADAM_WD * w_ref[...])
        mo_ref[...] = mn
        vo_ref[...] = vn

    blk = pl.BlockSpec((tr, C), lambda i: (i, 0))
    return pl.pallas_call(
        body, name=name, grid=(R // tr,), in_specs=[blk] * 5, out_specs=[blk] * 4,
        out_shape=[_sds((R, C), F32)] * 4, compiler_params=_cp("parallel"))(w, ga, gb, m, v)


def _rope_tables(T):
    pos = jnp.arange(T)
    row = (pos // GRID_W).astype(F32)
    col = (pos % GRID_W).astype(F32)
    inv = 1.0 / (ROPE_THETA ** (jnp.arange(AXIS_DIM // 2, dtype=F32) * 2.0 / AXIS_DIM))
    ar, ac = row[:, None] * inv[None, :], col[:, None] * inv[None, :]
    cos = jnp.concatenate([jnp.cos(ar), jnp.cos(ar), jnp.cos(ac), jnp.cos(ac)], axis=-1)
    sin = jnp.concatenate([-jnp.sin(ar), jnp.sin(ar), -jnp.sin(ac), jnp.sin(ac)], axis=-1)
    return jnp.tile(cos, (1, LANES // HEAD_DIM)), jnp.tile(sin, (1, LANES // HEAD_DIM))


def _head_mean_matrix():
    h = jnp.arange(LANES) // HEAD_DIM
    return jnp.where(h[:, None] == h[None, :], 1.0 / HEAD_DIM, 0.0).astype(BF16)


def _pack(arrays):
    flat = jnp.concatenate([a.reshape(-1) for a in arrays])
    rows = -(-flat.shape[0] // LANES)
    rows = -(-rows // 256) * 256
    return jnp.pad(flat, (0, rows * LANES - flat.shape[0])).reshape(rows, LANES)


def _unpack(packed, like):
    flat = packed.reshape(-1)
    out, off = [], 0
    for a in like:
        out.append(flat[off:off + a.size].reshape(a.shape))
        off += a.size
    return out


def _layer_fwd(x, h, lw, consts, ffn_weights, next_g):
    cos, sin, bmat = consts
    T = x.shape[0]
    tk = min(T, _ATTN_TK)
    tq = min(T, _ATTN_TQ)
    if h is None:
        h = _norm_fwd(x, lw["norm1_g"])
    p = _mm_nn(h, lw["w_in"], out_dtype=F32, name="mm_p", tm=_WIDE_ROW_TILE)
    qn, kt, kd, vt, v1 = _qk_prep(p, lw["gq"], lw["gk"], cos, sin, bmat, tk, min(tk, _ATTN_FWD_TK))
    o, lse = _attn_fwd(qn, kd, v1, tq)
    go = _sg_fwd(p, lw["sg_norm_g"], lw["sg_w"], lw["sg_bias"])
    co = _convmix_fwd(p, lw["conv_w"])
    mix = jnp.concatenate([o, go, co], axis=1)
    x_mid, h2 = _mm_nn(mix, lw["w_out"], out_dtype=F32, name="mm_out", res=x, norm_g=lw["norm2_g"], tm=_WIDE_ROW_TILE)
    lw.update(ffn_weights(x_mid))
    up = _mm_nn(h2, lw["w_up"], out_dtype=F32, name="mm_up", tm=_FFN_ROW_TILE)
    act = _ffn_act_fwd(up, lw["ffn_conv_w"])
    if next_g is None:
        x_out, h_next = _mm_nn(act, lw["w_down"], out_dtype=F32, name="mm_down_last", res=x_mid), None
    else:
        x_out, h_next = _mm_nn(act, lw["w_down"], out_dtype=F32, name="mm_down", res=x_mid, norm_g=next_g)
    saved = dict(x=x, h=h, p=p, qn=qn, kt=kt, kd=kd, vt=vt, o=o, lse=lse, mix=mix, x_mid=x_mid, h2=h2, up=up, act=act)
    return x_out, h_next, saved


def _layer_bwd(dx, s, lw, consts, send):
    cos, sin, bmat = consts
    T = dx.shape[0]
    tq = min(T, _ATTN_TQ)
    g = {}
    d_act = _mm_nt(dx, lw["w_down"], name="mm_dact")
    g_down = _mm_tn(s["act"], dx, tk=D_FF // 2, tn=D_MODEL, name="mm_dwdown", tm=_WGRAD_ROWS // 2)
    tok = send("w_down", g_down.reshape(N_CHIPS, D_FF // N_CHIPS, D_MODEL))
    d_up, d_cw = _ffn_act_bwd(s["up"], d_act, lw["ffn_conv_w"] + tok)
    g["ffn_conv_w"] = d_cw.transpose(1, 0, 2).reshape(3, 2 * D_FF)
    tok = send("w_up", _mm_tn(s["h2"], d_up, tk=512, tn=D_FF // 2, name="mm_dwup", shards=N_CHIPS, tm=_WGRAD_ROWS))
    dx2, g["norm2_g"] = _mm_nt(d_up, lw["w_up"], name="mm_dh2", norm=(s["x_mid"], dx, lw["norm2_g"] + tok))
    d_mix = _mm_nt(dx2, lw["w_out"], name="mm_dmix", tm=_WIDE_ROW_TILE)
    g_out = _mm_tn(s["mix"], dx2, tk=512, tn=D_MODEL, name="mm_dwout", tm=_WGRAD_ROWS // 2)
    tok = send("w_out", g_out.reshape(N_CHIPS, D_MODEL // N_CHIPS, D_MODEL))
    dp_c, g["conv_w"] = _convmix_bwd(s["p"], d_mix, lw["conv_w"] + tok)
    dp_b, g["sg_w"], d_bias, g["sg_norm_g"] = _sg_bwd(s["p"], d_mix, lw["sg_norm_g"], lw["sg_w"], lw["sg_wt"], lw["sg_bias"])
    g["sg_b"] = d_bias.reshape(SG_CHUNK, SG_W // HEAD_DIM, HEAD_DIM).sum(axis=-1).T
    dq, dkt, dvt = _attn_bwd(s["qn"], s["o"], d_mix, s["lse"], s["kt"], s["kd"], s["vt"], tq)
    dp_a, d_gq, d_gk = _qk_bwd(s["p"], dq, dkt, dvt, lw["gq"], lw["gk"], cos, sin, bmat)
    g["q_norm_g"] = d_gq.reshape(ATTN_W // HEAD_DIM, HEAD_DIM).sum(axis=0)
    g["k_norm_g"] = d_gk.reshape(KV_W // HEAD_DIM, HEAD_DIM).sum(axis=0)
    dp = jnp.concatenate([dp_a, dp_b, dp_c], axis=1)
    tok = send("w_in", _mm_tn(s["h"], dp, tk=D_MODEL, tn=512, name="mm_dwin", shards=N_CHIPS, tm=_WGRAD_ROWS))
    dx_in, g["norm1_g"] = _mm_nt(dp, lw["w_in"], name="mm_dh", norm=(s["x"], dx2, lw["norm1_g"] + tok), tm=_WIDE_ROW_TILE)
    return dx_in, g


def _layer_weights(l, full, small):
    sg_w = small["sg_w"][l]
    sg_b = small["sg_b"][l]
    return dict(
        norm1_g=small["norm1_g"][l][None, :], norm2_g=small["norm2_g"][l][None, :],
        gq=jnp.tile(small["q_norm_g"][l], ATTN_W // HEAD_DIM)[None, :],
        gk=jnp.tile(small["k_norm_g"][l], KV_W // HEAD_DIM)[None, :],
        sg_norm_g=small["sg_norm_g"][l][None, :],
        sg_w=sg_w.astype(BF16), sg_wt=sg_w.transpose(0, 2, 1).astype(BF16),
        sg_bias=jnp.repeat(sg_b.T, HEAD_DIM, axis=1),
        **full)


_BIG = ("w_in", "w_out", "ffn_w_up", "ffn_w_down")
_SMALL_REPL = ("norm1_g", "q_norm_g", "k_norm_g", "sg_norm_g", "sg_w", "sg_b", "norm2_g")
_SMALL_SHARD = ("conv_w", "ffn_conv_w")
_ORDER = ("norm1_g", "w_in", "q_norm_g", "k_norm_g", "sg_norm_g", "sg_w", "sg_b", "conv_w", "w_out", "norm2_g",
          "ffn_w_up", "ffn_conv_w", "ffn_w_down")


def kernel(x, norm1_g, w_in, q_norm_g, k_norm_g, sg_norm_g, sg_w, sg_b, conv_w, w_out, norm2_g, ffn_w_up, ffn_conv_w, ffn_w_down, loss_target, m_norm1_g, m_w_in, m_q_norm_g, m_k_norm_g, m_sg_norm_g, m_sg_w, m_sg_b, m_conv_w, m_w_out, m_norm2_g, m_ffn_w_up, m_ffn_conv_w, m_ffn_w_down, v_norm1_g, v_w_in, v_q_norm_g, v_k_norm_g, v_sg_norm_g, v_sg_w, v_sg_b, v_conv_w, v_w_out, v_norm2_g, v_ffn_w_up, v_ffn_conv_w, v_ffn_w_down):
    w = dict(norm1_g=norm1_g, w_in=w_in, q_norm_g=q_norm_g, k_norm_g=k_norm_g, sg_norm_g=sg_norm_g, sg_w=sg_w,
             sg_b=sg_b, conv_w=conv_w, w_out=w_out, norm2_g=norm2_g, ffn_w_up=ffn_w_up, ffn_conv_w=ffn_conv_w,
             ffn_w_down=ffn_w_down)
    mom = dict(norm1_g=m_norm1_g, w_in=m_w_in, q_norm_g=m_q_norm_g, k_norm_g=m_k_norm_g, sg_norm_g=m_sg_norm_g,
               sg_w=m_sg_w, sg_b=m_sg_b, conv_w=m_conv_w, w_out=m_w_out, norm2_g=m_norm2_g, ffn_w_up=m_ffn_w_up,
               ffn_conv_w=m_ffn_conv_w, ffn_w_down=m_ffn_w_down)
    var = dict(norm1_g=v_norm1_g, w_in=v_w_in, q_norm_g=v_q_norm_g, k_norm_g=v_k_norm_g, sg_norm_g=v_sg_norm_g,
               sg_w=v_sg_w, sg_b=v_sg_b, conv_w=v_conv_w, w_out=v_w_out, norm2_g=v_norm2_g, ffn_w_up=v_ffn_w_up,
               ffn_conv_w=v_ffn_conv_w, ffn_w_down=v_ffn_w_down)
    L = DEPTH
    T = x.shape[1]
    xs = x.reshape(T, D_MODEL)
    target = loss_target.reshape(T, D_MODEL)

    chip = 2 * lax.axis_index("x") + lax.axis_index("y")

    shards = [_cast_bf16(w[n].reshape(-1, w[n].shape[-1]), "cast_" + n).reshape(w[n].shape) for n in _BIG]
    shards += [conv_w, ffn_conv_w]
    w_in_s, w_out_s, w_up_s, w_down_s, conv_s, fconv_s = shards
    gathers = []
    for l in range(L):
        gathers.append((_chips_start([w_in_s[l], w_out_s[l], conv_s[l]], False, "gather_start_%da" % l),
                        _chips_start([w_up_s[l], w_down_s[l], fconv_s[l]], False, "gather_start_%db" % l)))
    start_token = sum(h[4][0, 0] for pair in gathers for h in pair)
    consts = _rope_tables(T) + (_head_mean_matrix(),)

    def gathered(handle, after, name):
        own, lands = _chips_wait(handle, after, False, name)
        return [lax.dynamic_update_slice(ld, o[None], (chip,) + (jnp.int32(0),) * o.ndim) for ld, o in zip(lands, own)]

    saved, lws = [], []
    act_x, act_h = xs, None
    for l in range(L):
        g_in, g_out, g_conv = gathered(gathers[l][0], act_x if l else gathers[-1][1][4], "gather_wait_%da" % l)
        lw = _layer_weights(l, dict(w_in=g_in, w_out=g_out.reshape(D_MODEL, D_MODEL),
                                    conv_w=g_conv.transpose(1, 0, 2).reshape(3, CONV_W)), w)
        if l == 0:
            lw["norm1_g"] = lw["norm1_g"] + start_token

        def ffn_weights(after, l=l):
            g_up, g_down, g_fconv = gathered(gathers[l][1], after, "gather_wait_%db" % l)
            return dict(w_up=g_up, w_down=g_down.reshape(D_FF, D_MODEL),
                        ffn_conv_w=g_fconv.transpose(1, 0, 2).reshape(3, 2 * D_FF))

        next_g = w["norm1_g"][l + 1][None, :] if l + 1 < L else None
        act_x, act_h, s = _layer_fwd(act_x, act_h, lw, consts, ffn_weights, next_g)
        saved.append(s)
        lws.append(lw)
    dx, loss_blk = _loss_head(act_x, target)
    loss = lax.psum(loss_blk[0, 0], ("x", "y", "c"))

    grads = [None] * L
    partial = [None] * L

    def collect(pending, after, l):
        sums = {}
        for name, handle in pending:
            own, lands = _chips_wait(handle, after, True, "grad_wait_%d_%s" % (l, name))
            sums[name] = _sum_chips_own(lands[0], own[0], "sum_chips_" + name)
        return sums

    pending_prev = None
    for l in reversed(range(L)):
        pending = []

        def send(name, g4, l=l, pending=pending):
            handle = _chips_start([g4], True, "grad_start_%d_%s" % (l, name))
            pending.append((name, handle))
            return handle[4][0, 0]

        dx, g = _layer_bwd(dx, saved[l], lws[l], consts, send)
        grads[l] = g
        if pending_prev is not None:
            partial[l + 1] = collect(pending_prev, dx, l + 1)
        pending_prev = pending
    grad_x = dx.reshape(x.shape)

    small_names = _SMALL_REPL + _SMALL_SHARD
    small_local = [jnp.stack([grads[l][n].reshape(-1) for l in range(L)]) for n in small_names]
    small_handle = _chips_start([_pack(small_local)], False, "small_start")
    partial[0] = collect(pending_prev, small_handle[4], 0)

    short = dict(w_in="w_in", w_out="w_out", ffn_w_up="w_up", ffn_w_down="w_down")
    mine = [jnp.stack([partial[l][short[n]] for l in range(L)]) for n in _BIG]
    small_own, small_lands = _chips_wait(small_handle, mine[-1], False, "small_wait")
    mine.append(_sum_chips_own(small_lands[0], small_own[0], "sum_chips_small"))
    order = sorted(range(len(mine)), key=lambda k: -mine[k].size)
    swaps = {k: _sibling_start(mine[k], "swap_start_%d" % k) for k in order}
    all_started = sum(swaps[k][4] for k in order)
    theirs = [None] * len(mine)
    after = all_started
    delta, new_m, new_v, big = {}, {}, {}, {}
    for k in order:
        mine[k], theirs[k] = _sibling_wait(swaps[k], after, "swap_wait_%d" % k)
        if k < len(_BIG):
            n = _BIG[k]
            shp = w[n].shape
            v2 = lambda a: a.reshape(-1, shp[-1])
            outs = _adamw_sum(v2(w[n]), v2(mine[k]), v2(theirs[k]), v2(mom[n]), v2(var[n]), "adamw_" + n)
            big[n], delta[n], new_m[n], new_v[n] = [o.reshape(shp) for o in outs]
            after = outs[0]
    grad = dict(zip(small_names, _unpack(_add_pair(mine[-1], theirs[-1], "add_small"), small_local)))
    for n in _SMALL_REPL:
        grad[n] = grad[n].reshape(w[n].shape)
    for n in _SMALL_SHARD:
        full_w = grad[n].reshape(L, 3, -1)
        width = w[n].shape[-1]
        grad[n] = lax.dynamic_slice_in_dim(full_w, chip * width, width, axis=2)
    grad.update(big)
    for group, gname in ((_SMALL_REPL, "adamw_small"), (_SMALL_SHARD, "adamw_conv")):
        like = [w[n] for n in group]
        outs = _adamw(_pack([w[n] for n in group]), _pack([grad[n] for n in group]), _pack([mom[n] for n in group]),
                      _pack([var[n] for n in group]), gname)
        for res, dst in zip(outs, (delta, new_m, new_v)):
            for n, a in zip(group, _unpack(res, like)):
                dst[n] = a

    return (loss, grad_x, *[grad[n] for n in _ORDER], *[delta[n] for n in _ORDER],
            *[new_m[n] for n in _ORDER], *[new_v[n] for n in _ORDER])
```
